```python
import jax
import jax.numpy as jnp
from jax import lax
import numpy as np

D_MODEL = 1024
BATCH = 32
SEQ = 2048
DEPTH = 2

GRID_W = 64
CTX_LEN = 256
NORM_EPS = 1e-6

MLA_HEADS = 8
MLA_Q_RANK = 256
MLA_KV_RANK = 128
MLA_NOPE = 64
MLA_ROPE = 32
MLA_V = 64
MLA_WIDTH = MLA_HEADS * MLA_V
ROPE_BASE = 10000.0
Q_BLOCK = 128

POOL_WINDOWS = (2, 4, 8, 16)
POOL_WIDTH = 512
POOL_GROUP = POOL_WIDTH // len(POOL_WINDOWS)

GLA_HEADS = 4
GLA_DK = 64
GLA_DV = 128
GLA_KW = GLA_HEADS * GLA_DK
GLA_WIDTH = GLA_HEADS * GLA_DV
GLA_GATE_RANK = 16
GLA_TAU = 16.0
GLA_CHUNK = 64

N_BRANCH = 3

IN_NAMES = ('mla_q', 'mla_kv', 'mla_kr', 'mla_gate', 'pool_x', 'pool_gate',
            'gla_q', 'gla_k', 'gla_v', 'gla_af', 'gla_ab', 'gla_gate', 'merge')
IN_SIZES = (MLA_Q_RANK, MLA_KV_RANK, MLA_ROPE, MLA_WIDTH, POOL_WIDTH, POOL_WIDTH,
            GLA_KW, GLA_KW, GLA_WIDTH, GLA_GATE_RANK, GLA_GATE_RANK, GLA_WIDTH, N_BRANCH * D_MODEL)
D_IN = sum(IN_SIZES)

kernel_name = 'hybrid_mla_pool_gla_prefix_dit'


def rmsnorm(x, g):
    xf = x.astype(jnp.float32)
    y = xf * lax.rsqrt(jnp.mean(xf * xf, axis=-1, keepdims=True) + NORM_EPS)
    return (y * g.astype(jnp.float32)).astype(x.dtype)


def split_columns(z):
    offsets = [int(o) for o in np.cumsum(IN_SIZES)[:-1]]
    return dict(zip(IN_NAMES, jnp.split(z, offsets, axis=-1)))


def flip(t):
    return t[:, ::-1]


def axial_rope_tables(row, col):
    half = MLA_ROPE // 2
    inv = ROPE_BASE ** (-jnp.arange(0, half, 2, dtype=jnp.float32) / half)
    ang_r = row.astype(jnp.float32)[:, None] * inv
    ang_c = col.astype(jnp.float32)[:, None] * inv
    ang = jnp.concatenate([ang_r, ang_r, ang_c, ang_c], axis=-1)
    return jnp.cos(ang), jnp.sin(ang)


def apply_rope(x, cos, sin):
    half = MLA_ROPE // 2
    quarter = half // 2

    def rot(v):
        return jnp.concatenate([-v[..., quarter:], v[..., :quarter]], axis=-1)

    rotated = jnp.concatenate([rot(x[..., :half]), rot(x[..., half:])], axis=-1)
    return (x * cos + rotated * sin).astype(x.dtype)


def softmax_attention(q, k, v, scale):
    s = jnp.einsum('bqhd,bkhd->bhqk', q, k).astype(jnp.float32) * scale
    p = jax.nn.softmax(s, axis=-1).astype(v.dtype)
    return jnp.einsum('bhqk,bkhd->bqhd', p, v)


def blocked_attention(q, k, v, scale):
    B, L, H, dk = q.shape
    nblk = L // Q_BLOCK
    qb = q.reshape(B, nblk, Q_BLOCK, H, dk).transpose(1, 0, 2, 3, 4)
    ob = lax.map(lambda qq: softmax_attention(qq, k, v, scale), qb)
    return ob.transpose(1, 0, 2, 3, 4).reshape(B, L, H, v.shape[-1])


def mla_queries(zz, q_norm, w_uq):
    B, L, _ = zz['mla_q'].shape
    q = (rmsnorm(zz['mla_q'], q_norm) @ w_uq).reshape(B, L, MLA_HEADS, MLA_NOPE + MLA_ROPE)
    return q[..., :MLA_NOPE], q[..., MLA_NOPE:]


def mla_keys_values(zz, kv_norm, w_ukv):
    B, L, _ = zz['mla_kv'].shape
    kv = (rmsnorm(zz['mla_kv'], kv_norm) @ w_ukv).reshape(B, L, MLA_HEADS, MLA_NOPE + MLA_V)
    return kv[..., :MLA_NOPE], kv[..., MLA_NOPE:]


def assemble_keys(k_nope, k_rope):
    B, L, H, _ = k_nope.shape
    return jnp.concatenate([k_nope, jnp.broadcast_to(k_rope[:, :, None, :], (B, L, H, MLA_ROPE))], axis=-1)


def mla_branch(z, zc, cos, sin, q_norm, w_uq, kv_norm, w_ukv, with_ctx_out):
    B, L, _ = z['mla_q'].shape
    scale = (MLA_NOPE + MLA_ROPE) ** -0.5
    q_nope, q_rope = mla_queries(z, q_norm, w_uq)
    q = jnp.concatenate([q_nope, apply_rope(q_rope, cos[:, None, :], sin[:, None, :])], axis=-1)
    k_nope, v = mla_keys_values(z, kv_norm, w_ukv)
    k = assemble_keys(k_nope, apply_rope(z['mla_kr'], cos, sin))
    kc_nope, vc = mla_keys_values(zc, kv_norm, w_ukv)
    kc = assemble_keys(kc_nope, zc['mla_kr'])
    k_all = jnp.concatenate([kc, k], axis=1)
    v_all = jnp.concatenate([vc, v], axis=1)
    y = blocked_attention(q, k_all, v_all, scale).reshape(B, L, MLA_WIDTH)
    y = y * jax.nn.silu(z['mla_gate'])
    if with_ctx_out:
        Bc, Lc, _ = zc['mla_q'].shape
        qc = jnp.concatenate(mla_queries(zc, q_norm, w_uq), axis=-1)
        yc = softmax_attention(qc, kc, vc, scale).reshape(Bc, Lc, MLA_WIDTH) * jax.nn.silu(zc['mla_gate'])
    else:
        yc = None
    return y, yc


def multiscale_pool(u):
    B, L, W = u.shape
    uf = u.astype(jnp.float32)
    csum = jnp.concatenate([jnp.zeros((B, 1, W), jnp.float32), jnp.cumsum(uf, axis=1)], axis=1)
    t = jnp.arange(L)
    outs = []
    for g, w in enumerate(POOL_WINDOWS):
        lo = jnp.clip(t - w // 2, 0, L)
        hi = jnp.clip(t + w // 2, 0, L)
        cs = csum[..., g * POOL_GROUP:(g + 1) * POOL_GROUP]
        count = (hi - lo).astype(jnp.float32)[None, :, None]
        outs.append((jnp.take(cs, hi, axis=1) - jnp.take(cs, lo, axis=1)) / count)
    return (jnp.concatenate(outs, axis=-1) - uf).astype(u.dtype)


def pool_branch(zz, pool_w, pool_scale):
    B, L, _ = zz['pool_x'].shape
    pooled = multiscale_pool(zz['pool_x']).reshape(B, L, len(POOL_WINDOWS), POOL_GROUP)
    mixed = jnp.einsum('blgi,gio->blgo', pooled, pool_w).reshape(B, L, POOL_WIDTH)
    return mixed * pool_scale * jax.nn.silu(zz['pool_gate'])


def gla_scan(q, k, v, log_a, s0, with_out):
    B, L, H, _ = q.shape
    n = L // GLA_CHUNK

    def to_chunks(t):
        return t.reshape(B, n, GLA_CHUNK, H, t.shape[-1]).transpose(1, 0, 3, 2, 4)

    mask = jnp.tril(jnp.ones((GLA_CHUNK, GLA_CHUNK), bool))[:, :, None]

    def step(s, inp):
        qq, kk, vv, aa = inp
        b = jnp.cumsum(aa, axis=2)
        b_last = b[:, :, -1:, :]
        s_new = jnp.exp(b_last)[:, :, 0, :, None] * s + jnp.einsum('bhcd,bhce->bhde', kk * jnp.exp(b_last - b), vv)
        if not with_out:
            return s_new, None
        inter = jnp.einsum('bhcd,bhde->bhce', qq * jnp.exp(b), s)
        decay = jnp.exp(jnp.where(mask, b[:, :, :, None, :] - b[:, :, None, :, :], -jnp.inf))
        attn = jnp.einsum('bhid,bhjd,bhijd->bhij', qq, kk, decay)
        intra = jnp.einsum('bhij,bhje->bhie', attn, vv)
        return s_new, inter + intra

    s_fin, out = lax.scan(step, s0, (to_chunks(q), to_chunks(k), to_chunks(v), to_chunks(log_a)))
    if with_out:
        out = out.transpose(1, 0, 3, 2, 4).reshape(B, L, H, v.shape[-1])
    return s_fin, out


def gla_inputs(zz, af_w2, af_b, ab_w2, ab_b):
    B, L, _ = zz['gla_v'].shape
    f32 = jnp.float32
    q = zz['gla_q'].astype(f32).reshape(B, L, GLA_HEADS, GLA_DK) * GLA_DK ** -0.5
    k = zz['gla_k'].astype(f32).reshape(B, L, GLA_HEADS, GLA_DK)
    v = zz['gla_v'].astype(f32).reshape(B, L, GLA_HEADS, GLA_DV)
    log_a_f = (jax.nn.log_sigmoid((zz['gla_af'] @ af_w2 + af_b).astype(f32)) / GLA_TAU).reshape(B, L, GLA_HEADS, GLA_DK)
    log_a_b = (jax.nn.log_sigmoid((zz['gla_ab'] @ ab_w2 + ab_b).astype(f32)) / GLA_TAU).reshape(B, L, GLA_HEADS, GLA_DK)
    return q, k, v, log_a_f, log_a_b


def gla_output(o, zz, g):
    B, L = o.shape[:2]
    o = rmsnorm(o, g).reshape(B, L, GLA_WIDTH).astype(zz['gla_gate'].dtype)
    return o * jax.nn.silu(zz['gla_gate'])


def gla_branch(z, zc, af_w2, af_b, ab_w2, ab_b, gla_norm, with_ctx_out):
    qc, kc, vc, afc, abc = gla_inputs(zc, af_w2, af_b, ab_w2, ab_b)
    s0 = jnp.zeros((qc.shape[0], GLA_HEADS, GLA_DK, GLA_DV), jnp.float32)
    sc_f, oc_f = gla_scan(qc, kc, vc, afc, s0, with_ctx_out)
    sc_b, oc_b = gla_scan(flip(qc), flip(kc), flip(vc), flip(abc), s0, with_ctx_out)
    q, k, v, af, ab = gla_inputs(z, af_w2, af_b, ab_w2, ab_b)
    _, o_f = gla_scan(q, k, v, af, sc_f, True)
    _, o_b = gla_scan(flip(q), flip(k), flip(v), flip(ab), sc_b, True)
    y = gla_output(o_f + flip(o_b), z, gla_norm)
    yc = gla_output(oc_f + flip(oc_b), zc, gla_norm) if with_ctx_out else None
    return y, yc


def merge_branches(zz, y_mla, y_pool, y_gla, w_bm, w_bp, w_bg, w_out):
    gates = jax.nn.sigmoid(zz['merge'].astype(jnp.float32)).astype(y_mla.dtype)
    g_mla, g_pool, g_gla = jnp.split(gates, N_BRANCH, axis=-1)
    merged = g_mla * (y_mla @ w_bm) + g_pool * (y_pool @ w_bp) + g_gla * (y_gla @ w_bg)
    return merged @ w_out


def trunk_layer(x, xc, mod, mod_c, cos, sin, pre_g, post_g, w_in, mla_q_norm, mla_w_uq, mla_kv_norm,
                mla_w_ukv, pool_w, pool_scale, gla_af_w2, gla_af_b, gla_ab_w2, gla_ab_b, gla_norm,
                w_branch_mla, w_branch_pool, w_branch_gla, w_out, with_ctx_out):
    shift, scale, gate = jnp.split(mod[:, None, :], 3, axis=-1)
    shift_c, scale_c, gate_c = jnp.split(mod_c[None, None, :], 3, axis=-1)
    z = split_columns((rmsnorm(x, pre_g) * (1 + scale) + shift) @ w_in)
    zc = split_columns((rmsnorm(xc, pre_g) * (1 + scale_c) + shift_c) @ w_in)
    y_mla, yc_mla = mla_branch(z, zc, cos, sin, mla_q_norm, mla_w_uq, mla_kv_norm, mla_w_ukv, with_ctx_out)
    y_pool = pool_branch(z, pool_w, pool_scale)
    y_gla, yc_gla = gla_branch(z, zc, gla_af_w2, gla_af_b, gla_ab_w2, gla_ab_b, gla_norm, with_ctx_out)
    out = merge_branches(z, y_mla, y_pool, y_gla, w_branch_mla, w_branch_pool, w_branch_gla, w_out)
    x = x + gate * rmsnorm(out, post_g)
    if with_ctx_out:
        yc_pool = pool_branch(zc, pool_w, pool_scale)
        out_c = merge_branches(zc, yc_mla, yc_pool, yc_gla, w_branch_mla, w_branch_pool, w_branch_gla, w_out)
        xc = xc + gate_c * rmsnorm(out_c, post_g)
    return x, xc


def _fwd_setup_inputs(seed: int = 0) -> dict:
    key = jax.random.key(seed)
    ks = jax.random.split(key, 24)
    f32 = jnp.float32

    def nrm(k, shape, s):
        return jax.random.normal(k, shape, f32) * s

    def gain(k, n):
        return 1.0 + 0.1 * jax.random.normal(k, (DEPTH, n), f32)

    return {
        'x': nrm(ks[0], (BATCH, SEQ, D_MODEL), 1.0),
        'c': nrm(ks[1], (BATCH, D_MODEL), 1.0),
        'ctx': nrm(ks[2], (BATCH, CTX_LEN, D_MODEL), 1.0),
        'c_ctx': nrm(ks[3], (D_MODEL,), 1.0),
        'mod_w': nrm(ks[4], (DEPTH, D_MODEL, 3 * D_MODEL), 0.5 * D_MODEL ** -0.5),
        'mod_b': nrm(ks[5], (DEPTH, 3 * D_MODEL), 0.02),
        'pre_norm': gain(ks[6], D_MODEL),
        'post_norm': gain(ks[7], D_MODEL),
        'w_in': nrm(ks[8], (DEPTH, D_MODEL, D_IN), D_MODEL ** -0.5),
        'mla_q_norm': gain(ks[9], MLA_Q_RANK),
        'mla_w_uq': nrm(ks[10], (DEPTH, MLA_Q_RANK, MLA_HEADS * (MLA_NOPE + MLA_ROPE)), MLA_Q_RANK ** -0.5),
        'mla_kv_norm': gain(ks[11], MLA_KV_RANK),
        'mla_w_ukv': nrm(ks[12], (DEPTH, MLA_KV_RANK, MLA_HEADS * (MLA_NOPE + MLA_V)), MLA_KV_RANK ** -0.5),
        'pool_w': nrm(ks[13], (DEPTH, len(POOL_WINDOWS), POOL_GROUP, POOL_GROUP), POOL_GROUP ** -0.5),
        'pool_scale': gain(ks[14], POOL_WIDTH),
        'gla_af_w2': nrm(ks[15], (DEPTH, GLA_GATE_RANK, GLA_KW), GLA_GATE_RANK ** -0.5),
        'gla_af_b': nrm(ks[16], (DEPTH, GLA_KW), 0.1),
        'gla_ab_w2': nrm(ks[17], (DEPTH, GLA_GATE_RANK, GLA_KW), GLA_GATE_RANK ** -0.5),
        'gla_ab_b': nrm(ks[18], (DEPTH, GLA_KW), 0.1),
        'gla_norm': gain(ks[19], GLA_DV),
        'w_branch_mla': nrm(ks[20], (DEPTH, MLA_WIDTH, D_MODEL), MLA_WIDTH ** -0.5),
        'w_branch_pool': nrm(ks[21], (DEPTH, POOL_WIDTH, D_MODEL), POOL_WIDTH ** -0.5),
        'w_branch_gla': nrm(ks[22], (DEPTH, GLA_WIDTH, D_MODEL), GLA_WIDTH ** -0.5),
        'w_out': nrm(ks[23], (DEPTH, D_MODEL, D_MODEL), D_MODEL ** -0.5),
    }


def _fwd_reference(x, c, ctx, c_ctx, mod_w, mod_b, pre_norm, post_norm, w_in, mla_q_norm, mla_w_uq,
              mla_kv_norm, mla_w_ukv, pool_w, pool_scale, gla_af_w2, gla_af_b, gla_ab_w2, gla_ab_b,
              gla_norm, w_branch_mla, w_branch_pool, w_branch_gla, w_out):
    n_tok = x.shape[1]
    rows = n_tok // GRID_W
    row = jnp.repeat(jnp.arange(rows), GRID_W)
    col = jnp.tile(jnp.arange(GRID_W), rows)
    cos, sin = axial_rope_tables(row, col)
    silu_c = jax.nn.silu(c)
    silu_cc = jax.nn.silu(c_ctx)
    xc = ctx
    for l in range(DEPTH):
        mod = silu_c @ mod_w[l] + mod_b[l]
        mod_c = silu_cc @ mod_w[l] + mod_b[l]
        x, xc = trunk_layer(x, xc, mod, mod_c, cos, sin, pre_norm[l], post_norm[l], w_in[l],
                            mla_q_norm[l], mla_w_uq[l], mla_kv_norm[l], mla_w_ukv[l], pool_w[l],
                            pool_scale[l], gla_af_w2[l], gla_af_b[l], gla_ab_w2[l], gla_ab_b[l],
                            gla_norm[l], w_branch_mla[l], w_branch_pool[l], w_branch_gla[l], w_out[l],
                            l < DEPTH - 1)
    return x


import jax as _jax
import jax.numpy as _jnp

TWIN_FORMAT = 'train_step'
FWD_PARAMS = ['x', 'c', 'ctx', 'c_ctx', 'mod_w', 'mod_b', 'pre_norm', 'post_norm', 'w_in', 'mla_q_norm', 'mla_w_uq', 'mla_kv_norm', 'mla_w_ukv', 'pool_w', 'pool_scale', 'gla_af_w2', 'gla_af_b', 'gla_ab_w2', 'gla_ab_b', 'gla_norm', 'w_branch_mla', 'w_branch_pool', 'w_branch_gla', 'w_out']
TWIN_WEIGHTS = ['c_ctx', 'mod_w', 'mod_b', 'pre_norm', 'post_norm', 'w_in', 'mla_q_norm', 'mla_w_uq', 'mla_kv_norm', 'mla_w_ukv', 'pool_w', 'pool_scale', 'gla_af_w2', 'gla_af_b', 'gla_ab_w2', 'gla_ab_b', 'gla_norm', 'w_branch_mla', 'w_branch_pool', 'w_branch_gla', 'w_out']
TWIN_DIFF_INPUT = 'x'
TWIN_INPUTS = ['x', 'c', 'ctx', 'c_ctx', 'mod_w', 'mod_b', 'pre_norm', 'post_norm', 'w_in', 'mla_q_norm', 'mla_w_uq', 'mla_kv_norm', 'mla_w_ukv', 'pool_w', 'pool_scale', 'gla_af_w2', 'gla_af_b', 'gla_ab_w2', 'gla_ab_b', 'gla_norm', 'w_branch_mla', 'w_branch_pool', 'w_branch_gla', 'w_out', 'loss_target', 'm_c_ctx', 'm_mod_w', 'm_mod_b', 'm_pre_norm', 'm_post_norm', 'm_w_in', 'm_mla_q_norm', 'm_mla_w_uq', 'm_mla_kv_norm', 'm_mla_w_ukv', 'm_pool_w', 'm_pool_scale', 'm_gla_af_w2', 'm_gla_af_b', 'm_gla_ab_w2', 'm_gla_ab_b', 'm_gla_norm', 'm_w_branch_mla', 'm_w_branch_pool', 'm_w_branch_gla', 'm_w_out', 'v_c_ctx', 'v_mod_w', 'v_mod_b', 'v_pre_norm', 'v_post_norm', 'v_w_in', 'v_mla_q_norm', 'v_mla_w_uq', 'v_mla_kv_norm', 'v_mla_w_ukv', 'v_pool_w', 'v_pool_scale', 'v_gla_af_w2', 'v_gla_af_b', 'v_gla_ab_w2', 'v_gla_ab_b', 'v_gla_norm', 'v_w_branch_mla', 'v_w_branch_pool', 'v_w_branch_gla', 'v_w_out']
TWIN_OUTPUTS = ['loss', 'grad_x', 'grad_c_ctx', 'grad_mod_w', 'grad_mod_b', 'grad_pre_norm', 'grad_post_norm', 'grad_w_in', 'grad_mla_q_norm', 'grad_mla_w_uq', 'grad_mla_kv_norm', 'grad_mla_w_ukv', 'grad_pool_w', 'grad_pool_scale', 'grad_gla_af_w2', 'grad_gla_af_b', 'grad_gla_ab_w2', 'grad_gla_ab_b', 'grad_gla_norm', 'grad_w_branch_mla', 'grad_w_branch_pool', 'grad_w_branch_gla', 'grad_w_out', 'delta_c_ctx', 'delta_mod_w', 'delta_mod_b', 'delta_pre_norm', 'delta_post_norm', 'delta_w_in', 'delta_mla_q_norm', 'delta_mla_w_uq', 'delta_mla_kv_norm', 'delta_mla_w_ukv', 'delta_pool_w', 'delta_pool_scale', 'delta_gla_af_w2', 'delta_gla_af_b', 'delta_gla_ab_w2', 'delta_gla_ab_b', 'delta_gla_norm', 'delta_w_branch_mla', 'delta_w_branch_pool', 'delta_w_branch_gla', 'delta_w_out', 'new_m_c_ctx', 'new_m_mod_w', 'new_m_mod_b', 'new_m_pre_norm', 'new_m_post_norm', 'new_m_w_in', 'new_m_mla_q_norm', 'new_m_mla_w_uq', 'new_m_mla_kv_norm', 'new_m_mla_w_ukv', 'new_m_pool_w', 'new_m_pool_scale', 'new_m_gla_af_w2', 'new_m_gla_af_b', 'new_m_gla_ab_w2', 'new_m_gla_ab_b', 'new_m_gla_norm', 'new_m_w_branch_mla', 'new_m_w_branch_pool', 'new_m_w_branch_gla', 'new_m_w_out', 'new_v_c_ctx', 'new_v_mod_w', 'new_v_mod_b', 'new_v_pre_norm', 'new_v_post_norm', 'new_v_w_in', 'new_v_mla_q_norm', 'new_v_mla_w_uq', 'new_v_mla_kv_norm', 'new_v_mla_w_ukv', 'new_v_pool_w', 'new_v_pool_scale', 'new_v_gla_af_w2', 'new_v_gla_af_b', 'new_v_gla_ab_w2', 'new_v_gla_ab_b', 'new_v_gla_norm', 'new_v_w_branch_mla', 'new_v_w_branch_pool', 'new_v_w_branch_gla', 'new_v_w_out']
TWIN_LEAF_KINDS = {'loss': 'loss', 'grad_x': 'grad_x', 'grad_c_ctx': 'grad_w', 'grad_mod_w': 'grad_w', 'grad_mod_b': 'grad_w', 'grad_pre_norm': 'grad_w', 'grad_post_norm': 'grad_w', 'grad_w_in': 'grad_w', 'grad_mla_q_norm': 'grad_w', 'grad_mla_w_uq': 'grad_w', 'grad_mla_kv_norm': 'grad_w', 'grad_mla_w_ukv': 'grad_w', 'grad_pool_w': 'grad_w', 'grad_pool_scale': 'grad_w', 'grad_gla_af_w2': 'grad_w', 'grad_gla_af_b': 'grad_w', 'grad_gla_ab_w2': 'grad_w', 'grad_gla_ab_b': 'grad_w', 'grad_gla_norm': 'grad_w', 'grad_w_branch_mla': 'grad_w', 'grad_w_branch_pool': 'grad_w', 'grad_w_branch_gla': 'grad_w', 'grad_w_out': 'grad_w', 'delta_c_ctx': 'delta_w', 'delta_mod_w': 'delta_w', 'delta_mod_b': 'delta_w', 'delta_pre_norm': 'delta_w', 'delta_post_norm': 'delta_w', 'delta_w_in': 'delta_w', 'delta_mla_q_norm': 'delta_w', 'delta_mla_w_uq': 'delta_w', 'delta_mla_kv_norm': 'delta_w', 'delta_mla_w_ukv': 'delta_w', 'delta_pool_w': 'delta_w', 'delta_pool_scale': 'delta_w', 'delta_gla_af_w2': 'delta_w', 'delta_gla_af_b': 'delta_w', 'delta_gla_ab_w2': 'delta_w', 'delta_gla_ab_b': 'delta_w', 'delta_gla_norm': 'delta_w', 'delta_w_branch_mla': 'delta_w', 'delta_w_branch_pool': 'delta_w', 'delta_w_branch_gla': 'delta_w', 'delta_w_out': 'delta_w', 'new_m_c_ctx': 'new_m', 'new_m_mod_w': 'new_m', 'new_m_mod_b': 'new_m', 'new_m_pre_norm': 'new_m', 'new_m_post_norm': 'new_m', 'new_m_w_in': 'new_m', 'new_m_mla_q_norm': 'new_m', 'new_m_mla_w_uq': 'new_m', 'new_m_mla_kv_norm': 'new_m', 'new_m_mla_w_ukv': 'new_m', 'new_m_pool_w': 'new_m', 'new_m_pool_scale': 'new_m', 'new_m_gla_af_w2': 'new_m', 'new_m_gla_af_b': 'new_m', 'new_m_gla_ab_w2': 'new_m', 'new_m_gla_ab_b': 'new_m', 'new_m_gla_norm': 'new_m', 'new_m_w_branch_mla': 'new_m', 'new_m_w_branch_pool': 'new_m', 'new_m_w_branch_gla': 'new_m', 'new_m_w_out': 'new_m', 'new_v_c_ctx': 'new_v', 'new_v_mod_w': 'new_v', 'new_v_mod_b': 'new_v', 'new_v_pre_norm': 'new_v', 'new_v_post_norm': 'new_v', 'new_v_w_in': 'new_v', 'new_v_mla_q_norm': 'new_v', 'new_v_mla_w_uq': 'new_v', 'new_v_mla_kv_norm': 'new_v', 'new_v_mla_w_ukv': 'new_v', 'new_v_pool_w': 'new_v', 'new_v_pool_scale': 'new_v', 'new_v_gla_af_w2': 'new_v', 'new_v_gla_af_b': 'new_v', 'new_v_gla_ab_w2': 'new_v', 'new_v_gla_ab_b': 'new_v', 'new_v_gla_norm': 'new_v', 'new_v_w_branch_mla': 'new_v', 'new_v_w_branch_pool': 'new_v', 'new_v_w_branch_gla': 'new_v', 'new_v_w_out': 'new_v'}


def _forward(args):
    return _fwd_reference(*[args[k] for k in FWD_PARAMS])


def _output_shape():
    out = _jax.eval_shape(lambda: _forward(_fwd_setup_inputs(0)))
    return out.shape, out.dtype

N_MICROBATCH = 1
ADAM_LR = 0.001
ADAM_B1 = 0.9
ADAM_B2 = 0.999
ADAM_EPS = 1e-08
ADAM_WD = 0.01
ADAM_STEP = 10
PER_EXAMPLE_BATCH_AXIS = {'x': 0, 'c': 0, 'ctx': 0, 'loss_target': 0}
SHARED_INPUTS = []
_WEIGHT_DTYPES = {'c_ctx': _jnp.float32, 'mod_w': _jnp.float32, 'mod_b': _jnp.float32, 'pre_norm': _jnp.float32, 'post_norm': _jnp.float32, 'w_in': _jnp.float32, 'mla_q_norm': _jnp.float32, 'mla_w_uq': _jnp.float32, 'mla_kv_norm': _jnp.float32, 'mla_w_ukv': _jnp.float32, 'pool_w': _jnp.float32, 'pool_scale': _jnp.float32, 'gla_af_w2': _jnp.float32, 'gla_af_b': _jnp.float32, 'gla_ab_w2': _jnp.float32, 'gla_ab_b': _jnp.float32, 'gla_norm': _jnp.float32, 'w_branch_mla': _jnp.float32, 'w_branch_pool': _jnp.float32, 'w_branch_gla': _jnp.float32, 'w_out': _jnp.float32}
MOMENT_SCALE = {'c_ctx': 2.995527e-02, 'mod_w': 3.385503e+00, 'mod_b': 6.435000e+00, 'pre_norm': 2.367898e-01, 'post_norm': 7.403055e+00, 'w_in': 9.680428e-02, 'mla_q_norm': 1.980483e-02, 'mla_w_uq': 1.170827e-02, 'mla_kv_norm': 1.379888e-01, 'mla_w_ukv': 3.795332e-02, 'pool_w': 1.257152e-01, 'pool_scale': 1.549734e-01, 'gla_af_w2': 2.645655e-02, 'gla_af_b': 6.840765e-02, 'gla_ab_w2': 2.434966e-02, 'gla_ab_b': 6.109541e-02, 'gla_norm': 2.956160e-01, 'w_branch_mla': 4.452991e-02, 'w_branch_pool': 9.672350e-02, 'w_branch_gla': 1.043899e-01, 'w_out': 1.613428e-01}


def _to_microbatches(a, axis):
    t = _jnp.moveaxis(a, axis, 0)
    t = t.reshape((N_MICROBATCH, t.shape[0] // N_MICROBATCH) + t.shape[1:])
    return _jnp.moveaxis(t, 1, axis + 1)


def setup_inputs(seed: int = 0) -> dict:
    inp = _fwd_setup_inputs(seed)
    key = _jax.random.fold_in(_jax.random.key(seed), 7919)
    shape, _ = _output_shape()
    out = dict(inp)
    out["loss_target"] = _jax.random.normal(_jax.random.fold_in(key, 0), shape, _jnp.float32)
    for i, name in enumerate(TWIN_WEIGHTS):
        w = inp[name].astype(_jnp.float32)
        if MOMENT_SCALE is None:
            s = _jnp.sqrt(_jnp.mean(_jnp.square(w)) + 1e-30)
        else:
            s = MOMENT_SCALE[name]
        km, kv = _jax.random.split(_jax.random.fold_in(key, i + 1))
        out[name] = w
        out["m_" + name] = s * _jax.random.normal(km, w.shape, _jnp.float32)
        out["v_" + name] = (s * s) * _jax.random.uniform(kv, w.shape, _jnp.float32, 0.5, 1.5)
    if N_MICROBATCH > 1:
        for name, axis in PER_EXAMPLE_BATCH_AXIS.items():
            out[name] = _to_microbatches(out[name], axis)
    return {'x': out['x'], 'c': out['c'], 'ctx': out['ctx'], 'c_ctx': out['c_ctx'], 'mod_w': out['mod_w'], 'mod_b': out['mod_b'], 'pre_norm': out['pre_norm'], 'post_norm': out['post_norm'], 'w_in': out['w_in'], 'mla_q_norm': out['mla_q_norm'], 'mla_w_uq': out['mla_w_uq'], 'mla_kv_norm': out['mla_kv_norm'], 'mla_w_ukv': out['mla_w_ukv'], 'pool_w': out['pool_w'], 'pool_scale': out['pool_scale'], 'gla_af_w2': out['gla_af_w2'], 'gla_af_b': out['gla_af_b'], 'gla_ab_w2': out['gla_ab_w2'], 'gla_ab_b': out['gla_ab_b'], 'gla_norm': out['gla_norm'], 'w_branch_mla': out['w_branch_mla'], 'w_branch_pool': out['w_branch_pool'], 'w_branch_gla': out['w_branch_gla'], 'w_out': out['w_out'], 'loss_target': out['loss_target'], 'm_c_ctx': out['m_c_ctx'], 'm_mod_w': out['m_mod_w'], 'm_mod_b': out['m_mod_b'], 'm_pre_norm': out['m_pre_norm'], 'm_post_norm': out['m_post_norm'], 'm_w_in': out['m_w_in'], 'm_mla_q_norm': out['m_mla_q_norm'], 'm_mla_w_uq': out['m_mla_w_uq'], 'm_mla_kv_norm': out['m_mla_kv_norm'], 'm_mla_w_ukv': out['m_mla_w_ukv'], 'm_pool_w': out['m_pool_w'], 'm_pool_scale': out['m_pool_scale'], 'm_gla_af_w2': out['m_gla_af_w2'], 'm_gla_af_b': out['m_gla_af_b'], 'm_gla_ab_w2': out['m_gla_ab_w2'], 'm_gla_ab_b': out['m_gla_ab_b'], 'm_gla_norm': out['m_gla_norm'], 'm_w_branch_mla': out['m_w_branch_mla'], 'm_w_branch_pool': out['m_w_branch_pool'], 'm_w_branch_gla': out['m_w_branch_gla'], 'm_w_out': out['m_w_out'], 'v_c_ctx': out['v_c_ctx'], 'v_mod_w': out['v_mod_w'], 'v_mod_b': out['v_mod_b'], 'v_pre_norm': out['v_pre_norm'], 'v_post_norm': out['v_post_norm'], 'v_w_in': out['v_w_in'], 'v_mla_q_norm': out['v_mla_q_norm'], 'v_mla_w_uq': out['v_mla_w_uq'], 'v_mla_kv_norm': out['v_mla_kv_norm'], 'v_mla_w_ukv': out['v_mla_w_ukv'], 'v_pool_w': out['v_pool_w'], 'v_pool_scale': out['v_pool_scale'], 'v_gla_af_w2': out['v_gla_af_w2'], 'v_gla_af_b': out['v_gla_af_b'], 'v_gla_ab_w2': out['v_gla_ab_w2'], 'v_gla_ab_b': out['v_gla_ab_b'], 'v_gla_norm': out['v_gla_norm'], 'v_w_branch_mla': out['v_w_branch_mla'], 'v_w_branch_pool': out['v_w_branch_pool'], 'v_w_branch_gla': out['v_w_branch_gla'], 'v_w_out': out['v_w_out']}


def _loss(weights, diff, rest, loss_target):
    with _jax.named_scope("forward"):
        args = {**rest, TWIN_DIFF_INPUT: diff, **{k: w.astype(_WEIGHT_DTYPES[k]) for k, w in weights.items()}}
        y = _forward(args)
    with _jax.named_scope("loss_head"):
        err = _jnp.square(y.astype(_jnp.float32) - loss_target)
        return 0.5 * _jnp.sum(_jnp.mean(err, axis=-1)) if err.ndim else 0.5 * err


def _adamw(w, g, m, v):
    m = ADAM_B1 * m + (1.0 - ADAM_B1) * g
    v = ADAM_B2 * v + (1.0 - ADAM_B2) * _jnp.square(g)
    m_hat = m / (1.0 - ADAM_B1 ** ADAM_STEP)
    v_hat = v / (1.0 - ADAM_B2 ** ADAM_STEP)
    delta = -ADAM_LR * (m_hat / (_jnp.sqrt(v_hat) + ADAM_EPS) + ADAM_WD * w)
    return delta, m, v


def reference(x, c, ctx, c_ctx, mod_w, mod_b, pre_norm, post_norm, w_in, mla_q_norm, mla_w_uq, mla_kv_norm, mla_w_ukv, pool_w, pool_scale, gla_af_w2, gla_af_b, gla_ab_w2, gla_ab_b, gla_norm, w_branch_mla, w_branch_pool, w_branch_gla, w_out, loss_target, m_c_ctx, m_mod_w, m_mod_b, m_pre_norm, m_post_norm, m_w_in, m_mla_q_norm, m_mla_w_uq, m_mla_kv_norm, m_mla_w_ukv, m_pool_w, m_pool_scale, m_gla_af_w2, m_gla_af_b, m_gla_ab_w2, m_gla_ab_b, m_gla_norm, m_w_branch_mla, m_w_branch_pool, m_w_branch_gla, m_w_out, v_c_ctx, v_mod_w, v_mod_b, v_pre_norm, v_post_norm, v_w_in, v_mla_q_norm, v_mla_w_uq, v_mla_kv_norm, v_mla_w_ukv, v_pool_w, v_pool_scale, v_gla_af_w2, v_gla_af_b, v_gla_ab_w2, v_gla_ab_b, v_gla_norm, v_w_branch_mla, v_w_branch_pool, v_w_branch_gla, v_w_out):
    given = dict(x=x, c=c, ctx=ctx, c_ctx=c_ctx, mod_w=mod_w, mod_b=mod_b, pre_norm=pre_norm, post_norm=post_norm, w_in=w_in, mla_q_norm=mla_q_norm, mla_w_uq=mla_w_uq, mla_kv_norm=mla_kv_norm, mla_w_ukv=mla_w_ukv, pool_w=pool_w, pool_scale=pool_scale, gla_af_w2=gla_af_w2, gla_af_b=gla_af_b, gla_ab_w2=gla_ab_w2, gla_ab_b=gla_ab_b, gla_norm=gla_norm, w_branch_mla=w_branch_mla, w_branch_pool=w_branch_pool, w_branch_gla=w_branch_gla, w_out=w_out, loss_target=loss_target, m_c_ctx=m_c_ctx, m_mod_w=m_mod_w, m_mod_b=m_mod_b, m_pre_norm=m_pre_norm, m_post_norm=m_post_norm, m_w_in=m_w_in, m_mla_q_norm=m_mla_q_norm, m_mla_w_uq=m_mla_w_uq, m_mla_kv_norm=m_mla_kv_norm, m_mla_w_ukv=m_mla_w_ukv, m_pool_w=m_pool_w, m_pool_scale=m_pool_scale, m_gla_af_w2=m_gla_af_w2, m_gla_af_b=m_gla_af_b, m_gla_ab_w2=m_gla_ab_w2, m_gla_ab_b=m_gla_ab_b, m_gla_norm=m_gla_norm, m_w_branch_mla=m_w_branch_mla, m_w_branch_pool=m_w_branch_pool, m_w_branch_gla=m_w_branch_gla, m_w_out=m_w_out, v_c_ctx=v_c_ctx, v_mod_w=v_mod_w, v_mod_b=v_mod_b, v_pre_norm=v_pre_norm, v_post_norm=v_post_norm, v_w_in=v_w_in, v_mla_q_norm=v_mla_q_norm, v_mla_w_uq=v_mla_w_uq, v_mla_kv_norm=v_mla_kv_norm, v_mla_w_ukv=v_mla_w_ukv, v_pool_w=v_pool_w, v_pool_scale=v_pool_scale, v_gla_af_w2=v_gla_af_w2, v_gla_af_b=v_gla_af_b, v_gla_ab_w2=v_gla_ab_w2, v_gla_ab_b=v_gla_ab_b, v_gla_norm=v_gla_norm, v_w_branch_mla=v_w_branch_mla, v_w_branch_pool=v_w_branch_pool, v_w_branch_gla=v_w_branch_gla, v_w_out=v_w_out)
    weights = {n: given[n] for n in TWIN_WEIGHTS}
    shared = {n: given[n] for n in SHARED_INPUTS}
    per_example = {n: given[n] for n in ['x', 'c', 'ctx']}
    grad_fn = _jax.value_and_grad(_loss, argnums=(0, 1))

    def one_microbatch(ex, loss_target):
        ex = dict(ex)
        diff = ex.pop(TWIN_DIFF_INPUT)
        return grad_fn(weights, diff, {**shared, **ex}, loss_target)

    if N_MICROBATCH == 1:
        loss, (grad_w, grad_x) = one_microbatch(per_example, given["loss_target"])
    else:
        def body(carry, xs):
            loss_sum, grad_sum = carry
            l_k, (gw_k, gx_k) = one_microbatch(xs[0], xs[1])
            with _jax.named_scope("update"):
                return (loss_sum + l_k, _jax.tree.map(_jnp.add, grad_sum, gw_k)), gx_k

        init = (_jnp.zeros((), _jnp.float32), _jax.tree.map(_jnp.zeros_like, weights))
        (loss, grad_w), grad_x = _jax.lax.scan(body, init, (per_example, given["loss_target"]))
    with _jax.named_scope("update"):
        delta_w, new_m, new_v = {}, {}, {}
        for n in TWIN_WEIGHTS:
            delta_w[n], new_m[n], new_v[n] = _adamw(weights[n], grad_w[n], given["m_" + n], given["v_" + n])
    return (loss, grad_x, *[grad_w[n] for n in TWIN_WEIGHTS], *[delta_w[n] for n in TWIN_WEIGHTS],
            *[new_m[n] for n in TWIN_WEIGHTS], *[new_v[n] for n in TWIN_WEIGHTS])
```

```python
import functools

import numpy as np
import jax
import jax.numpy as jnp
from jax import lax
from jax.experimental import pallas as pl
from jax.experimental.pallas import tpu as pltpu

F32 = jnp.float32
BF16 = jnp.bfloat16
HIGHEST = lax.Precision.HIGHEST
MESH = pl.DeviceIdType.MESH

D = 1024
DEPTH = 2
EPS = 1e-6
GRID_W = 64
MLA_H, MLA_NOPE, MLA_ROPE, MLA_V = 8, 64, 32, 64
MLA_QK = MLA_NOPE + MLA_ROPE
ROPE_BASE = 10000.0
POOL_WINDOWS = (2, 4, 8, 16)
GLA_H, GLA_DK, GLA_DV, GLA_RANK, GLA_TAU, GLA_C = 4, 64, 128, 16, 16.0, 64
EXP_CLAMP = 80.0
ADAM_LR, ADAM_B1, ADAM_B2, ADAM_EPS, ADAM_WD, ADAM_STEP = 0.001, 0.9, 0.999, 1e-08, 0.01, 10

TM = 256
LANES = 128
N_CHIPS = 4

IN_NAMES = ('mla_q', 'mla_kv', 'mla_kr', 'mla_gate', 'pool_x', 'pool_gate',
            'gla_q', 'gla_k', 'gla_v', 'gla_af', 'gla_ab', 'gla_gate', 'merge')
IN_SIZES = (256, 128, 32, 512, 512, 512, 256, 256, 512, 16, 16, 512, 3 * D)
NAT_OFF = dict(zip(IN_NAMES, [int(o) for o in np.cumsum((0,) + IN_SIZES[:-1])]))
NAT_SIZE = dict(zip(IN_NAMES, IN_SIZES))
PAD_ORDER = ('merge', 'mla_gate', 'mla_q', 'mla_kv', 'mla_kr', 'pool_x', 'pool_gate',
             'gla_v', 'gla_gate', 'gla_q', 'gla_k', 'gla_af', 'gla_ab')
SLAB = {n: max(NAT_SIZE[n], LANES) for n in IN_NAMES}
PAD_OFF = dict(zip(PAD_ORDER, [int(o) for o in np.cumsum([0] + [SLAB[n] for n in PAD_ORDER[:-1]])]))
D_PAD = sum(SLAB.values())


def _blk(name):
    return PAD_OFF[name] // SLAB[name]


SHARDED = (('mod_w', 2), ('w_in', 2), ('mla_w_uq', 2), ('mla_w_ukv', 2), ('gla_af_w2', 2), ('gla_ab_w2', 2),
           ('w_branch_mla', 2), ('w_branch_pool', 2), ('w_branch_gla', 2), ('w_out', 1))
REPLICATED = ('c_ctx', 'mod_b', 'pre_norm', 'post_norm', 'mla_q_norm', 'mla_kv_norm', 'pool_w', 'pool_scale',
              'gla_af_b', 'gla_ab_b', 'gla_norm')
WEIGHTS = ('c_ctx', 'mod_w', 'mod_b', 'pre_norm', 'post_norm', 'w_in', 'mla_q_norm', 'mla_w_uq', 'mla_kv_norm',
           'mla_w_ukv', 'pool_w', 'pool_scale', 'gla_af_w2', 'gla_af_b', 'gla_ab_w2', 'gla_ab_b', 'gla_norm',
           'w_branch_mla', 'w_branch_pool', 'w_branch_gla', 'w_out')


def _cp(vmem_mb=None, sem=None):
    kw = {}
    if vmem_mb is not None:
        kw['vmem_limit_bytes'] = vmem_mb * 1024 * 1024
    if sem is not None:
        kw['dimension_semantics'] = sem
    return pltpu.CompilerParams(**kw)


def _bdot(a, b):
    return jnp.dot(a.astype(BF16), b.astype(BF16), preferred_element_type=F32)


def _bdot_nt(a, b):
    return lax.dot_general(a.astype(BF16), b.astype(BF16), (((1,), (1,)), ((), ())), preferred_element_type=F32)


def _bdot_tn(a, b):
    return lax.dot_general(a.astype(BF16), b.astype(BF16), (((0,), (0,)), ((), ())), preferred_element_type=F32)


def _xdot(a, b):
    return jnp.dot(a, b, precision=HIGHEST, preferred_element_type=F32)


def _xdot_tn(a, b):
    return lax.dot_general(a, b, (((0,), (0,)), ((), ())), precision=HIGHEST, preferred_element_type=F32)


def _xdot_nt(a, b):
    return lax.dot_general(a, b, (((1,), (1,)), ((), ())), precision=HIGHEST, preferred_element_type=F32)


def _sigmoid(x):
    return jax.nn.sigmoid(x)


def _silu(x):
    return x * _sigmoid(x)


def _dsilu(x):
    s = _sigmoid(x)
    return s * (1.0 + x * (1.0 - s))


def _rstd(x):
    return lax.rsqrt(jnp.mean(x * x, axis=-1, keepdims=True) + EPS)


def _norm_bwd(xhat, r, dy):
    return r * (dy - xhat * jnp.mean(xhat * dy, axis=-1, keepdims=True))


def _colsum(a):
    return jnp.sum(a, axis=0, keepdims=True)


def _modrow(i, tpe):
    return jnp.where(i % tpe == 0, 4, i // tpe)


def _rot(x):
    n = x.shape[-1]
    lane = lax.broadcasted_iota(jnp.int32, x.shape, x.ndim - 1)
    return jnp.where(lane % 16 < 8, -pltpu.roll(x, n - 8, x.ndim - 1), pltpu.roll(x, 8, x.ndim - 1))


def _mod_fwd(cv, mod_w, mod_b):
    def body(cv_ref, w_ref, b_ref, o_ref):
        s = _silu(cv_ref[...])
        for l in range(DEPTH):
            o_ref[l] = _bdot(s, w_ref[l]) + b_ref[l]

    return pl.pallas_call(body, name="mod_fwd", out_shape=jax.ShapeDtypeStruct((DEPTH, 8, 3 * D), F32),
                          compiler_params=_cp(40))(cv, mod_w, mod_b)


def _mod_bwd(cv, sel, st_a, st_b, w_t, dcv_in, name):
    def body(cv_ref, sel_ref, sa_ref, sb_ref, wt_ref, dcin_ref, dw_ref, db_ref, dcv_ref, dcc_ref, dpre_ref, dpost_ref):
        cvv = cv_ref[...]
        s = _silu(cvv)
        sel_v = sel_ref[...]
        dmod = jnp.concatenate([_xdot(sel_v, sa_ref[0]), _xdot(sel_v, sa_ref[1]), _xdot(sel_v, sb_ref[0])], axis=1)
        dw_ref[...] = _bdot_tn(s, dmod)
        db_ref[...] = _colsum(dmod)
        dcv = dcin_ref[...] + _bdot(dmod, wt_ref[...])
        dcv_ref[...] = dcv
        dcc_ref[...] = dcv * _dsilu(cvv)
        dpre_ref[...] = _colsum(sa_ref[2])
        dpost_ref[...] = _colsum(sb_ref[1])

    shapes = (jax.ShapeDtypeStruct((D, 3 * D), F32), jax.ShapeDtypeStruct((1, 3 * D), F32),
              jax.ShapeDtypeStruct((8, D), F32), jax.ShapeDtypeStruct((8, D), F32),
              jax.ShapeDtypeStruct((1, D), F32), jax.ShapeDtypeStruct((1, D), F32))
    return pl.pallas_call(body, name=name, out_shape=shapes, compiler_params=_cp(48))(cv, sel, st_a, st_b, w_t, dcv_in)


def _pre_fwd(x, modl, pre_g, w, tpe, name):
    n = x.shape[0]
    nt = n // TM
    ncb = 3
    tn = D_PAD // ncb

    def body(x_ref, m_ref, g_ref, w_ref, z_ref, h_ref):
        xv = x_ref[...]
        m = m_ref[0]
        h = xv * _rstd(xv) * g_ref[...] * (1.0 + m[:, D:2 * D]) + m[:, 0:D]
        hb = h.astype(BF16)

        @pl.when(pl.program_id(1) == 0)
        def _():
            h_ref[...] = hb

        z_ref[...] = jnp.dot(hb, w_ref[...], preferred_element_type=F32)

    return pl.pallas_call(
        body, name=name, grid=(nt, ncb),
        in_specs=[pl.BlockSpec((TM, D), lambda i, j: (i, 0)),
                  pl.BlockSpec((1, 1, 3 * D), lambda i, j: (_modrow(i, tpe), 0, 0)),
                  pl.BlockSpec((1, D), lambda i, j: (0, 0)),
                  pl.BlockSpec((D, tn), lambda i, j: (0, j))],
        out_specs=[pl.BlockSpec((TM, tn), lambda i, j: (i, j)),
                   pl.BlockSpec((TM, D), lambda i, j: (i, 0))],
        out_shape=(jax.ShapeDtypeStruct((n, D_PAD), F32), jax.ShapeDtypeStruct((n, D), BF16)),
        compiler_params=_cp(48, ("arbitrary", "arbitrary")),
    )(x, modl.reshape(8, 1, 3 * D), pre_g, w)


def _pre_bwd(dz, w_t, x, dxres, modl, pre_g, tpe, name):
    n = x.shape[0]
    nt = n // TM

    def body(dz_ref, wt_ref, x_ref, dr_ref, m_ref, g_ref, dx_ref, st_ref):
        dh = jnp.dot(dz_ref[...], wt_ref[...], preferred_element_type=F32)
        xv = x_ref[...]
        r = _rstd(xv)
        xn = xv * r
        m = m_ref[0]
        sc1 = 1.0 + m[:, D:2 * D]
        g = g_ref[...]
        st_ref[0, 0:1, :] = _colsum(dh)
        st_ref[0, 1:2, :] = _colsum(dh * xn * g)
        st_ref[0, 2:3, :] = _colsum(dh * xn * sc1)
        st_ref[0, 3:8, :] = jnp.zeros((5, D), F32)
        dx_ref[...] = dr_ref[...] + _norm_bwd(xn, r, dh * g * sc1)

    return pl.pallas_call(
        body, name=name, grid=(nt,),
        in_specs=[pl.BlockSpec((TM, D_PAD), lambda i: (i, 0)),
                  pl.BlockSpec((D_PAD, D), lambda i: (0, 0)),
                  pl.BlockSpec((TM, D), lambda i: (i, 0)),
                  pl.BlockSpec((TM, D), lambda i: (i, 0)),
                  pl.BlockSpec((1, 1, 3 * D), lambda i: (_modrow(i, tpe), 0, 0)),
                  pl.BlockSpec((1, D), lambda i: (0, 0))],
        out_specs=[pl.BlockSpec((TM, D), lambda i: (i, 0)),
                   pl.BlockSpec((1, 8, D), lambda i: (i, 0, 0))],
        out_shape=(jax.ShapeDtypeStruct((n, D), F32), jax.ShapeDtypeStruct((nt, 8, D), F32)),
        compiler_params=_cp(56, ("arbitrary",)),
    )(dz, w_t, x, dxres, modl.reshape(8, 1, 3 * D), pre_g)


def _matmul_tn(a, b, tn, tk, name):
    n, k1 = a.shape
    k2 = b.shape[1]

    def body(a_ref, b_ref, o_ref):
        @pl.when(pl.program_id(1) == 0)
        def _():
            o_ref[...] = jnp.zeros(o_ref.shape, F32)

        o_ref[...] += lax.dot_general(a_ref[...], b_ref[...], (((0,), (0,)), ((), ())), preferred_element_type=F32)

    return pl.pallas_call(
        body, name=name, grid=(k2 // tn, n // tk),
        in_specs=[pl.BlockSpec((tk, k1), lambda j, k: (k, 0)), pl.BlockSpec((tk, tn), lambda j, k: (k, j))],
        out_specs=pl.BlockSpec((k1, tn), lambda j, k: (0, j)),
        out_shape=jax.ShapeDtypeStruct((k1, k2), F32),
        compiler_params=_cp(48, ("arbitrary", "arbitrary")),
    )(a, b)


def _mla_pre(z, q_norm, kv_norm, w_uq, w_ukv, cq, sq, ck, sk, tpe, name):
    n = z.shape[0]
    nt = n // TM
    scale = MLA_QK ** -0.5

    def body(zq_ref, zkv_ref, zkr_ref, gq_ref, gkv_ref, wq_ref, wkv_ref, cq_ref, sq_ref, ck_ref, sk_ref,
             q_ref, kv_ref, kr_ref):
        zq = zq_ref[...]
        qn = zq * _rstd(zq) * gq_ref[...]
        qraw = _bdot(qn, wq_ref[...])
        q_ref[...] = ((qraw * cq_ref[...] + _rot(qraw) * sq_ref[...]) * scale).astype(BF16)
        zkv = zkv_ref[...]
        kvn = zkv * _rstd(zkv) * gkv_ref[...]
        kv_ref[...] = _bdot(kvn, wkv_ref[...]).astype(BF16)
        zkr = zkr_ref[...]
        kr_ref[...] = (zkr * ck_ref[...] + _rot(zkr) * sk_ref[...]).astype(BF16)

    hq, hkv = MLA_H * MLA_QK, MLA_H * (MLA_NOPE + MLA_V)
    const = lambda i: (0, 0)
    tab = lambda i: (i % tpe, 0)
    return pl.pallas_call(
        body, name=name, grid=(nt,),
        in_specs=[pl.BlockSpec((TM, 256), lambda i: (i, _blk('mla_q'))),
                  pl.BlockSpec((TM, 128), lambda i: (i, _blk('mla_kv'))),
                  pl.BlockSpec((TM, 128), lambda i: (i, _blk('mla_kr'))),
                  pl.BlockSpec((1, 256), const), pl.BlockSpec((1, 128), const),
                  pl.BlockSpec((256, hq), const), pl.BlockSpec((128, hkv), const),
                  pl.BlockSpec((TM, hq), tab), pl.BlockSpec((TM, hq), tab),
                  pl.BlockSpec((TM, 128), tab), pl.BlockSpec((TM, 128), tab)],
        out_specs=[pl.BlockSpec((TM, hq), lambda i: (i, 0)), pl.BlockSpec((TM, hkv), lambda i: (i, 0)),
                   pl.BlockSpec((TM, 128), lambda i: (i, 0))],
        out_shape=(jax.ShapeDtypeStruct((n, hq), BF16), jax.ShapeDtypeStruct((n, hkv), BF16),
                   jax.ShapeDtypeStruct((n, 128), BF16)),
        compiler_params=_cp(32, ("arbitrary",)),
    )(z, z, z, q_norm, kv_norm, w_uq, w_ukv, cq, sq, ck, sk)


def _mla_pre_bwd(z, dq, dkv, dkr8, q_norm, kv_norm, w_uq_t, w_ukv_t, cq, sq, ck, sk, tpe, name):
    n = z.shape[0]
    nt = n // TM
    scale = MLA_QK ** -0.5
    hq, hkv = MLA_H * MLA_QK, MLA_H * (MLA_NOPE + MLA_V)

    def body(zq_ref, zkv_ref, dq_ref, dkv_ref, dkr_ref, gq_ref, gkv_ref, wqt_ref, wkvt_ref, cq_ref, sq_ref,
             ck_ref, sk_ref, dzq_ref, dzkv_ref, dzkr_ref, dwq_ref, dwkv_ref, dgq_ref, dgkv_ref):
        @pl.when(pl.program_id(0) == 0)
        def _():
            dwq_ref[...] = jnp.zeros(dwq_ref.shape, F32)
            dwkv_ref[...] = jnp.zeros(dwkv_ref.shape, F32)
            dgq_ref[...] = jnp.zeros(dgq_ref.shape, F32)
            dgkv_ref[...] = jnp.zeros(dgkv_ref.shape, F32)

        zq = zq_ref[...]
        rq = _rstd(zq)
        qhat = zq * rq
        gq = gq_ref[...]
        dqs = dq_ref[...] * scale
        dqraw = dqs * cq_ref[...] - _rot(dqs * sq_ref[...])
        dwq_ref[...] += _bdot_tn(qhat * gq, dqraw)
        dqn = _bdot(dqraw, wqt_ref[...])
        dgq_ref[...] += _colsum(dqn * qhat)
        dzq_ref[...] = _norm_bwd(qhat, rq, dqn * gq).astype(BF16)

        zkv = zkv_ref[...]
        rkv = _rstd(zkv)
        khat = zkv * rkv
        gkv = gkv_ref[...]
        dkvv = dkv_ref[...]
        dwkv_ref[...] += _bdot_tn(khat * gkv, dkvv)
        dkvn = _bdot(dkvv, wkvt_ref[...])
        dgkv_ref[...] += _colsum(dkvn * khat)
        dzkv_ref[...] = _norm_bwd(khat, rkv, dkvn * gkv).astype(BF16)

        d8 = dkr_ref[...]
        t = d8[:, 0:128] + d8[:, 128:256]
        t = t + pltpu.roll(t, 64, 1)
        t = t + pltpu.roll(t, 32, 1)
        lane = lax.broadcasted_iota(jnp.int32, t.shape, 1)
        dkr = jnp.where(lane < MLA_ROPE, t, 0.0)
        dzkr_ref[...] = (dkr * ck_ref[...] - _rot(dkr * sk_ref[...])).astype(BF16)

    const = lambda i: (0, 0)
    tab = lambda i: (i % tpe, 0)
    row = lambda i: (i, 0)
    return pl.pallas_call(
        body, name=name, grid=(nt,),
        in_specs=[pl.BlockSpec((TM, 256), lambda i: (i, _blk('mla_q'))),
                  pl.BlockSpec((TM, 128), lambda i: (i, _blk('mla_kv'))),
                  pl.BlockSpec((TM, hq), row), pl.BlockSpec((TM, hkv), row), pl.BlockSpec((TM, 256), row),
                  pl.BlockSpec((1, 256), const), pl.BlockSpec((1, 128), const),
                  pl.BlockSpec((hq, 256), const), pl.BlockSpec((hkv, 128), const),
                  pl.BlockSpec((TM, hq), tab), pl.BlockSpec((TM, hq), tab),
                  pl.BlockSpec((TM, 128), tab), pl.BlockSpec((TM, 128), tab)],
        out_specs=[pl.BlockSpec((TM, 256), row), pl.BlockSpec((TM, 128), row), pl.BlockSpec((TM, 128), row),
                   pl.BlockSpec((256, hq), const), pl.BlockSpec((128, hkv), const),
                   pl.BlockSpec((1, 256), const), pl.BlockSpec((1, 128), const)],
        out_shape=(jax.ShapeDtypeStruct((n, 256), BF16), jax.ShapeDtypeStruct((n, 128), BF16),
                   jax.ShapeDtypeStruct((n, 128), BF16), jax.ShapeDtypeStruct((256, hq), F32),
                   jax.ShapeDtypeStruct((128, hkv), F32), jax.ShapeDtypeStruct((1, 256), F32),
                   jax.ShapeDtypeStruct((1, 128), F32)),
        compiler_params=_cp(32, ("arbitrary",)),
    )(z, z, dq, dkv, dkr8, q_norm, kv_norm, w_uq_t, w_ukv_t, cq, sq, ck, sk)


def _attn_scores(q_ref, k_ref, j, lc, t):
    s = lax.dot_general(q_ref[0], k_ref[0], (((1,), (1,)), ((), ())), preferred_element_type=F32)
    col = lax.broadcasted_iota(jnp.int32, s.shape, 1)
    limit = jnp.where(j == 0, lc, t)
    return jnp.where(col < limit, s, -1e30)


def _attn_fwd(q, k, v, lc, name):
    bh, t, dq = q.shape
    dv = v.shape[2]

    def body(q_ref, k_ref, v_ref, o_ref, lse_ref):
        s = _attn_scores(q_ref, k_ref, pl.program_id(1), lc, t)
        m = jnp.max(s, axis=-1, keepdims=True)
        p = jnp.exp(s - m)
        l = jnp.sum(p, axis=-1, keepdims=True)
        o_ref[0] = jnp.dot(p.astype(BF16), v_ref[0], preferred_element_type=F32) / l
        lse_ref[0] = m + jnp.log(l)

    return pl.pallas_call(
        body, name=name, grid=(bh, t // TM),
        in_specs=[pl.BlockSpec((1, TM, dq), lambda b, j: (b, j, 0)),
                  pl.BlockSpec((1, t, dq), lambda b, j: (b, 0, 0)),
                  pl.BlockSpec((1, t, dv), lambda b, j: (b, 0, 0))],
        out_specs=[pl.BlockSpec((1, TM, dv), lambda b, j: (b, j, 0)),
                   pl.BlockSpec((1, TM, 1), lambda b, j: (b, j, 0))],
        out_shape=(jax.ShapeDtypeStruct((bh, t, dv), F32), jax.ShapeDtypeStruct((bh, t, 1), F32)),
        compiler_params=_cp(48, ("arbitrary", "arbitrary")),
    )(q, k, v)


def _attn_bwd(q, k, v, o, lse, do, lc, name):
    bh, t, dq = q.shape
    dv = v.shape[2]

    def body(q_ref, k_ref, v_ref, o_ref, lse_ref, do_ref, dq_ref, dk_ref, dv_ref):
        @pl.when(pl.program_id(1) == 0)
        def _():
            dk_ref[...] = jnp.zeros(dk_ref.shape, F32)
            dv_ref[...] = jnp.zeros(dv_ref.shape, F32)

        s = _attn_scores(q_ref, k_ref, pl.program_id(1), lc, t)
        p = jnp.exp(s - lse_ref[0])
        dov = do_ref[0]
        delta = jnp.sum(dov * o_ref[0], axis=-1, keepdims=True)
        dob = dov.astype(BF16)
        dp = lax.dot_general(dob, v_ref[0], (((1,), (1,)), ((), ())), preferred_element_type=F32)
        ds = (p * (dp - delta)).astype(BF16)
        dq_ref[0] = jnp.dot(ds, k_ref[0], preferred_element_type=F32)
        dk_ref[0] += lax.dot_general(ds, q_ref[0], (((0,), (0,)), ((), ())), preferred_element_type=F32)
        dv_ref[0] += lax.dot_general(p.astype(BF16), dob, (((0,), (0,)), ((), ())), preferred_element_type=F32)

    tile = lambda b, j: (b, j, 0)
    full = lambda b, j: (b, 0, 0)
    return pl.pallas_call(
        body, name=name, grid=(bh, t // TM),
        in_specs=[pl.BlockSpec((1, TM, dq), tile), pl.BlockSpec((1, t, dq), full), pl.BlockSpec((1, t, dv), full),
                  pl.BlockSpec((1, TM, dv), tile), pl.BlockSpec((1, TM, 1), tile), pl.BlockSpec((1, TM, dv), tile)],
        out_specs=[pl.BlockSpec((1, TM, dq), tile), pl.BlockSpec((1, t, dq), full), pl.BlockSpec((1, t, dv), full)],
        out_shape=(jax.ShapeDtypeStruct((bh, t, dq), F32), jax.ShapeDtypeStruct((bh, t, dq), F32),
                   jax.ShapeDtypeStruct((bh, t, dv), F32)),
        compiler_params=_cp(56, ("arbitrary", "arbitrary")),
    )(q, k, v, o, lse, do)


def _pool_window(ug, pos, seglen, w, transpose):
    t = ug.shape[0]
    cnt = (jnp.minimum(pos + w // 2, seglen) - jnp.maximum(pos - w // 2, 0)).astype(F32)
    if transpose:
        ug = ug / cnt
    acc = jnp.zeros_like(ug)
    for j in range(-(w // 2), w // 2):
        jj = -j if transpose else j
        src = pos + jj
        valid = (src >= 0) & (src < seglen)
        acc = acc + jnp.where(valid, pltpu.roll(ug, (-jj) % t, 0), 0.0)
    return acc if transpose else acc / cnt


def _by_group(g, fn):
    for k, w in enumerate(POOL_WINDOWS):
        pl.when(g == k)(functools.partial(fn, w))


def _pool_specs(t):
    px, pg = PAD_OFF['pool_x'] // LANES, PAD_OFF['pool_gate'] // LANES
    return [pl.BlockSpec((t, LANES), lambda g, b: (b, px + g)),
            pl.BlockSpec((t, LANES), lambda g, b: (b, pg + g)),
            pl.BlockSpec((1, LANES, LANES), lambda g, b: (g, 0, 0)),
            pl.BlockSpec((1, LANES), lambda g, b: (0, g)),
            pl.BlockSpec((t, 1), lambda g, b: (0, 0)), pl.BlockSpec((t, 1), lambda g, b: (0, 0))]


def _pool_fwd(z, pool_w, pool_scale, pos, seglen, nb, name):
    n = z.shape[0]
    t = n // nb

    def body(u_ref, zg_ref, pw_ref, ps_ref, pos_ref, sl_ref, y_ref):
        def run(w):
            u = u_ref[...]
            pooled = _pool_window(u, pos_ref[...], sl_ref[...], w, False) - u
            y_ref[...] = (_bdot(pooled, pw_ref[0]) * ps_ref[...] * _silu(zg_ref[...])).astype(BF16)

        _by_group(pl.program_id(0), run)

    return pl.pallas_call(
        body, name=name, grid=(4, nb), in_specs=_pool_specs(t),
        out_specs=pl.BlockSpec((t, LANES), lambda g, b: (b, g)),
        out_shape=jax.ShapeDtypeStruct((n, 512), BF16),
        compiler_params=_cp(48, ("arbitrary", "arbitrary")),
    )(z, z, pool_w, pool_scale, pos, seglen)


def _pool_bwd(z, dy, pool_w, pool_w_t, pool_scale, pos, seglen, nb, name):
    n = z.shape[0]
    t = n // nb

    def body(u_ref, zg_ref, pw_ref, ps_ref, pos_ref, sl_ref, dy_ref, pwt_ref, du_ref, dg_ref, dpw_ref, dps_ref):
        @pl.when(pl.program_id(1) == 0)
        def _():
            dpw_ref[...] = jnp.zeros(dpw_ref.shape, F32)
            dps_ref[...] = jnp.zeros(dps_ref.shape, F32)

        def run(w):
            u = u_ref[...]
            pos_v, sl_v = pos_ref[...], sl_ref[...]
            pooled = _pool_window(u, pos_v, sl_v, w, False) - u
            mixed = _bdot(pooled, pw_ref[0])
            zg = zg_ref[...]
            sg = _silu(zg)
            ps = ps_ref[...]
            dyv = dy_ref[...]
            dps_ref[...] += _colsum(dyv * mixed * sg)
            dg_ref[...] = (dyv * mixed * ps * _dsilu(zg)).astype(BF16)
            dmixed = dyv * ps * sg
            dpw_ref[0] += _bdot_tn(pooled, dmixed)
            dpooled = _bdot(dmixed, pwt_ref[0])
            du_ref[...] = (_pool_window(dpooled, pos_v, sl_v, w, True) - dpooled).astype(BF16)

        _by_group(pl.program_id(0), run)

    blk = pl.BlockSpec((t, LANES), lambda g, b: (b, g))
    return pl.pallas_call(
        body, name=name, grid=(4, nb),
        in_specs=_pool_specs(t) + [blk, pl.BlockSpec((1, LANES, LANES), lambda g, b: (g, 0, 0))],
        out_specs=[blk, blk, pl.BlockSpec((1, LANES, LANES), lambda g, b: (g, 0, 0)),
                   pl.BlockSpec((1, LANES), lambda g, b: (0, g))],
        out_shape=(jax.ShapeDtypeStruct((n, 512), BF16), jax.ShapeDtypeStruct((n, 512), BF16),
                   jax.ShapeDtypeStruct((4, 128, 128), F32), jax.ShapeDtypeStruct((1, 512), F32)),
        compiler_params=_cp(48, ("arbitrary", "arbitrary")),
    )(z, z, pool_w, pool_scale, pos, seglen, dy, pool_w_t)


def _gla_chunk(q_ref, k_ref, a_ref, w2_ref, b2_ref, h):
    c = GLA_C
    x = _bdot(a_ref[0], w2_ref[0, h]) + b2_ref[0, h]
    la = (jnp.minimum(x, 0.0) - jnp.log(1.0 + jnp.exp(-jnp.abs(x)))) * (1.0 / GLA_TAU)
    row = lax.broadcasted_iota(jnp.int32, (c, c), 0)
    col = lax.broadcasted_iota(jnp.int32, (c, c), 1)
    low = (col <= row)
    b = _xdot(low.astype(F32), la)
    tok = lax.broadcasted_iota(jnp.int32, la.shape, 0)
    bref = _colsum(jnp.where(tok < c // 2, la, 0.0))
    blast = _colsum(la)
    eq = jnp.exp(jnp.minimum(b - bref, EXP_CLAMP))
    ek = jnp.exp(jnp.minimum(bref - b, EXP_CLAMP))
    qs = q_ref[0, h] * (GLA_DK ** -0.5)
    kk = k_ref[0, h]
    bl_col = _xdot_tn(la, jnp.ones((c, GLA_DV), F32))
    return dict(x=x, b=b, low=low, eq=eq, ek=ek, qs=qs, kk=kk, qd=qs * eq, kd=kk * ek, qe=qs * jnp.exp(b),
                etail=jnp.exp(blast - b), kl=kk * jnp.exp(blast - b), ebl=jnp.exp(bl_col))


def _gla_specs(nb, reverse, nc):
    cidx = (lambda j: nc - 1 - j) if reverse else (lambda j: j)
    return [pl.BlockSpec((1, GLA_H, GLA_C, GLA_DK), lambda s, j: (s, 0, cidx(j), 0)),
            pl.BlockSpec((1, GLA_H, GLA_C, GLA_DK), lambda s, j: (s, 0, cidx(j), 0)),
            pl.BlockSpec((1, GLA_H, GLA_C, GLA_DV), lambda s, j: (s, 0, cidx(j), 0)),
            pl.BlockSpec((1, GLA_C, 128), lambda s, j: (s, cidx(j), 0)),
            pl.BlockSpec((1, GLA_H, 128, GLA_DK), lambda s, j: (s // nb, 0, 0, 0)),
            pl.BlockSpec((1, GLA_H, 1, GLA_DK), lambda s, j: (s // nb, 0, 0, 0))]


def _gla_fwd(q8, k8, v8, a8, w2, b2, nb, name):
    ns, _, t, _ = q8.shape
    nc = t // GLA_C

    def body(q_ref, k_ref, v_ref, a_ref, w2_ref, b2_ref, o_ref, st_ref, s_sc):
        @pl.when(pl.program_id(1) == 0)
        def _():
            s_sc[...] = jnp.zeros(s_sc.shape, F32)

        for h in range(GLA_H):
            ch = _gla_chunk(q_ref, k_ref, a_ref, w2_ref, b2_ref, h)
            vv = v_ref[0, h]
            s_prev = s_sc[h]
            st_ref[0, 0, h] = s_prev
            att = jnp.where(ch['low'], _bdot_nt(ch['qd'], ch['kd']), 0.0)
            o_ref[0, h] = _bdot(att, vv) + _bdot(ch['qe'], s_prev)
            s_sc[h] = ch['ebl'] * s_prev + _xdot_tn(ch['kl'], vv)

    return pl.pallas_call(
        body, name=name, grid=(ns, nc),
        in_specs=_gla_specs(nb, False, nc),
        out_specs=[pl.BlockSpec((1, GLA_H, GLA_C, GLA_DV), lambda s, j: (s, 0, j, 0)),
                   pl.BlockSpec((1, 1, GLA_H, GLA_DK, GLA_DV), lambda s, j: (s, j, 0, 0, 0))],
        out_shape=(jax.ShapeDtypeStruct((ns, GLA_H, t, GLA_DV), F32),
                   jax.ShapeDtypeStruct((ns, nc, GLA_H, GLA_DK, GLA_DV), F32)),
        scratch_shapes=[pltpu.VMEM((GLA_H, GLA_DK, GLA_DV), F32)],
        compiler_params=_cp(32, ("arbitrary", "arbitrary")),
    )(q8, k8, v8, a8, w2, b2)


def _gla_bwd(q8, k8, v8, a8, w2, w2_t, b2, st, do8, nb, name):
    ns, _, t, _ = q8.shape
    nc = t // GLA_C

    def body(q_ref, k_ref, v_ref, a_ref, w2_ref, b2_ref, w2t_ref, st_ref, do_ref,
             dq_ref, dk_ref, dv_ref, da_ref, dw2_ref, db2_ref, ds_sc, sfx_sc):
        s_id, j = pl.program_id(0), pl.program_id(1)

        @pl.when(j == 0)
        def _():
            ds_sc[...] = jnp.zeros(ds_sc.shape, F32)
            sfx_sc[...] = jnp.zeros(sfx_sc.shape, F32)

        @pl.when((j == 0) & (s_id % nb == 0))
        def _():
            dw2_ref[...] = jnp.zeros(dw2_ref.shape, F32)
            db2_ref[...] = jnp.zeros(db2_ref.shape, F32)

        c = GLA_C
        row = lax.broadcasted_iota(jnp.int32, (c, c), 0)
        col = lax.broadcasted_iota(jnp.int32, (c, c), 1)
        upp = (col >= row).astype(F32)
        av = a_ref[0]
        da = jnp.zeros((c, 128), F32)
        for h in range(GLA_H):
            ch = _gla_chunk(q_ref, k_ref, a_ref, w2_ref, b2_ref, h)
            vv = v_ref[0, h]
            dov = do_ref[0, h]
            s_prev = st_ref[0, 0, h]
            ds_new = ds_sc[h]
            att = jnp.where(ch['low'], _bdot_nt(ch['qd'], ch['kd']), 0.0)
            datt = jnp.where(ch['low'], _xdot_nt(dov, vv), 0.0)
            dv_ref[0, h] = _bdot_tn(att, dov) + _bdot(ch['kl'], ds_new)
            dq = _xdot(datt, ch['kd']) * ch['eq'] + _xdot_nt(dov, s_prev) * jnp.exp(ch['b'])
            dk = _xdot_tn(datt, ch['qd']) * ch['ek'] + _xdot_nt(vv, ds_new) * ch['etail']
            dq_ref[0, h] = dq * (GLA_DK ** -0.5)
            dk_ref[0, h] = dk
            db = ch['qs'] * dq - ch['kk'] * dk
            dla = _xdot(upp, db) + sfx_sc[h]
            sfx_sc[h] = sfx_sc[h] + _colsum(db)
            dx = dla * (1.0 / GLA_TAU) * _sigmoid(-ch['x'])
            da = da + _bdot(dx, w2t_ref[0, h])
            dw2_ref[0, h] += _bdot_tn(av, dx)
            db2_ref[0, h] += _colsum(dx)
            ds_sc[h] = ch['ebl'] * ds_new + _xdot_tn(ch['qe'], dov)
        da_ref[0] = da

    rev = lambda s, j: (s, 0, nc - 1 - j, 0)
    return pl.pallas_call(
        body, name=name, grid=(ns, nc),
        in_specs=_gla_specs(nb, True, nc) + [
            pl.BlockSpec((1, GLA_H, GLA_DK, 128), lambda s, j: (s // nb, 0, 0, 0)),
            pl.BlockSpec((1, 1, GLA_H, GLA_DK, GLA_DV), lambda s, j: (s, nc - 1 - j, 0, 0, 0)),
            pl.BlockSpec((1, GLA_H, GLA_C, GLA_DV), rev)],
        out_specs=[pl.BlockSpec((1, GLA_H, GLA_C, GLA_DK), rev), pl.BlockSpec((1, GLA_H, GLA_C, GLA_DK), rev),
                   pl.BlockSpec((1, GLA_H, GLA_C, GLA_DV), rev),
                   pl.BlockSpec((1, GLA_C, 128), lambda s, j: (s, nc - 1 - j, 0)),
                   pl.BlockSpec((1, GLA_H, 128, GLA_DK), lambda s, j: (s // nb, 0, 0, 0)),
                   pl.BlockSpec((1, GLA_H, 1, GLA_DK), lambda s, j: (s // nb, 0, 0, 0))],
        out_shape=(jax.ShapeDtypeStruct((ns, GLA_H, t, GLA_DK), F32), jax.ShapeDtypeStruct((ns, GLA_H, t, GLA_DK), F32),
                   jax.ShapeDtypeStruct((ns, GLA_H, t, GLA_DV), F32), jax.ShapeDtypeStruct((ns, t, 128), F32),
                   jax.ShapeDtypeStruct((2, GLA_H, 128, GLA_DK), F32), jax.ShapeDtypeStruct((2, GLA_H, 1, GLA_DK), F32)),
        scratch_shapes=[pltpu.VMEM((GLA_H, GLA_DK, GLA_DV), F32), pltpu.VMEM((GLA_H, 1, GLA_DK), F32)],
        compiler_params=_cp(32, ("arbitrary", "arbitrary")),
    )(q8, k8, v8, a8, w2, b2, w2_t, st, do8)


def _add_cast(a, b, name):
    n, w = a.shape

    def body(a_ref, b_ref, o_ref):
        o_ref[...] = (a_ref[...] + b_ref[...]).astype(BF16)

    return pl.pallas_call(
        body, name=name, grid=(n // TM,),
        in_specs=[pl.BlockSpec((TM, w), lambda i: (i, 0)), pl.BlockSpec((TM, w), lambda i: (i, 0))],
        out_specs=pl.BlockSpec((TM, w), lambda i: (i, 0)),
        out_shape=jax.ShapeDtypeStruct((n, w), BF16),
        compiler_params=_cp(32, ("arbitrary",)),
    )(a, b)


def _gla_out_norm(og):
    hats, rs = [], []
    for h in range(GLA_H):
        seg = og[:, h * GLA_DV:(h + 1) * GLA_DV]
        r = _rstd(seg)
        hats.append(seg * r)
        rs.append(r)
    return hats, rs


def _merge_branches(zm_ref, zgm_ref, zgg_ref, om_ref, yp_ref, ogf_ref, ogb_ref, gn_ref, wbm_ref, wbp_ref, wbg_ref):
    zgm, zgg = zgm_ref[...], zgg_ref[...]
    om = om_ref[...]
    y_mla = om * _silu(zgm)
    hats, rs = _gla_out_norm(ogf_ref[...] + ogb_ref[...])
    gn = gn_ref[...]
    sgg = _silu(zgg)
    y_gla = jnp.concatenate([hats[h] * gn for h in range(GLA_H)], axis=1) * sgg
    ys = (y_mla, yp_ref[...], y_gla)
    ps = (_bdot(y_mla, wbm_ref[...]), jnp.dot(yp_ref[...], wbp_ref[...], preferred_element_type=F32),
          _bdot(y_gla, wbg_ref[...]))
    zm = zm_ref[...]
    gs = tuple(_sigmoid(zm[:, a * D:(a + 1) * D]) for a in range(3))
    merged = gs[0] * ps[0] + gs[1] * ps[1] + gs[2] * ps[2]
    return dict(zgm=zgm, zgg=zgg, om=om, hats=hats, rs=rs, gn=gn, sgg=sgg, ys=ys, ps=ps, gs=gs, merged=merged)


def _merge_in_specs(tpe):
    row = lambda i: (i, 0)
    const = lambda i: (0, 0)
    return [pl.BlockSpec((TM, 3 * D), lambda i: (i, _blk('merge'))),
            pl.BlockSpec((TM, 512), lambda i: (i, _blk('mla_gate'))),
            pl.BlockSpec((TM, 512), lambda i: (i, _blk('gla_gate'))),
            pl.BlockSpec((TM, 512), row), pl.BlockSpec((TM, 512), row), pl.BlockSpec((TM, 512), row),
            pl.BlockSpec((TM, 512), row), pl.BlockSpec((1, 128), const),
            pl.BlockSpec((512, D), const), pl.BlockSpec((512, D), const), pl.BlockSpec((512, D), const),
            pl.BlockSpec((1, 1, 3 * D), lambda i: (_modrow(i, tpe), 0, 0)), pl.BlockSpec((1, D), const)]


def _merge_fwd(x, z, o_mla, y_pool, ogf, ogb, gla_n, wbm, wbp, wbg, wout, modl, post_g, tpe, name):
    n = x.shape[0]

    def body(zm_ref, zgm_ref, zgg_ref, om_ref, yp_ref, ogf_ref, ogb_ref, gn_ref, wbm_ref, wbp_ref, wbg_ref,
             m_ref, pg_ref, x_ref, wo_ref, xn_ref, out_ref):
        mb = _merge_branches(zm_ref, zgm_ref, zgg_ref, om_ref, yp_ref, ogf_ref, ogb_ref, gn_ref,
                             wbm_ref, wbp_ref, wbg_ref)
        out = _bdot(mb['merged'], wo_ref[...])
        gate = m_ref[0][:, 2 * D:3 * D]
        xn_ref[...] = x_ref[...] + gate * (out * _rstd(out) * pg_ref[...])
        out_ref[...] = out

    row = lambda i: (i, 0)
    return pl.pallas_call(
        body, name=name, grid=(n // TM,),
        in_specs=_merge_in_specs(tpe) + [pl.BlockSpec((TM, D), row), pl.BlockSpec((D, D), lambda i: (0, 0))],
        out_specs=[pl.BlockSpec((TM, D), row), pl.BlockSpec((TM, D), row)],
        out_shape=(jax.ShapeDtypeStruct((n, D), F32), jax.ShapeDtypeStruct((n, D), F32)),
        compiler_params=_cp(48, ("arbitrary",)),
    )(z, z, z, o_mla, y_pool, ogf, ogb, gla_n, wbm, wbp, wbg, modl.reshape(8, 1, 3 * D), post_g, x, wout)


def _merge_bwd(dxn, out, z, o_mla, y_pool, ogf, ogb, gla_n, wbm, wbp, wbg, wbm_t, wbp_t, wbg_t, wout_t,
               modl, post_g, tpe, name):
    n = out.shape[0]
    nt = n // TM

    def body(zm_ref, zgm_ref, zgg_ref, om_ref, yp_ref, ogf_ref, ogb_ref, gn_ref, wbm_ref, wbp_ref, wbg_ref,
             m_ref, pg_ref, dxn_ref, out_ref, wbmt_ref, wbpt_ref, wbgt_ref, wot_ref,
             dzm_ref, dom_ref, dzgm_ref, dyp_ref, dog_ref, dzgg_ref, st_ref,
             dwbm_ref, dwbp_ref, dwbg_ref, dwo_ref, dgn_ref):
        @pl.when(pl.program_id(0) == 0)
        def _():
            for r in (dwbm_ref, dwbp_ref, dwbg_ref, dwo_ref, dgn_ref):
                r[...] = jnp.zeros(r.shape, F32)

        mb = _merge_branches(zm_ref, zgm_ref, zgg_ref, om_ref, yp_ref, ogf_ref, ogb_ref, gn_ref,
                             wbm_ref, wbp_ref, wbg_ref)
        out = out_ref[...]
        r2 = _rstd(out)
        on = out * r2
        pg = pg_ref[...]
        gate = m_ref[0][:, 2 * D:3 * D]
        dxn_v = dxn_ref[...]
        st_ref[0, 0:1, :] = _colsum(dxn_v * on * pg)
        st_ref[0, 1:2, :] = _colsum(dxn_v * gate * on)
        st_ref[0, 2:8, :] = jnp.zeros((6, D), F32)
        dout = _norm_bwd(on, r2, dxn_v * gate * pg)
        dwo_ref[...] += _bdot_tn(mb['merged'], dout)
        dmerged = _bdot(dout, wot_ref[...])
        dys = []
        for a, (dw_ref, wt_ref) in enumerate(((dwbm_ref, wbmt_ref), (dwbp_ref, wbpt_ref), (dwbg_ref, wbgt_ref))):
            g = mb['gs'][a]
            dzm_ref[:, a * D:(a + 1) * D] = (dmerged * mb['ps'][a] * g * (1.0 - g)).astype(BF16)
            dp = dmerged * g
            dw_ref[...] += _bdot_tn(mb['ys'][a], dp)
            dys.append(_bdot(dp, wt_ref[...]))
        dom_ref[...] = dys[0] * _silu(mb['zgm'])
        dzgm_ref[...] = (dys[0] * mb['om'] * _dsilu(mb['zgm'])).astype(BF16)
        dyp_ref[...] = dys[1]
        gn = mb['gn']
        dgn = jnp.zeros((1, GLA_DV), F32)
        dzgg, dog = [], []
        for h in range(GLA_H):
            sl = slice(h * GLA_DV, (h + 1) * GLA_DV)
            dyg = dys[2][:, sl]
            hat = mb['hats'][h]
            dzgg.append(dyg * hat * gn * _dsilu(mb['zgg'][:, sl]))
            dn = dyg * mb['sgg'][:, sl]
            dgn = dgn + _colsum(dn * hat)
            dog.append(_norm_bwd(hat, mb['rs'][h], dn * gn))
        dgn_ref[...] += dgn
        dzgg_ref[...] = jnp.concatenate(dzgg, axis=1).astype(BF16)
        dog_ref[...] = jnp.concatenate(dog, axis=1)

    row = lambda i: (i, 0)
    const = lambda i: (0, 0)
    wspec = pl.BlockSpec((512, D), const)
    wtspec = pl.BlockSpec((D, 512), const)
    return pl.pallas_call(
        body, name=name, grid=(nt,),
        in_specs=_merge_in_specs(tpe) + [pl.BlockSpec((TM, D), row), pl.BlockSpec((TM, D), row),
                                         wtspec, wtspec, wtspec,
                                         pl.BlockSpec((D, D), const)],
        out_specs=[pl.BlockSpec((TM, 3 * D), row), pl.BlockSpec((TM, 512), row), pl.BlockSpec((TM, 512), row),
                   pl.BlockSpec((TM, 512), row), pl.BlockSpec((TM, 512), row), pl.BlockSpec((TM, 512), row),
                   pl.BlockSpec((1, 8, D), lambda i: (i, 0, 0)),
                   wspec, wspec, wspec, pl.BlockSpec((D, D), const), pl.BlockSpec((1, 128), const)],
        out_shape=(jax.ShapeDtypeStruct((n, 3 * D), BF16), jax.ShapeDtypeStruct((n, 512), F32),
                   jax.ShapeDtypeStruct((n, 512), BF16), jax.ShapeDtypeStruct((n, 512), F32),
                   jax.ShapeDtypeStruct((n, 512), F32), jax.ShapeDtypeStruct((n, 512), BF16),
                   jax.ShapeDtypeStruct((nt, 8, D), F32),
                   jax.ShapeDtypeStruct((512, D), F32), jax.ShapeDtypeStruct((512, D), F32),
                   jax.ShapeDtypeStruct((512, D), F32), jax.ShapeDtypeStruct((D, D), F32),
                   jax.ShapeDtypeStruct((1, 128), F32)),
        compiler_params=_cp(56, ("arbitrary",)),
    )(z, z, z, o_mla, y_pool, ogf, ogb, gla_n, wbm, wbp, wbg, modl.reshape(8, 1, 3 * D), post_g,
      dxn, out, wbm_t, wbp_t, wbg_t, wout_t)


def _loss_grad(xf, tgt, nb, tpe):
    n = xf.shape[0]

    def body(x_ref, t_ref, dx_ref, l_ref):
        j = pl.program_id(1)
        d = x_ref[...] - t_ref[...]
        latent = j > 0
        dx_ref[...] = jnp.where(latent, d * (1.0 / D), 0.0)
        l_ref[...] = jnp.full(l_ref.shape, jnp.where(latent, 0.5 / D * jnp.sum(d * d), 0.0), F32)

    return pl.pallas_call(
        body, name="loss_grad", grid=(nb, tpe),
        in_specs=[pl.BlockSpec((TM, D), lambda b, j: (b * tpe + j, 0)),
                  pl.BlockSpec((TM, D), lambda b, j: (b * (tpe - 1) + jnp.maximum(j - 1, 0), 0))],
        out_specs=[pl.BlockSpec((TM, D), lambda b, j: (b * tpe + j, 0)),
                   pl.BlockSpec((1, 8, 128), lambda b, j: (b * tpe + j, 0, 0))],
        out_shape=(jax.ShapeDtypeStruct((n, D), F32), jax.ShapeDtypeStruct((n // TM, 8, 128), F32)),
        compiler_params=_cp(32, ("arbitrary", "arbitrary")),
    )(xf, tgt)


def _to_padded(w_nat):
    parts = []
    for nme in PAD_ORDER:
        p = w_nat[..., NAT_OFF[nme]:NAT_OFF[nme] + NAT_SIZE[nme]]
        if SLAB[nme] > NAT_SIZE[nme]:
            p = jnp.pad(p, [(0, 0)] * (p.ndim - 1) + [(0, SLAB[nme] - NAT_SIZE[nme])])
        parts.append(p)
    return jnp.concatenate(parts, axis=-1)


def _from_padded(w_pad):
    return jnp.concatenate([w_pad[..., PAD_OFF[nme]:PAD_OFF[nme] + NAT_SIZE[nme]] for nme in IN_NAMES], axis=-1)


def _segflip(a, lc, axis):
    ctx = lax.slice_in_dim(a, 0, lc, axis=axis)
    lat = lax.slice_in_dim(a, lc, a.shape[axis], axis=axis)
    return jnp.concatenate([jnp.flip(ctx, axis), jnp.flip(lat, axis)], axis=axis)


def _rope_tables(lc, l):
    half = MLA_ROPE // 2
    inv = ROPE_BASE ** (-jnp.arange(0, half, 2, dtype=F32) / half)
    tok = jnp.arange(l)
    ang_r = (tok // GRID_W).astype(F32)[:, None] * inv
    ang_c = (tok % GRID_W).astype(F32)[:, None] * inv
    ang = jnp.concatenate([ang_r, ang_r, ang_c, ang_c], axis=-1)
    cos = jnp.concatenate([jnp.ones((lc, MLA_ROPE), F32), jnp.cos(ang)], axis=0)
    sin = jnp.concatenate([jnp.zeros((lc, MLA_ROPE), F32), jnp.sin(ang)], axis=0)
    t = lc + l
    cq = jnp.tile(jnp.concatenate([jnp.ones((t, MLA_NOPE), F32), cos], axis=1), (1, MLA_H))
    sq = jnp.tile(jnp.concatenate([jnp.zeros((t, MLA_NOPE), F32), sin], axis=1), (1, MLA_H))
    ck = jnp.concatenate([cos, jnp.ones((t, 128 - MLA_ROPE), F32)], axis=1)
    sk = jnp.concatenate([sin, jnp.zeros((t, 128 - MLA_ROPE), F32)], axis=1)
    return cq, sq, ck, sk


def _heads(a, nb, t, h, d):
    return a.reshape(nb, t, h, d).transpose(0, 2, 1, 3).reshape(nb * h, t, d)


def _unheads(a, nb, t, h, d):
    return a.reshape(nb, h, t, d).transpose(0, 2, 1, 3).reshape(nb * t, h * d)


def _gla_w2(w2):
    return jnp.pad(w2, ((0, 128 - GLA_RANK), (0, 0))).reshape(128, GLA_H, GLA_DK).transpose(1, 0, 2)


def _local_step(x, c, ctx, tgt, wf):
    nb, l, _ = x.shape
    lc = ctx.shape[1]
    assert lc == TM and l % TM == 0
    t = lc + l
    tpe = t // TM
    n = nb * t
    nt = n // TM
    bf = lambda a: a.astype(BF16)

    xs = jnp.concatenate([ctx, x], axis=1).reshape(n, D)
    assert nb <= 4
    cv = jnp.concatenate([c, jnp.zeros((4 - nb, D), F32), wf['c_ctx'][None, :], jnp.zeros((3, D), F32)], axis=0)
    mod_w_b = bf(wf['mod_w'])
    mod_all = _mod_fwd(cv, mod_w_b, wf['mod_b'].reshape(DEPTH, 1, 3 * D))
    cq, sq, ck, sk = _rope_tables(lc, l)
    pos = jnp.concatenate([jnp.arange(lc), jnp.arange(l)]).astype(jnp.int32)[:, None]
    seglen = jnp.concatenate([jnp.full((lc,), lc), jnp.full((l,), l)]).astype(jnp.int32)[:, None]
    tiles = np.arange(nt)
    ntp = -(-nt // LANES) * LANES
    sel = np.zeros((8, ntp), np.float32)
    sel[np.where(tiles % tpe == 0, 4, tiles // tpe), tiles] = 1.0
    sel = jnp.asarray(sel)

    def tile_sums(st):
        return jnp.pad(st.transpose(1, 0, 2), ((0, 0), (0, ntp - nt), (0, 0)))

    lw = []
    for ly in range(DEPTH):
        w_in_p = bf(_to_padded(wf['w_in'][ly]))
        w2 = jnp.stack([_gla_w2(wf['gla_af_w2'][ly]), _gla_w2(wf['gla_ab_w2'][ly])])
        lw.append(dict(
            w_in=w_in_p, w_in_t=w_in_p.T,
            w_uq=bf(wf['mla_w_uq'][ly]), w_uq_t=bf(wf['mla_w_uq'][ly]).T,
            w_ukv=bf(wf['mla_w_ukv'][ly]), w_ukv_t=bf(wf['mla_w_ukv'][ly]).T,
            pool_w=bf(wf['pool_w'][ly]), pool_w_t=bf(wf['pool_w'][ly]).transpose(0, 2, 1),
            w2=bf(w2), w2_t=bf(w2).transpose(0, 1, 3, 2),
            b2=jnp.stack([wf['gla_af_b'][ly], wf['gla_ab_b'][ly]]).reshape(2, GLA_H, 1, GLA_DK),
            wbm=bf(wf['w_branch_mla'][ly]), wbp=bf(wf['w_branch_pool'][ly]), wbg=bf(wf['w_branch_gla'][ly]),
            wout=bf(wf['w_out'][ly]),
            wbm_t=bf(wf['w_branch_mla'][ly]).T, wbp_t=bf(wf['w_branch_pool'][ly]).T,
            wbg_t=bf(wf['w_branch_gla'][ly]).T, wout_t=bf(wf['w_out'][ly]).T,
            pre_g=wf['pre_norm'][ly][None, :], post_g=wf['post_norm'][ly][None, :],
            q_norm=wf['mla_q_norm'][ly][None, :], kv_norm=wf['mla_kv_norm'][ly][None, :],
            pool_scale=wf['pool_scale'][ly][None, :], gla_norm=wf['gla_norm'][ly][None, :]))

    def gla_seqs(a, h, d):
        a = a.reshape(nb, t, h, d)
        return jnp.concatenate([a, _segflip(a, lc, 1)], axis=0).transpose(0, 2, 1, 3)

    def gla_unseqs(a, d):
        a = a.transpose(0, 2, 1, 3).reshape(2 * nb, t, GLA_H * d)
        return a[:nb].reshape(n, GLA_H * d), _segflip(a[nb:], lc, 1).reshape(n, GLA_H * d)

    def zcol(z, nme):
        return z[:, PAD_OFF[nme]:PAD_OFF[nme] + SLAB[nme]]

    saved = []
    xcur = xs
    for ly in range(DEPTH):
        w = lw[ly]
        z, h = _pre_fwd(xcur, mod_all[ly], w['pre_g'], w['w_in'], tpe, f"pre_fwd{ly}")
        qb, kvb, krb = _mla_pre(z, w['q_norm'], w['kv_norm'], w['w_uq'], w['w_ukv'], cq, sq, ck, sk, tpe, f"mla_pre{ly}")
        qh = _heads(qb, nb, t, MLA_H, MLA_QK)
        kv4 = kvb.reshape(nb, t, MLA_H, MLA_NOPE + MLA_V)
        kr4 = jnp.broadcast_to(krb[:, :MLA_ROPE].reshape(nb, t, 1, MLA_ROPE), (nb, t, MLA_H, MLA_ROPE))
        kh = jnp.concatenate([kv4[..., :MLA_NOPE], kr4], axis=-1).transpose(0, 2, 1, 3).reshape(nb * MLA_H, t, MLA_QK)
        vh = kv4[..., MLA_NOPE:].transpose(0, 2, 1, 3).reshape(nb * MLA_H, t, MLA_V)
        oh, lse = _attn_fwd(qh, kh, vh, lc, f"attn_fwd{ly}")
        o_mla = _unheads(oh, nb, t, MLA_H, MLA_V)
        y_pool = _pool_fwd(z, w['pool_w'], w['pool_scale'], pos, seglen, nb, f"pool_fwd{ly}")
        q8 = gla_seqs(zcol(z, 'gla_q'), GLA_H, GLA_DK)
        k8 = gla_seqs(zcol(z, 'gla_k'), GLA_H, GLA_DK)
        v8 = gla_seqs(zcol(z, 'gla_v'), GLA_H, GLA_DV)
        a8 = jnp.concatenate([zcol(z, 'gla_af').reshape(nb, t, 128),
                              _segflip(zcol(z, 'gla_ab').reshape(nb, t, 128), lc, 1)], axis=0)
        o8, st = _gla_fwd(q8, k8, v8, a8, w['w2'], w['b2'], nb, f"gla_fwd{ly}")
        ogf, ogb = gla_unseqs(o8, GLA_DV)
        xnew, out = _merge_fwd(xcur, z, o_mla, y_pool, ogf, ogb, w['gla_norm'], w['wbm'], w['wbp'], w['wbg'],
                               w['wout'], mod_all[ly], w['post_g'], tpe, f"merge_fwd{ly}")
        saved.append(dict(x=xcur, z=z, h=h, qh=qh, kh=kh, vh=vh, oh=oh, lse=lse, o_mla=o_mla, y_pool=y_pool,
                          q8=q8, k8=k8, v8=v8, a8=a8, st=st, ogf=ogf, ogb=ogb, out=out))
        xcur = xnew

    dxcur, lparts = _loss_grad(xcur, tgt.reshape(nb * l, D), nb, tpe)
    loss = jnp.sum(lparts[:, 0, 0])

    g = {k: [None] * DEPTH for k in WEIGHTS if k != 'c_ctx'}
    dcv = jnp.zeros((8, D), F32)
    dcc = None
    for ly in reversed(range(DEPTH)):
        w, s = lw[ly], saved[ly]
        (dzm, dom, dzgm, dyp, dog, dzgg, st_b, g['w_branch_mla'][ly], g['w_branch_pool'][ly], g['w_branch_gla'][ly],
         g['w_out'][ly], dgn) = _merge_bwd(
            dxcur, s['out'], s['z'], s['o_mla'], s['y_pool'], s['ogf'], s['ogb'], w['gla_norm'], w['wbm'], w['wbp'],
            w['wbg'], w['wbm_t'], w['wbp_t'], w['wbg_t'], w['wout_t'], mod_all[ly], w['post_g'], tpe,
            f"merge_bwd{ly}")
        g['gla_norm'][ly] = dgn[0]
        doh = _heads(dom, nb, t, MLA_H, MLA_V)
        dqh, dkh, dvh = _attn_bwd(s['qh'], s['kh'], s['vh'], s['oh'], s['lse'], doh, lc, f"attn_bwd{ly}")
        dq = _unheads(dqh, nb, t, MLA_H, MLA_QK)
        dk4 = dkh.reshape(nb, MLA_H, t, MLA_QK).transpose(0, 2, 1, 3)
        dv4 = dvh.reshape(nb, MLA_H, t, MLA_V).transpose(0, 2, 1, 3)
        dkv = jnp.concatenate([dk4[..., :MLA_NOPE], dv4], axis=-1).reshape(n, MLA_H * (MLA_NOPE + MLA_V))
        dkr8 = dk4[..., MLA_NOPE:].reshape(n, MLA_H * MLA_ROPE)
        dzq, dzkv, dzkr, g['mla_w_uq'][ly], g['mla_w_ukv'][ly], dgq, dgkv = _mla_pre_bwd(
            s['z'], dq, dkv, dkr8, w['q_norm'], w['kv_norm'], w['w_uq_t'], w['w_ukv_t'], cq, sq, ck, sk, tpe,
            f"mla_pre_bwd{ly}")
        g['mla_q_norm'][ly], g['mla_kv_norm'][ly] = dgq[0], dgkv[0]
        dzpx, dzpg, g['pool_w'][ly], dps = _pool_bwd(s['z'], dyp, w['pool_w'], w['pool_w_t'], w['pool_scale'],
                                                     pos, seglen, nb, f"pool_bwd{ly}")
        g['pool_scale'][ly] = dps[0]
        dog4 = dog.reshape(nb, t, GLA_H, GLA_DV)
        do8 = jnp.concatenate([dog4, _segflip(dog4, lc, 1)], axis=0).transpose(0, 2, 1, 3)
        dq8, dk8, dv8, da8, dw2, db2 = _gla_bwd(s['q8'], s['k8'], s['v8'], s['a8'], w['w2'], w['w2_t'], w['b2'],
                                                s['st'], do8, nb, f"gla_bwd{ly}")
        dzgq = _add_cast(*gla_unseqs(dq8, GLA_DK), f"gla_dq{ly}")
        dzgk = _add_cast(*gla_unseqs(dk8, GLA_DK), f"gla_dk{ly}")
        dzgv = _add_cast(*gla_unseqs(dv8, GLA_DV), f"gla_dv{ly}")
        dzaf = da8[:nb].reshape(n, 128).astype(BF16)
        dzab = _segflip(da8[nb:], lc, 1).reshape(n, 128).astype(BF16)
        dw2n = dw2.transpose(0, 2, 1, 3).reshape(2, 128, GLA_H * GLA_DK)[:, :GLA_RANK]
        g['gla_af_w2'][ly], g['gla_ab_w2'][ly] = dw2n[0], dw2n[1]
        g['gla_af_b'][ly], g['gla_ab_b'][ly] = db2[0].reshape(-1), db2[1].reshape(-1)
        parts = dict(merge=dzm, mla_gate=dzgm, mla_q=dzq, mla_kv=dzkv, mla_kr=dzkr, pool_x=dzpx, pool_gate=dzpg,
                     gla_v=dzgv, gla_gate=dzgg, gla_q=dzgq, gla_k=dzgk, gla_af=dzaf, gla_ab=dzab)
        dz = jnp.concatenate([parts[nme] for nme in PAD_ORDER], axis=1)
        dxcur, st_a = _pre_bwd(dz, w['w_in_t'], s['x'], dxcur, mod_all[ly], w['pre_g'], tpe, f"pre_bwd{ly}")
        g['w_in'][ly] = _from_padded(_matmul_tn(s['h'], dz, 768, 512 if n % 512 == 0 else TM, f"w_in_grad{ly}"))
        dmw, dmb, dcv, dcc, dpre, dpost = _mod_bwd(cv, sel, tile_sums(st_a), tile_sums(st_b),
                                                   bf(wf['mod_w'][ly]).T, dcv, f"mod_bwd{ly}")
        g['mod_w'][ly], g['mod_b'][ly] = dmw, dmb[0]
        g['pre_norm'][ly], g['post_norm'][ly] = dpre[0], dpost[0]

    grads = {k: jnp.stack(v) for k, v in g.items()}
    grads['c_ctx'] = dcc[4]
    grad_x = dxcur.reshape(nb, t, D)[:, lc:, :]
    return loss, grad_x, grads


def _place():
    x, y, c = lax.axis_index("x"), lax.axis_index("y"), lax.axis_index("c")
    chips = [(1 - x, y), (x, 1 - y), (1 - x, 1 - y)]
    return x, y, c, chips


def _hbm_call(body, name, out_shape, n_in, sems):
    any_spec = pl.BlockSpec(memory_space=pl.ANY)
    return pl.pallas_call(body, name=name, out_shape=out_shape, in_specs=[any_spec] * n_in,
                          out_specs=jax.tree.map(lambda _: any_spec, out_shape), scratch_shapes=sems)


def _all_gather_shards(wp):
    def body(wp_ref, out_ref, send_sems, recv_sems, loc_sem):
        x, y, c, chips = _place()
        me = 2 * x + y
        loc = pltpu.make_async_copy(wp_ref, out_ref.at[me], loc_sem)
        loc.start()

        def copy(k, chip, to, src=None):
            dst = out_ref.at[2 * chip[0] + chip[1], c]
            return pltpu.make_async_remote_copy(src_ref=dst if src is None else src, dst_ref=dst,
                                                send_sem=send_sems.at[k], recv_sem=recv_sems.at[k],
                                                device_id=to, device_id_type=MESH)

        first = [copy(j, (x, y), (*chip, c), src=wp_ref.at[c]) for j, chip in enumerate(chips)]
        for cp in first:
            cp.start()
        passed = [copy(3 + j, chip, (x, y, 1 - c)) for j, chip in enumerate(chips)]
        for j, chip in enumerate(chips):
            copy(j, chip, (x, y, c)).wait_recv()
            passed[j].start()
        for j, chip in enumerate(chips):
            dst = out_ref.at[2 * chip[0] + chip[1], 1 - c]
            pltpu.make_async_remote_copy(src_ref=dst, dst_ref=dst, send_sem=send_sems.at[3 + j],
                                         recv_sem=recv_sems.at[3 + j], device_id=(x, y, 1 - c),
                                         device_id_type=MESH).wait_recv()
        for cp in first + passed:
            cp.wait_send()
        loc.wait()

    return _hbm_call(body, "all_gather_shards", jax.ShapeDtypeStruct((N_CHIPS,) + wp.shape, wp.dtype), 1,
                     [pltpu.SemaphoreType.DMA((6,)), pltpu.SemaphoreType.DMA((6,)), pltpu.SemaphoreType.DMA])(wp)


def _swap_halves(g):
    def body(g_ref, own_ref, got_ref, send_sem, recv_sem, loc_sem):
        x, y, c, _ = _place()
        loc = pltpu.make_async_copy(g_ref.at[c], own_ref, loc_sem)
        loc.start()
        cp = pltpu.make_async_remote_copy(src_ref=g_ref.at[1 - c], dst_ref=got_ref, send_sem=send_sem,
                                          recv_sem=recv_sem, device_id=(x, y, 1 - c), device_id_type=MESH)
        cp.start()
        cp.wait()
        loc.wait()

    shp = jax.ShapeDtypeStruct(g.shape[1:], g.dtype)
    return _hbm_call(body, "swap_halves", (shp, shp), 1,
                     [pltpu.SemaphoreType.DMA, pltpu.SemaphoreType.DMA, pltpu.SemaphoreType.DMA])(g)


def _scatter_to_chips(hs):
    def body(h_ref, out_ref, send_sems, recv_sems, loc_sem):
        x, y, c, chips = _place()
        me = 2 * x + y
        loc = pltpu.make_async_copy(h_ref.at[me], out_ref.at[me], loc_sem)
        loc.start()
        sends = []
        for j, chip in enumerate(chips):
            cp = pltpu.make_async_remote_copy(src_ref=h_ref.at[2 * chip[0] + chip[1]], dst_ref=out_ref.at[me],
                                              send_sem=send_sems.at[j], recv_sem=recv_sems.at[j],
                                              device_id=(*chip, c), device_id_type=MESH)
            cp.start()
            sends.append(cp)
        for j, chip in enumerate(chips):
            dst = out_ref.at[2 * chip[0] + chip[1]]
            pltpu.make_async_remote_copy(src_ref=dst, dst_ref=dst, send_sem=send_sems.at[j], recv_sem=recv_sems.at[j],
                                         device_id=(*chip, c), device_id_type=MESH).wait_recv()
        for cp in sends:
            cp.wait_send()
        loc.wait()

    return _hbm_call(body, "scatter_to_chips", jax.ShapeDtypeStruct(hs.shape, hs.dtype), 1,
                     [pltpu.SemaphoreType.DMA((3,)), pltpu.SemaphoreType.DMA((3,)), pltpu.SemaphoreType.DMA])(hs)


def _join_halves(rh):
    def body(r_ref, out_ref, send_sem, recv_sem, loc_sem):
        x, y, c, _ = _place()
        loc = pltpu.make_async_copy(r_ref, out_ref.at[c], loc_sem)
        loc.start()
        cp = pltpu.make_async_remote_copy(src_ref=r_ref, dst_ref=out_ref.at[c], send_sem=send_sem, recv_sem=recv_sem,
                                          device_id=(x, y, 1 - c), device_id_type=MESH)
        cp.start()
        cp.wait_send()
        dst = out_ref.at[1 - c]
        pltpu.make_async_remote_copy(src_ref=dst, dst_ref=dst, send_sem=send_sem, recv_sem=recv_sem,
                                     device_id=(x, y, 1 - c), device_id_type=MESH).wait_recv()
        loc.wait()

    return _hbm_call(body, "join_halves", jax.ShapeDtypeStruct((2,) + rh.shape, rh.dtype), 1,
                     [pltpu.SemaphoreType.DMA, pltpu.SemaphoreType.DMA, pltpu.SemaphoreType.DMA])(rh)


def _row_block(r):
    for br in (2048, 1024, 512, 256, 128, 64, 32, 16, 8):
        if r % br == 0:
            return br
    raise ValueError(r)


def _add2(a, b, name):
    s, r, _ = a.shape
    br = _row_block(r)

    def body(a_ref, b_ref, o_ref):
        o_ref[...] = a_ref[...] + b_ref[...]

    spec = pl.BlockSpec((1, br, LANES), lambda i, j: (i, j, 0))
    return pl.pallas_call(body, name=name, grid=(s, r // br), in_specs=[spec, spec], out_specs=spec,
                          out_shape=jax.ShapeDtypeStruct(a.shape, F32))(a, b)


def _sum_chips(parts):
    _, r, _ = parts.shape
    br = _row_block(r)

    def body(p_ref, o_ref):
        o_ref[...] = ((p_ref[0] + p_ref[1]) + p_ref[2]) + p_ref[3]

    return pl.pallas_call(body, name="sum_chips", grid=(r // br,),
                          in_specs=[pl.BlockSpec((N_CHIPS, br, LANES), lambda j: (0, j, 0))],
                          out_specs=pl.BlockSpec((br, LANES), lambda j: (j, 0)),
                          out_shape=jax.ShapeDtypeStruct((r, LANES), F32))(parts)


def _adamw(w, g, m, v):
    r, _ = w.shape
    br = _row_block(r)

    def body(w_ref, g_ref, m_ref, v_ref, d_ref, nm_ref, nv_ref):
        gv = g_ref[...]
        m2 = ADAM_B1 * m_ref[...] + (1.0 - ADAM_B1) * gv
        v2 = ADAM_B2 * v_ref[...] + (1.0 - ADAM_B2) * jnp.square(gv)
        m_hat = m2 / (1.0 - ADAM_B1 ** ADAM_STEP)
        v_hat = v2 / (1.0 - ADAM_B2 ** ADAM_STEP)
        d_ref[...] = -ADAM_LR * (m_hat / (jnp.sqrt(v_hat) + ADAM_EPS) + ADAM_WD * w_ref[...])
        nm_ref[...] = m2
        nv_ref[...] = v2

    spec = pl.BlockSpec((br, LANES), lambda j: (j, 0))
    shp = jax.ShapeDtypeStruct(w.shape, F32)
    return pl.pallas_call(body, name="adamw", grid=(r // br,), in_specs=[spec] * 4, out_specs=[spec] * 3,
                          out_shape=(shp, shp, shp))(w, g, m, v)


PACK_UNIT = 2 * LANES * 512


def _pack(tensors, dtype):
    flat = jnp.concatenate([a.reshape(-1).astype(dtype) for a in tensors])
    p = -(-flat.shape[0] // PACK_UNIT) * PACK_UNIT
    return jnp.pad(flat, (0, p - flat.shape[0])).reshape(2, p // (2 * LANES), LANES)


def _unpack(packed, shapes):
    flat = packed.reshape(-1)
    out, off = [], 0
    for shp in shapes:
        size = int(np.prod(shp))
        out.append(flat[off:off + size].reshape(shp))
        off += size
    return out


def _shard_of(a, axis, s):
    w = a.shape[axis] // N_CHIPS
    return lax.slice_in_dim(a, s * w, (s + 1) * w, axis=axis)


def kernel(x, c, ctx, c_ctx, mod_w, mod_b, pre_norm, post_norm, w_in, mla_q_norm, mla_w_uq, mla_kv_norm, mla_w_ukv, pool_w, pool_scale, gla_af_w2, gla_af_b, gla_ab_w2, gla_ab_b, gla_norm, w_branch_mla, w_branch_pool, w_branch_gla, w_out, loss_target, m_c_ctx, m_mod_w, m_mod_b, m_pre_norm, m_post_norm, m_w_in, m_mla_q_norm, m_mla_w_uq, m_mla_kv_norm, m_mla_w_ukv, m_pool_w, m_pool_scale, m_gla_af_w2, m_gla_af_b, m_gla_ab_w2, m_gla_ab_b, m_gla_norm, m_w_branch_mla, m_w_branch_pool, m_w_branch_gla, m_w_out, v_c_ctx, v_mod_w, v_mod_b, v_pre_norm, v_post_norm, v_w_in, v_mla_q_norm, v_mla_w_uq, v_mla_kv_norm, v_mla_w_ukv, v_pool_w, v_pool_scale, v_gla_af_w2, v_gla_af_b, v_gla_ab_w2, v_gla_ab_b, v_gla_norm, v_w_branch_mla, v_w_branch_pool, v_w_branch_gla, v_w_out):
    given = dict(locals())
    wts = {k: given[k] for k in WEIGHTS}
    order = [k for k, _ in SHARDED] + list(REPLICATED)
    shard_shapes = [wts[k].shape for k in order]

    sharded_names = [k for k, _ in SHARDED]
    gathered = _all_gather_shards(_pack([wts[k] for k in sharded_names], BF16))
    full = dict(wts)
    per_chip = [_unpack(gathered[s], [wts[k].shape for k in sharded_names]) for s in range(N_CHIPS)]
    for i, (k, axis) in enumerate(SHARDED):
        full[k] = jnp.concatenate([per_chip[s][i] for s in range(N_CHIPS)], axis=axis).astype(F32)

    loss_local, grad_x, grads = _local_step(x, c, ctx, loss_target, full)
    loss = lax.psum(loss_local, ("x", "y", "c"))

    axes = dict(SHARDED)
    packs = [_pack([_shard_of(grads[k], axes[k], s) if k in axes else grads[k] for k in order], F32)
             for s in range(N_CHIPS)]
    gpack = jnp.stack(packs, axis=1)
    own, got = _swap_halves(gpack)
    chip_sum = _add2(own, got, "add_cores")
    reduced_half = _sum_chips(_scatter_to_chips(chip_sum))
    gsum = _join_halves(reduced_half)

    r2 = gsum.shape[0] * gsum.shape[1]
    pk = lambda pre: _pack([given[pre + k] for k in order], F32).reshape(r2, LANES)
    delta, new_m, new_v = _adamw(pk(''), gsum.reshape(r2, LANES), pk('m_'), pk('v_'))
    outs = {}
    for label, arr in (('grad', gsum), ('delta', delta), ('new_m', new_m), ('new_v', new_v)):
        outs[label] = dict(zip(order, _unpack(arr, shard_shapes)))
    return (loss, grad_x, *[outs[lab][k] for lab in ('grad', 'delta', 'new_m', 'new_v') for k in WEIGHTS])
```

```python
import functools

import numpy as np
import jax
import jax.numpy as jnp
from jax import lax
from jax.experimental import pallas as pl
from jax.experimental.pallas import tpu as pltpu

F32 = jnp.float32
BF16 = jnp.bfloat16
HIGHEST = lax.Precision.HIGHEST
MESH = pl.DeviceIdType.MESH

D = 1024
DEPTH = 2
EPS = 1e-6
GRID_W = 64
MLA_H, MLA_NOPE, MLA_ROPE, MLA_V = 8, 64, 32, 64
MLA_QK = MLA_NOPE + MLA_ROPE
ROPE_BASE = 10000.0
POOL_WINDOWS = (2, 4, 8, 16)
GLA_H, GLA_DK, GLA_DV, GLA_RANK, GLA_TAU, GLA_C = 4, 64, 128, 16, 16.0, 64
EXP_CLAMP = 80.0
ADAM_LR, ADAM_B1, ADAM_B2, ADAM_EPS, ADAM_WD, ADAM_STEP = 0.001, 0.9, 0.999, 1e-08, 0.01, 10

TM = 256
LANES = 128
N_CHIPS = 4

IN_NAMES = ('mla_q', 'mla_kv', 'mla_kr', 'mla_gate', 'pool_x', 'pool_gate',
            'gla_q', 'gla_k', 'gla_v', 'gla_af', 'gla_ab', 'gla_gate', 'merge')
IN_SIZES = (256, 128, 32, 512, 512, 512, 256, 256, 512, 16, 16, 512, 3 * D)
NAT_OFF = dict(zip(IN_NAMES, [int(o) for o in np.cumsum((0,) + IN_SIZES[:-1])]))
NAT_SIZE = dict(zip(IN_NAMES, IN_SIZES))
PAD_ORDER = ('merge', 'mla_gate', 'mla_q', 'mla_kv', 'mla_kr', 'pool_x', 'pool_gate',
             'gla_v', 'gla_gate', 'gla_q', 'gla_k', 'gla_af', 'gla_ab')
SLAB = {n: max(NAT_SIZE[n], LANES) for n in IN_NAMES}
PAD_OFF = dict(zip(PAD_ORDER, [int(o) for o in np.cumsum([0] + [SLAB[n] for n in PAD_ORDER[:-1]])]))
D_PAD = sum(SLAB.values())
IN_SLAB = {n: 0 for n in IN_NAMES}
IN_SLAB['mla_kr'] = MLA_NOPE
MLA_HP = 128


def _blk(name):
    return PAD_OFF[name] // SLAB[name]


SHARDED = (('mod_w', 2), ('w_in', 2), ('mla_w_uq', 2), ('mla_w_ukv', 2), ('gla_af_w2', 2), ('gla_ab_w2', 2),
           ('w_branch_mla', 2), ('w_branch_pool', 2), ('w_branch_gla', 2), ('w_out', 1))
REPLICATED = ('c_ctx', 'mod_b', 'pre_norm', 'post_norm', 'mla_q_norm', 'mla_kv_norm', 'pool_w', 'pool_scale',
              'gla_af_b', 'gla_ab_b', 'gla_norm')
WEIGHTS = ('c_ctx', 'mod_w', 'mod_b', 'pre_norm', 'post_norm', 'w_in', 'mla_q_norm', 'mla_w_uq', 'mla_kv_norm',
           'mla_w_ukv', 'pool_w', 'pool_scale', 'gla_af_w2', 'gla_af_b', 'gla_ab_w2', 'gla_ab_b', 'gla_norm',
           'w_branch_mla', 'w_branch_pool', 'w_branch_gla', 'w_out')


def _cp(vmem_mb=None, sem=None):
    kw = {}
    if vmem_mb is not None:
        kw['vmem_limit_bytes'] = vmem_mb * 1024 * 1024
    if sem is not None:
        kw['dimension_semantics'] = sem
    return pltpu.CompilerParams(**kw)


def _bdot(a, b):
    return jnp.dot(a.astype(BF16), b.astype(BF16), preferred_element_type=F32)


def _bdot_nt(a, b):
    return lax.dot_general(a.astype(BF16), b.astype(BF16), (((1,), (1,)), ((), ())), preferred_element_type=F32)


def _bdot_tn(a, b):
    return lax.dot_general(a.astype(BF16), b.astype(BF16), (((0,), (0,)), ((), ())), preferred_element_type=F32)


def _xdot(a, b):
    return jnp.dot(a, b, precision=HIGHEST, preferred_element_type=F32)


def _xdot_tn(a, b):
    return lax.dot_general(a, b, (((0,), (0,)), ((), ())), precision=HIGHEST, preferred_element_type=F32)


def _xdot_nt(a, b):
    return lax.dot_general(a, b, (((1,), (1,)), ((), ())), precision=HIGHEST, preferred_element_type=F32)


def _sigmoid(x):
    return jax.nn.sigmoid(x)


def _silu(x):
    return x * _sigmoid(x)


def _dsilu(x):
    s = _sigmoid(x)
    return s * (1.0 + x * (1.0 - s))


def _rstd(x):
    return lax.rsqrt(jnp.mean(x * x, axis=-1, keepdims=True) + EPS)


def _norm_bwd(xhat, r, dy):
    return r * (dy - xhat * jnp.mean(xhat * dy, axis=-1, keepdims=True))


def _colsum(a):
    return jnp.sum(a, axis=0, keepdims=True)


def _modrow(i, tpe):
    return jnp.where(i % tpe == 0, 4, i // tpe)


def _rot(x):
    n = x.shape[-1]
    lane = lax.broadcasted_iota(jnp.int32, x.shape, x.ndim - 1)
    return jnp.where(lane % 16 < 8, -pltpu.roll(x, n - 8, x.ndim - 1), pltpu.roll(x, 8, x.ndim - 1))


def _mod_fwd(cv, mod_w, mod_b):
    def body(cv_ref, w_ref, b_ref, o_ref):
        s = _silu(cv_ref[...])
        for l in range(DEPTH):
            o_ref[l] = _bdot(s, w_ref[l]) + b_ref[l]

    return pl.pallas_call(body, name="mod_fwd", out_shape=jax.ShapeDtypeStruct((DEPTH, 8, 3 * D), F32),
                          compiler_params=_cp(40))(cv, mod_w, mod_b)


def _mod_bwd(cv, sel, st_a, st_b, w_t, dcv_in, name):
    def body(cv_ref, sel_ref, sa_ref, sb_ref, wt_ref, dcin_ref, dw_ref, db_ref, dcv_ref, dcc_ref, dpre_ref, dpost_ref):
        cvv = cv_ref[...]
        s = _silu(cvv)
        sel_v = sel_ref[...]
        dmod = jnp.concatenate([_xdot(sel_v, sa_ref[0]), _xdot(sel_v, sa_ref[1]), _xdot(sel_v, sb_ref[0])], axis=1)
        dw_ref[...] = _bdot_tn(s, dmod)
        db_ref[...] = _colsum(dmod)
        dcv = dcin_ref[...] + _bdot(dmod, wt_ref[...])
        dcv_ref[...] = dcv
        dcc_ref[...] = dcv * _dsilu(cvv)
        dpre_ref[...] = _colsum(sa_ref[2])
        dpost_ref[...] = _colsum(sb_ref[1])

    shapes = (jax.ShapeDtypeStruct((D, 3 * D), F32), jax.ShapeDtypeStruct((1, 3 * D), F32),
              jax.ShapeDtypeStruct((8, D), F32), jax.ShapeDtypeStruct((8, D), F32),
              jax.ShapeDtypeStruct((1, D), F32), jax.ShapeDtypeStruct((1, D), F32))
    return pl.pallas_call(body, name=name, out_shape=shapes, compiler_params=_cp(48))(cv, sel, st_a, st_b, w_t, dcv_in)


def _pre_fwd(x, modl, pre_g, w, tpe, name):
    n = x.shape[0]
    nt = n // TM
    ncb = 3
    tn = D_PAD // ncb

    def body(x_ref, m_ref, g_ref, w_ref, z_ref, h_ref):
        xv = x_ref[...]
        m = m_ref[0]
        h = xv * _rstd(xv) * g_ref[...] * (1.0 + m[:, D:2 * D]) + m[:, 0:D]
        hb = h.astype(BF16)

        @pl.when(pl.program_id(1) == 0)
        def _():
            h_ref[...] = hb

        z_ref[...] = jnp.dot(hb, w_ref[...], preferred_element_type=F32)

    return pl.pallas_call(
        body, name=name, grid=(nt, ncb),
        in_specs=[pl.BlockSpec((TM, D), lambda i, j: (i, 0)),
                  pl.BlockSpec((1, 1, 3 * D), lambda i, j: (_modrow(i, tpe), 0, 0)),
                  pl.BlockSpec((1, D), lambda i, j: (0, 0)),
                  pl.BlockSpec((D, tn), lambda i, j: (0, j))],
        out_specs=[pl.BlockSpec((TM, tn), lambda i, j: (i, j)),
                   pl.BlockSpec((TM, D), lambda i, j: (i, 0))],
        out_shape=(jax.ShapeDtypeStruct((n, D_PAD), F32), jax.ShapeDtypeStruct((n, D), BF16)),
        compiler_params=_cp(48, ("arbitrary", "arbitrary")),
    )(x, modl.reshape(8, 1, 3 * D), pre_g, w)


def _pre_bwd(dz, w_t, x, dxres, modl, pre_g, tpe, name):
    n = x.shape[0]
    nt = n // TM

    def body(dz_ref, wt_ref, x_ref, dr_ref, m_ref, g_ref, dx_ref, st_ref):
        dh = jnp.dot(dz_ref[...], wt_ref[...], preferred_element_type=F32)
        xv = x_ref[...]
        r = _rstd(xv)
        xn = xv * r
        m = m_ref[0]
        sc1 = 1.0 + m[:, D:2 * D]
        g = g_ref[...]
        st_ref[0, 0:1, :] = _colsum(dh)
        st_ref[0, 1:2, :] = _colsum(dh * xn * g)
        st_ref[0, 2:3, :] = _colsum(dh * xn * sc1)
        st_ref[0, 3:8, :] = jnp.zeros((5, D), F32)
        dx_ref[...] = dr_ref[...] + _norm_bwd(xn, r, dh * g * sc1)

    return pl.pallas_call(
        body, name=name, grid=(nt,),
        in_specs=[pl.BlockSpec((TM, D_PAD), lambda i: (i, 0)),
                  pl.BlockSpec((D_PAD, D), lambda i: (0, 0)),
                  pl.BlockSpec((TM, D), lambda i: (i, 0)),
                  pl.BlockSpec((TM, D), lambda i: (i, 0)),
                  pl.BlockSpec((1, 1, 3 * D), lambda i: (_modrow(i, tpe), 0, 0)),
                  pl.BlockSpec((1, D), lambda i: (0, 0))],
        out_specs=[pl.BlockSpec((TM, D), lambda i: (i, 0)),
                   pl.BlockSpec((1, 8, D), lambda i: (i, 0, 0))],
        out_shape=(jax.ShapeDtypeStruct((n, D), F32), jax.ShapeDtypeStruct((nt, 8, D), F32)),
        compiler_params=_cp(56, ("arbitrary",)),
    )(dz, w_t, x, dxres, modl.reshape(8, 1, 3 * D), pre_g)


def _matmul_tn(a, b, tn, tk, name):
    n, k1 = a.shape
    k2 = b.shape[1]

    def body(a_ref, b_ref, o_ref):
        @pl.when(pl.program_id(1) == 0)
        def _():
            o_ref[...] = jnp.zeros(o_ref.shape, F32)

        o_ref[...] += lax.dot_general(a_ref[...], b_ref[...], (((0,), (0,)), ((), ())), preferred_element_type=F32)

    return pl.pallas_call(
        body, name=name, grid=(k2 // tn, n // tk),
        in_specs=[pl.BlockSpec((tk, k1), lambda j, k: (k, 0)), pl.BlockSpec((tk, tn), lambda j, k: (k, j))],
        out_specs=pl.BlockSpec((k1, tn), lambda j, k: (0, j)),
        out_shape=jax.ShapeDtypeStruct((k1, k2), F32),
        compiler_params=_cp(48, ("arbitrary", "arbitrary")),
    )(a, b)


def _mla_pre(z, q_norm, kv_norm, w_uq, w_ukv, cq, sq, ck, sk, tpe, name):
    n = z.shape[0]
    nt = n // TM
    scale = MLA_QK ** -0.5

    def body(zq_ref, zkv_ref, zkr_ref, gq_ref, gkv_ref, wq_ref, wkv_ref, cq_ref, sq_ref, ck_ref, sk_ref,
             q_ref, kv_ref, kr_ref):
        zq = zq_ref[...]
        qn = zq * _rstd(zq) * gq_ref[...]
        qraw = _bdot(qn, wq_ref[...])
        q_ref[...] = ((qraw * cq_ref[...] + _rot(qraw) * sq_ref[...]) * scale).astype(BF16)
        zkv = zkv_ref[...]
        kvn = zkv * _rstd(zkv) * gkv_ref[...]
        kv_ref[...] = _bdot(kvn, wkv_ref[...]).astype(BF16)
        zkr = zkr_ref[...]
        kr_ref[...] = (zkr * ck_ref[...] + _rot(zkr) * sk_ref[...]).astype(BF16)

    hq, hkv = MLA_H * MLA_HP, MLA_H * (MLA_NOPE + MLA_V)
    const = lambda i: (0, 0)
    tab = lambda i: (i % tpe, 0)
    return pl.pallas_call(
        body, name=name, grid=(nt,),
        in_specs=[pl.BlockSpec((TM, 256), lambda i: (i, _blk('mla_q'))),
                  pl.BlockSpec((TM, 128), lambda i: (i, _blk('mla_kv'))),
                  pl.BlockSpec((TM, 128), lambda i: (i, _blk('mla_kr'))),
                  pl.BlockSpec((1, 256), const), pl.BlockSpec((1, 128), const),
                  pl.BlockSpec((256, hq), const), pl.BlockSpec((128, hkv), const),
                  pl.BlockSpec((TM, hq), tab), pl.BlockSpec((TM, hq), tab),
                  pl.BlockSpec((TM, 128), tab), pl.BlockSpec((TM, 128), tab)],
        out_specs=[pl.BlockSpec((TM, hq), lambda i: (i, 0)), pl.BlockSpec((TM, hkv), lambda i: (i, 0)),
                   pl.BlockSpec((TM, 128), lambda i: (i, 0))],
        out_shape=(jax.ShapeDtypeStruct((n, hq), BF16), jax.ShapeDtypeStruct((n, hkv), BF16),
                   jax.ShapeDtypeStruct((n, 128), BF16)),
        compiler_params=_cp(32, ("arbitrary",)),
    )(z, z, z, q_norm, kv_norm, w_uq, w_ukv, cq, sq, ck, sk)


def _mla_pre_bwd(z, dq, dkv, dkr, q_norm, kv_norm, w_uq_t, w_ukv_t, cq, sq, ck, sk, tpe, name):
    n = z.shape[0]
    nt = n // TM
    scale = MLA_QK ** -0.5
    hq, hkv = MLA_H * MLA_HP, MLA_H * (MLA_NOPE + MLA_V)

    def body(zq_ref, zkv_ref, dq_ref, dkv_ref, dkr_ref, gq_ref, gkv_ref, wqt_ref, wkvt_ref, cq_ref, sq_ref,
             ck_ref, sk_ref, dzq_ref, dzkv_ref, dzkr_ref, dwq_ref, dwkv_ref, dgq_ref, dgkv_ref):
        @pl.when(pl.program_id(0) == 0)
        def _():
            dwq_ref[...] = jnp.zeros(dwq_ref.shape, F32)
            dwkv_ref[...] = jnp.zeros(dwkv_ref.shape, F32)
            dgq_ref[...] = jnp.zeros(dgq_ref.shape, F32)
            dgkv_ref[...] = jnp.zeros(dgkv_ref.shape, F32)

        zq = zq_ref[...]
        rq = _rstd(zq)
        qhat = zq * rq
        gq = gq_ref[...]
        dqs = dq_ref[...] * scale
        dqraw = dqs * cq_ref[...] - _rot(dqs * sq_ref[...])
        dwq_ref[...] += _bdot_tn(qhat * gq, dqraw)
        dqn = _bdot(dqraw, wqt_ref[...])
        dgq_ref[...] += _colsum(dqn * qhat)
        dzq_ref[...] = _norm_bwd(qhat, rq, dqn * gq).astype(BF16)

        zkv = zkv_ref[...]
        rkv = _rstd(zkv)
        khat = zkv * rkv
        gkv = gkv_ref[...]
        dkvv = dkv_ref[...]
        dwkv_ref[...] += _bdot_tn(khat * gkv, dkvv)
        dkvn = _bdot(dkvv, wkvt_ref[...])
        dgkv_ref[...] += _colsum(dkvn * khat)
        dzkv_ref[...] = _norm_bwd(khat, rkv, dkvn * gkv).astype(BF16)

        dkr = dkr_ref[...]
        dzkr_ref[...] = (dkr * ck_ref[...] - _rot(dkr * sk_ref[...])).astype(BF16)

    const = lambda i: (0, 0)
    tab = lambda i: (i % tpe, 0)
    row = lambda i: (i, 0)
    return pl.pallas_call(
        body, name=name, grid=(nt,),
        in_specs=[pl.BlockSpec((TM, 256), lambda i: (i, _blk('mla_q'))),
                  pl.BlockSpec((TM, 128), lambda i: (i, _blk('mla_kv'))),
                  pl.BlockSpec((TM, hq), row), pl.BlockSpec((TM, hkv), row), pl.BlockSpec((TM, 128), row),
                  pl.BlockSpec((1, 256), const), pl.BlockSpec((1, 128), const),
                  pl.BlockSpec((hq, 256), const), pl.BlockSpec((hkv, 128), const),
                  pl.BlockSpec((TM, hq), tab), pl.BlockSpec((TM, hq), tab),
                  pl.BlockSpec((TM, 128), tab), pl.BlockSpec((TM, 128), tab)],
        out_specs=[pl.BlockSpec((TM, 256), row), pl.BlockSpec((TM, 128), row), pl.BlockSpec((TM, 128), row),
                   pl.BlockSpec((256, hq), const), pl.BlockSpec((128, hkv), const),
                   pl.BlockSpec((1, 256), const), pl.BlockSpec((1, 128), const)],
        out_shape=(jax.ShapeDtypeStruct((n, 256), BF16), jax.ShapeDtypeStruct((n, 128), BF16),
                   jax.ShapeDtypeStruct((n, 128), BF16), jax.ShapeDtypeStruct((256, hq), F32),
                   jax.ShapeDtypeStruct((128, hkv), F32), jax.ShapeDtypeStruct((1, 256), F32),
                   jax.ShapeDtypeStruct((1, 128), F32)),
        compiler_params=_cp(32, ("arbitrary",)),
    )(z, z, dq, dkv, dkr, q_norm, kv_norm, w_uq_t, w_ukv_t, cq, sq, ck, sk)


def _attn_head(q_ref, kv_ref, kr_ref, hh, j, lc):
    t = kv_ref.shape[0]
    kvh = kv_ref[:, hh * MLA_HP:(hh + 1) * MLA_HP]
    lane = lax.broadcasted_iota(jnp.int32, kvh.shape, 1)
    kh = jnp.where(lane < MLA_NOPE, kvh, kr_ref[...])
    qh = q_ref[:, hh * MLA_HP:(hh + 1) * MLA_HP]
    s = lax.dot_general(qh, kh, (((1,), (1,)), ((), ())), preferred_element_type=F32)
    col = lax.broadcasted_iota(jnp.int32, s.shape, 1)
    limit = jnp.where(j == 0, lc, t)
    return kvh, kh, qh, jnp.where(col < limit, s, -1e30)


def _attn_specs(nb, tpe, t):
    tile = lambda b, p, j: (b * tpe + j, p)
    return [pl.BlockSpec((TM, 2 * MLA_HP), tile),
            pl.BlockSpec((t, 2 * MLA_HP), lambda b, p, j: (b, p)),
            pl.BlockSpec((t, MLA_HP), lambda b, p, j: (b, 0))]


def _attn_fwd(q, kv, kr, nb, lc, name):
    n = q.shape[0]
    t = n // nb
    tpe = t // TM

    def body(q_ref, kv_ref, kr_ref, o_ref, lse_ref):
        j = pl.program_id(2)
        lane = lax.broadcasted_iota(jnp.int32, (TM, MLA_HP), 1)
        res, lses = [], []
        for hh in range(2):
            kvh, _, _, s = _attn_head(q_ref, kv_ref, kr_ref, hh, j, lc)
            m = jnp.max(s, axis=-1, keepdims=True)
            p = jnp.exp(s - m)
            l = jnp.sum(p, axis=-1, keepdims=True)
            res.append(jnp.dot(p.astype(BF16), kvh, preferred_element_type=F32) / l)
            lses.append(m + jnp.log(l))
        o_ref[...] = jnp.where(lane < MLA_V, pltpu.roll(res[0], MLA_V, 1), res[1])
        lane2 = lax.broadcasted_iota(jnp.int32, (TM, 2), 1)
        lse_ref[0] = jnp.where(lane2 == 0, lses[0], lses[1])

    return pl.pallas_call(
        body, name=name, grid=(nb, MLA_H // 2, tpe),
        in_specs=_attn_specs(nb, tpe, t),
        out_specs=[pl.BlockSpec((TM, 2 * MLA_V), lambda b, p, j: (b * tpe + j, p)),
                   pl.BlockSpec((1, TM, 2), lambda b, p, j: (p, b * tpe + j, 0))],
        out_shape=(jax.ShapeDtypeStruct((n, MLA_H * MLA_V), F32), jax.ShapeDtypeStruct((MLA_H // 2, n, 2), F32)),
        compiler_params=_cp(48, ("arbitrary", "arbitrary", "arbitrary")),
    )(q, kv, kr)


def _attn_bwd(q, kv, kr, o, lse, do, nb, lc, name):
    n = q.shape[0]
    t = n // nb
    tpe = t // TM

    def body(q_ref, kv_ref, kr_ref, o_ref, lse_ref, do_ref, dq_ref, dkv_ref, dkr_ref):
        p_id, j = pl.program_id(1), pl.program_id(2)

        @pl.when(j == 0)
        def _():
            dkv_ref[...] = jnp.zeros(dkv_ref.shape, F32)

        @pl.when((j == 0) & (p_id == 0))
        def _():
            dkr_ref[...] = jnp.zeros(dkr_ref.shape, F32)

        lane = lax.broadcasted_iota(jnp.int32, (TM, MLA_HP), 1)
        lane_t = lax.broadcasted_iota(jnp.int32, (t, MLA_HP), 1)
        lane2 = lax.broadcasted_iota(jnp.int32, (TM, 2), 1)
        lse = lse_ref[0]
        dov, ov = do_ref[...], o_ref[...]
        dkr = jnp.zeros((t, MLA_HP), F32)
        for hh in range(2):
            kvh, kh, qh, s = _attn_head(q_ref, kv_ref, kr_ref, hh, j, lc)
            p = jnp.exp(s - jnp.sum(jnp.where(lane2 == hh, lse, 0.0), axis=1, keepdims=True))
            do_pos = jnp.where(lane >= MLA_NOPE, pltpu.roll(dov, MLA_V, 1) if hh == 0 else dov, 0.0)
            o_pos = jnp.where(lane >= MLA_NOPE, pltpu.roll(ov, MLA_V, 1) if hh == 0 else ov, 0.0)
            delta = jnp.sum(do_pos * o_pos, axis=-1, keepdims=True)
            dob = do_pos.astype(BF16)
            dp = lax.dot_general(dob, kvh, (((1,), (1,)), ((), ())), preferred_element_type=F32)
            ds = (p * (dp - delta)).astype(BF16)
            dq_ref[:, hh * MLA_HP:(hh + 1) * MLA_HP] = jnp.dot(ds, kh, preferred_element_type=F32)
            dkf = lax.dot_general(ds, qh, (((0,), (0,)), ((), ())), preferred_element_type=F32)
            dvp = lax.dot_general(p.astype(BF16), dob, (((0,), (0,)), ((), ())), preferred_element_type=F32)
            dkv_ref[:, hh * MLA_HP:(hh + 1) * MLA_HP] += jnp.where(lane_t < MLA_NOPE, dkf, dvp)
            dkr = dkr + jnp.where(lane_t >= MLA_NOPE, dkf, 0.0)
        dkr_ref[...] += dkr

    tile = lambda b, p, j: (b * tpe + j, p)
    return pl.pallas_call(
        body, name=name, grid=(nb, MLA_H // 2, tpe),
        in_specs=_attn_specs(nb, tpe, t) + [pl.BlockSpec((TM, 2 * MLA_V), tile),
                                            pl.BlockSpec((1, TM, 2), lambda b, p, j: (p, b * tpe + j, 0)),
                                            pl.BlockSpec((TM, 2 * MLA_V), tile)],
        out_specs=[pl.BlockSpec((TM, 2 * MLA_HP), tile),
                   pl.BlockSpec((t, 2 * MLA_HP), lambda b, p, j: (b, p)),
                   pl.BlockSpec((t, MLA_HP), lambda b, p, j: (b, 0))],
        out_shape=(jax.ShapeDtypeStruct((n, MLA_H * MLA_HP), F32), jax.ShapeDtypeStruct((n, MLA_H * MLA_HP), F32),
                   jax.ShapeDtypeStruct((n, MLA_HP), F32)),
        compiler_params=_cp(56, ("arbitrary", "arbitrary", "arbitrary")),
    )(q, kv, kr, o, lse, do)


def _pool_window(ug, pos, seglen, w, transpose):
    t = ug.shape[0]
    cnt = (jnp.minimum(pos + w // 2, seglen) - jnp.maximum(pos - w // 2, 0)).astype(F32)
    if transpose:
        ug = ug / cnt
    acc = jnp.zeros_like(ug)
    for j in range(-(w // 2), w // 2):
        jj = -j if transpose else j
        src = pos + jj
        valid = (src >= 0) & (src < seglen)
        acc = acc + jnp.where(valid, pltpu.roll(ug, (-jj) % t, 0), 0.0)
    return acc if transpose else acc / cnt


def _by_group(g, fn):
    for k, w in enumerate(POOL_WINDOWS):
        pl.when(g == k)(functools.partial(fn, w))


def _pool_specs(t):
    px, pg = PAD_OFF['pool_x'] // LANES, PAD_OFF['pool_gate'] // LANES
    return [pl.BlockSpec((t, LANES), lambda g, b: (b, px + g)),
            pl.BlockSpec((t, LANES), lambda g, b: (b, pg + g)),
            pl.BlockSpec((1, LANES, LANES), lambda g, b: (g, 0, 0)),
            pl.BlockSpec((1, LANES), lambda g, b: (0, g)),
            pl.BlockSpec((t, 1), lambda g, b: (0, 0)), pl.BlockSpec((t, 1), lambda g, b: (0, 0))]


def _pool_fwd(z, pool_w, pool_scale, pos, seglen, nb, name):
    n = z.shape[0]
    t = n // nb

    def body(u_ref, zg_ref, pw_ref, ps_ref, pos_ref, sl_ref, y_ref):
        def run(w):
            u = u_ref[...]
            pooled = _pool_window(u, pos_ref[...], sl_ref[...], w, False) - u
            y_ref[...] = (_bdot(pooled, pw_ref[0]) * ps_ref[...] * _silu(zg_ref[...])).astype(BF16)

        _by_group(pl.program_id(0), run)

    return pl.pallas_call(
        body, name=name, grid=(4, nb), in_specs=_pool_specs(t),
        out_specs=pl.BlockSpec((t, LANES), lambda g, b: (b, g)),
        out_shape=jax.ShapeDtypeStruct((n, 512), BF16),
        compiler_params=_cp(48, ("arbitrary", "arbitrary")),
    )(z, z, pool_w, pool_scale, pos, seglen)


def _pool_bwd(z, dy, pool_w, pool_w_t, pool_scale, pos, seglen, nb, name):
    n = z.shape[0]
    t = n // nb

    def body(u_ref, zg_ref, pw_ref, ps_ref, pos_ref, sl_ref, dy_ref, pwt_ref, du_ref, dg_ref, dpw_ref, dps_ref):
        @pl.when(pl.program_id(1) == 0)
        def _():
            dpw_ref[...] = jnp.zeros(dpw_ref.shape, F32)
            dps_ref[...] = jnp.zeros(dps_ref.shape, F32)

        def run(w):
            u = u_ref[...]
            pos_v, sl_v = pos_ref[...], sl_ref[...]
            pooled = _pool_window(u, pos_v, sl_v, w, False) - u
            mixed = _bdot(pooled, pw_ref[0])
            zg = zg_ref[...]
            sg = _silu(zg)
            ps = ps_ref[...]
            dyv = dy_ref[...]
            dps_ref[...] += _colsum(dyv * mixed * sg)
            dg_ref[...] = (dyv * mixed * ps * _dsilu(zg)).astype(BF16)
            dmixed = dyv * ps * sg
            dpw_ref[0] += _bdot_tn(pooled, dmixed)
            dpooled = _bdot(dmixed, pwt_ref[0])
            du_ref[...] = (_pool_window(dpooled, pos_v, sl_v, w, True) - dpooled).astype(BF16)

        _by_group(pl.program_id(0), run)

    blk = pl.BlockSpec((t, LANES), lambda g, b: (b, g))
    return pl.pallas_call(
        body, name=name, grid=(4, nb),
        in_specs=_pool_specs(t) + [blk, pl.BlockSpec((1, LANES, LANES), lambda g, b: (g, 0, 0))],
        out_specs=[blk, blk, pl.BlockSpec((1, LANES, LANES), lambda g, b: (g, 0, 0)),
                   pl.BlockSpec((1, LANES), lambda g, b: (0, g))],
        out_shape=(jax.ShapeDtypeStruct((n, 512), BF16), jax.ShapeDtypeStruct((n, 512), BF16),
                   jax.ShapeDtypeStruct((4, 128, 128), F32), jax.ShapeDtypeStruct((1, 512), F32)),
        compiler_params=_cp(48, ("arbitrary", "arbitrary")),
    )(z, z, pool_w, pool_scale, pos, seglen, dy, pool_w_t)


def _gla_chunk(q_ref, k_ref, a_ref, w2_ref, b2_ref, reverse):
    c = GLA_C
    x = _bdot(a_ref[...], w2_ref[0]) + b2_ref[0]
    la = (jnp.minimum(x, 0.0) - jnp.log(1.0 + jnp.exp(-jnp.abs(x)))) * (1.0 / GLA_TAU)
    row = lax.broadcasted_iota(jnp.int32, (c, c), 0)
    col = lax.broadcasted_iota(jnp.int32, (c, c), 1)
    tri = (col >= row) if reverse else (col <= row)
    tri_t = (col <= row) if reverse else (col >= row)
    b = _xdot(tri.astype(F32), la)
    tok = lax.broadcasted_iota(jnp.int32, la.shape, 0)
    bref = _colsum(jnp.where((tok >= c // 2) if reverse else (tok < c // 2), la, 0.0))
    blast = _colsum(la)
    eq = jnp.exp(jnp.minimum(b - bref, EXP_CLAMP))
    ek = jnp.exp(jnp.minimum(bref - b, EXP_CLAMP))
    qs = q_ref[...] * (GLA_DK ** -0.5)
    kk = k_ref[...]
    eb = jnp.exp(b)
    etail = jnp.exp(blast - b)
    return dict(x=x, la=la, tri=tri, tri_t=tri_t, eq=eq, ek=ek, qs=qs, kk=kk, qd=qs * eq, kd=kk * ek, qe=qs * eb,
                kl=kk * etail, eb=eb, etail=etail)


def _pair(a, p):
    return a[:, p * LANES:(p + 1) * LANES]


def _head_masks():
    lane = lax.broadcasted_iota(jnp.int32, (GLA_C, LANES), 1)
    return (lane < GLA_DK, lane >= GLA_DK)


def _state_decay(la, p):
    return jnp.exp(_xdot_tn(_pair(la, p), jnp.ones((GLA_C, GLA_DV), F32)))


def _gla_chunk_maps(nb, nc, ncc, order):
    def rmap(j):
        return jnp.where(j < ncc, ncc - 1 - j, nc - 1 - (j - ncc))

    if order == 'scan':
        return (lambda b, j: b * nc + j), (lambda b, j: b * nc + rmap(j))
    return (lambda b, j: b * nc + nc - 1 - j), (lambda b, j: b * nc + rmap(nc - 1 - j))


def _gla_in_specs(maps):
    specs = []
    for d, cm in enumerate(maps):
        gate = 'gla_af' if d == 0 else 'gla_ab'
        specs += [pl.BlockSpec((GLA_C, 256), lambda b, j, cm=cm: (cm(b, j), _blk('gla_q'))),
                  pl.BlockSpec((GLA_C, 256), lambda b, j, cm=cm: (cm(b, j), _blk('gla_k'))),
                  pl.BlockSpec((GLA_C, 512), lambda b, j, cm=cm: (cm(b, j), _blk('gla_v'))),
                  pl.BlockSpec((GLA_C, LANES), lambda b, j, cm=cm, gate=gate: (cm(b, j), _blk(gate))),
                  pl.BlockSpec((1, LANES, 256), lambda b, j, d=d: (d, 0, 0)),
                  pl.BlockSpec((1, 1, 256), lambda b, j, d=d: (d, 0, 0))]
    return specs


def _gla_fwd(z, w2, b2, nb, lc, name):
    n = z.shape[0]
    nc = n // nb // GLA_C
    maps = _gla_chunk_maps(nb, nc, lc // GLA_C, 'scan')

    def body(*refs):
        ins, (of_ref, ob_ref, sf_ref, sb_ref, s_sc) = refs[:12], refs[12:]

        @pl.when(pl.program_id(1) == 0)
        def _():
            s_sc[...] = jnp.zeros(s_sc.shape, F32)

        masks = _head_masks()
        for d, (o_ref, st_ref) in enumerate(((of_ref, sf_ref), (ob_ref, sb_ref))):
            q_ref, k_ref, v_ref, a_ref, w2_ref, b2_ref = ins[6 * d:6 * d + 6]
            ch = _gla_chunk(q_ref, k_ref, a_ref, w2_ref, b2_ref, d == 1)
            for p in range(2):
                s_prev = s_sc[d, p]
                st_ref[0, p] = s_prev
                s_new = _state_decay(ch['la'], p) * s_prev
                kd_p = _pair(ch['kd'], p)
                for hh in range(2):
                    h = 2 * p + hh
                    vv = v_ref[:, h * GLA_DV:(h + 1) * GLA_DV]
                    att = jnp.where(ch['tri'], _bdot_nt(jnp.where(masks[hh], _pair(ch['qd'], p), 0.0), kd_p), 0.0)
                    o_ref[:, h * GLA_DV:(h + 1) * GLA_DV] = (
                        _bdot(att, vv) + _bdot(jnp.where(masks[hh], _pair(ch['qe'], p), 0.0), s_prev))
                    s_new = s_new + _xdot_tn(jnp.where(masks[hh], _pair(ch['kl'], p), 0.0), vv)
                s_sc[d, p] = s_new

    o_shape = jax.ShapeDtypeStruct((n, 512), F32)
    st_shape = jax.ShapeDtypeStruct((n // GLA_C, 2, LANES, GLA_DV), F32)
    return pl.pallas_call(
        body, name=name, grid=(nb, nc),
        in_specs=_gla_in_specs(maps),
        out_specs=[pl.BlockSpec((GLA_C, 512), lambda b, j: (maps[0](b, j), 0)),
                   pl.BlockSpec((GLA_C, 512), lambda b, j: (maps[1](b, j), 0)),
                   pl.BlockSpec((1, 2, LANES, GLA_DV), lambda b, j: (maps[0](b, j), 0, 0, 0)),
                   pl.BlockSpec((1, 2, LANES, GLA_DV), lambda b, j: (maps[1](b, j), 0, 0, 0))],
        out_shape=(o_shape, o_shape, st_shape, st_shape),
        scratch_shapes=[pltpu.VMEM((2, 2, LANES, GLA_DV), F32)],
        compiler_params=_cp(32, ("arbitrary", "arbitrary")),
    )(z, z, z, z, w2, b2, z, z, z, z, w2, b2)


def _gla_bwd(z, w2, w2_t, b2, st_f, st_b, dog, nb, lc, name):
    n = z.shape[0]
    nc = n // nb // GLA_C
    maps = _gla_chunk_maps(nb, nc, lc // GLA_C, 'back')

    def body(*refs):
        ins, extra, outs, (ds_sc, sfx_sc) = refs[:12], refs[12:18], refs[18:30], refs[30:]

        @pl.when(pl.program_id(1) == 0)
        def _():
            ds_sc[...] = jnp.zeros(ds_sc.shape, F32)
            sfx_sc[...] = jnp.zeros(sfx_sc.shape, F32)

        @pl.when((pl.program_id(0) == 0) & (pl.program_id(1) == 0))
        def _():
            for r in outs[8:12]:
                r[...] = jnp.zeros(r.shape, F32)

        masks = _head_masks()
        for d in range(2):
            q_ref, k_ref, v_ref, a_ref, w2_ref, b2_ref = ins[6 * d:6 * d + 6]
            w2t_ref, st_ref, do_ref = extra[3 * d:3 * d + 3]
            dq_ref, dk_ref, dv_ref, da_ref = outs[4 * d:4 * d + 4]
            dw2_ref, db2_ref = outs[8 + 2 * d], outs[9 + 2 * d]
            ch = _gla_chunk(q_ref, k_ref, a_ref, w2_ref, b2_ref, d == 1)
            dqs, dks = [], []
            for p in range(2):
                s_prev = st_ref[0, p]
                ds_new = ds_sc[d, p]
                qd_p, kd_p, qe_p, kl_p = (_pair(ch[nme], p) for nme in ('qd', 'kd', 'qe', 'kl'))
                ds_prev = _state_decay(ch['la'], p) * ds_new
                dq_h, dk_h = [], []
                for hh in range(2):
                    h = 2 * p + hh
                    vv = v_ref[:, h * GLA_DV:(h + 1) * GLA_DV]
                    dov = do_ref[:, h * GLA_DV:(h + 1) * GLA_DV]
                    att = jnp.where(ch['tri'], _bdot_nt(jnp.where(masks[hh], qd_p, 0.0), kd_p), 0.0)
                    dv_ref[:, h * GLA_DV:(h + 1) * GLA_DV] = (
                        _bdot_tn(att, dov) + _bdot(jnp.where(masks[hh], kl_p, 0.0), ds_new))
                    datt = jnp.where(ch['tri'], _xdot_nt(dov, vv), 0.0)
                    dq_h.append(_xdot(datt, kd_p) * _pair(ch['eq'], p) + _xdot_nt(dov, s_prev) * _pair(ch['eb'], p))
                    dk_h.append(_xdot_tn(datt, qd_p) * _pair(ch['ek'], p)
                                + _xdot_nt(vv, ds_new) * _pair(ch['etail'], p))
                    ds_prev = ds_prev + _xdot_tn(jnp.where(masks[hh], qe_p, 0.0), dov)
                ds_sc[d, p] = ds_prev
                dqs.append(jnp.where(masks[0], dq_h[0], dq_h[1]))
                dks.append(jnp.where(masks[0], dk_h[0], dk_h[1]))
            dq = jnp.concatenate(dqs, axis=1)
            dk = jnp.concatenate(dks, axis=1)
            dq_ref[...] = dq * (GLA_DK ** -0.5)
            dk_ref[...] = dk
            db = ch['qs'] * dq - ch['kk'] * dk
            dla = _xdot(ch['tri_t'].astype(F32), db) + sfx_sc[d]
            sfx_sc[d] = sfx_sc[d] + _colsum(db)
            dx = dla * (1.0 / GLA_TAU) * _sigmoid(-ch['x'])
            da_ref[...] = _bdot(dx, w2t_ref[0])
            dw2_ref[...] += _bdot_tn(a_ref[...], dx)
            db2_ref[...] += _colsum(dx)

    extra_specs, out_specs = [], []
    for d, cm in enumerate(maps):
        extra_specs += [pl.BlockSpec((1, 256, LANES), lambda b, j, d=d: (d, 0, 0)),
                        pl.BlockSpec((1, 2, LANES, GLA_DV), lambda b, j, cm=cm: (cm(b, j), 0, 0, 0)),
                        pl.BlockSpec((GLA_C, 512), lambda b, j, cm=cm: (cm(b, j), 0))]
        out_specs += [pl.BlockSpec((GLA_C, 256), lambda b, j, cm=cm: (cm(b, j), 0)),
                      pl.BlockSpec((GLA_C, 256), lambda b, j, cm=cm: (cm(b, j), 0)),
                      pl.BlockSpec((GLA_C, 512), lambda b, j, cm=cm: (cm(b, j), 0)),
                      pl.BlockSpec((GLA_C, LANES), lambda b, j, cm=cm: (cm(b, j), 0))]
    const2 = lambda b, j: (0, 0)
    out_specs += [pl.BlockSpec((LANES, 256), const2), pl.BlockSpec((1, 256), const2)] * 2
    per_dir = (jax.ShapeDtypeStruct((n, 256), F32), jax.ShapeDtypeStruct((n, 256), F32),
               jax.ShapeDtypeStruct((n, 512), F32), jax.ShapeDtypeStruct((n, LANES), F32))
    wshape = (jax.ShapeDtypeStruct((LANES, 256), F32), jax.ShapeDtypeStruct((1, 256), F32))
    return pl.pallas_call(
        body, name=name, grid=(nb, nc),
        in_specs=_gla_in_specs(maps) + extra_specs,
        out_specs=out_specs,
        out_shape=per_dir + per_dir + wshape + wshape,
        scratch_shapes=[pltpu.VMEM((2, 2, LANES, GLA_DV), F32), pltpu.VMEM((2, 1, 256), F32)],
        compiler_params=_cp(32, ("arbitrary", "arbitrary")),
    )(z, z, z, z, w2, b2, z, z, z, z, w2, b2, w2_t, st_f, dog, w2_t, st_b, dog)


def _add_cast(a, b, name):
    n, w = a.shape

    def body(a_ref, b_ref, o_ref):
        o_ref[...] = (a_ref[...] + b_ref[...]).astype(BF16)

    return pl.pallas_call(
        body, name=name, grid=(n // TM,),
        in_specs=[pl.BlockSpec((TM, w), lambda i: (i, 0)), pl.BlockSpec((TM, w), lambda i: (i, 0))],
        out_specs=pl.BlockSpec((TM, w), lambda i: (i, 0)),
        out_shape=jax.ShapeDtypeStruct((n, w), BF16),
        compiler_params=_cp(32, ("arbitrary",)),
    )(a, b)


def _gla_out_norm(og):
    hats, rs = [], []
    for h in range(GLA_H):
        seg = og[:, h * GLA_DV:(h + 1) * GLA_DV]
        r = _rstd(seg)
        hats.append(seg * r)
        rs.append(r)
    return hats, rs


def _merge_branches(zm_ref, zgm_ref, zgg_ref, om_ref, yp_ref, ogf_ref, ogb_ref, gn_ref, wbm_ref, wbp_ref, wbg_ref):
    zgm, zgg = zgm_ref[...], zgg_ref[...]
    om = om_ref[...]
    y_mla = om * _silu(zgm)
    hats, rs = _gla_out_norm(ogf_ref[...] + ogb_ref[...])
    gn = gn_ref[...]
    sgg = _silu(zgg)
    y_gla = jnp.concatenate([hats[h] * gn for h in range(GLA_H)], axis=1) * sgg
    ys = (y_mla, yp_ref[...], y_gla)
    ps = (_bdot(y_mla, wbm_ref[...]), jnp.dot(yp_ref[...], wbp_ref[...], preferred_element_type=F32),
          _bdot(y_gla, wbg_ref[...]))
    zm = zm_ref[...]
    gs = tuple(_sigmoid(zm[:, a * D:(a + 1) * D]) for a in range(3))
    merged = gs[0] * ps[0] + gs[1] * ps[1] + gs[2] * ps[2]
    return dict(zgm=zgm, zgg=zgg, om=om, hats=hats, rs=rs, gn=gn, sgg=sgg, ys=ys, ps=ps, gs=gs, merged=merged)


def _merge_in_specs(tpe):
    row = lambda i: (i, 0)
    const = lambda i: (0, 0)
    return [pl.BlockSpec((TM, 3 * D), lambda i: (i, _blk('merge'))),
            pl.BlockSpec((TM, 512), lambda i: (i, _blk('mla_gate'))),
            pl.BlockSpec((TM, 512), lambda i: (i, _blk('gla_gate'))),
            pl.BlockSpec((TM, 512), row), pl.BlockSpec((TM, 512), row), pl.BlockSpec((TM, 512), row),
            pl.BlockSpec((TM, 512), row), pl.BlockSpec((1, 128), const),
            pl.BlockSpec((512, D), const), pl.BlockSpec((512, D), const), pl.BlockSpec((512, D), const),
            pl.BlockSpec((1, 1, 3 * D), lambda i: (_modrow(i, tpe), 0, 0)), pl.BlockSpec((1, D), const)]


def _merge_fwd(x, z, o_mla, y_pool, ogf, ogb, gla_n, wbm, wbp, wbg, wout, modl, post_g, tpe, name):
    n = x.shape[0]

    def body(zm_ref, zgm_ref, zgg_ref, om_ref, yp_ref, ogf_ref, ogb_ref, gn_ref, wbm_ref, wbp_ref, wbg_ref,
             m_ref, pg_ref, x_ref, wo_ref, xn_ref, out_ref):
        mb = _merge_branches(zm_ref, zgm_ref, zgg_ref, om_ref, yp_ref, ogf_ref, ogb_ref, gn_ref,
                             wbm_ref, wbp_ref, wbg_ref)
        out = _bdot(mb['merged'], wo_ref[...])
        gate = m_ref[0][:, 2 * D:3 * D]
        xn_ref[...] = x_ref[...] + gate * (out * _rstd(out) * pg_ref[...])
        out_ref[...] = out

    row = lambda i: (i, 0)
    return pl.pallas_call(
        body, name=name, grid=(n // TM,),
        in_specs=_merge_in_specs(tpe) + [pl.BlockSpec((TM, D), row), pl.BlockSpec((D, D), lambda i: (0, 0))],
        out_specs=[pl.BlockSpec((TM, D), row), pl.BlockSpec((TM, D), row)],
        out_shape=(jax.ShapeDtypeStruct((n, D), F32), jax.ShapeDtypeStruct((n, D), F32)),
        compiler_params=_cp(48, ("arbitrary",)),
    )(z, z, z, o_mla, y_pool, ogf, ogb, gla_n, wbm, wbp, wbg, modl.reshape(8, 1, 3 * D), post_g, x, wout)


def _merge_bwd(dxn, out, z, o_mla, y_pool, ogf, ogb, gla_n, wbm, wbp, wbg, wbm_t, wbp_t, wbg_t, wout_t,
               modl, post_g, tpe, name):
    n = out.shape[0]
    nt = n // TM

    def body(zm_ref, zgm_ref, zgg_ref, om_ref, yp_ref, ogf_ref, ogb_ref, gn_ref, wbm_ref, wbp_ref, wbg_ref,
             m_ref, pg_ref, dxn_ref, out_ref, wbmt_ref, wbpt_ref, wbgt_ref, wot_ref,
             dzm_ref, dom_ref, dzgm_ref, dyp_ref, dog_ref, dzgg_ref, st_ref,
             dwbm_ref, dwbp_ref, dwbg_ref, dwo_ref, dgn_ref):
        @pl.when(pl.program_id(0) == 0)
        def _():
            for r in (dwbm_ref, dwbp_ref, dwbg_ref, dwo_ref, dgn_ref):
                r[...] = jnp.zeros(r.shape, F32)

        mb = _merge_branches(zm_ref, zgm_ref, zgg_ref, om_ref, yp_ref, ogf_ref, ogb_ref, gn_ref,
                             wbm_ref, wbp_ref, wbg_ref)
        out = out_ref[...]
        r2 = _rstd(out)
        on = out * r2
        pg = pg_ref[...]
        gate = m_ref[0][:, 2 * D:3 * D]
        dxn_v = dxn_ref[...]
        st_ref[0, 0:1, :] = _colsum(dxn_v * on * pg)
        st_ref[0, 1:2, :] = _colsum(dxn_v * gate * on)
        st_ref[0, 2:8, :] = jnp.zeros((6, D), F32)
        dout = _norm_bwd(on, r2, dxn_v * gate * pg)
        dwo_ref[...] += _bdot_tn(mb['merged'], dout)
        dmerged = _bdot(dout, wot_ref[...])
        dys = []
        for a, (dw_ref, wt_ref) in enumerate(((dwbm_ref, wbmt_ref), (dwbp_ref, wbpt_ref), (dwbg_ref, wbgt_ref))):
            g = mb['gs'][a]
            dzm_ref[:, a * D:(a + 1) * D] = (dmerged * mb['ps'][a] * g * (1.0 - g)).astype(BF16)
            dp = dmerged * g
            dw_ref[...] += _bdot_tn(mb['ys'][a], dp)
            dys.append(_bdot(dp, wt_ref[...]))
        dom_ref[...] = dys[0] * _silu(mb['zgm'])
        dzgm_ref[...] = (dys[0] * mb['om'] * _dsilu(mb['zgm'])).astype(BF16)
        dyp_ref[...] = dys[1]
        gn = mb['gn']
        dgn = jnp.zeros((1, GLA_DV), F32)
        dzgg, dog = [], []
        for h in range(GLA_H):
            sl = slice(h * GLA_DV, (h + 1) * GLA_DV)
            dyg = dys[2][:, sl]
            hat = mb['hats'][h]
            dzgg.append(dyg * hat * gn * _dsilu(mb['zgg'][:, sl]))
            dn = dyg * mb['sgg'][:, sl]
            dgn = dgn + _colsum(dn * hat)
            dog.append(_norm_bwd(hat, mb['rs'][h], dn * gn))
        dgn_ref[...] += dgn
        dzgg_ref[...] = jnp.concatenate(dzgg, axis=1).astype(BF16)
        dog_ref[...] = jnp.concatenate(dog, axis=1)

    row = lambda i: (i, 0)
    const = lambda i: (0, 0)
    wspec = pl.BlockSpec((512, D), const)
    wtspec = pl.BlockSpec((D, 512), const)
    return pl.pallas_call(
        body, name=name, grid=(nt,),
        in_specs=_merge_in_specs(tpe) + [pl.BlockSpec((TM, D), row), pl.BlockSpec((TM, D), row),
                                         wtspec, wtspec, wtspec,
                                         pl.BlockSpec((D, D), const)],
        out_specs=[pl.BlockSpec((TM, 3 * D), row), pl.BlockSpec((TM, 512), row), pl.BlockSpec((TM, 512), row),
                   pl.BlockSpec((TM, 512), row), pl.BlockSpec((TM, 512), row), pl.BlockSpec((TM, 512), row),
                   pl.BlockSpec((1, 8, D), lambda i: (i, 0, 0)),
                   wspec, wspec, wspec, pl.BlockSpec((D, D), const), pl.BlockSpec((1, 128), const)],
        out_shape=(jax.ShapeDtypeStruct((n, 3 * D), BF16), jax.ShapeDtypeStruct((n, 512), F32),
                   jax.ShapeDtypeStruct((n, 512), BF16), jax.ShapeDtypeStruct((n, 512), F32),
                   jax.ShapeDtypeStruct((n, 512), F32), jax.ShapeDtypeStruct((n, 512), BF16),
                   jax.ShapeDtypeStruct((nt, 8, D), F32),
                   jax.ShapeDtypeStruct((512, D), F32), jax.ShapeDtypeStruct((512, D), F32),
                   jax.ShapeDtypeStruct((512, D), F32), jax.ShapeDtypeStruct((D, D), F32),
                   jax.ShapeDtypeStruct((1, 128), F32)),
        compiler_params=_cp(56, ("arbitrary",)),
    )(z, z, z, o_mla, y_pool, ogf, ogb, gla_n, wbm, wbp, wbg, modl.reshape(8, 1, 3 * D), post_g,
      dxn, out, wbm_t, wbp_t, wbg_t, wout_t)


def _loss_grad(xf, tgt, nb, tpe):
    n = xf.shape[0]

    def body(x_ref, t_ref, dx_ref, l_ref):
        j = pl.program_id(1)
        d = x_ref[...] - t_ref[...]
        latent = j > 0
        dx_ref[...] = jnp.where(latent, d * (1.0 / D), 0.0)
        l_ref[...] = jnp.full(l_ref.shape, jnp.where(latent, 0.5 / D * jnp.sum(d * d), 0.0), F32)

    return pl.pallas_call(
        body, name="loss_grad", grid=(nb, tpe),
        in_specs=[pl.BlockSpec((TM, D), lambda b, j: (b * tpe + j, 0)),
                  pl.BlockSpec((TM, D), lambda b, j: (b * (tpe - 1) + jnp.maximum(j - 1, 0), 0))],
        out_specs=[pl.BlockSpec((TM, D), lambda b, j: (b * tpe + j, 0)),
                   pl.BlockSpec((1, 8, 128), lambda b, j: (b * tpe + j, 0, 0))],
        out_shape=(jax.ShapeDtypeStruct((n, D), F32), jax.ShapeDtypeStruct((n // TM, 8, 128), F32)),
        compiler_params=_cp(32, ("arbitrary", "arbitrary")),
    )(xf, tgt)


def _to_padded(w_nat):
    parts = []
    for nme in PAD_ORDER:
        p = w_nat[..., NAT_OFF[nme]:NAT_OFF[nme] + NAT_SIZE[nme]]
        if SLAB[nme] > NAT_SIZE[nme]:
            p = jnp.pad(p, [(0, 0)] * (p.ndim - 1) + [(IN_SLAB[nme], SLAB[nme] - NAT_SIZE[nme] - IN_SLAB[nme])])
        parts.append(p)
    return jnp.concatenate(parts, axis=-1)


def _from_padded(w_pad):
    return jnp.concatenate([w_pad[..., PAD_OFF[nme] + IN_SLAB[nme]:PAD_OFF[nme] + IN_SLAB[nme] + NAT_SIZE[nme]]
                            for nme in IN_NAMES], axis=-1)


def _rope_tables(lc, l):
    half = MLA_ROPE // 2
    inv = ROPE_BASE ** (-jnp.arange(0, half, 2, dtype=F32) / half)
    tok = jnp.arange(l)
    ang_r = (tok // GRID_W).astype(F32)[:, None] * inv
    ang_c = (tok % GRID_W).astype(F32)[:, None] * inv
    ang = jnp.concatenate([ang_r, ang_r, ang_c, ang_c], axis=-1)
    cos = jnp.concatenate([jnp.ones((lc, MLA_ROPE), F32), jnp.cos(ang)], axis=0)
    sin = jnp.concatenate([jnp.zeros((lc, MLA_ROPE), F32), jnp.sin(ang)], axis=0)
    t = lc + l
    tail = MLA_HP - MLA_QK
    ck = jnp.concatenate([jnp.ones((t, MLA_NOPE), F32), cos, jnp.ones((t, tail), F32)], axis=1)
    sk = jnp.concatenate([jnp.zeros((t, MLA_NOPE), F32), sin, jnp.zeros((t, tail), F32)], axis=1)
    return jnp.tile(ck, (1, MLA_H)), jnp.tile(sk, (1, MLA_H)), ck, sk


def _pad_heads(w):
    lead = w.shape[:-1]
    w = w.reshape(lead + (MLA_H, MLA_QK))
    return jnp.pad(w, [(0, 0)] * len(lead) + [(0, 0), (0, MLA_HP - MLA_QK)]).reshape(lead + (MLA_H * MLA_HP,))


def _unpad_heads(w):
    lead = w.shape[:-1]
    return w.reshape(lead + (MLA_H, MLA_HP))[..., :MLA_QK].reshape(lead + (MLA_H * MLA_QK,))


def _local_step(x, c, ctx, tgt, wf):
    nb, l, _ = x.shape
    lc = ctx.shape[1]
    assert lc == TM and l % TM == 0
    t = lc + l
    tpe = t // TM
    n = nb * t
    nt = n // TM
    bf = lambda a: a.astype(BF16)

    xs = jnp.concatenate([ctx, x], axis=1).reshape(n, D)
    assert nb <= 4
    cv = jnp.concatenate([c, jnp.zeros((4 - nb, D), F32), wf['c_ctx'][None, :], jnp.zeros((3, D), F32)], axis=0)
    mod_w_b = bf(wf['mod_w'])
    mod_all = _mod_fwd(cv, mod_w_b, wf['mod_b'].reshape(DEPTH, 1, 3 * D))
    cq, sq, ck, sk = _rope_tables(lc, l)
    pos = jnp.concatenate([jnp.arange(lc), jnp.arange(l)]).astype(jnp.int32)[:, None]
    seglen = jnp.concatenate([jnp.full((lc,), lc), jnp.full((l,), l)]).astype(jnp.int32)[:, None]
    tiles = np.arange(nt)
    ntp = -(-nt // LANES) * LANES
    sel = np.zeros((8, ntp), np.float32)
    sel[np.where(tiles % tpe == 0, 4, tiles // tpe), tiles] = 1.0
    sel = jnp.asarray(sel)

    def tile_sums(st):
        return jnp.pad(st.transpose(1, 0, 2), ((0, 0), (0, ntp - nt), (0, 0)))

    lw = []
    for ly in range(DEPTH):
        w_in_p = _to_padded(bf(wf['w_in'][ly]))
        w_uq_p = _pad_heads(bf(wf['mla_w_uq'][ly]))
        w2 = jnp.pad(jnp.stack([bf(wf['gla_af_w2'][ly]), bf(wf['gla_ab_w2'][ly])]),
                     ((0, 0), (0, LANES - GLA_RANK), (0, 0)))
        lw.append(dict(
            w_in=w_in_p, w_in_t=w_in_p.T,
            w_uq=w_uq_p, w_uq_t=w_uq_p.T,
            w_ukv=bf(wf['mla_w_ukv'][ly]), w_ukv_t=bf(wf['mla_w_ukv'][ly]).T,
            pool_w=bf(wf['pool_w'][ly]), pool_w_t=bf(wf['pool_w'][ly]).transpose(0, 2, 1),
            w2=w2, w2_t=w2.transpose(0, 2, 1),
            b2=jnp.stack([wf['gla_af_b'][ly], wf['gla_ab_b'][ly]]).reshape(2, 1, GLA_H * GLA_DK),
            wbm=bf(wf['w_branch_mla'][ly]), wbp=bf(wf['w_branch_pool'][ly]), wbg=bf(wf['w_branch_gla'][ly]),
            wout=bf(wf['w_out'][ly]),
            wbm_t=bf(wf['w_branch_mla'][ly]).T, wbp_t=bf(wf['w_branch_pool'][ly]).T,
            wbg_t=bf(wf['w_branch_gla'][ly]).T, wout_t=bf(wf['w_out'][ly]).T,
            pre_g=wf['pre_norm'][ly][None, :], post_g=wf['post_norm'][ly][None, :],
            q_norm=wf['mla_q_norm'][ly][None, :], kv_norm=wf['mla_kv_norm'][ly][None, :],
            pool_scale=wf['pool_scale'][ly][None, :], gla_norm=wf['gla_norm'][ly][None, :]))

    saved = []
    xcur = xs
    for ly in range(DEPTH):
        w = lw[ly]
        z, h = _pre_fwd(xcur, mod_all[ly], w['pre_g'], w['w_in'], tpe, f"pre_fwd{ly}")
        qb, kvb, krb = _mla_pre(z, w['q_norm'], w['kv_norm'], w['w_uq'], w['w_ukv'], cq, sq, ck, sk, tpe, f"mla_pre{ly}")
        o_mla, lse = _attn_fwd(qb, kvb, krb, nb, lc, f"attn_fwd{ly}")
        y_pool = _pool_fwd(z, w['pool_w'], w['pool_scale'], pos, seglen, nb, f"pool_fwd{ly}")
        ogf, ogb, st_f, st_r = _gla_fwd(z, w['w2'], w['b2'], nb, lc, f"gla_fwd{ly}")
        xnew, out = _merge_fwd(xcur, z, o_mla, y_pool, ogf, ogb, w['gla_norm'], w['wbm'], w['wbp'], w['wbg'],
                               w['wout'], mod_all[ly], w['post_g'], tpe, f"merge_fwd{ly}")
        saved.append(dict(x=xcur, z=z, h=h, qb=qb, kvb=kvb, krb=krb, lse=lse, o_mla=o_mla, y_pool=y_pool,
                          st_f=st_f, st_r=st_r, ogf=ogf, ogb=ogb, out=out))
        xcur = xnew

    dxcur, lparts = _loss_grad(xcur, tgt.reshape(nb * l, D), nb, tpe)
    loss = jnp.sum(lparts[:, 0, 0])

    g = {k: [None] * DEPTH for k in WEIGHTS if k != 'c_ctx'}
    dcv = jnp.zeros((8, D), F32)
    dcc = None
    for ly in reversed(range(DEPTH)):
        w, s = lw[ly], saved[ly]
        (dzm, dom, dzgm, dyp, dog, dzgg, st_b, g['w_branch_mla'][ly], g['w_branch_pool'][ly], g['w_branch_gla'][ly],
         g['w_out'][ly], dgn) = _merge_bwd(
            dxcur, s['out'], s['z'], s['o_mla'], s['y_pool'], s['ogf'], s['ogb'], w['gla_norm'], w['wbm'], w['wbp'],
            w['wbg'], w['wbm_t'], w['wbp_t'], w['wbg_t'], w['wout_t'], mod_all[ly], w['post_g'], tpe,
            f"merge_bwd{ly}")
        g['gla_norm'][ly] = dgn[0]
        dq, dkv, dkr = _attn_bwd(s['qb'], s['kvb'], s['krb'], s['o_mla'], s['lse'], dom, nb, lc, f"attn_bwd{ly}")
        dzq, dzkv, dzkr, dwq, g['mla_w_ukv'][ly], dgq, dgkv = _mla_pre_bwd(
            s['z'], dq, dkv, dkr, w['q_norm'], w['kv_norm'], w['w_uq_t'], w['w_ukv_t'], cq, sq, ck, sk, tpe,
            f"mla_pre_bwd{ly}")
        g['mla_w_uq'][ly] = _unpad_heads(dwq)
        g['mla_q_norm'][ly], g['mla_kv_norm'][ly] = dgq[0], dgkv[0]
        dzpx, dzpg, g['pool_w'][ly], dps = _pool_bwd(s['z'], dyp, w['pool_w'], w['pool_w_t'], w['pool_scale'],
                                                     pos, seglen, nb, f"pool_bwd{ly}")
        g['pool_scale'][ly] = dps[0]
        (dq_f, dk_f, dv_f, da_f, dq_r, dk_r, dv_r, da_r, dw2_f, db2_f, dw2_r, db2_r) = _gla_bwd(
            s['z'], w['w2'], w['w2_t'], w['b2'], s['st_f'], s['st_r'], dog, nb, lc, f"gla_bwd{ly}")
        dzgq = _add_cast(dq_f, dq_r, f"gla_dq{ly}")
        dzgk = _add_cast(dk_f, dk_r, f"gla_dk{ly}")
        dzgv = _add_cast(dv_f, dv_r, f"gla_dv{ly}")
        g['gla_af_w2'][ly], g['gla_ab_w2'][ly] = dw2_f[:GLA_RANK], dw2_r[:GLA_RANK]
        g['gla_af_b'][ly], g['gla_ab_b'][ly] = db2_f[0], db2_r[0]
        parts = dict(merge=dzm, mla_gate=dzgm, mla_q=dzq, mla_kv=dzkv, mla_kr=dzkr, pool_x=dzpx, pool_gate=dzpg,
                     gla_v=dzgv, gla_gate=dzgg, gla_q=dzgq, gla_k=dzgk, gla_af=bf(da_f), gla_ab=bf(da_r))
        dz = jnp.concatenate([parts[nme] for nme in PAD_ORDER], axis=1)
        dxcur, st_a = _pre_bwd(dz, w['w_in_t'], s['x'], dxcur, mod_all[ly], w['pre_g'], tpe, f"pre_bwd{ly}")
        g['w_in'][ly] = _from_padded(_matmul_tn(s['h'], dz, 768, 512 if n % 512 == 0 else TM, f"w_in_grad{ly}"))
        dmw, dmb, dcv, dcc, dpre, dpost = _mod_bwd(cv, sel, tile_sums(st_a), tile_sums(st_b),
                                                   mod_w_b[ly].T, dcv, f"mod_bwd{ly}")
        g['mod_w'][ly], g['mod_b'][ly] = dmw, dmb[0]
        g['pre_norm'][ly], g['post_norm'][ly] = dpre[0], dpost[0]

    grads = {k: jnp.stack(v) for k, v in g.items()}
    grads['c_ctx'] = dcc[4]
    grad_x = dxcur.reshape(nb, t, D)[:, lc:, :]
    return loss, grad_x, grads


def _place():
    x, y, c = lax.axis_index("x"), lax.axis_index("y"), lax.axis_index("c")
    chips = [(1 - x, y), (x, 1 - y), (1 - x, 1 - y)]
    return x, y, c, chips


def _hbm_call(body, name, out_shape, n_in, sems):
    any_spec = pl.BlockSpec(memory_space=pl.ANY)
    return pl.pallas_call(body, name=name, out_shape=out_shape, in_specs=[any_spec] * n_in,
                          out_specs=jax.tree.map(lambda _: any_spec, out_shape), scratch_shapes=sems)


def _all_gather_shards(wp):
    def body(wp_ref, out_ref, send_sems, recv_sems):
        x, y, c, chips = _place()

        def copy(k, chip, to, src=None):
            dst = out_ref.at[2 * chip[0] + chip[1], c]
            return pltpu.make_async_remote_copy(src_ref=dst if src is None else src, dst_ref=dst,
                                                send_sem=send_sems.at[k], recv_sem=recv_sems.at[k],
                                                device_id=to, device_id_type=MESH)

        first = [copy(j, (x, y), (*chip, c), src=wp_ref.at[c]) for j, chip in enumerate(chips)]
        for cp in first:
            cp.start()
        passed = [copy(3 + j, chip, (x, y, 1 - c)) for j, chip in enumerate(chips)]
        for j, chip in enumerate(chips):
            copy(j, chip, (x, y, c)).wait_recv()
            passed[j].start()
        for j, chip in enumerate(chips):
            dst = out_ref.at[2 * chip[0] + chip[1], 1 - c]
            pltpu.make_async_remote_copy(src_ref=dst, dst_ref=dst, send_sem=send_sems.at[3 + j],
                                         recv_sem=recv_sems.at[3 + j], device_id=(x, y, 1 - c),
                                         device_id_type=MESH).wait_recv()
        for cp in first + passed:
            cp.wait_send()

    return _hbm_call(body, "all_gather_shards", jax.ShapeDtypeStruct((N_CHIPS,) + wp.shape, wp.dtype), 1,
                     [pltpu.SemaphoreType.DMA((6,)), pltpu.SemaphoreType.DMA((6,))])(wp)


def _to_sibling(a, name):
    def body(a_ref, got_ref, send_sem, recv_sem):
        x, y, c, _ = _place()
        cp = pltpu.make_async_remote_copy(src_ref=a_ref, dst_ref=got_ref, send_sem=send_sem, recv_sem=recv_sem,
                                          device_id=(x, y, 1 - c), device_id_type=MESH)
        cp.start()
        cp.wait()

    return _hbm_call(body, name, jax.ShapeDtypeStruct(a.shape, a.dtype), 1,
                     [pltpu.SemaphoreType.DMA, pltpu.SemaphoreType.DMA])(a)


def _scatter_to_chips(hs):
    def body(h_ref, out_ref, send_sems, recv_sems):
        x, y, c, chips = _place()
        me = 2 * x + y
        sends = []
        for j, chip in enumerate(chips):
            cp = pltpu.make_async_remote_copy(src_ref=h_ref.at[2 * chip[0] + chip[1]], dst_ref=out_ref.at[me],
                                              send_sem=send_sems.at[j], recv_sem=recv_sems.at[j],
                                              device_id=(*chip, c), device_id_type=MESH)
            cp.start()
            sends.append(cp)
        for j, chip in enumerate(chips):
            dst = out_ref.at[2 * chip[0] + chip[1]]
            pltpu.make_async_remote_copy(src_ref=dst, dst_ref=dst, send_sem=send_sems.at[j], recv_sem=recv_sems.at[j],
                                         device_id=(*chip, c), device_id_type=MESH).wait_recv()
        for cp in sends:
            cp.wait_send()

    return _hbm_call(body, "scatter_to_chips", jax.ShapeDtypeStruct(hs.shape, hs.dtype), 1,
                     [pltpu.SemaphoreType.DMA((3,)), pltpu.SemaphoreType.DMA((3,))])(hs)


def _row_block(r):
    for br in (2048, 1024, 512, 256, 128, 64, 32, 16, 8):
        if r % br == 0:
            return br
    raise ValueError(r)


def _add2(a, b, name):
    s, r, _ = a.shape
    br = _row_block(r)

    def body(a_ref, b_ref, o_ref):
        o_ref[...] = a_ref[...] + b_ref[...]

    spec = pl.BlockSpec((1, br, LANES), lambda i, j: (i, j, 0))
    return pl.pallas_call(body, name=name, grid=(s, r // br), in_specs=[spec, spec], out_specs=spec,
                          out_shape=jax.ShapeDtypeStruct(a.shape, F32))(a, b)


def _sum_chips(own, got):
    _, r, _ = own.shape
    br = _row_block(r)

    def body(own_ref, got_ref, o_ref):
        me = 2 * lax.axis_index("x") + lax.axis_index("y")
        part = [jnp.where(me == j, own_ref[j], got_ref[j]) for j in range(N_CHIPS)]
        o_ref[...] = ((part[0] + part[1]) + part[2]) + part[3]

    spec = pl.BlockSpec((N_CHIPS, br, LANES), lambda j: (0, j, 0))
    return pl.pallas_call(body, name="sum_chips", grid=(r // br,), in_specs=[spec, spec],
                          out_specs=pl.BlockSpec((br, LANES), lambda j: (j, 0)),
                          out_shape=jax.ShapeDtypeStruct((r, LANES), F32))(own, got)


def _adamw(w, g, m, v):
    r, _ = w.shape
    br = _row_block(r)

    def body(w_ref, g_ref, m_ref, v_ref, d_ref, nm_ref, nv_ref):
        gv = g_ref[...]
        m2 = ADAM_B1 * m_ref[...] + (1.0 - ADAM_B1) * gv
        v2 = ADAM_B2 * v_ref[...] + (1.0 - ADAM_B2) * jnp.square(gv)
        m_hat = m2 / (1.0 - ADAM_B1 ** ADAM_STEP)
        v_hat = v2 / (1.0 - ADAM_B2 ** ADAM_STEP)
        d_ref[...] = -ADAM_LR * (m_hat / (jnp.sqrt(v_hat) + ADAM_EPS) + ADAM_WD * w_ref[...])
        nm_ref[...] = m2
        nv_ref[...] = v2

    spec = pl.BlockSpec((br, LANES), lambda j: (j, 0))
    shp = jax.ShapeDtypeStruct(w.shape, F32)
    return pl.pallas_call(body, name="adamw", grid=(r // br,), in_specs=[spec] * 4, out_specs=[spec] * 3,
                          out_shape=(shp, shp, shp))(w, g, m, v)


PACK_UNIT = 2 * LANES * 512


def _pack(tensors, dtype):
    flat = jnp.concatenate([a.reshape(-1).astype(dtype) for a in tensors])
    p = -(-flat.shape[0] // PACK_UNIT) * PACK_UNIT
    return jnp.pad(flat, (0, p - flat.shape[0])).reshape(2, p // (2 * LANES), LANES)


def _unpack(packed, shapes):
    flat = packed.reshape(-1)
    out, off = [], 0
    for shp in shapes:
        size = int(np.prod(shp))
        out.append(flat[off:off + size].reshape(shp))
        off += size
    return out


def _shard_of(a, axis, s):
    w = a.shape[axis] // N_CHIPS
    return lax.slice_in_dim(a, s * w, (s + 1) * w, axis=axis)


def kernel(x, c, ctx, c_ctx, mod_w, mod_b, pre_norm, post_norm, w_in, mla_q_norm, mla_w_uq, mla_kv_norm, mla_w_ukv, pool_w, pool_scale, gla_af_w2, gla_af_b, gla_ab_w2, gla_ab_b, gla_norm, w_branch_mla, w_branch_pool, w_branch_gla, w_out, loss_target, m_c_ctx, m_mod_w, m_mod_b, m_pre_norm, m_post_norm, m_w_in, m_mla_q_norm, m_mla_w_uq, m_mla_kv_norm, m_mla_w_ukv, m_pool_w, m_pool_scale, m_gla_af_w2, m_gla_af_b, m_gla_ab_w2, m_gla_ab_b, m_gla_norm, m_w_branch_mla, m_w_branch_pool, m_w_branch_gla, m_w_out, v_c_ctx, v_mod_w, v_mod_b, v_pre_norm, v_post_norm, v_w_in, v_mla_q_norm, v_mla_w_uq, v_mla_kv_norm, v_mla_w_ukv, v_pool_w, v_pool_scale, v_gla_af_w2, v_gla_af_b, v_gla_ab_w2, v_gla_ab_b, v_gla_norm, v_w_branch_mla, v_w_branch_pool, v_w_branch_gla, v_w_out):
    given = dict(locals())
    wts = {k: given[k] for k in WEIGHTS}
    order = [k for k, _ in SHARDED] + list(REPLICATED)
    shard_shapes = [wts[k].shape for k in order]
    my_chip = 2 * lax.axis_index("x") + lax.axis_index("y")
    my_core = lax.axis_index("c")

    sharded_names = [k for k, _ in SHARDED]
    mine = _pack([wts[k] for k in sharded_names], BF16)
    gathered = _all_gather_shards(mine)
    full = dict(wts)
    per_chip = [_unpack(jnp.where(my_chip == s, mine, gathered[s]), [wts[k].shape for k in sharded_names])
                for s in range(N_CHIPS)]
    for i, (k, axis) in enumerate(SHARDED):
        full[k] = jnp.concatenate([per_chip[s][i] for s in range(N_CHIPS)], axis=axis)

    loss_local, grad_x, grads = _local_step(x, c, ctx, loss_target, full)
    loss = lax.psum(loss_local, ("x", "y", "c"))

    axes = dict(SHARDED)
    packs = [_pack([_shard_of(grads[k], axes[k], s) if k in axes else grads[k] for k in order], F32)
             for s in range(N_CHIPS)]
    gpack = jnp.stack(packs, axis=1)
    own = lax.dynamic_index_in_dim(gpack, my_core, 0, keepdims=False)
    got = _to_sibling(lax.dynamic_index_in_dim(gpack, 1 - my_core, 0, keepdims=False), "swap_halves")
    chip_sum = _add2(own, got, "add_cores")
    reduced_half = _sum_chips(chip_sum, _scatter_to_chips(chip_sum))
    other_half = _to_sibling(reduced_half, "join_halves")
    gsum = jnp.where(my_core == 0, jnp.stack([reduced_half, other_half]), jnp.stack([other_half, reduced_half]))

    r2 = gsum.shape[0] * gsum.shape[1]
    pk = lambda pre: _pack([given[pre + k] for k in order], F32).reshape(r2, LANES)
    delta, new_m, new_v = _adamw(pk(''), gsum.reshape(r2, LANES), pk('m_'), pk('v_'))
    outs = {}
    for label, arr in (('grad', gsum), ('delta', delta), ('new_m', new_m), ('new_v', new_v)):
        outs[label] = dict(zip(order, _unpack(arr, shard_shapes)))
    return (loss, grad_x, *[outs[lab][k] for lab in ('grad', 'delta', 'new_m', 'new_v') for k in WEIGHTS])
```

```python
import functools

import numpy as np
import jax
import jax.numpy as jnp
from jax import lax
from jax.experimental import pallas as pl
from jax.experimental.pallas import tpu as pltpu

F32 = jnp.float32
BF16 = jnp.bfloat16
HIGHEST = lax.Precision.HIGHEST
MESH = pl.DeviceIdType.MESH

D = 1024
DEPTH = 2
EPS = 1e-6
GRID_W = 64
MLA_H, MLA_NOPE, MLA_ROPE, MLA_V = 8, 64, 32, 64
MLA_QK = MLA_NOPE + MLA_ROPE
ROPE_BASE = 10000.0
POOL_WINDOWS = (2, 4, 8, 16)
GLA_H, GLA_DK, GLA_DV, GLA_RANK, GLA_TAU, GLA_C = 4, 64, 128, 16, 16.0, 64
EXP_CLAMP = 80.0
ADAM_LR, ADAM_B1, ADAM_B2, ADAM_EPS, ADAM_WD, ADAM_STEP = 0.001, 0.9, 0.999, 1e-08, 0.01, 10

TM = 256
LANES = 128
N_CHIPS = 4

IN_NAMES = ('mla_q', 'mla_kv', 'mla_kr', 'mla_gate', 'pool_x', 'pool_gate',
            'gla_q', 'gla_k', 'gla_v', 'gla_af', 'gla_ab', 'gla_gate', 'merge')
IN_SIZES = (256, 128, 32, 512, 512, 512, 256, 256, 512, 16, 16, 512, 3 * D)
NAT_OFF = dict(zip(IN_NAMES, [int(o) for o in np.cumsum((0,) + IN_SIZES[:-1])]))
NAT_SIZE = dict(zip(IN_NAMES, IN_SIZES))
PAD_ORDER = ('merge', 'mla_gate', 'mla_q', 'mla_kv', 'mla_kr', 'pool_x', 'pool_gate',
             'gla_v', 'gla_gate', 'gla_q', 'gla_k', 'gla_af', 'gla_ab')
SLAB = {n: max(NAT_SIZE[n], LANES) for n in IN_NAMES}
PAD_OFF = dict(zip(PAD_ORDER, [int(o) for o in np.cumsum([0] + [SLAB[n] for n in PAD_ORDER[:-1]])]))
D_PAD = sum(SLAB.values())
IN_SLAB = {n: 0 for n in IN_NAMES}
IN_SLAB['mla_kr'] = MLA_NOPE
MLA_HP = 128


def _blk(name):
    return PAD_OFF[name] // SLAB[name]


SHARDED = (('mod_w', 2), ('w_in', 2), ('mla_w_uq', 2), ('mla_w_ukv', 2), ('gla_af_w2', 2), ('gla_ab_w2', 2),
           ('w_branch_mla', 2), ('w_branch_pool', 2), ('w_branch_gla', 2), ('w_out', 1))
REPLICATED = ('c_ctx', 'mod_b', 'pre_norm', 'post_norm', 'mla_q_norm', 'mla_kv_norm', 'pool_w', 'pool_scale',
              'gla_af_b', 'gla_ab_b', 'gla_norm')
WEIGHTS = ('c_ctx', 'mod_w', 'mod_b', 'pre_norm', 'post_norm', 'w_in', 'mla_q_norm', 'mla_w_uq', 'mla_kv_norm',
           'mla_w_ukv', 'pool_w', 'pool_scale', 'gla_af_w2', 'gla_af_b', 'gla_ab_w2', 'gla_ab_b', 'gla_norm',
           'w_branch_mla', 'w_branch_pool', 'w_branch_gla', 'w_out')


def _cp(vmem_mb=None, sem=None):
    kw = {}
    if vmem_mb is not None:
        kw['vmem_limit_bytes'] = vmem_mb * 1024 * 1024
    if sem is not None:
        kw['dimension_semantics'] = sem
    return pltpu.CompilerParams(**kw)


def _bdot(a, b):
    return jnp.dot(a.astype(BF16), b.astype(BF16), preferred_element_type=F32)


def _bdot_nt(a, b):
    return lax.dot_general(a.astype(BF16), b.astype(BF16), (((1,), (1,)), ((), ())), preferred_element_type=F32)


def _bdot_tn(a, b):
    return lax.dot_general(a.astype(BF16), b.astype(BF16), (((0,), (0,)), ((), ())), preferred_element_type=F32)


def _xdot(a, b):
    return jnp.dot(a, b, precision=HIGHEST, preferred_element_type=F32)


def _xdot_tn(a, b):
    return lax.dot_general(a, b, (((0,), (0,)), ((), ())), precision=HIGHEST, preferred_element_type=F32)


NN = (((1,), (0,)), ((), ()))
NT = (((1,), (1,)), ((), ()))
TN = (((0,), (0,)), ((), ()))


def _split(a):
    hi = a.astype(BF16)
    return hi, (a - hi.astype(F32)).astype(BF16)


def _dot3(a, b, dims):
    (ah, al), (bh, bl) = a, b
    f = lambda u, v: lax.dot_general(u, v, dims, preferred_element_type=F32)
    return f(ah, bh) + (f(ah, bl) + f(al, bh))


def _sigmoid(x):
    return jax.nn.sigmoid(x)


def _silu(x):
    return x * _sigmoid(x)


def _dsilu(x):
    s = _sigmoid(x)
    return s * (1.0 + x * (1.0 - s))


def _rstd(x):
    return lax.rsqrt(jnp.mean(x * x, axis=-1, keepdims=True) + EPS)


def _norm_bwd(xhat, r, dy):
    return r * (dy - xhat * jnp.mean(xhat * dy, axis=-1, keepdims=True))


def _colsum(a):
    return jnp.sum(a, axis=0, keepdims=True)


def _modrow(i, tpe):
    return jnp.where(i % tpe == 0, 4, i // tpe)


def _rot(x):
    n = x.shape[-1]
    lane = lax.broadcasted_iota(jnp.int32, x.shape, x.ndim - 1)
    return jnp.where(lane % 16 < 8, -pltpu.roll(x, n - 8, x.ndim - 1), pltpu.roll(x, 8, x.ndim - 1))


def _mod_fwd(cv, mod_w, mod_b):
    def body(cv_ref, w_ref, b_ref, o_ref):
        s = _silu(cv_ref[...])
        for l in range(DEPTH):
            o_ref[l] = _bdot(s, w_ref[l]) + b_ref[l]

    return pl.pallas_call(body, name="mod_fwd", out_shape=jax.ShapeDtypeStruct((DEPTH, 8, 3 * D), F32),
                          compiler_params=_cp(40))(cv, mod_w, mod_b)


def _mod_bwd(cv, sel, st_a, st_b, w_t, dcv_in, name):
    def body(cv_ref, sel_ref, sa_ref, sb_ref, wt_ref, dcin_ref, dw_ref, db_ref, dcv_ref, dcc_ref, dpre_ref, dpost_ref):
        cvv = cv_ref[...]
        s = _silu(cvv)
        sel_v = sel_ref[...]
        dmod = jnp.concatenate([_xdot(sel_v, sa_ref[0]), _xdot(sel_v, sa_ref[1]), _xdot(sel_v, sb_ref[0])], axis=1)
        dw_ref[...] = _bdot_tn(s, dmod)
        db_ref[...] = _colsum(dmod)
        dcv = dcin_ref[...] + _bdot(dmod, wt_ref[...])
        dcv_ref[...] = dcv
        dcc_ref[...] = dcv * _dsilu(cvv)
        dpre_ref[...] = _colsum(sa_ref[2])
        dpost_ref[...] = _colsum(sb_ref[1])

    shapes = (jax.ShapeDtypeStruct((D, 3 * D), F32), jax.ShapeDtypeStruct((1, 3 * D), F32),
              jax.ShapeDtypeStruct((8, D), F32), jax.ShapeDtypeStruct((8, D), F32),
              jax.ShapeDtypeStruct((1, D), F32), jax.ShapeDtypeStruct((1, D), F32))
    return pl.pallas_call(body, name=name, out_shape=shapes, compiler_params=_cp(48))(cv, sel, st_a, st_b, w_t, dcv_in)


def _pre_fwd(x, modl, pre_g, w, tpe, name):
    n = x.shape[0]
    nt = n // TM
    ncb = 3
    tn = D_PAD // ncb

    def body(x_ref, m_ref, g_ref, w_ref, z_ref, h_ref):
        xv = x_ref[...]
        m = m_ref[0]
        h = xv * _rstd(xv) * g_ref[...] * (1.0 + m[:, D:2 * D]) + m[:, 0:D]
        hb = h.astype(BF16)

        @pl.when(pl.program_id(1) == 0)
        def _():
            h_ref[...] = hb

        z_ref[...] = jnp.dot(hb, w_ref[...], preferred_element_type=F32)

    return pl.pallas_call(
        body, name=name, grid=(nt, ncb),
        in_specs=[pl.BlockSpec((TM, D), lambda i, j: (i, 0)),
                  pl.BlockSpec((1, 1, 3 * D), lambda i, j: (_modrow(i, tpe), 0, 0)),
                  pl.BlockSpec((1, D), lambda i, j: (0, 0)),
                  pl.BlockSpec((D, tn), lambda i, j: (0, j))],
        out_specs=[pl.BlockSpec((TM, tn), lambda i, j: (i, j)),
                   pl.BlockSpec((TM, D), lambda i, j: (i, 0))],
        out_shape=(jax.ShapeDtypeStruct((n, D_PAD), F32), jax.ShapeDtypeStruct((n, D), BF16)),
        compiler_params=_cp(48, ("arbitrary", "arbitrary")),
    )(x, modl.reshape(8, 1, 3 * D), pre_g, w)


def _pre_bwd(dz, w_t, x, dxres, modl, pre_g, tpe, name):
    n = x.shape[0]
    nt = n // TM

    def body(dz_ref, wt_ref, x_ref, dr_ref, m_ref, g_ref, dx_ref, st_ref):
        dh = jnp.dot(dz_ref[...], wt_ref[...], preferred_element_type=F32)
        xv = x_ref[...]
        r = _rstd(xv)
        xn = xv * r
        m = m_ref[0]
        sc1 = 1.0 + m[:, D:2 * D]
        g = g_ref[...]
        st_ref[0, 0:1, :] = _colsum(dh)
        st_ref[0, 1:2, :] = _colsum(dh * xn * g)
        st_ref[0, 2:3, :] = _colsum(dh * xn * sc1)
        st_ref[0, 3:8, :] = jnp.zeros((5, D), F32)
        dx_ref[...] = dr_ref[...] + _norm_bwd(xn, r, dh * g * sc1)

    return pl.pallas_call(
        body, name=name, grid=(nt,),
        in_specs=[pl.BlockSpec((TM, D_PAD), lambda i: (i, 0)),
                  pl.BlockSpec((D_PAD, D), lambda i: (0, 0)),
                  pl.BlockSpec((TM, D), lambda i: (i, 0)),
                  pl.BlockSpec((TM, D), lambda i: (i, 0)),
                  pl.BlockSpec((1, 1, 3 * D), lambda i: (_modrow(i, tpe), 0, 0)),
                  pl.BlockSpec((1, D), lambda i: (0, 0))],
        out_specs=[pl.BlockSpec((TM, D), lambda i: (i, 0)),
                   pl.BlockSpec((1, 8, D), lambda i: (i, 0, 0))],
        out_shape=(jax.ShapeDtypeStruct((n, D), F32), jax.ShapeDtypeStruct((nt, 8, D), F32)),
        compiler_params=_cp(56, ("arbitrary",)),
    )(dz, w_t, x, dxres, modl.reshape(8, 1, 3 * D), pre_g)


def _matmul_tn(a, b, tn, tk, name):
    n, k1 = a.shape
    k2 = b.shape[1]

    def body(a_ref, b_ref, o_ref):
        @pl.when(pl.program_id(1) == 0)
        def _():
            o_ref[...] = jnp.zeros(o_ref.shape, F32)

        o_ref[...] += lax.dot_general(a_ref[...], b_ref[...], (((0,), (0,)), ((), ())), preferred_element_type=F32)

    return pl.pallas_call(
        body, name=name, grid=(k2 // tn, n // tk),
        in_specs=[pl.BlockSpec((tk, k1), lambda j, k: (k, 0)), pl.BlockSpec((tk, tn), lambda j, k: (k, j))],
        out_specs=pl.BlockSpec((k1, tn), lambda j, k: (0, j)),
        out_shape=jax.ShapeDtypeStruct((k1, k2), F32),
        compiler_params=_cp(48, ("arbitrary", "arbitrary")),
    )(a, b)


def _mla_pre(z, q_norm, kv_norm, w_uq, w_ukv, cq, sq, ck, sk, tpe, name):
    n = z.shape[0]
    nt = n // TM
    scale = MLA_QK ** -0.5

    def body(zq_ref, zkv_ref, zkr_ref, gq_ref, gkv_ref, wq_ref, wkv_ref, cq_ref, sq_ref, ck_ref, sk_ref,
             q_ref, kv_ref, kr_ref):
        zq = zq_ref[...]
        qn = zq * _rstd(zq) * gq_ref[...]
        qraw = _bdot(qn, wq_ref[...])
        q_ref[...] = ((qraw * cq_ref[...] + _rot(qraw) * sq_ref[...]) * scale).astype(BF16)
        zkv = zkv_ref[...]
        kvn = zkv * _rstd(zkv) * gkv_ref[...]
        kv_ref[...] = _bdot(kvn, wkv_ref[...]).astype(BF16)
        zkr = zkr_ref[...]
        kr_ref[...] = (zkr * ck_ref[...] + _rot(zkr) * sk_ref[...]).astype(BF16)

    hq, hkv = MLA_H * MLA_HP, MLA_H * (MLA_NOPE + MLA_V)
    const = lambda i: (0, 0)
    tab = lambda i: (i % tpe, 0)
    return pl.pallas_call(
        body, name=name, grid=(nt,),
        in_specs=[pl.BlockSpec((TM, 256), lambda i: (i, _blk('mla_q'))),
                  pl.BlockSpec((TM, 128), lambda i: (i, _blk('mla_kv'))),
                  pl.BlockSpec((TM, 128), lambda i: (i, _blk('mla_kr'))),
                  pl.BlockSpec((1, 256), const), pl.BlockSpec((1, 128), const),
                  pl.BlockSpec((256, hq), const), pl.BlockSpec((128, hkv), const),
                  pl.BlockSpec((TM, hq), tab), pl.BlockSpec((TM, hq), tab),
                  pl.BlockSpec((TM, 128), tab), pl.BlockSpec((TM, 128), tab)],
        out_specs=[pl.BlockSpec((TM, hq), lambda i: (i, 0)), pl.BlockSpec((TM, hkv), lambda i: (i, 0)),
                   pl.BlockSpec((TM, 128), lambda i: (i, 0))],
        out_shape=(jax.ShapeDtypeStruct((n, hq), BF16), jax.ShapeDtypeStruct((n, hkv), BF16),
                   jax.ShapeDtypeStruct((n, 128), BF16)),
        compiler_params=_cp(32, ("arbitrary",)),
    )(z, z, z, q_norm, kv_norm, w_uq, w_ukv, cq, sq, ck, sk)


def _mla_pre_bwd(z, dq, dkv, dkr, q_norm, kv_norm, w_uq_t, w_ukv_t, cq, sq, ck, sk, tpe, name):
    n = z.shape[0]
    nt = n // TM
    scale = MLA_QK ** -0.5
    hq, hkv = MLA_H * MLA_HP, MLA_H * (MLA_NOPE + MLA_V)

    def body(zq_ref, zkv_ref, dq_ref, dkv_ref, dkr_ref, gq_ref, gkv_ref, wqt_ref, wkvt_ref, cq_ref, sq_ref,
             ck_ref, sk_ref, dzq_ref, dzkv_ref, dzkr_ref, dwq_ref, dwkv_ref, dgq_ref, dgkv_ref):
        @pl.when(pl.program_id(0) == 0)
        def _():
            dwq_ref[...] = jnp.zeros(dwq_ref.shape, F32)
            dwkv_ref[...] = jnp.zeros(dwkv_ref.shape, F32)
            dgq_ref[...] = jnp.zeros(dgq_ref.shape, F32)
            dgkv_ref[...] = jnp.zeros(dgkv_ref.shape, F32)

        zq = zq_ref[...]
        rq = _rstd(zq)
        qhat = zq * rq
        gq = gq_ref[...]
        dqs = dq_ref[...] * scale
        dqraw = dqs * cq_ref[...] - _rot(dqs * sq_ref[...])
        dwq_ref[...] += _bdot_tn(qhat * gq, dqraw)
        dqn = _bdot(dqraw, wqt_ref[...])
        dgq_ref[...] += _colsum(dqn * qhat)
        dzq_ref[...] = _norm_bwd(qhat, rq, dqn * gq).astype(BF16)

        zkv = zkv_ref[...]
        rkv = _rstd(zkv)
        khat = zkv * rkv
        gkv = gkv_ref[...]
        dkvv = dkv_ref[...]
        dwkv_ref[...] += _bdot_tn(khat * gkv, dkvv)
        dkvn = _bdot(dkvv, wkvt_ref[...])
        dgkv_ref[...] += _colsum(dkvn * khat)
        dzkv_ref[...] = _norm_bwd(khat, rkv, dkvn * gkv).astype(BF16)

        dkr = dkr_ref[...]
        dzkr_ref[...] = (dkr * ck_ref[...] - _rot(dkr * sk_ref[...])).astype(BF16)

    const = lambda i: (0, 0)
    tab = lambda i: (i % tpe, 0)
    row = lambda i: (i, 0)
    return pl.pallas_call(
        body, name=name, grid=(nt,),
        in_specs=[pl.BlockSpec((TM, 256), lambda i: (i, _blk('mla_q'))),
                  pl.BlockSpec((TM, 128), lambda i: (i, _blk('mla_kv'))),
                  pl.BlockSpec((TM, hq), row), pl.BlockSpec((TM, hkv), row), pl.BlockSpec((TM, 128), row),
                  pl.BlockSpec((1, 256), const), pl.BlockSpec((1, 128), const),
                  pl.BlockSpec((hq, 256), const), pl.BlockSpec((hkv, 128), const),
                  pl.BlockSpec((TM, hq), tab), pl.BlockSpec((TM, hq), tab),
                  pl.BlockSpec((TM, 128), tab), pl.BlockSpec((TM, 128), tab)],
        out_specs=[pl.BlockSpec((TM, 256), row), pl.BlockSpec((TM, 128), row), pl.BlockSpec((TM, 128), row),
                   pl.BlockSpec((256, hq), const), pl.BlockSpec((128, hkv), const),
                   pl.BlockSpec((1, 256), const), pl.BlockSpec((1, 128), const)],
        out_shape=(jax.ShapeDtypeStruct((n, 256), BF16), jax.ShapeDtypeStruct((n, 128), BF16),
                   jax.ShapeDtypeStruct((n, 128), BF16), jax.ShapeDtypeStruct((256, hq), F32),
                   jax.ShapeDtypeStruct((128, hkv), F32), jax.ShapeDtypeStruct((1, 256), F32),
                   jax.ShapeDtypeStruct((1, 128), F32)),
        compiler_params=_cp(32, ("arbitrary",)),
    )(z, z, dq, dkv, dkr, q_norm, kv_norm, w_uq_t, w_ukv_t, cq, sq, ck, sk)


def _attn_head(q_ref, kv_ref, kr_ref, hh, j, lc):
    t = kv_ref.shape[0]
    kvh = kv_ref[:, hh * MLA_HP:(hh + 1) * MLA_HP]
    lane = lax.broadcasted_iota(jnp.int32, kvh.shape, 1)
    kh = jnp.where(lane < MLA_NOPE, kvh, kr_ref[...])
    qh = q_ref[:, hh * MLA_HP:(hh + 1) * MLA_HP]
    s = lax.dot_general(qh, kh, (((1,), (1,)), ((), ())), preferred_element_type=F32)
    col = lax.broadcasted_iota(jnp.int32, s.shape, 1)
    limit = jnp.where(j == 0, lc, t)
    return kvh, kh, qh, jnp.where(col < limit, s, -1e30)


def _attn_specs(nb, tpe, t):
    tile = lambda b, p, j: (b * tpe + j, p)
    return [pl.BlockSpec((TM, 2 * MLA_HP), tile),
            pl.BlockSpec((t, 2 * MLA_HP), lambda b, p, j: (b, p)),
            pl.BlockSpec((t, MLA_HP), lambda b, p, j: (b, 0))]


def _attn_fwd(q, kv, kr, nb, lc, name):
    n = q.shape[0]
    t = n // nb
    tpe = t // TM

    def body(q_ref, kv_ref, kr_ref, o_ref, lse_ref):
        j = pl.program_id(2)
        lane = lax.broadcasted_iota(jnp.int32, (TM, MLA_HP), 1)
        res, lses = [], []
        for hh in range(2):
            kvh, _, _, s = _attn_head(q_ref, kv_ref, kr_ref, hh, j, lc)
            m = jnp.max(s, axis=-1, keepdims=True)
            p = jnp.exp(s - m)
            l = jnp.sum(p, axis=-1, keepdims=True)
            res.append(jnp.dot(p.astype(BF16), kvh, preferred_element_type=F32) / l)
            lses.append(m + jnp.log(l))
        o_ref[...] = jnp.where(lane < MLA_V, pltpu.roll(res[0], MLA_V, 1), res[1])
        lane2 = lax.broadcasted_iota(jnp.int32, (TM, 2), 1)
        lse_ref[0] = jnp.where(lane2 == 0, lses[0], lses[1])

    return pl.pallas_call(
        body, name=name, grid=(nb, MLA_H // 2, tpe),
        in_specs=_attn_specs(nb, tpe, t),
        out_specs=[pl.BlockSpec((TM, 2 * MLA_V), lambda b, p, j: (b * tpe + j, p)),
                   pl.BlockSpec((1, TM, 2), lambda b, p, j: (p, b * tpe + j, 0))],
        out_shape=(jax.ShapeDtypeStruct((n, MLA_H * MLA_V), F32), jax.ShapeDtypeStruct((MLA_H // 2, n, 2), F32)),
        compiler_params=_cp(48, ("arbitrary", "arbitrary", "arbitrary")),
    )(q, kv, kr)


def _attn_bwd(q, kv, kr, o, lse, do, nb, lc, name):
    n = q.shape[0]
    t = n // nb
    tpe = t // TM

    def body(q_ref, kv_ref, kr_ref, o_ref, lse_ref, do_ref, dq_ref, dkv_ref, dkr_ref):
        p_id, j = pl.program_id(1), pl.program_id(2)

        @pl.when(j == 0)
        def _():
            dkv_ref[...] = jnp.zeros(dkv_ref.shape, F32)

        @pl.when((j == 0) & (p_id == 0))
        def _():
            dkr_ref[...] = jnp.zeros(dkr_ref.shape, F32)

        lane = lax.broadcasted_iota(jnp.int32, (TM, MLA_HP), 1)
        lane_t = lax.broadcasted_iota(jnp.int32, (t, MLA_HP), 1)
        lane2 = lax.broadcasted_iota(jnp.int32, (TM, 2), 1)
        lse = lse_ref[0]
        dov, ov = do_ref[...], o_ref[...]
        dkr = jnp.zeros((t, MLA_HP), F32)
        for hh in range(2):
            kvh, kh, qh, s = _attn_head(q_ref, kv_ref, kr_ref, hh, j, lc)
            p = jnp.exp(s - jnp.sum(jnp.where(lane2 == hh, lse, 0.0), axis=1, keepdims=True))
            do_pos = jnp.where(lane >= MLA_NOPE, pltpu.roll(dov, MLA_V, 1) if hh == 0 else dov, 0.0)
            o_pos = jnp.where(lane >= MLA_NOPE, pltpu.roll(ov, MLA_V, 1) if hh == 0 else ov, 0.0)
            delta = jnp.sum(do_pos * o_pos, axis=-1, keepdims=True)
            dob = do_pos.astype(BF16)
            dp = lax.dot_general(dob, kvh, (((1,), (1,)), ((), ())), preferred_element_type=F32)
            ds = (p * (dp - delta)).astype(BF16)
            dq_ref[:, hh * MLA_HP:(hh + 1) * MLA_HP] = jnp.dot(ds, kh, preferred_element_type=F32)
            dkf = lax.dot_general(ds, qh, (((0,), (0,)), ((), ())), preferred_element_type=F32)
            dvp = lax.dot_general(p.astype(BF16), dob, (((0,), (0,)), ((), ())), preferred_element_type=F32)
            dkv_ref[:, hh * MLA_HP:(hh + 1) * MLA_HP] += jnp.where(lane_t < MLA_NOPE, dkf, dvp)
            dkr = dkr + jnp.where(lane_t >= MLA_NOPE, dkf, 0.0)
        dkr_ref[...] += dkr

    tile = lambda b, p, j: (b * tpe + j, p)
    return pl.pallas_call(
        body, name=name, grid=(nb, MLA_H // 2, tpe),
        in_specs=_attn_specs(nb, tpe, t) + [pl.BlockSpec((TM, 2 * MLA_V), tile),
                                            pl.BlockSpec((1, TM, 2), lambda b, p, j: (p, b * tpe + j, 0)),
                                            pl.BlockSpec((TM, 2 * MLA_V), tile)],
        out_specs=[pl.BlockSpec((TM, 2 * MLA_HP), tile),
                   pl.BlockSpec((t, 2 * MLA_HP), lambda b, p, j: (b, p)),
                   pl.BlockSpec((t, MLA_HP), lambda b, p, j: (b, 0))],
        out_shape=(jax.ShapeDtypeStruct((n, MLA_H * MLA_HP), F32), jax.ShapeDtypeStruct((n, MLA_H * MLA_HP), F32),
                   jax.ShapeDtypeStruct((n, MLA_HP), F32)),
        compiler_params=_cp(56, ("arbitrary", "arbitrary", "arbitrary")),
    )(q, kv, kr, o, lse, do)


def _pool_window(ug, pos, seglen, w, transpose):
    t = ug.shape[0]
    cnt = (jnp.minimum(pos + w // 2, seglen) - jnp.maximum(pos - w // 2, 0)).astype(F32)
    if transpose:
        ug = ug / cnt
    acc = jnp.zeros_like(ug)
    for j in range(-(w // 2), w // 2):
        jj = -j if transpose else j
        src = pos + jj
        valid = (src >= 0) & (src < seglen)
        acc = acc + jnp.where(valid, pltpu.roll(ug, (-jj) % t, 0), 0.0)
    return acc if transpose else acc / cnt


def _by_group(g, fn):
    for k, w in enumerate(POOL_WINDOWS):
        pl.when(g == k)(functools.partial(fn, w))


def _pool_specs(t):
    px, pg = PAD_OFF['pool_x'] // LANES, PAD_OFF['pool_gate'] // LANES
    return [pl.BlockSpec((t, LANES), lambda g, b: (b, px + g)),
            pl.BlockSpec((t, LANES), lambda g, b: (b, pg + g)),
            pl.BlockSpec((1, LANES, LANES), lambda g, b: (g, 0, 0)),
            pl.BlockSpec((1, LANES), lambda g, b: (0, g)),
            pl.BlockSpec((t, 1), lambda g, b: (0, 0)), pl.BlockSpec((t, 1), lambda g, b: (0, 0))]


def _pool_fwd(z, pool_w, pool_scale, pos, seglen, nb, name):
    n = z.shape[0]
    t = n // nb

    def body(u_ref, zg_ref, pw_ref, ps_ref, pos_ref, sl_ref, y_ref):
        def run(w):
            u = u_ref[...]
            pooled = _pool_window(u, pos_ref[...], sl_ref[...], w, False) - u
            y_ref[...] = (_bdot(pooled, pw_ref[0]) * ps_ref[...] * _silu(zg_ref[...])).astype(BF16)

        _by_group(pl.program_id(0), run)

    return pl.pallas_call(
        body, name=name, grid=(4, nb), in_specs=_pool_specs(t),
        out_specs=pl.BlockSpec((t, LANES), lambda g, b: (b, g)),
        out_shape=jax.ShapeDtypeStruct((n, 512), BF16),
        compiler_params=_cp(48, ("arbitrary", "arbitrary")),
    )(z, z, pool_w, pool_scale, pos, seglen)


def _pool_bwd(z, dy, pool_w, pool_w_t, pool_scale, pos, seglen, nb, name):
    n = z.shape[0]
    t = n // nb

    def body(u_ref, zg_ref, pw_ref, ps_ref, pos_ref, sl_ref, dy_ref, pwt_ref, du_ref, dg_ref, dpw_ref, dps_ref):
        @pl.when(pl.program_id(1) == 0)
        def _():
            dpw_ref[...] = jnp.zeros(dpw_ref.shape, F32)
            dps_ref[...] = jnp.zeros(dps_ref.shape, F32)

        def run(w):
            u = u_ref[...]
            pos_v, sl_v = pos_ref[...], sl_ref[...]
            pooled = _pool_window(u, pos_v, sl_v, w, False) - u
            mixed = _bdot(pooled, pw_ref[0])
            zg = zg_ref[...]
            sg = _silu(zg)
            ps = ps_ref[...]
            dyv = dy_ref[...]
            dps_ref[...] += _colsum(dyv * mixed * sg)
            dg_ref[...] = (dyv * mixed * ps * _dsilu(zg)).astype(BF16)
            dmixed = dyv * ps * sg
            dpw_ref[0] += _bdot_tn(pooled, dmixed)
            dpooled = _bdot(dmixed, pwt_ref[0])
            du_ref[...] = (_pool_window(dpooled, pos_v, sl_v, w, True) - dpooled).astype(BF16)

        _by_group(pl.program_id(0), run)

    blk = pl.BlockSpec((t, LANES), lambda g, b: (b, g))
    return pl.pallas_call(
        body, name=name, grid=(4, nb),
        in_specs=_pool_specs(t) + [blk, pl.BlockSpec((1, LANES, LANES), lambda g, b: (g, 0, 0))],
        out_specs=[blk, blk, pl.BlockSpec((1, LANES, LANES), lambda g, b: (g, 0, 0)),
                   pl.BlockSpec((1, LANES), lambda g, b: (0, g))],
        out_shape=(jax.ShapeDtypeStruct((n, 512), BF16), jax.ShapeDtypeStruct((n, 512), BF16),
                   jax.ShapeDtypeStruct((4, 128, 128), F32), jax.ShapeDtypeStruct((1, 512), F32)),
        compiler_params=_cp(48, ("arbitrary", "arbitrary")),
    )(z, z, pool_w, pool_scale, pos, seglen, dy, pool_w_t)


def _gla_chunk(q_ref, k_ref, a_ref, w2_ref, b2_ref, reverse):
    c = GLA_C
    x = _bdot(a_ref[...], w2_ref[0]) + b2_ref[0]
    la = (jnp.minimum(x, 0.0) - jnp.log(1.0 + jnp.exp(-jnp.abs(x)))) * (1.0 / GLA_TAU)
    row = lax.broadcasted_iota(jnp.int32, (c, c), 0)
    col = lax.broadcasted_iota(jnp.int32, (c, c), 1)
    tri = (col >= row) if reverse else (col <= row)
    tri_t = (col <= row) if reverse else (col >= row)
    b = _xdot(tri.astype(F32), la)
    tok = lax.broadcasted_iota(jnp.int32, la.shape, 0)
    bref = _colsum(jnp.where((tok >= c // 2) if reverse else (tok < c // 2), la, 0.0))
    blast = _colsum(la)
    eq = jnp.exp(jnp.minimum(b - bref, EXP_CLAMP))
    ek = jnp.exp(jnp.minimum(bref - b, EXP_CLAMP))
    qs = q_ref[...] * (GLA_DK ** -0.5)
    kk = k_ref[...]
    eb = jnp.exp(b)
    etail = jnp.exp(blast - b)
    return dict(x=x, la=la, tri=tri, tri_t=tri_t, eq=eq, ek=ek, qs=qs, kk=kk, qd=qs * eq, kd=kk * ek, qe=qs * eb,
                kl=kk * etail, eb=eb, etail=etail)


def _pair(a, p):
    return a[:, p * LANES:(p + 1) * LANES]


def _head_masks():
    lane = lax.broadcasted_iota(jnp.int32, (GLA_C, LANES), 1)
    return (lane < GLA_DK, lane >= GLA_DK)


def _state_decay(la, p):
    return jnp.exp(_xdot_tn(_pair(la, p), jnp.ones((GLA_C, GLA_DV), F32)))


def _gla_chunk_maps(nb, nc, ncc, order):
    def rmap(j):
        return jnp.where(j < ncc, ncc - 1 - j, nc - 1 - (j - ncc))

    if order == 'scan':
        return (lambda b, j: b * nc + j), (lambda b, j: b * nc + rmap(j))
    return (lambda b, j: b * nc + nc - 1 - j), (lambda b, j: b * nc + rmap(nc - 1 - j))


def _gla_in_specs(maps):
    specs = []
    for d, cm in enumerate(maps):
        gate = 'gla_af' if d == 0 else 'gla_ab'
        specs += [pl.BlockSpec((GLA_C, 256), lambda b, j, cm=cm: (cm(b, j), _blk('gla_q'))),
                  pl.BlockSpec((GLA_C, 256), lambda b, j, cm=cm: (cm(b, j), _blk('gla_k'))),
                  pl.BlockSpec((GLA_C, 512), lambda b, j, cm=cm: (cm(b, j), _blk('gla_v'))),
                  pl.BlockSpec((GLA_C, LANES), lambda b, j, cm=cm, gate=gate: (cm(b, j), _blk(gate))),
                  pl.BlockSpec((1, LANES, 256), lambda b, j, d=d: (d, 0, 0)),
                  pl.BlockSpec((1, 1, 256), lambda b, j, d=d: (d, 0, 0))]
    return specs


def _gla_fwd(z, w2, b2, nb, lc, name):
    n = z.shape[0]
    nc = n // nb // GLA_C
    maps = _gla_chunk_maps(nb, nc, lc // GLA_C, 'scan')

    def body(*refs):
        ins, (of_ref, ob_ref, sf_ref, sb_ref, s_sc) = refs[:12], refs[12:]

        @pl.when(pl.program_id(1) == 0)
        def _():
            s_sc[...] = jnp.zeros(s_sc.shape, F32)

        masks = _head_masks()
        for d, (o_ref, st_ref) in enumerate(((of_ref, sf_ref), (ob_ref, sb_ref))):
            q_ref, k_ref, v_ref, a_ref, w2_ref, b2_ref = ins[6 * d:6 * d + 6]
            ch = _gla_chunk(q_ref, k_ref, a_ref, w2_ref, b2_ref, d == 1)
            for p in range(2):
                s_prev = s_sc[d, p]
                st_ref[0, p] = s_prev
                s_new = _state_decay(ch['la'], p) * s_prev
                kd_p = _pair(ch['kd'], p)
                for hh in range(2):
                    h = 2 * p + hh
                    vv = v_ref[:, h * GLA_DV:(h + 1) * GLA_DV]
                    att = jnp.where(ch['tri'], _bdot_nt(jnp.where(masks[hh], _pair(ch['qd'], p), 0.0), kd_p), 0.0)
                    o_ref[:, h * GLA_DV:(h + 1) * GLA_DV] = (
                        _bdot(att, vv) + _bdot(jnp.where(masks[hh], _pair(ch['qe'], p), 0.0), s_prev))
                    s_new = s_new + _dot3(_split(jnp.where(masks[hh], _pair(ch['kl'], p), 0.0)), _split(vv), TN)
                s_sc[d, p] = s_new

    o_shape = jax.ShapeDtypeStruct((n, 512), F32)
    st_shape = jax.ShapeDtypeStruct((n // GLA_C, 2, LANES, GLA_DV), F32)
    return pl.pallas_call(
        body, name=name, grid=(nb, nc),
        in_specs=_gla_in_specs(maps),
        out_specs=[pl.BlockSpec((GLA_C, 512), lambda b, j: (maps[0](b, j), 0)),
                   pl.BlockSpec((GLA_C, 512), lambda b, j: (maps[1](b, j), 0)),
                   pl.BlockSpec((1, 2, LANES, GLA_DV), lambda b, j: (maps[0](b, j), 0, 0, 0)),
                   pl.BlockSpec((1, 2, LANES, GLA_DV), lambda b, j: (maps[1](b, j), 0, 0, 0))],
        out_shape=(o_shape, o_shape, st_shape, st_shape),
        scratch_shapes=[pltpu.VMEM((2, 2, LANES, GLA_DV), F32)],
        compiler_params=_cp(32, ("arbitrary", "arbitrary")),
    )(z, z, z, z, w2, b2, z, z, z, z, w2, b2)


def _gla_bwd(z, w2, w2_t, b2, st_f, st_b, dog, nb, lc, name):
    n = z.shape[0]
    nc = n // nb // GLA_C
    maps = _gla_chunk_maps(nb, nc, lc // GLA_C, 'back')

    def body(*refs):
        ins, extra, outs, (ds_sc, sfx_sc) = refs[:12], refs[12:18], refs[18:30], refs[30:]

        @pl.when(pl.program_id(1) == 0)
        def _():
            ds_sc[...] = jnp.zeros(ds_sc.shape, F32)
            sfx_sc[...] = jnp.zeros(sfx_sc.shape, F32)

        @pl.when((pl.program_id(0) == 0) & (pl.program_id(1) == 0))
        def _():
            for r in outs[8:12]:
                r[...] = jnp.zeros(r.shape, F32)

        masks = _head_masks()
        for d in range(2):
            q_ref, k_ref, v_ref, a_ref, w2_ref, b2_ref = ins[6 * d:6 * d + 6]
            w2t_ref, st_ref, do_ref = extra[3 * d:3 * d + 3]
            dq_ref, dk_ref, dv_ref, da_ref = outs[4 * d:4 * d + 4]
            dw2_ref, db2_ref = outs[8 + 2 * d], outs[9 + 2 * d]
            ch = _gla_chunk(q_ref, k_ref, a_ref, w2_ref, b2_ref, d == 1)
            dqs, dks = [], []
            for p in range(2):
                s_prev = st_ref[0, p]
                ds_new = ds_sc[d, p]
                qd_p, kd_p, qe_p, kl_p = (_pair(ch[nme], p) for nme in ('qd', 'kd', 'qe', 'kl'))
                ds_prev = _state_decay(ch['la'], p) * ds_new
                qd_s, kd_s, sp_s, dsn_s = _split(qd_p), _split(kd_p), _split(s_prev), _split(ds_new)
                dq_h, dk_h = [], []
                for hh in range(2):
                    h = 2 * p + hh
                    vv = v_ref[:, h * GLA_DV:(h + 1) * GLA_DV]
                    dov = do_ref[:, h * GLA_DV:(h + 1) * GLA_DV]
                    att = jnp.where(ch['tri'], _bdot_nt(jnp.where(masks[hh], qd_p, 0.0), kd_p), 0.0)
                    dv_ref[:, h * GLA_DV:(h + 1) * GLA_DV] = (
                        _bdot_tn(att, dov) + _bdot(jnp.where(masks[hh], kl_p, 0.0), ds_new))
                    vv_s, dov_s = _split(vv), _split(dov)
                    datt_s = _split(jnp.where(ch['tri'], _dot3(dov_s, vv_s, NT), 0.0))
                    dq_h.append(_dot3(datt_s, kd_s, NN) * _pair(ch['eq'], p)
                                + _dot3(dov_s, sp_s, NT) * _pair(ch['eb'], p))
                    dk_h.append(_dot3(datt_s, qd_s, TN) * _pair(ch['ek'], p)
                                + _dot3(vv_s, dsn_s, NT) * _pair(ch['etail'], p))
                    ds_prev = ds_prev + _dot3(_split(jnp.where(masks[hh], qe_p, 0.0)), dov_s, TN)
                ds_sc[d, p] = ds_prev
                dqs.append(jnp.where(masks[0], dq_h[0], dq_h[1]))
                dks.append(jnp.where(masks[0], dk_h[0], dk_h[1]))
            dq = jnp.concatenate(dqs, axis=1)
            dk = jnp.concatenate(dks, axis=1)
            dq_ref[...] = dq * (GLA_DK ** -0.5)
            dk_ref[...] = dk
            db = ch['qs'] * dq - ch['kk'] * dk
            dla = _xdot(ch['tri_t'].astype(F32), db) + sfx_sc[d]
            sfx_sc[d] = sfx_sc[d] + _colsum(db)
            dx = dla * (1.0 / GLA_TAU) * _sigmoid(-ch['x'])
            da_ref[...] = _bdot(dx, w2t_ref[0])
            dw2_ref[...] += _bdot_tn(a_ref[...], dx)
            db2_ref[...] += _colsum(dx)

    extra_specs, out_specs = [], []
    for d, cm in enumerate(maps):
        extra_specs += [pl.BlockSpec((1, 256, LANES), lambda b, j, d=d: (d, 0, 0)),
                        pl.BlockSpec((1, 2, LANES, GLA_DV), lambda b, j, cm=cm: (cm(b, j), 0, 0, 0)),
                        pl.BlockSpec((GLA_C, 512), lambda b, j, cm=cm: (cm(b, j), 0))]
        out_specs += [pl.BlockSpec((GLA_C, 256), lambda b, j, cm=cm: (cm(b, j), 0)),
                      pl.BlockSpec((GLA_C, 256), lambda b, j, cm=cm: (cm(b, j), 0)),
                      pl.BlockSpec((GLA_C, 512), lambda b, j, cm=cm: (cm(b, j), 0)),
                      pl.BlockSpec((GLA_C, LANES), lambda b, j, cm=cm: (cm(b, j), 0))]
    const2 = lambda b, j: (0, 0)
    out_specs += [pl.BlockSpec((LANES, 256), const2), pl.BlockSpec((1, 256), const2)] * 2
    per_dir = (jax.ShapeDtypeStruct((n, 256), F32), jax.ShapeDtypeStruct((n, 256), F32),
               jax.ShapeDtypeStruct((n, 512), F32), jax.ShapeDtypeStruct((n, LANES), F32))
    wshape = (jax.ShapeDtypeStruct((LANES, 256), F32), jax.ShapeDtypeStruct((1, 256), F32))
    return pl.pallas_call(
        body, name=name, grid=(nb, nc),
        in_specs=_gla_in_specs(maps) + extra_specs,
        out_specs=out_specs,
        out_shape=per_dir + per_dir + wshape + wshape,
        scratch_shapes=[pltpu.VMEM((2, 2, LANES, GLA_DV), F32), pltpu.VMEM((2, 1, 256), F32)],
        compiler_params=_cp(32, ("arbitrary", "arbitrary")),
    )(z, z, z, z, w2, b2, z, z, z, z, w2, b2, w2_t, st_f, dog, w2_t, st_b, dog)


def _add_cast(a, b, name):
    n, w = a.shape

    def body(a_ref, b_ref, o_ref):
        o_ref[...] = (a_ref[...] + b_ref[...]).astype(BF16)

    return pl.pallas_call(
        body, name=name, grid=(n // TM,),
        in_specs=[pl.BlockSpec((TM, w), lambda i: (i, 0)), pl.BlockSpec((TM, w), lambda i: (i, 0))],
        out_specs=pl.BlockSpec((TM, w), lambda i: (i, 0)),
        out_shape=jax.ShapeDtypeStruct((n, w), BF16),
        compiler_params=_cp(32, ("arbitrary",)),
    )(a, b)


def _gla_out_norm(og):
    hats, rs = [], []
    for h in range(GLA_H):
        seg = og[:, h * GLA_DV:(h + 1) * GLA_DV]
        r = _rstd(seg)
        hats.append(seg * r)
        rs.append(r)
    return hats, rs


def _merge_branches(zm_ref, zgm_ref, zgg_ref, om_ref, yp_ref, ogf_ref, ogb_ref, gn_ref, wbm_ref, wbp_ref, wbg_ref):
    zgm, zgg = zgm_ref[...], zgg_ref[...]
    om = om_ref[...]
    y_mla = om * _silu(zgm)
    hats, rs = _gla_out_norm(ogf_ref[...] + ogb_ref[...])
    gn = gn_ref[...]
    sgg = _silu(zgg)
    y_gla = jnp.concatenate([hats[h] * gn for h in range(GLA_H)], axis=1) * sgg
    ys = (y_mla, yp_ref[...], y_gla)
    ps = (_bdot(y_mla, wbm_ref[...]), jnp.dot(yp_ref[...], wbp_ref[...], preferred_element_type=F32),
          _bdot(y_gla, wbg_ref[...]))
    zm = zm_ref[...]
    gs = tuple(_sigmoid(zm[:, a * D:(a + 1) * D]) for a in range(3))
    merged = gs[0] * ps[0] + gs[1] * ps[1] + gs[2] * ps[2]
    return dict(zgm=zgm, zgg=zgg, om=om, hats=hats, rs=rs, gn=gn, sgg=sgg, ys=ys, ps=ps, gs=gs, merged=merged)


def _merge_in_specs(tpe):
    row = lambda i: (i, 0)
    const = lambda i: (0, 0)
    return [pl.BlockSpec((TM, 3 * D), lambda i: (i, _blk('merge'))),
            pl.BlockSpec((TM, 512), lambda i: (i, _blk('mla_gate'))),
            pl.BlockSpec((TM, 512), lambda i: (i, _blk('gla_gate'))),
            pl.BlockSpec((TM, 512), row), pl.BlockSpec((TM, 512), row), pl.BlockSpec((TM, 512), row),
            pl.BlockSpec((TM, 512), row), pl.BlockSpec((1, 128), const),
            pl.BlockSpec((512, D), const), pl.BlockSpec((512, D), const), pl.BlockSpec((512, D), const),
            pl.BlockSpec((1, 1, 3 * D), lambda i: (_modrow(i, tpe), 0, 0)), pl.BlockSpec((1, D), const)]


def _merge_fwd(x, z, o_mla, y_pool, ogf, ogb, gla_n, wbm, wbp, wbg, wout, modl, post_g, tpe, name):
    n = x.shape[0]

    def body(zm_ref, zgm_ref, zgg_ref, om_ref, yp_ref, ogf_ref, ogb_ref, gn_ref, wbm_ref, wbp_ref, wbg_ref,
             m_ref, pg_ref, x_ref, wo_ref, xn_ref, out_ref):
        mb = _merge_branches(zm_ref, zgm_ref, zgg_ref, om_ref, yp_ref, ogf_ref, ogb_ref, gn_ref,
                             wbm_ref, wbp_ref, wbg_ref)
        out = _bdot(mb['merged'], wo_ref[...])
        gate = m_ref[0][:, 2 * D:3 * D]
        xn_ref[...] = x_ref[...] + gate * (out * _rstd(out) * pg_ref[...])
        out_ref[...] = out

    row = lambda i: (i, 0)
    return pl.pallas_call(
        body, name=name, grid=(n // TM,),
        in_specs=_merge_in_specs(tpe) + [pl.BlockSpec((TM, D), row), pl.BlockSpec((D, D), lambda i: (0, 0))],
        out_specs=[pl.BlockSpec((TM, D), row), pl.BlockSpec((TM, D), row)],
        out_shape=(jax.ShapeDtypeStruct((n, D), F32), jax.ShapeDtypeStruct((n, D), F32)),
        compiler_params=_cp(48, ("arbitrary",)),
    )(z, z, z, o_mla, y_pool, ogf, ogb, gla_n, wbm, wbp, wbg, modl.reshape(8, 1, 3 * D), post_g, x, wout)


def _merge_bwd(dxn, out, z, o_mla, y_pool, ogf, ogb, gla_n, wbm, wbp, wbg, wbm_t, wbp_t, wbg_t, wout_t,
               modl, post_g, tpe, name):
    n = out.shape[0]
    nt = n // TM

    def body(zm_ref, zgm_ref, zgg_ref, om_ref, yp_ref, ogf_ref, ogb_ref, gn_ref, wbm_ref, wbp_ref, wbg_ref,
             m_ref, pg_ref, dxn_ref, out_ref, wbmt_ref, wbpt_ref, wbgt_ref, wot_ref,
             dzm_ref, dom_ref, dzgm_ref, dyp_ref, dog_ref, dzgg_ref, st_ref,
             dwbm_ref, dwbp_ref, dwbg_ref, dwo_ref, dgn_ref):
        @pl.when(pl.program_id(0) == 0)
        def _():
            for r in (dwbm_ref, dwbp_ref, dwbg_ref, dwo_ref, dgn_ref):
                r[...] = jnp.zeros(r.shape, F32)

        mb = _merge_branches(zm_ref, zgm_ref, zgg_ref, om_ref, yp_ref, ogf_ref, ogb_ref, gn_ref,
                             wbm_ref, wbp_ref, wbg_ref)
        out = out_ref[...]
        r2 = _rstd(out)
        on = out * r2
        pg = pg_ref[...]
        gate = m_ref[0][:, 2 * D:3 * D]
        dxn_v = dxn_ref[...]
        st_ref[0, 0:1, :] = _colsum(dxn_v * on * pg)
        st_ref[0, 1:2, :] = _colsum(dxn_v * gate * on)
        st_ref[0, 2:8, :] = jnp.zeros((6, D), F32)
        dout = _norm_bwd(on, r2, dxn_v * gate * pg)
        dwo_ref[...] += _bdot_tn(mb['merged'], dout)
        dmerged = _bdot(dout, wot_ref[...])
        dys = []
        for a, (dw_ref, wt_ref) in enumerate(((dwbm_ref, wbmt_ref), (dwbp_ref, wbpt_ref), (dwbg_ref, wbgt_ref))):
            g = mb['gs'][a]
            dzm_ref[:, a * D:(a + 1) * D] = (dmerged * mb['ps'][a] * g * (1.0 - g)).astype(BF16)
            dp = dmerged * g
            dw_ref[...] += _bdot_tn(mb['ys'][a], dp)
            dys.append(_bdot(dp, wt_ref[...]))
        dom_ref[...] = dys[0] * _silu(mb['zgm'])
        dzgm_ref[...] = (dys[0] * mb['om'] * _dsilu(mb['zgm'])).astype(BF16)
        dyp_ref[...] = dys[1]
        gn = mb['gn']
        dgn = jnp.zeros((1, GLA_DV), F32)
        dzgg, dog = [], []
        for h in range(GLA_H):
            sl = slice(h * GLA_DV, (h + 1) * GLA_DV)
            dyg = dys[2][:, sl]
            hat = mb['hats'][h]
            dzgg.append(dyg * hat * gn * _dsilu(mb['zgg'][:, sl]))
            dn = dyg * mb['sgg'][:, sl]
            dgn = dgn + _colsum(dn * hat)
            dog.append(_norm_bwd(hat, mb['rs'][h], dn * gn))
        dgn_ref[...] += dgn
        dzgg_ref[...] = jnp.concatenate(dzgg, axis=1).astype(BF16)
        dog_ref[...] = jnp.concatenate(dog, axis=1)

    row = lambda i: (i, 0)
    const = lambda i: (0, 0)
    wspec = pl.BlockSpec((512, D), const)
    wtspec = pl.BlockSpec((D, 512), const)
    return pl.pallas_call(
        body, name=name, grid=(nt,),
        in_specs=_merge_in_specs(tpe) + [pl.BlockSpec((TM, D), row), pl.BlockSpec((TM, D), row),
                                         wtspec, wtspec, wtspec,
                                         pl.BlockSpec((D, D), const)],
        out_specs=[pl.BlockSpec((TM, 3 * D), row), pl.BlockSpec((TM, 512), row), pl.BlockSpec((TM, 512), row),
                   pl.BlockSpec((TM, 512), row), pl.BlockSpec((TM, 512), row), pl.BlockSpec((TM, 512), row),
                   pl.BlockSpec((1, 8, D), lambda i: (i, 0, 0)),
                   wspec, wspec, wspec, pl.BlockSpec((D, D), const), pl.BlockSpec((1, 128), const)],
        out_shape=(jax.ShapeDtypeStruct((n, 3 * D), BF16), jax.ShapeDtypeStruct((n, 512), F32),
                   jax.ShapeDtypeStruct((n, 512), BF16), jax.ShapeDtypeStruct((n, 512), F32),
                   jax.ShapeDtypeStruct((n, 512), F32), jax.ShapeDtypeStruct((n, 512), BF16),
                   jax.ShapeDtypeStruct((nt, 8, D), F32),
                   jax.ShapeDtypeStruct((512, D), F32), jax.ShapeDtypeStruct((512, D), F32),
                   jax.ShapeDtypeStruct((512, D), F32), jax.ShapeDtypeStruct((D, D), F32),
                   jax.ShapeDtypeStruct((1, 128), F32)),
        compiler_params=_cp(56, ("arbitrary",)),
    )(z, z, z, o_mla, y_pool, ogf, ogb, gla_n, wbm, wbp, wbg, modl.reshape(8, 1, 3 * D), post_g,
      dxn, out, wbm_t, wbp_t, wbg_t, wout_t)


def _loss_grad(xf, tgt, nb, tpe):
    n = xf.shape[0]

    def body(x_ref, t_ref, dx_ref, l_ref):
        j = pl.program_id(1)
        d = x_ref[...] - t_ref[...]
        latent = j > 0
        dx_ref[...] = jnp.where(latent, d * (1.0 / D), 0.0)
        l_ref[...] = jnp.full(l_ref.shape, jnp.where(latent, 0.5 / D * jnp.sum(d * d), 0.0), F32)

    return pl.pallas_call(
        body, name="loss_grad", grid=(nb, tpe),
        in_specs=[pl.BlockSpec((TM, D), lambda b, j: (b * tpe + j, 0)),
                  pl.BlockSpec((TM, D), lambda b, j: (b * (tpe - 1) + jnp.maximum(j - 1, 0), 0))],
        out_specs=[pl.BlockSpec((TM, D), lambda b, j: (b * tpe + j, 0)),
                   pl.BlockSpec((1, 8, 128), lambda b, j: (b * tpe + j, 0, 0))],
        out_shape=(jax.ShapeDtypeStruct((n, D), F32), jax.ShapeDtypeStruct((n // TM, 8, 128), F32)),
        compiler_params=_cp(32, ("arbitrary", "arbitrary")),
    )(xf, tgt)


def _to_padded(w_nat):
    parts = []
    for nme in PAD_ORDER:
        p = w_nat[..., NAT_OFF[nme]:NAT_OFF[nme] + NAT_SIZE[nme]]
        if SLAB[nme] > NAT_SIZE[nme]:
            p = jnp.pad(p, [(0, 0)] * (p.ndim - 1) + [(IN_SLAB[nme], SLAB[nme] - NAT_SIZE[nme] - IN_SLAB[nme])])
        parts.append(p)
    return jnp.concatenate(parts, axis=-1)


def _from_padded(w_pad):
    return jnp.concatenate([w_pad[..., PAD_OFF[nme] + IN_SLAB[nme]:PAD_OFF[nme] + IN_SLAB[nme] + NAT_SIZE[nme]]
                            for nme in IN_NAMES], axis=-1)


def _rope_tables(lc, l):
    half = MLA_ROPE // 2
    inv = ROPE_BASE ** (-jnp.arange(0, half, 2, dtype=F32) / half)
    tok = jnp.arange(l)
    ang_r = (tok // GRID_W).astype(F32)[:, None] * inv
    ang_c = (tok % GRID_W).astype(F32)[:, None] * inv
    ang = jnp.concatenate([ang_r, ang_r, ang_c, ang_c], axis=-1)
    cos = jnp.concatenate([jnp.ones((lc, MLA_ROPE), F32), jnp.cos(ang)], axis=0)
    sin = jnp.concatenate([jnp.zeros((lc, MLA_ROPE), F32), jnp.sin(ang)], axis=0)
    t = lc + l
    tail = MLA_HP - MLA_QK
    ck = jnp.concatenate([jnp.ones((t, MLA_NOPE), F32), cos, jnp.ones((t, tail), F32)], axis=1)
    sk = jnp.concatenate([jnp.zeros((t, MLA_NOPE), F32), sin, jnp.zeros((t, tail), F32)], axis=1)
    return jnp.tile(ck, (1, MLA_H)), jnp.tile(sk, (1, MLA_H)), ck, sk


def _pad_heads(w):
    lead = w.shape[:-1]
    w = w.reshape(lead + (MLA_H, MLA_QK))
    return jnp.pad(w, [(0, 0)] * len(lead) + [(0, 0), (0, MLA_HP - MLA_QK)]).reshape(lead + (MLA_H * MLA_HP,))


def _unpad_heads(w):
    lead = w.shape[:-1]
    return w.reshape(lead + (MLA_H, MLA_HP))[..., :MLA_QK].reshape(lead + (MLA_H * MLA_QK,))


def _local_step(x, c, ctx, tgt, wf):
    nb, l, _ = x.shape
    lc = ctx.shape[1]
    assert lc == TM and l % TM == 0
    t = lc + l
    tpe = t // TM
    n = nb * t
    nt = n // TM
    bf = lambda a: a.astype(BF16)

    xs = jnp.concatenate([ctx, x], axis=1).reshape(n, D)
    assert nb <= 4
    cv = jnp.concatenate([c, jnp.zeros((4 - nb, D), F32), wf['c_ctx'][None, :], jnp.zeros((3, D), F32)], axis=0)
    mod_w_b = bf(wf['mod_w'])
    mod_all = _mod_fwd(cv, mod_w_b, wf['mod_b'].reshape(DEPTH, 1, 3 * D))
    cq, sq, ck, sk = _rope_tables(lc, l)
    pos = jnp.concatenate([jnp.arange(lc), jnp.arange(l)]).astype(jnp.int32)[:, None]
    seglen = jnp.concatenate([jnp.full((lc,), lc), jnp.full((l,), l)]).astype(jnp.int32)[:, None]
    tiles = np.arange(nt)
    ntp = -(-nt // LANES) * LANES
    sel = np.zeros((8, ntp), np.float32)
    sel[np.where(tiles % tpe == 0, 4, tiles // tpe), tiles] = 1.0
    sel = jnp.asarray(sel)

    def tile_sums(st):
        return jnp.pad(st.transpose(1, 0, 2), ((0, 0), (0, ntp - nt), (0, 0)))

    lw = []
    for ly in range(DEPTH):
        w_in_p = _to_padded(bf(wf['w_in'][ly]))
        w_uq_p = _pad_heads(bf(wf['mla_w_uq'][ly]))
        w2 = jnp.pad(jnp.stack([bf(wf['gla_af_w2'][ly]), bf(wf['gla_ab_w2'][ly])]),
                     ((0, 0), (0, LANES - GLA_RANK), (0, 0)))
        lw.append(dict(
            w_in=w_in_p, w_in_t=w_in_p.T,
            w_uq=w_uq_p, w_uq_t=w_uq_p.T,
            w_ukv=bf(wf['mla_w_ukv'][ly]), w_ukv_t=bf(wf['mla_w_ukv'][ly]).T,
            pool_w=bf(wf['pool_w'][ly]), pool_w_t=bf(wf['pool_w'][ly]).transpose(0, 2, 1),
            w2=w2, w2_t=w2.transpose(0, 2, 1),
            b2=jnp.stack([wf['gla_af_b'][ly], wf['gla_ab_b'][ly]]).reshape(2, 1, GLA_H * GLA_DK),
            wbm=bf(wf['w_branch_mla'][ly]), wbp=bf(wf['w_branch_pool'][ly]), wbg=bf(wf['w_branch_gla'][ly]),
            wout=bf(wf['w_out'][ly]),
            wbm_t=bf(wf['w_branch_mla'][ly]).T, wbp_t=bf(wf['w_branch_pool'][ly]).T,
            wbg_t=bf(wf['w_branch_gla'][ly]).T, wout_t=bf(wf['w_out'][ly]).T,
            pre_g=wf['pre_norm'][ly][None, :], post_g=wf['post_norm'][ly][None, :],
            q_norm=wf['mla_q_norm'][ly][None, :], kv_norm=wf['mla_kv_norm'][ly][None, :],
            pool_scale=wf['pool_scale'][ly][None, :], gla_norm=wf['gla_norm'][ly][None, :]))

    saved = []
    xcur = xs
    for ly in range(DEPTH):
        w = lw[ly]
        z, h = _pre_fwd(xcur, mod_all[ly], w['pre_g'], w['w_in'], tpe, f"pre_fwd{ly}")
        qb, kvb, krb = _mla_pre(z, w['q_norm'], w['kv_norm'], w['w_uq'], w['w_ukv'], cq, sq, ck, sk, tpe, f"mla_pre{ly}")
        o_mla, lse = _attn_fwd(qb, kvb, krb, nb, lc, f"attn_fwd{ly}")
        y_pool = _pool_fwd(z, w['pool_w'], w['pool_scale'], pos, seglen, nb, f"pool_fwd{ly}")
        ogf, ogb, st_f, st_r = _gla_fwd(z, w['w2'], w['b2'], nb, lc, f"gla_fwd{ly}")
        xnew, out = _merge_fwd(xcur, z, o_mla, y_pool, ogf, ogb, w['gla_norm'], w['wbm'], w['wbp'], w['wbg'],
                               w['wout'], mod_all[ly], w['post_g'], tpe, f"merge_fwd{ly}")
        saved.append(dict(x=xcur, z=z, h=h, qb=qb, kvb=kvb, krb=krb, lse=lse, o_mla=o_mla, y_pool=y_pool,
                          st_f=st_f, st_r=st_r, ogf=ogf, ogb=ogb, out=out))
        xcur = xnew

    dxcur, lparts = _loss_grad(xcur, tgt.reshape(nb * l, D), nb, tpe)
    loss = jnp.sum(lparts[:, 0, 0])

    g = {k: [None] * DEPTH for k in WEIGHTS if k != 'c_ctx'}
    dcv = jnp.zeros((8, D), F32)
    dcc = None
    for ly in reversed(range(DEPTH)):
        w, s = lw[ly], saved[ly]
        (dzm, dom, dzgm, dyp, dog, dzgg, st_b, g['w_branch_mla'][ly], g['w_branch_pool'][ly], g['w_branch_gla'][ly],
         g['w_out'][ly], dgn) = _merge_bwd(
            dxcur, s['out'], s['z'], s['o_mla'], s['y_pool'], s['ogf'], s['ogb'], w['gla_norm'], w['wbm'], w['wbp'],
            w['wbg'], w['wbm_t'], w['wbp_t'], w['wbg_t'], w['wout_t'], mod_all[ly], w['post_g'], tpe,
            f"merge_bwd{ly}")
        g['gla_norm'][ly] = dgn[0]
        dq, dkv, dkr = _attn_bwd(s['qb'], s['kvb'], s['krb'], s['o_mla'], s['lse'], dom, nb, lc, f"attn_bwd{ly}")
        dzq, dzkv, dzkr, dwq, g['mla_w_ukv'][ly], dgq, dgkv = _mla_pre_bwd(
            s['z'], dq, dkv, dkr, w['q_norm'], w['kv_norm'], w['w_uq_t'], w['w_ukv_t'], cq, sq, ck, sk, tpe,
            f"mla_pre_bwd{ly}")
        g['mla_w_uq'][ly] = _unpad_heads(dwq)
        g['mla_q_norm'][ly], g['mla_kv_norm'][ly] = dgq[0], dgkv[0]
        dzpx, dzpg, g['pool_w'][ly], dps = _pool_bwd(s['z'], dyp, w['pool_w'], w['pool_w_t'], w['pool_scale'],
                                                     pos, seglen, nb, f"pool_bwd{ly}")
        g['pool_scale'][ly] = dps[0]
        (dq_f, dk_f, dv_f, da_f, dq_r, dk_r, dv_r, da_r, dw2_f, db2_f, dw2_r, db2_r) = _gla_bwd(
            s['z'], w['w2'], w['w2_t'], w['b2'], s['st_f'], s['st_r'], dog, nb, lc, f"gla_bwd{ly}")
        dzgq = _add_cast(dq_f, dq_r, f"gla_dq{ly}")
        dzgk = _add_cast(dk_f, dk_r, f"gla_dk{ly}")
        dzgv = _add_cast(dv_f, dv_r, f"gla_dv{ly}")
        g['gla_af_w2'][ly], g['gla_ab_w2'][ly] = dw2_f[:GLA_RANK], dw2_r[:GLA_RANK]
        g['gla_af_b'][ly], g['gla_ab_b'][ly] = db2_f[0], db2_r[0]
        parts = dict(merge=dzm, mla_gate=dzgm, mla_q=dzq, mla_kv=dzkv, mla_kr=dzkr, pool_x=dzpx, pool_gate=dzpg,
                     gla_v=dzgv, gla_gate=dzgg, gla_q=dzgq, gla_k=dzgk, gla_af=bf(da_f), gla_ab=bf(da_r))
        dz = jnp.concatenate([parts[nme] for nme in PAD_ORDER], axis=1)
        dxcur, st_a = _pre_bwd(dz, w['w_in_t'], s['x'], dxcur, mod_all[ly], w['pre_g'], tpe, f"pre_bwd{ly}")
        g['w_in'][ly] = _from_padded(_matmul_tn(s['h'], dz, 768, 512 if n % 512 == 0 else TM, f"w_in_grad{ly}"))
        dmw, dmb, dcv, dcc, dpre, dpost = _mod_bwd(cv, sel, tile_sums(st_a), tile_sums(st_b),
                                                   mod_w_b[ly].T, dcv, f"mod_bwd{ly}")
        g['mod_w'][ly], g['mod_b'][ly] = dmw, dmb[0]
        g['pre_norm'][ly], g['post_norm'][ly] = dpre[0], dpost[0]

    grads = {k: jnp.stack(v) for k, v in g.items()}
    grads['c_ctx'] = dcc[4]
    grad_x = dxcur.reshape(nb, t, D)[:, lc:, :]
    return loss, grad_x, grads


def _place():
    x, y, c = lax.axis_index("x"), lax.axis_index("y"), lax.axis_index("c")
    chips = [(1 - x, y), (x, 1 - y), (1 - x, 1 - y)]
    return x, y, c, chips


def _hbm_call(body, name, out_shape, n_in, sems):
    any_spec = pl.BlockSpec(memory_space=pl.ANY)
    return pl.pallas_call(body, name=name, out_shape=out_shape, in_specs=[any_spec] * n_in,
                          out_specs=jax.tree.map(lambda _: any_spec, out_shape), scratch_shapes=sems)


def _all_gather_shards(ws):
    n = len(ws)

    def body(*refs):
        ins, outs, (send_sems, recv_sems) = refs[:n], refs[n:2 * n], refs[2 * n:]
        x, y, c, chips = _place()

        def copy(k, q, chip, half, to, src=None):
            dst = outs[k].at[2 * chip[0] + chip[1], half]
            return pltpu.make_async_remote_copy(src_ref=dst if src is None else src, dst_ref=dst,
                                                send_sem=send_sems.at[k, q], recv_sem=recv_sems.at[k, q],
                                                device_id=to, device_id_type=MESH)

        first = [copy(k, j, (x, y), c, (*chip, c), src=ins[k].at[c]) for k in range(n) for j, chip in enumerate(chips)]
        for cp in first:
            cp.start()
        passed = []
        for k in range(n):
            for j, chip in enumerate(chips):
                copy(k, j, chip, c, (x, y, c)).wait_recv()
                passed.append(copy(k, 3 + j, chip, c, (x, y, 1 - c)))
                passed[-1].start()
        for k in range(n):
            for j, chip in enumerate(chips):
                copy(k, 3 + j, chip, 1 - c, (x, y, 1 - c)).wait_recv()
        for cp in first + passed:
            cp.wait_send()

    shapes = tuple(jax.ShapeDtypeStruct((N_CHIPS,) + w.shape, w.dtype) for w in ws)
    return _hbm_call(body, "all_gather_shards", shapes, n,
                     [pltpu.SemaphoreType.DMA((n, 6)), pltpu.SemaphoreType.DMA((n, 6))])(*ws)


def _to_sibling(arrs, other_layer, name):
    n = len(arrs)

    def body(*refs):
        ins, outs, (send_sems, recv_sems) = refs[:n], refs[n:2 * n], refs[2 * n:]
        x, y, c, _ = _place()
        cps = [pltpu.make_async_remote_copy(src_ref=ins[k].at[1 - c] if other_layer else ins[k], dst_ref=outs[k],
                                            send_sem=send_sems.at[k], recv_sem=recv_sems.at[k],
                                            device_id=(x, y, 1 - c), device_id_type=MESH) for k in range(n)]
        for cp in cps:
            cp.start()
        for cp in cps:
            cp.wait()

    shapes = tuple(jax.ShapeDtypeStruct(a.shape[1:] if other_layer else a.shape, a.dtype) for a in arrs)
    return _hbm_call(body, name, shapes, n, [pltpu.SemaphoreType.DMA((n,)), pltpu.SemaphoreType.DMA((n,))])(*arrs)


def _scatter_to_chips(hs):
    n = len(hs)

    def body(*refs):
        ins, outs, (send_sems, recv_sems) = refs[:n], refs[n:2 * n], refs[2 * n:]
        x, y, c, chips = _place()
        me = 2 * x + y
        sends = []
        for k in range(n):
            for j, chip in enumerate(chips):
                cp = pltpu.make_async_remote_copy(src_ref=ins[k].at[2 * chip[0] + chip[1]], dst_ref=outs[k].at[me],
                                                  send_sem=send_sems.at[k, j], recv_sem=recv_sems.at[k, j],
                                                  device_id=(*chip, c), device_id_type=MESH)
                cp.start()
                sends.append(cp)
        for k in range(n):
            for j, chip in enumerate(chips):
                dst = outs[k].at[2 * chip[0] + chip[1]]
                pltpu.make_async_remote_copy(src_ref=dst, dst_ref=dst, send_sem=send_sems.at[k, j],
                                             recv_sem=recv_sems.at[k, j], device_id=(*chip, c),
                                             device_id_type=MESH).wait_recv()
        for cp in sends:
            cp.wait_send()

    shapes = tuple(jax.ShapeDtypeStruct(h.shape, h.dtype) for h in hs)
    return _hbm_call(body, "scatter_to_chips", shapes, n,
                     [pltpu.SemaphoreType.DMA((n, 3)), pltpu.SemaphoreType.DMA((n, 3))])(*hs)


BLOCK_BYTES = 2 * 1024 * 1024


def _row_block(r, cols):
    if r * cols * 4 <= BLOCK_BYTES or r % 8:
        return r
    br = 8
    while r % (2 * br) == 0 and 2 * br * cols * 4 <= BLOCK_BYTES:
        br *= 2
    return br


def _add_cores(b, got, name):
    _, ns, r, cols = b.shape
    br = _row_block(r, 2 * cols)

    def body(b_ref, g_ref, o_ref):
        mine = jnp.where(lax.axis_index("c") == 0, b_ref[0, 0], b_ref[1, 0])
        o_ref[0] = (mine + g_ref[0]).astype(BF16)

    spec = pl.BlockSpec((1, br, cols), lambda i, j: (i, j, 0))
    return pl.pallas_call(body, name=name, grid=(ns, r // br),
                          in_specs=[pl.BlockSpec((2, 1, br, cols), lambda i, j: (0, i, j, 0)), spec], out_specs=spec,
                          out_shape=jax.ShapeDtypeStruct((ns, r, cols), BF16))(b, got)


def _sum_chips(own, got, name):
    _, r, cols = own.shape
    br = _row_block(r, 4 * cols)

    def body(own_ref, got_ref, o_ref):
        me = 2 * lax.axis_index("x") + lax.axis_index("y")
        part = [jnp.where(me == j, own_ref[j], got_ref[j]).astype(F32) for j in range(N_CHIPS)]
        o_ref[...] = ((part[0] + part[1]) + part[2]) + part[3]

    spec = pl.BlockSpec((N_CHIPS, br, cols), lambda j: (0, j, 0))
    return pl.pallas_call(body, name=name, grid=(r // br,), in_specs=[spec, spec],
                          out_specs=pl.BlockSpec((br, cols), lambda j: (j, 0)),
                          out_shape=jax.ShapeDtypeStruct((r, cols), F32))(own, got)


def _adamw(w, g_mine, g_other, m, v, name):
    _, r, cols = w.shape
    br = _row_block(r, 4 * cols)

    def body(w_ref, gm_ref, go_ref, m_ref, v_ref, g_ref, d_ref, nm_ref, nv_ref):
        gv = jnp.where(pl.program_id(0) == lax.axis_index("c"), gm_ref[...], go_ref[...])
        m2 = ADAM_B1 * m_ref[0] + (1.0 - ADAM_B1) * gv
        v2 = ADAM_B2 * v_ref[0] + (1.0 - ADAM_B2) * jnp.square(gv)
        m_hat = m2 / (1.0 - ADAM_B1 ** ADAM_STEP)
        v_hat = v2 / (1.0 - ADAM_B2 ** ADAM_STEP)
        g_ref[0] = gv
        d_ref[0] = -ADAM_LR * (m_hat / (jnp.sqrt(v_hat) + ADAM_EPS) + ADAM_WD * w_ref[0])
        nm_ref[0] = m2
        nv_ref[0] = v2

    lay = pl.BlockSpec((1, br, cols), lambda l, j: (l, j, 0))
    flat = pl.BlockSpec((br, cols), lambda l, j: (j, 0))
    shp = jax.ShapeDtypeStruct(w.shape, F32)
    return pl.pallas_call(body, name=name, grid=(2, r // br), in_specs=[lay, flat, flat, lay, lay],
                          out_specs=[lay] * 4, out_shape=(shp,) * 4)(w, g_mine, g_other, m, v)


def _pack_small(ts):
    flat = jnp.concatenate([ts[k].reshape(DEPTH, -1) for k in REPLICATED], axis=1)
    return flat.reshape(DEPTH, flat.shape[1] // LANES, LANES)


def _unpack_small(packed, like):
    flat = packed.reshape(DEPTH, -1)
    out, off = {}, 0
    for k in REPLICATED:
        size = like[k].size // DEPTH
        out[k] = flat[:, off:off + size].reshape(like[k].shape)
        off += size
    return out


def _shard_major(a, axis):
    if axis == 1:
        return a.reshape(DEPTH, N_CHIPS, a.shape[1] // N_CHIPS, a.shape[2])
    return a.reshape(DEPTH, a.shape[1], N_CHIPS, a.shape[2] // N_CHIPS).transpose(0, 2, 1, 3)


def kernel(x, c, ctx, c_ctx, mod_w, mod_b, pre_norm, post_norm, w_in, mla_q_norm, mla_w_uq, mla_kv_norm, mla_w_ukv, pool_w, pool_scale, gla_af_w2, gla_af_b, gla_ab_w2, gla_ab_b, gla_norm, w_branch_mla, w_branch_pool, w_branch_gla, w_out, loss_target, m_c_ctx, m_mod_w, m_mod_b, m_pre_norm, m_post_norm, m_w_in, m_mla_q_norm, m_mla_w_uq, m_mla_kv_norm, m_mla_w_ukv, m_pool_w, m_pool_scale, m_gla_af_w2, m_gla_af_b, m_gla_ab_w2, m_gla_ab_b, m_gla_norm, m_w_branch_mla, m_w_branch_pool, m_w_branch_gla, m_w_out, v_c_ctx, v_mod_w, v_mod_b, v_pre_norm, v_post_norm, v_w_in, v_mla_q_norm, v_mla_w_uq, v_mla_kv_norm, v_mla_w_ukv, v_pool_w, v_pool_scale, v_gla_af_w2, v_gla_af_b, v_gla_ab_w2, v_gla_ab_b, v_gla_norm, v_w_branch_mla, v_w_branch_pool, v_w_branch_gla, v_w_out):
    given = dict(locals())
    wts = {k: given[k] for k in WEIGHTS}
    my_chip = 2 * lax.axis_index("x") + lax.axis_index("y")

    mine = [wts[k].astype(BF16) for k, _ in SHARDED]
    gathered = _all_gather_shards(mine)
    full = dict(wts)
    for (k, axis), own, got in zip(SHARDED, mine, gathered):
        full[k] = jnp.concatenate([jnp.where(my_chip == s, own, got[s]) for s in range(N_CHIPS)], axis=axis)

    loss_local, grad_x, grads = _local_step(x, c, ctx, loss_target, full)
    loss = lax.psum(loss_local, ("x", "y", "c"))

    small = _pack_small(grads)
    bufs = [_shard_major(grads[k], axis) for k, axis in SHARDED]
    bufs.append(jnp.broadcast_to(small[:, None], (DEPTH, N_CHIPS) + small.shape[1:]))
    got = _to_sibling(bufs, True, "swap_halves")
    chip_sum = [_add_cores(b, g, f"add_cores{i}") for i, (b, g) in enumerate(zip(bufs, got))]
    recv = _scatter_to_chips(chip_sum)
    mine_red = [_sum_chips(cs, rc, f"sum_chips{i}") for i, (cs, rc) in enumerate(zip(chip_sum, recv))]
    other_red = _to_sibling(mine_red, False, "join_halves")

    outs = {}
    for i, (k, _) in enumerate(SHARDED):
        outs[k] = _adamw(wts[k], mine_red[i], other_red[i], given['m_' + k], given['v_' + k], f"adamw{i}")
    packed = _adamw(_pack_small(wts), mine_red[-1], other_red[-1],
                    _pack_small({k: given['m_' + k] for k in REPLICATED}),
                    _pack_small({k: given['v_' + k] for k in REPLICATED}), "adamw_small")
    unpacked = [_unpack_small(p, wts) for p in packed]
    for k in REPLICATED:
        outs[k] = tuple(u[k] for u in unpacked)
    return (loss, grad_x, *[outs[k][q] for q in range(4) for k in WEIGHTS])
```

```python
import functools

import numpy as np
import jax
import jax.numpy as jnp
from jax import lax
from jax.experimental import pallas as pl
from jax.experimental.pallas import tpu as pltpu

F32 = jnp.float32
BF16 = jnp.bfloat16
HIGHEST = lax.Precision.HIGHEST
MESH = pl.DeviceIdType.MESH

D = 1024
DEPTH = 2
EPS = 1e-6
GRID_W = 64
MLA_H, MLA_NOPE, MLA_ROPE, MLA_V = 8, 64, 32, 64
MLA_QK = MLA_NOPE + MLA_ROPE
ROPE_BASE = 10000.0
POOL_WINDOWS = (2, 4, 8, 16)
GLA_H, GLA_DK, GLA_DV, GLA_RANK, GLA_TAU = 4, 64, 128, 16, 16.0
GLA_C = 128
EXP_CLAMP = 80.0
ADAM_LR, ADAM_B1, ADAM_B2, ADAM_EPS, ADAM_WD, ADAM_STEP = 0.001, 0.9, 0.999, 1e-08, 0.01, 10

TM = 256
LANES = 128
N_CHIPS = 4

IN_NAMES = ('mla_q', 'mla_kv', 'mla_kr', 'mla_gate', 'pool_x', 'pool_gate',
            'gla_q', 'gla_k', 'gla_v', 'gla_af', 'gla_ab', 'gla_gate', 'merge')
IN_SIZES = (256, 128, 32, 512, 512, 512, 256, 256, 512, 16, 16, 512, 3 * D)
NAT_OFF = dict(zip(IN_NAMES, [int(o) for o in np.cumsum((0,) + IN_SIZES[:-1])]))
NAT_SIZE = dict(zip(IN_NAMES, IN_SIZES))
PAD_ORDER = ('merge', 'mla_gate', 'mla_q', 'mla_kv', 'mla_kr', 'pool_x', 'pool_gate',
             'gla_v', 'gla_gate', 'gla_q', 'gla_k', 'gla_af', 'gla_ab')
SLAB = {n: max(NAT_SIZE[n], LANES) for n in IN_NAMES}
PAD_OFF = dict(zip(PAD_ORDER, [int(o) for o in np.cumsum([0] + [SLAB[n] for n in PAD_ORDER[:-1]])]))
D_PAD = sum(SLAB.values())
IN_SLAB = {n: 0 for n in IN_NAMES}
IN_SLAB['mla_kr'] = MLA_NOPE
MLA_HP = 128


def _blk(name):
    return PAD_OFF[name] // SLAB[name]


SHARDED = (('mod_w', 2), ('w_in', 2), ('mla_w_uq', 2), ('mla_w_ukv', 2), ('gla_af_w2', 2), ('gla_ab_w2', 2),
           ('w_branch_mla', 2), ('w_branch_pool', 2), ('w_branch_gla', 2), ('w_out', 1))
REPLICATED = ('c_ctx', 'mod_b', 'pre_norm', 'post_norm', 'mla_q_norm', 'mla_kv_norm', 'pool_w', 'pool_scale',
              'gla_af_b', 'gla_ab_b', 'gla_norm')
WEIGHTS = ('c_ctx', 'mod_w', 'mod_b', 'pre_norm', 'post_norm', 'w_in', 'mla_q_norm', 'mla_w_uq', 'mla_kv_norm',
           'mla_w_ukv', 'pool_w', 'pool_scale', 'gla_af_w2', 'gla_af_b', 'gla_ab_w2', 'gla_ab_b', 'gla_norm',
           'w_branch_mla', 'w_branch_pool', 'w_branch_gla', 'w_out')


def _cp(vmem_mb=None, sem=None):
    kw = {}
    if vmem_mb is not None:
        kw['vmem_limit_bytes'] = vmem_mb * 1024 * 1024
    if sem is not None:
        kw['dimension_semantics'] = sem
    return pltpu.CompilerParams(**kw)


def _bdot(a, b):
    return jnp.dot(a.astype(BF16), b.astype(BF16), preferred_element_type=F32)


def _bdot_nt(a, b):
    return lax.dot_general(a.astype(BF16), b.astype(BF16), (((1,), (1,)), ((), ())), preferred_element_type=F32)


def _bdot_tn(a, b):
    return lax.dot_general(a.astype(BF16), b.astype(BF16), (((0,), (0,)), ((), ())), preferred_element_type=F32)


def _xdot(a, b):
    return jnp.dot(a, b, precision=HIGHEST, preferred_element_type=F32)


def _xdot_tn(a, b):
    return lax.dot_general(a, b, (((0,), (0,)), ((), ())), precision=HIGHEST, preferred_element_type=F32)


NN = (((1,), (0,)), ((), ()))
NT = (((1,), (1,)), ((), ()))
TN = (((0,), (0,)), ((), ()))


def _split(a):
    hi = a.astype(BF16)
    return hi, (a - hi.astype(F32)).astype(BF16)


def _dot3(a, b, dims):
    (ah, al), (bh, bl) = a, b
    f = lambda u, v: lax.dot_general(u, v, dims, preferred_element_type=F32)
    return f(ah, bh) + (f(ah, bl) + f(al, bh))


def _sigmoid(x):
    return jax.nn.sigmoid(x)


def _silu(x):
    return x * _sigmoid(x)


def _dsilu(x):
    s = _sigmoid(x)
    return s * (1.0 + x * (1.0 - s))


def _rstd(x):
    return lax.rsqrt(jnp.mean(x * x, axis=-1, keepdims=True) + EPS)


def _norm_bwd(xhat, r, dy):
    return r * (dy - xhat * jnp.mean(xhat * dy, axis=-1, keepdims=True))


def _colsum(a):
    return jnp.sum(a, axis=0, keepdims=True)


def _modrow(i, tpe):
    return jnp.where(i % tpe == 0, 4, i // tpe)


def _rot(x):
    n = x.shape[-1]
    lane = lax.broadcasted_iota(jnp.int32, x.shape, x.ndim - 1)
    return jnp.where(lane % 16 < 8, -pltpu.roll(x, n - 8, x.ndim - 1), pltpu.roll(x, 8, x.ndim - 1))


def _mod_fwd(cv, mod_w, mod_b):
    def body(cv_ref, w_ref, b_ref, o_ref):
        s = _silu(cv_ref[...])
        for l in range(DEPTH):
            o_ref[l] = _bdot(s, w_ref[l]) + b_ref[l]

    return pl.pallas_call(body, name="mod_fwd", out_shape=jax.ShapeDtypeStruct((DEPTH, 8, 3 * D), F32),
                          compiler_params=_cp(40))(cv, mod_w, mod_b)


def _mod_bwd(cv, sel, st_a, st_b, w_t, dcv_in, name):
    def body(cv_ref, sel_ref, sa_ref, sb_ref, wt_ref, dcin_ref, dw_ref, db_ref, dcv_ref, dcc_ref, dpre_ref, dpost_ref):
        cvv = cv_ref[...]
        s = _silu(cvv)
        sel_v = sel_ref[...]
        dmod = jnp.concatenate([_xdot(sel_v, sa_ref[0]), _xdot(sel_v, sa_ref[1]), _xdot(sel_v, sb_ref[0])], axis=1)
        dw_ref[...] = _bdot_tn(s, dmod)
        db_ref[...] = _colsum(dmod)
        dcv = dcin_ref[...] + _bdot(dmod, wt_ref[...])
        dcv_ref[...] = dcv
        dcc_ref[...] = dcv * _dsilu(cvv)
        dpre_ref[...] = _colsum(sa_ref[2])
        dpost_ref[...] = _colsum(sb_ref[1])

    shapes = (jax.ShapeDtypeStruct((D, 3 * D), F32), jax.ShapeDtypeStruct((1, 3 * D), F32),
              jax.ShapeDtypeStruct((8, D), F32), jax.ShapeDtypeStruct((8, D), F32),
              jax.ShapeDtypeStruct((1, D), F32), jax.ShapeDtypeStruct((1, D), F32))
    return pl.pallas_call(body, name=name, out_shape=shapes, compiler_params=_cp(48))(cv, sel, st_a, st_b, w_t, dcv_in)


def _pre_fwd(x, modl, pre_g, w, tpe, name):
    n = x.shape[0]
    nt = n // TM
    ncb = 3
    tn = D_PAD // ncb
    tm = 2 * TM if n % (2 * TM) == 0 else TM

    def norm_body(x_ref, m_ref, g_ref, h_ref):
        xv = x_ref[...]
        m = m_ref[0]
        h_ref[...] = (xv * _rstd(xv) * g_ref[...] * (1.0 + m[:, D:2 * D]) + m[:, 0:D]).astype(BF16)

    h = pl.pallas_call(
        norm_body, name=name + "_norm", grid=(nt,),
        in_specs=[pl.BlockSpec((TM, D), lambda i: (i, 0)),
                  pl.BlockSpec((1, 1, 3 * D), lambda i: (_modrow(i, tpe), 0, 0)),
                  pl.BlockSpec((1, D), lambda i: (0, 0))],
        out_specs=pl.BlockSpec((TM, D), lambda i: (i, 0)),
        out_shape=jax.ShapeDtypeStruct((n, D), BF16),
        compiler_params=_cp(32, ("arbitrary",)),
    )(x, modl.reshape(8, 1, 3 * D), pre_g)

    def mm_body(h_ref, w_ref, z_ref):
        z_ref[...] = jnp.dot(h_ref[...], w_ref[...], preferred_element_type=F32)

    z = pl.pallas_call(
        mm_body, name=name, grid=(ncb, n // tm),
        in_specs=[pl.BlockSpec((tm, D), lambda j, i: (i, 0)), pl.BlockSpec((D, tn), lambda j, i: (0, j))],
        out_specs=pl.BlockSpec((tm, tn), lambda j, i: (i, j)),
        out_shape=jax.ShapeDtypeStruct((n, D_PAD), F32),
        compiler_params=_cp(48, ("arbitrary", "arbitrary")),
    )(h, w)
    return z, h


def _pre_bwd(dz, w_t, x, dxres, modl, pre_g, tpe, name):
    n = x.shape[0]
    nt = n // TM

    def body(dz_ref, wt_ref, x_ref, dr_ref, m_ref, g_ref, dx_ref, st_ref):
        dh = jnp.dot(dz_ref[...], wt_ref[...], preferred_element_type=F32)
        xv = x_ref[...]
        r = _rstd(xv)
        xn = xv * r
        m = m_ref[0]
        sc1 = 1.0 + m[:, D:2 * D]
        g = g_ref[...]
        st_ref[0, 0:1, :] = _colsum(dh)
        st_ref[0, 1:2, :] = _colsum(dh * xn * g)
        st_ref[0, 2:3, :] = _colsum(dh * xn * sc1)
        st_ref[0, 3:8, :] = jnp.zeros((5, D), F32)
        dx_ref[...] = dr_ref[...] + _norm_bwd(xn, r, dh * g * sc1)

    return pl.pallas_call(
        body, name=name, grid=(nt,),
        in_specs=[pl.BlockSpec((TM, D_PAD), lambda i: (i, 0)),
                  pl.BlockSpec((D_PAD, D), lambda i: (0, 0)),
                  pl.BlockSpec((TM, D), lambda i: (i, 0)),
                  pl.BlockSpec((TM, D), lambda i: (i, 0)),
                  pl.BlockSpec((1, 1, 3 * D), lambda i: (_modrow(i, tpe), 0, 0)),
                  pl.BlockSpec((1, D), lambda i: (0, 0))],
        out_specs=[pl.BlockSpec((TM, D), lambda i: (i, 0)),
                   pl.BlockSpec((1, 8, D), lambda i: (i, 0, 0))],
        out_shape=(jax.ShapeDtypeStruct((n, D), F32), jax.ShapeDtypeStruct((nt, 8, D), F32)),
        compiler_params=_cp(56, ("arbitrary",)),
    )(dz, w_t, x, dxres, modl.reshape(8, 1, 3 * D), pre_g)


def _matmul_tn(a, b, tn, tk, name):
    n, k1 = a.shape
    k2 = b.shape[1]

    def body(a_ref, b_ref, o_ref):
        @pl.when(pl.program_id(1) == 0)
        def _():
            o_ref[...] = jnp.zeros(o_ref.shape, F32)

        o_ref[...] += lax.dot_general(a_ref[...], b_ref[...], (((0,), (0,)), ((), ())), preferred_element_type=F32)

    return pl.pallas_call(
        body, name=name, grid=(k2 // tn, n // tk),
        in_specs=[pl.BlockSpec((tk, k1), lambda j, k: (k, 0)), pl.BlockSpec((tk, tn), lambda j, k: (k, j))],
        out_specs=pl.BlockSpec((k1, tn), lambda j, k: (0, j)),
        out_shape=jax.ShapeDtypeStruct((k1, k2), F32),
        compiler_params=_cp(48, ("arbitrary", "arbitrary")),
    )(a, b)


def _mla_pre(z, q_norm, kv_norm, w_uq, w_ukv, cq, sq, ck, sk, tpe, name):
    n = z.shape[0]
    nt = n // TM
    scale = MLA_QK ** -0.5

    def body(zq_ref, zkv_ref, zkr_ref, gq_ref, gkv_ref, wq_ref, wkv_ref, cq_ref, sq_ref, ck_ref, sk_ref,
             q_ref, kv_ref, kr_ref):
        zq = zq_ref[...]
        qn = zq * _rstd(zq) * gq_ref[...]
        qraw = _bdot(qn, wq_ref[...])
        q_ref[...] = ((qraw * cq_ref[...] + _rot(qraw) * sq_ref[...]) * scale).astype(BF16)
        zkv = zkv_ref[...]
        kvn = zkv * _rstd(zkv) * gkv_ref[...]
        kv_ref[...] = _bdot(kvn, wkv_ref[...]).astype(BF16)
        zkr = zkr_ref[...]
        kr_ref[...] = (zkr * ck_ref[...] + _rot(zkr) * sk_ref[...]).astype(BF16)

    hq, hkv = MLA_H * MLA_HP, MLA_H * (MLA_NOPE + MLA_V)
    const = lambda i: (0, 0)
    tab = lambda i: (i % tpe, 0)
    return pl.pallas_call(
        body, name=name, grid=(nt,),
        in_specs=[pl.BlockSpec((TM, 256), lambda i: (i, _blk('mla_q'))),
                  pl.BlockSpec((TM, 128), lambda i: (i, _blk('mla_kv'))),
                  pl.BlockSpec((TM, 128), lambda i: (i, _blk('mla_kr'))),
                  pl.BlockSpec((1, 256), const), pl.BlockSpec((1, 128), const),
                  pl.BlockSpec((256, hq), const), pl.BlockSpec((128, hkv), const),
                  pl.BlockSpec((TM, hq), tab), pl.BlockSpec((TM, hq), tab),
                  pl.BlockSpec((TM, 128), tab), pl.BlockSpec((TM, 128), tab)],
        out_specs=[pl.BlockSpec((TM, hq), lambda i: (i, 0)), pl.BlockSpec((TM, hkv), lambda i: (i, 0)),
                   pl.BlockSpec((TM, 128), lambda i: (i, 0))],
        out_shape=(jax.ShapeDtypeStruct((n, hq), BF16), jax.ShapeDtypeStruct((n, hkv), BF16),
                   jax.ShapeDtypeStruct((n, 128), BF16)),
        compiler_params=_cp(32, ("arbitrary",)),
    )(z, z, z, q_norm, kv_norm, w_uq, w_ukv, cq, sq, ck, sk)


def _mla_pre_bwd(z, dq, dkv, dkr, q_norm, kv_norm, w_uq_t, w_ukv_t, cq, sq, ck, sk, tpe, name):
    n = z.shape[0]
    nt = n // TM
    scale = MLA_QK ** -0.5
    hq, hkv = MLA_H * MLA_HP, MLA_H * (MLA_NOPE + MLA_V)

    def body(zq_ref, zkv_ref, dq_ref, dkv_ref, dkr_ref, gq_ref, gkv_ref, wqt_ref, wkvt_ref, cq_ref, sq_ref,
             ck_ref, sk_ref, dzq_ref, dzkv_ref, dzkr_ref, dwq_ref, dwkv_ref, dgq_ref, dgkv_ref):
        @pl.when(pl.program_id(0) == 0)
        def _():
            dwq_ref[...] = jnp.zeros(dwq_ref.shape, F32)
            dwkv_ref[...] = jnp.zeros(dwkv_ref.shape, F32)
            dgq_ref[...] = jnp.zeros(dgq_ref.shape, F32)
            dgkv_ref[...] = jnp.zeros(dgkv_ref.shape, F32)

        zq = zq_ref[...]
        rq = _rstd(zq)
        qhat = zq * rq
        gq = gq_ref[...]
        dqs = dq_ref[...] * scale
        dqraw = dqs * cq_ref[...] - _rot(dqs * sq_ref[...])
        dwq_ref[...] += _bdot_tn(qhat * gq, dqraw)
        dqn = _bdot(dqraw, wqt_ref[...])
        dgq_ref[...] += _colsum(dqn * qhat)
        dzq_ref[...] = _norm_bwd(qhat, rq, dqn * gq).astype(BF16)

        zkv = zkv_ref[...]
        rkv = _rstd(zkv)
        khat = zkv * rkv
        gkv = gkv_ref[...]
        dkvv = dkv_ref[...]
        dwkv_ref[...] += _bdot_tn(khat * gkv, dkvv)
        dkvn = _bdot(dkvv, wkvt_ref[...])
        dgkv_ref[...] += _colsum(dkvn * khat)
        dzkv_ref[...] = _norm_bwd(khat, rkv, dkvn * gkv).astype(BF16)

        dkr = dkr_ref[...]
        dzkr_ref[...] = (dkr * ck_ref[...] - _rot(dkr * sk_ref[...])).astype(BF16)

    const = lambda i: (0, 0)
    tab = lambda i: (i % tpe, 0)
    row = lambda i: (i, 0)
    return pl.pallas_call(
        body, name=name, grid=(nt,),
        in_specs=[pl.BlockSpec((TM, 256), lambda i: (i, _blk('mla_q'))),
                  pl.BlockSpec((TM, 128), lambda i: (i, _blk('mla_kv'))),
                  pl.BlockSpec((TM, hq), row), pl.BlockSpec((TM, hkv), row), pl.BlockSpec((TM, 128), row),
                  pl.BlockSpec((1, 256), const), pl.BlockSpec((1, 128), const),
                  pl.BlockSpec((hq, 256), const), pl.BlockSpec((hkv, 128), const),
                  pl.BlockSpec((TM, hq), tab), pl.BlockSpec((TM, hq), tab),
                  pl.BlockSpec((TM, 128), tab), pl.BlockSpec((TM, 128), tab)],
        out_specs=[pl.BlockSpec((TM, 256), row), pl.BlockSpec((TM, 128), row), pl.BlockSpec((TM, 128), row),
                   pl.BlockSpec((256, hq), const), pl.BlockSpec((128, hkv), const),
                   pl.BlockSpec((1, 256), const), pl.BlockSpec((1, 128), const)],
        out_shape=(jax.ShapeDtypeStruct((n, 256), BF16), jax.ShapeDtypeStruct((n, 128), BF16),
                   jax.ShapeDtypeStruct((n, 128), BF16), jax.ShapeDtypeStruct((256, hq), F32),
                   jax.ShapeDtypeStruct((128, hkv), F32), jax.ShapeDtypeStruct((1, 256), F32),
                   jax.ShapeDtypeStruct((1, 128), F32)),
        compiler_params=_cp(32, ("arbitrary",)),
    )(z, z, dq, dkv, dkr, q_norm, kv_norm, w_uq_t, w_ukv_t, cq, sq, ck, sk)


def _attn_head(q_ref, kv_ref, kr_ref, hh, nk):
    kvh = kv_ref[0:nk, hh * MLA_HP:(hh + 1) * MLA_HP]
    lane = lax.broadcasted_iota(jnp.int32, kvh.shape, 1)
    kh = jnp.where(lane < MLA_NOPE, kvh, kr_ref[0:nk, :])
    qh = q_ref[:, hh * MLA_HP:(hh + 1) * MLA_HP]
    return kvh, kh, qh, lax.dot_general(qh, kh, (((1,), (1,)), ((), ())), preferred_element_type=F32)


def _by_segment(j, lc, t, fn):
    pl.when(j == 0)(functools.partial(fn, lc))
    pl.when(j != 0)(functools.partial(fn, t))


def _attn_specs(nb, tpe, t):
    tile = lambda b, p, j: (b * tpe + j, p)
    return [pl.BlockSpec((TM, 2 * MLA_HP), tile),
            pl.BlockSpec((t, 2 * MLA_HP), lambda b, p, j: (b, p)),
            pl.BlockSpec((t, MLA_HP), lambda b, p, j: (b, 0))]


def _attn_fwd(q, kv, kr, nb, lc, name):
    n = q.shape[0]
    t = n // nb
    tpe = t // TM

    def body(q_ref, kv_ref, kr_ref, o_ref, lse_ref):
        def run(nk):
            lane = lax.broadcasted_iota(jnp.int32, (TM, MLA_HP), 1)
            res, lses = [], []
            for hh in range(2):
                kvh, _, _, s = _attn_head(q_ref, kv_ref, kr_ref, hh, nk)
                m = jnp.max(s, axis=-1, keepdims=True)
                p = jnp.exp(s - m)
                l = jnp.sum(p, axis=-1, keepdims=True)
                res.append(jnp.dot(p.astype(BF16), kvh, preferred_element_type=F32) / l)
                lses.append(m + jnp.log(l))
            o_ref[...] = jnp.where(lane < MLA_V, pltpu.roll(res[0], MLA_V, 1), res[1])
            lane2 = lax.broadcasted_iota(jnp.int32, (TM, 2), 1)
            lse_ref[0] = jnp.where(lane2 == 0, lses[0], lses[1])

        _by_segment(pl.program_id(2), lc, t, run)

    return pl.pallas_call(
        body, name=name, grid=(nb, MLA_H // 2, tpe),
        in_specs=_attn_specs(nb, tpe, t),
        out_specs=[pl.BlockSpec((TM, 2 * MLA_V), lambda b, p, j: (b * tpe + j, p)),
                   pl.BlockSpec((1, TM, 2), lambda b, p, j: (p, b * tpe + j, 0))],
        out_shape=(jax.ShapeDtypeStruct((n, MLA_H * MLA_V), F32), jax.ShapeDtypeStruct((MLA_H // 2, n, 2), F32)),
        compiler_params=_cp(48, ("arbitrary", "arbitrary", "arbitrary")),
    )(q, kv, kr)


def _attn_bwd(q, kv, kr, o, lse, do, nb, lc, name):
    n = q.shape[0]
    t = n // nb
    tpe = t // TM

    def body(q_ref, kv_ref, kr_ref, o_ref, lse_ref, do_ref, dq_ref, dkv_ref, dkr_ref):
        p_id, j = pl.program_id(1), pl.program_id(2)

        @pl.when(j == 0)
        def _():
            dkv_ref[...] = jnp.zeros(dkv_ref.shape, F32)

        @pl.when((j == 0) & (p_id == 0))
        def _():
            dkr_ref[...] = jnp.zeros(dkr_ref.shape, F32)

        def run(nk):
            lane = lax.broadcasted_iota(jnp.int32, (TM, MLA_HP), 1)
            lane_t = lax.broadcasted_iota(jnp.int32, (nk, MLA_HP), 1)
            lane2 = lax.broadcasted_iota(jnp.int32, (TM, 2), 1)
            lse = lse_ref[0]
            dov, ov = do_ref[...], o_ref[...]
            dkr = jnp.zeros((nk, MLA_HP), F32)
            for hh in range(2):
                kvh, kh, qh, s = _attn_head(q_ref, kv_ref, kr_ref, hh, nk)
                p = jnp.exp(s - jnp.sum(jnp.where(lane2 == hh, lse, 0.0), axis=1, keepdims=True))
                do_pos = jnp.where(lane >= MLA_NOPE, pltpu.roll(dov, MLA_V, 1) if hh == 0 else dov, 0.0)
                o_pos = jnp.where(lane >= MLA_NOPE, pltpu.roll(ov, MLA_V, 1) if hh == 0 else ov, 0.0)
                delta = jnp.sum(do_pos * o_pos, axis=-1, keepdims=True)
                dob = do_pos.astype(BF16)
                dp = lax.dot_general(dob, kvh, (((1,), (1,)), ((), ())), preferred_element_type=F32)
                ds = (p * (dp - delta)).astype(BF16)
                dq_ref[:, hh * MLA_HP:(hh + 1) * MLA_HP] = jnp.dot(ds, kh, preferred_element_type=F32)
                dkf = lax.dot_general(ds, qh, (((0,), (0,)), ((), ())), preferred_element_type=F32)
                dvp = lax.dot_general(p.astype(BF16), dob, (((0,), (0,)), ((), ())), preferred_element_type=F32)
                dkv_ref[0:nk, hh * MLA_HP:(hh + 1) * MLA_HP] += jnp.where(lane_t < MLA_NOPE, dkf, dvp)
                dkr = dkr + jnp.where(lane_t >= MLA_NOPE, dkf, 0.0)
            dkr_ref[0:nk, :] += dkr

        _by_segment(j, lc, t, run)

    tile = lambda b, p, j: (b * tpe + j, p)
    return pl.pallas_call(
        body, name=name, grid=(nb, MLA_H // 2, tpe),
        in_specs=_attn_specs(nb, tpe, t) + [pl.BlockSpec((TM, 2 * MLA_V), tile),
                                            pl.BlockSpec((1, TM, 2), lambda b, p, j: (p, b * tpe + j, 0)),
                                            pl.BlockSpec((TM, 2 * MLA_V), tile)],
        out_specs=[pl.BlockSpec((TM, 2 * MLA_HP), tile),
                   pl.BlockSpec((t, 2 * MLA_HP), lambda b, p, j: (b, p)),
                   pl.BlockSpec((t, MLA_HP), lambda b, p, j: (b, 0))],
        out_shape=(jax.ShapeDtypeStruct((n, MLA_H * MLA_HP), F32), jax.ShapeDtypeStruct((n, MLA_H * MLA_HP), F32),
                   jax.ShapeDtypeStruct((n, MLA_HP), F32)),
        compiler_params=_cp(56, ("arbitrary", "arbitrary", "arbitrary")),
    )(q, kv, kr, o, lse, do)


def _pool_window(ug, pos, seglen, w, transpose):
    t = ug.shape[0]
    cnt = (jnp.minimum(pos + w // 2, seglen) - jnp.maximum(pos - w // 2, 0)).astype(F32)
    if transpose:
        ug = ug / cnt
    acc = jnp.zeros_like(ug)
    for j in range(-(w // 2), w // 2):
        jj = -j if transpose else j
        src = pos + jj
        valid = (src >= 0) & (src < seglen)
        acc = acc + jnp.where(valid, pltpu.roll(ug, (-jj) % t, 0), 0.0)
    return acc if transpose else acc / cnt


def _by_group(g, fn):
    for k, w in enumerate(POOL_WINDOWS):
        pl.when(g == k)(functools.partial(fn, w))


def _pool_specs(t):
    px, pg = PAD_OFF['pool_x'] // LANES, PAD_OFF['pool_gate'] // LANES
    return [pl.BlockSpec((t, LANES), lambda g, b: (b, px + g)),
            pl.BlockSpec((t, LANES), lambda g, b: (b, pg + g)),
            pl.BlockSpec((1, LANES, LANES), lambda g, b: (g, 0, 0)),
            pl.BlockSpec((1, LANES), lambda g, b: (0, g)),
            pl.BlockSpec((t, 1), lambda g, b: (0, 0)), pl.BlockSpec((t, 1), lambda g, b: (0, 0))]


def _pool_fwd(z, pool_w, pool_scale, pos, seglen, nb, name):
    n = z.shape[0]
    t = n // nb

    def body(u_ref, zg_ref, pw_ref, ps_ref, pos_ref, sl_ref, y_ref):
        def run(w):
            u = u_ref[...]
            pooled = _pool_window(u, pos_ref[...], sl_ref[...], w, False) - u
            y_ref[...] = (_bdot(pooled, pw_ref[0]) * ps_ref[...] * _silu(zg_ref[...])).astype(BF16)

        _by_group(pl.program_id(0), run)

    return pl.pallas_call(
        body, name=name, grid=(4, nb), in_specs=_pool_specs(t),
        out_specs=pl.BlockSpec((t, LANES), lambda g, b: (b, g)),
        out_shape=jax.ShapeDtypeStruct((n, 512), BF16),
        compiler_params=_cp(48, ("arbitrary", "arbitrary")),
    )(z, z, pool_w, pool_scale, pos, seglen)


def _pool_bwd(z, dy, pool_w, pool_w_t, pool_scale, pos, seglen, nb, name):
    n = z.shape[0]
    t = n // nb

    def body(u_ref, zg_ref, pw_ref, ps_ref, pos_ref, sl_ref, dy_ref, pwt_ref, du_ref, dg_ref, dpw_ref, dps_ref):
        @pl.when(pl.program_id(1) == 0)
        def _():
            dpw_ref[...] = jnp.zeros(dpw_ref.shape, F32)
            dps_ref[...] = jnp.zeros(dps_ref.shape, F32)

        def run(w):
            u = u_ref[...]
            pos_v, sl_v = pos_ref[...], sl_ref[...]
            pooled = _pool_window(u, pos_v, sl_v, w, False) - u
            mixed = _bdot(pooled, pw_ref[0])
            zg = zg_ref[...]
            sg = _silu(zg)
            ps = ps_ref[...]
            dyv = dy_ref[...]
            dps_ref[...] += _colsum(dyv * mixed * sg)
            dg_ref[...] = (dyv * mixed * ps * _dsilu(zg)).astype(BF16)
            dmixed = dyv * ps * sg
            dpw_ref[0] += _bdot_tn(pooled, dmixed)
            dpooled = _bdot(dmixed, pwt_ref[0])
            du_ref[...] = (_pool_window(dpooled, pos_v, sl_v, w, True) - dpooled).astype(BF16)

        _by_group(pl.program_id(0), run)

    blk = pl.BlockSpec((t, LANES), lambda g, b: (b, g))
    return pl.pallas_call(
        body, name=name, grid=(4, nb),
        in_specs=_pool_specs(t) + [blk, pl.BlockSpec((1, LANES, LANES), lambda g, b: (g, 0, 0))],
        out_specs=[blk, blk, pl.BlockSpec((1, LANES, LANES), lambda g, b: (g, 0, 0)),
                   pl.BlockSpec((1, LANES), lambda g, b: (0, g))],
        out_shape=(jax.ShapeDtypeStruct((n, 512), BF16), jax.ShapeDtypeStruct((n, 512), BF16),
                   jax.ShapeDtypeStruct((4, 128, 128), F32), jax.ShapeDtypeStruct((1, 512), F32)),
        compiler_params=_cp(48, ("arbitrary", "arbitrary")),
    )(z, z, pool_w, pool_scale, pos, seglen, dy, pool_w_t)


def _gla_chunk(q_ref, k_ref, a_ref, w2_ref, b2_ref, reverse):
    c = GLA_C
    x = _bdot(a_ref[...], w2_ref[0]) + b2_ref[0]
    la = (jnp.minimum(x, 0.0) - jnp.log(1.0 + jnp.exp(-jnp.abs(x)))) * (1.0 / GLA_TAU)
    row = lax.broadcasted_iota(jnp.int32, (c, c), 0)
    col = lax.broadcasted_iota(jnp.int32, (c, c), 1)
    tri = (col >= row) if reverse else (col <= row)
    tri_t = (col <= row) if reverse else (col >= row)
    b = _xdot(tri.astype(F32), la)
    tok = lax.broadcasted_iota(jnp.int32, la.shape, 0)
    bref = _colsum(jnp.where((tok >= c // 2) if reverse else (tok < c // 2), la, 0.0))
    blast = _colsum(la)
    eq = jnp.exp(jnp.minimum(b - bref, EXP_CLAMP))
    ek = jnp.exp(jnp.minimum(bref - b, EXP_CLAMP))
    qs = q_ref[...] * (GLA_DK ** -0.5)
    kk = k_ref[...]
    eb = jnp.exp(b)
    etail = jnp.exp(blast - b)
    return dict(x=x, la=la, tri=tri, tri_t=tri_t, eq=eq, ek=ek, qs=qs, kk=kk, qd=qs * eq, kd=kk * ek, qe=qs * eb,
                kl=kk * etail, eb=eb, etail=etail)


def _pair(a, p):
    return a[:, p * LANES:(p + 1) * LANES]


def _head_masks():
    lane = lax.broadcasted_iota(jnp.int32, (GLA_C, LANES), 1)
    return (lane < GLA_DK, lane >= GLA_DK)


def _state_decay(la, p):
    return jnp.exp(_xdot_tn(_pair(la, p), jnp.ones((GLA_C, GLA_DV), F32)))


def _gla_chunk_maps(nb, nc, ncc, order):
    def rmap(j):
        return jnp.where(j < ncc, ncc - 1 - j, nc - 1 - (j - ncc))

    if order == 'scan':
        return (lambda b, j: b * nc + j), (lambda b, j: b * nc + rmap(j))
    return (lambda b, j: b * nc + nc - 1 - j), (lambda b, j: b * nc + rmap(nc - 1 - j))


def _gla_in_specs(maps):
    specs = []
    for d, cm in enumerate(maps):
        gate = 'gla_af' if d == 0 else 'gla_ab'
        specs += [pl.BlockSpec((GLA_C, 256), lambda b, j, cm=cm: (cm(b, j), _blk('gla_q'))),
                  pl.BlockSpec((GLA_C, 256), lambda b, j, cm=cm: (cm(b, j), _blk('gla_k'))),
                  pl.BlockSpec((GLA_C, 512), lambda b, j, cm=cm: (cm(b, j), _blk('gla_v'))),
                  pl.BlockSpec((GLA_C, LANES), lambda b, j, cm=cm, gate=gate: (cm(b, j), _blk(gate))),
                  pl.BlockSpec((1, LANES, 256), lambda b, j, d=d: (d, 0, 0)),
                  pl.BlockSpec((1, 1, 256), lambda b, j, d=d: (d, 0, 0))]
    return specs


def _gla_fwd(z, w2, b2, nb, lc, name):
    n = z.shape[0]
    nc = n // nb // GLA_C
    maps = _gla_chunk_maps(nb, nc, lc // GLA_C, 'scan')

    def body(*refs):
        ins, (of_ref, ob_ref, sf_ref, sb_ref, s_sc) = refs[:12], refs[12:]

        @pl.when(pl.program_id(1) == 0)
        def _():
            s_sc[...] = jnp.zeros(s_sc.shape, F32)

        masks = _head_masks()
        for d, (o_ref, st_ref) in enumerate(((of_ref, sf_ref), (ob_ref, sb_ref))):
            q_ref, k_ref, v_ref, a_ref, w2_ref, b2_ref = ins[6 * d:6 * d + 6]
            ch = _gla_chunk(q_ref, k_ref, a_ref, w2_ref, b2_ref, d == 1)
            for p in range(2):
                s_prev = s_sc[d, p]
                st_ref[0, p] = s_prev
                s_new = _state_decay(ch['la'], p) * s_prev
                kd_p = _pair(ch['kd'], p)
                for hh in range(2):
                    h = 2 * p + hh
                    vv = v_ref[:, h * GLA_DV:(h + 1) * GLA_DV]
                    att = jnp.where(ch['tri'], _bdot_nt(jnp.where(masks[hh], _pair(ch['qd'], p), 0.0), kd_p), 0.0)
                    o_ref[:, h * GLA_DV:(h + 1) * GLA_DV] = (
                        _bdot(att, vv) + _bdot(jnp.where(masks[hh], _pair(ch['qe'], p), 0.0), s_prev))
                    s_new = s_new + _dot3(_split(jnp.where(masks[hh], _pair(ch['kl'], p), 0.0)), _split(vv), TN)
                s_sc[d, p] = s_new

    o_shape = jax.ShapeDtypeStruct((n, 512), F32)
    st_shape = jax.ShapeDtypeStruct((n // GLA_C, 2, LANES, GLA_DV), F32)
    return pl.pallas_call(
        body, name=name, grid=(nb, nc),
        in_specs=_gla_in_specs(maps),
        out_specs=[pl.BlockSpec((GLA_C, 512), lambda b, j: (maps[0](b, j), 0)),
                   pl.BlockSpec((GLA_C, 512), lambda b, j: (maps[1](b, j), 0)),
                   pl.BlockSpec((1, 2, LANES, GLA_DV), lambda b, j: (maps[0](b, j), 0, 0, 0)),
                   pl.BlockSpec((1, 2, LANES, GLA_DV), lambda b, j: (maps[1](b, j), 0, 0, 0))],
        out_shape=(o_shape, o_shape, st_shape, st_shape),
        scratch_shapes=[pltpu.VMEM((2, 2, LANES, GLA_DV), F32)],
        compiler_params=_cp(32, ("arbitrary", "arbitrary")),
    )(z, z, z, z, w2, b2, z, z, z, z, w2, b2)


def _gla_bwd(z, w2, w2_t, b2, st_f, st_b, dog, nb, lc, name):
    n = z.shape[0]
    nc = n // nb // GLA_C
    maps = _gla_chunk_maps(nb, nc, lc // GLA_C, 'back')

    def body(*refs):
        ins, extra, outs, (ds_sc, sfx_sc) = refs[:12], refs[12:18], refs[18:30], refs[30:]

        @pl.when(pl.program_id(1) == 0)
        def _():
            ds_sc[...] = jnp.zeros(ds_sc.shape, F32)
            sfx_sc[...] = jnp.zeros(sfx_sc.shape, F32)

        @pl.when((pl.program_id(0) == 0) & (pl.program_id(1) == 0))
        def _():
            for r in outs[8:12]:
                r[...] = jnp.zeros(r.shape, F32)

        masks = _head_masks()
        for d in range(2):
            q_ref, k_ref, v_ref, a_ref, w2_ref, b2_ref = ins[6 * d:6 * d + 6]
            w2t_ref, st_ref, do_ref = extra[3 * d:3 * d + 3]
            dq_ref, dk_ref, dv_ref, da_ref = outs[4 * d:4 * d + 4]
            dw2_ref, db2_ref = outs[8 + 2 * d], outs[9 + 2 * d]
            ch = _gla_chunk(q_ref, k_ref, a_ref, w2_ref, b2_ref, d == 1)
            dqs, dks = [], []
            for p in range(2):
                s_prev = st_ref[0, p]
                ds_new = ds_sc[d, p]
                qd_p, kd_p, qe_p, kl_p = (_pair(ch[nme], p) for nme in ('qd', 'kd', 'qe', 'kl'))
                ds_prev = _state_decay(ch['la'], p) * ds_new
                qd_s, kd_s, sp_s, dsn_s = _split(qd_p), _split(kd_p), _split(s_prev), _split(ds_new)
                dq_h, dk_h = [], []
                for hh in range(2):
                    h = 2 * p + hh
                    vv = v_ref[:, h * GLA_DV:(h + 1) * GLA_DV]
                    dov = do_ref[:, h * GLA_DV:(h + 1) * GLA_DV]
                    att = jnp.where(ch['tri'], _bdot_nt(jnp.where(masks[hh], qd_p, 0.0), kd_p), 0.0)
                    dv_ref[:, h * GLA_DV:(h + 1) * GLA_DV] = (
                        _bdot_tn(att, dov) + _bdot(jnp.where(masks[hh], kl_p, 0.0), ds_new))
                    vv_s, dov_s = _split(vv), _split(dov)
                    datt_s = _split(jnp.where(ch['tri'], _dot3(dov_s, vv_s, NT), 0.0))
                    dq_h.append(_dot3(datt_s, kd_s, NN) * _pair(ch['eq'], p)
                                + _dot3(dov_s, sp_s, NT) * _pair(ch['eb'], p))
                    dk_h.append(_dot3(datt_s, qd_s, TN) * _pair(ch['ek'], p)
                                + _dot3(vv_s, dsn_s, NT) * _pair(ch['etail'], p))
                    ds_prev = ds_prev + _dot3(_split(jnp.where(masks[hh], qe_p, 0.0)), dov_s, TN)
                ds_sc[d, p] = ds_prev
                dqs.append(jnp.where(masks[0], dq_h[0], dq_h[1]))
                dks.append(jnp.where(masks[0], dk_h[0], dk_h[1]))
            dq = jnp.concatenate(dqs, axis=1)
            dk = jnp.concatenate(dks, axis=1)
            dq_ref[...] = dq * (GLA_DK ** -0.5)
            dk_ref[...] = dk
            db = ch['qs'] * dq - ch['kk'] * dk
            dla = _xdot(ch['tri_t'].astype(F32), db) + sfx_sc[d]
            sfx_sc[d] = sfx_sc[d] + _colsum(db)
            dx = dla * (1.0 / GLA_TAU) * _sigmoid(-ch['x'])
            da_ref[...] = _bdot(dx, w2t_ref[0])
            dw2_ref[...] += _bdot_tn(a_ref[...], dx)
            db2_ref[...] += _colsum(dx)

    extra_specs, out_specs = [], []
    for d, cm in enumerate(maps):
        extra_specs += [pl.BlockSpec((1, 256, LANES), lambda b, j, d=d: (d, 0, 0)),
                        pl.BlockSpec((1, 2, LANES, GLA_DV), lambda b, j, cm=cm: (cm(b, j), 0, 0, 0)),
                        pl.BlockSpec((GLA_C, 512), lambda b, j, cm=cm: (cm(b, j), 0))]
        out_specs += [pl.BlockSpec((GLA_C, 256), lambda b, j, cm=cm: (cm(b, j), 0)),
                      pl.BlockSpec((GLA_C, 256), lambda b, j, cm=cm: (cm(b, j), 0)),
                      pl.BlockSpec((GLA_C, 512), lambda b, j, cm=cm: (cm(b, j), 0)),
                      pl.BlockSpec((GLA_C, LANES), lambda b, j, cm=cm: (cm(b, j), 0))]
    const2 = lambda b, j: (0, 0)
    out_specs += [pl.BlockSpec((LANES, 256), const2), pl.BlockSpec((1, 256), const2)] * 2
    per_dir = (jax.ShapeDtypeStruct((n, 256), F32), jax.ShapeDtypeStruct((n, 256), F32),
               jax.ShapeDtypeStruct((n, 512), F32), jax.ShapeDtypeStruct((n, LANES), F32))
    wshape = (jax.ShapeDtypeStruct((LANES, 256), F32), jax.ShapeDtypeStruct((1, 256), F32))
    return pl.pallas_call(
        body, name=name, grid=(nb, nc),
        in_specs=_gla_in_specs(maps) + extra_specs,
        out_specs=out_specs,
        out_shape=per_dir + per_dir + wshape + wshape,
        scratch_shapes=[pltpu.VMEM((2, 2, LANES, GLA_DV), F32), pltpu.VMEM((2, 1, 256), F32)],
        compiler_params=_cp(32, ("arbitrary", "arbitrary")),
    )(z, z, z, z, w2, b2, z, z, z, z, w2, b2, w2_t, st_f, dog, w2_t, st_b, dog)


def _add_cast(a, b, name):
    n, w = a.shape

    def body(a_ref, b_ref, o_ref):
        o_ref[...] = (a_ref[...] + b_ref[...]).astype(BF16)

    return pl.pallas_call(
        body, name=name, grid=(n // TM,),
        in_specs=[pl.BlockSpec((TM, w), lambda i: (i, 0)), pl.BlockSpec((TM, w), lambda i: (i, 0))],
        out_specs=pl.BlockSpec((TM, w), lambda i: (i, 0)),
        out_shape=jax.ShapeDtypeStruct((n, w), BF16),
        compiler_params=_cp(32, ("arbitrary",)),
    )(a, b)


def _gla_out_norm(og):
    hats, rs = [], []
    for h in range(GLA_H):
        seg = og[:, h * GLA_DV:(h + 1) * GLA_DV]
        r = _rstd(seg)
        hats.append(seg * r)
        rs.append(r)
    return hats, rs


def _merge_branches(zm_ref, zgm_ref, zgg_ref, om_ref, yp_ref, ogf_ref, ogb_ref, gn_ref, wbm_ref, wbp_ref, wbg_ref):
    zgm, zgg = zgm_ref[...], zgg_ref[...]
    om = om_ref[...]
    y_mla = om * _silu(zgm)
    hats, rs = _gla_out_norm(ogf_ref[...] + ogb_ref[...])
    gn = gn_ref[...]
    sgg = _silu(zgg)
    y_gla = jnp.concatenate([hats[h] * gn for h in range(GLA_H)], axis=1) * sgg
    ys = (y_mla, yp_ref[...], y_gla)
    ps = (_bdot(y_mla, wbm_ref[...]), jnp.dot(yp_ref[...], wbp_ref[...], preferred_element_type=F32),
          _bdot(y_gla, wbg_ref[...]))
    zm = zm_ref[...]
    gs = tuple(_sigmoid(zm[:, a * D:(a + 1) * D]) for a in range(3))
    merged = gs[0] * ps[0] + gs[1] * ps[1] + gs[2] * ps[2]
    return dict(zgm=zgm, zgg=zgg, om=om, hats=hats, rs=rs, gn=gn, sgg=sgg, ys=ys, ps=ps, gs=gs, merged=merged)


def _merge_in_specs(tpe):
    row = lambda i: (i, 0)
    const = lambda i: (0, 0)
    return [pl.BlockSpec((TM, 3 * D), lambda i: (i, _blk('merge'))),
            pl.BlockSpec((TM, 512), lambda i: (i, _blk('mla_gate'))),
            pl.BlockSpec((TM, 512), lambda i: (i, _blk('gla_gate'))),
            pl.BlockSpec((TM, 512), row), pl.BlockSpec((TM, 512), row), pl.BlockSpec((TM, 512), row),
            pl.BlockSpec((TM, 512), row), pl.BlockSpec((1, 128), const),
            pl.BlockSpec((512, D), const), pl.BlockSpec((512, D), const), pl.BlockSpec((512, D), const),
            pl.BlockSpec((1, 1, 3 * D), lambda i: (_modrow(i, tpe), 0, 0)), pl.BlockSpec((1, D), const)]


def _merge_fwd(x, z, o_mla, y_pool, ogf, ogb, gla_n, wbm, wbp, wbg, wout, modl, post_g, tpe, name):
    n = x.shape[0]

    def body(zm_ref, zgm_ref, zgg_ref, om_ref, yp_ref, ogf_ref, ogb_ref, gn_ref, wbm_ref, wbp_ref, wbg_ref,
             m_ref, pg_ref, x_ref, wo_ref, xn_ref, out_ref):
        mb = _merge_branches(zm_ref, zgm_ref, zgg_ref, om_ref, yp_ref, ogf_ref, ogb_ref, gn_ref,
                             wbm_ref, wbp_ref, wbg_ref)
        out = _bdot(mb['merged'], wo_ref[...])
        gate = m_ref[0][:, 2 * D:3 * D]
        xn_ref[...] = x_ref[...] + gate * (out * _rstd(out) * pg_ref[...])
        out_ref[...] = out

    row = lambda i: (i, 0)
    return pl.pallas_call(
        body, name=name, grid=(n // TM,),
        in_specs=_merge_in_specs(tpe) + [pl.BlockSpec((TM, D), row), pl.BlockSpec((D, D), lambda i: (0, 0))],
        out_specs=[pl.BlockSpec((TM, D), row), pl.BlockSpec((TM, D), row)],
        out_shape=(jax.ShapeDtypeStruct((n, D), F32), jax.ShapeDtypeStruct((n, D), F32)),
        compiler_params=_cp(48, ("arbitrary",)),
    )(z, z, z, o_mla, y_pool, ogf, ogb, gla_n, wbm, wbp, wbg, modl.reshape(8, 1, 3 * D), post_g, x, wout)


def _merge_bwd(dxn, out, z, o_mla, y_pool, ogf, ogb, gla_n, wbm, wbp, wbg, wbm_t, wbp_t, wbg_t, wout_t,
               modl, post_g, tpe, name):
    n = out.shape[0]
    nt = n // TM

    def body(zm_ref, zgm_ref, zgg_ref, om_ref, yp_ref, ogf_ref, ogb_ref, gn_ref, wbm_ref, wbp_ref, wbg_ref,
             m_ref, pg_ref, dxn_ref, out_ref, wbmt_ref, wbpt_ref, wbgt_ref, wot_ref,
             dzm_ref, dom_ref, dzgm_ref, dyp_ref, dog_ref, dzgg_ref, st_ref,
             dwbm_ref, dwbp_ref, dwbg_ref, dwo_ref, dgn_ref):
        @pl.when(pl.program_id(0) == 0)
        def _():
            for r in (dwbm_ref, dwbp_ref, dwbg_ref, dwo_ref, dgn_ref):
                r[...] = jnp.zeros(r.shape, F32)

        mb = _merge_branches(zm_ref, zgm_ref, zgg_ref, om_ref, yp_ref, ogf_ref, ogb_ref, gn_ref,
                             wbm_ref, wbp_ref, wbg_ref)
        out = out_ref[...]
        r2 = _rstd(out)
        on = out * r2
        pg = pg_ref[...]
        gate = m_ref[0][:, 2 * D:3 * D]
        dxn_v = dxn_ref[...]
        st_ref[0, 0:1, :] = _colsum(dxn_v * on * pg)
        st_ref[0, 1:2, :] = _colsum(dxn_v * gate * on)
        st_ref[0, 2:8, :] = jnp.zeros((6, D), F32)
        dout = _norm_bwd(on, r2, dxn_v * gate * pg)
        dwo_ref[...] += _bdot_tn(mb['merged'], dout)
        dmerged = _bdot(dout, wot_ref[...])
        dys = []
        for a, (dw_ref, wt_ref) in enumerate(((dwbm_ref, wbmt_ref), (dwbp_ref, wbpt_ref), (dwbg_ref, wbgt_ref))):
            g = mb['gs'][a]
            dzm_ref[:, a * D:(a + 1) * D] = (dmerged * mb['ps'][a] * g * (1.0 - g)).astype(BF16)
            dp = dmerged * g
            dw_ref[...] += _bdot_tn(mb['ys'][a], dp)
            dys.append(_bdot(dp, wt_ref[...]))
        dom_ref[...] = dys[0] * _silu(mb['zgm'])
        dzgm_ref[...] = (dys[0] * mb['om'] * _dsilu(mb['zgm'])).astype(BF16)
        dyp_ref[...] = dys[1]
        gn = mb['gn']
        dgn = jnp.zeros((1, GLA_DV), F32)
        dzgg, dog = [], []
        for h in range(GLA_H):
            sl = slice(h * GLA_DV, (h + 1) * GLA_DV)
            dyg = dys[2][:, sl]
            hat = mb['hats'][h]
            dzgg.append(dyg * hat * gn * _dsilu(mb['zgg'][:, sl]))
            dn = dyg * mb['sgg'][:, sl]
            dgn = dgn + _colsum(dn * hat)
            dog.append(_norm_bwd(hat, mb['rs'][h], dn * gn))
        dgn_ref[...] += dgn
        dzgg_ref[...] = jnp.concatenate(dzgg, axis=1).astype(BF16)
        dog_ref[...] = jnp.concatenate(dog, axis=1)

    row = lambda i: (i, 0)
    const = lambda i: (0, 0)
    wspec = pl.BlockSpec((512, D), const)
    wtspec = pl.BlockSpec((D, 512), const)
    return pl.pallas_call(
        body, name=name, grid=(nt,),
        in_specs=_merge_in_specs(tpe) + [pl.BlockSpec((TM, D), row), pl.BlockSpec((TM, D), row),
                                         wtspec, wtspec, wtspec,
                                         pl.BlockSpec((D, D), const)],
        out_specs=[pl.BlockSpec((TM, 3 * D), row), pl.BlockSpec((TM, 512), row), pl.BlockSpec((TM, 512), row),
                   pl.BlockSpec((TM, 512), row), pl.BlockSpec((TM, 512), row), pl.BlockSpec((TM, 512), row),
                   pl.BlockSpec((1, 8, D), lambda i: (i, 0, 0)),
                   wspec, wspec, wspec, pl.BlockSpec((D, D), const), pl.BlockSpec((1, 128), const)],
        out_shape=(jax.ShapeDtypeStruct((n, 3 * D), BF16), jax.ShapeDtypeStruct((n, 512), F32),
                   jax.ShapeDtypeStruct((n, 512), BF16), jax.ShapeDtypeStruct((n, 512), F32),
                   jax.ShapeDtypeStruct((n, 512), F32), jax.ShapeDtypeStruct((n, 512), BF16),
                   jax.ShapeDtypeStruct((nt, 8, D), F32),
                   jax.ShapeDtypeStruct((512, D), F32), jax.ShapeDtypeStruct((512, D), F32),
                   jax.ShapeDtypeStruct((512, D), F32), jax.ShapeDtypeStruct((D, D), F32),
                   jax.ShapeDtypeStruct((1, 128), F32)),
        compiler_params=_cp(56, ("arbitrary",)),
    )(z, z, z, o_mla, y_pool, ogf, ogb, gla_n, wbm, wbp, wbg, modl.reshape(8, 1, 3 * D), post_g,
      dxn, out, wbm_t, wbp_t, wbg_t, wout_t)


def _loss_grad(xf, tgt, nb, tpe):
    n = xf.shape[0]

    def body(x_ref, t_ref, dx_ref, l_ref):
        j = pl.program_id(1)
        d = x_ref[...] - t_ref[...]
        latent = j > 0
        dx_ref[...] = jnp.where(latent, d * (1.0 / D), 0.0)
        l_ref[...] = jnp.full(l_ref.shape, jnp.where(latent, 0.5 / D * jnp.sum(d * d), 0.0), F32)

    return pl.pallas_call(
        body, name="loss_grad", grid=(nb, tpe),
        in_specs=[pl.BlockSpec((TM, D), lambda b, j: (b * tpe + j, 0)),
                  pl.BlockSpec((TM, D), lambda b, j: (b * (tpe - 1) + jnp.maximum(j - 1, 0), 0))],
        out_specs=[pl.BlockSpec((TM, D), lambda b, j: (b * tpe + j, 0)),
                   pl.BlockSpec((1, 8, 128), lambda b, j: (b * tpe + j, 0, 0))],
        out_shape=(jax.ShapeDtypeStruct((n, D), F32), jax.ShapeDtypeStruct((n // TM, 8, 128), F32)),
        compiler_params=_cp(32, ("arbitrary", "arbitrary")),
    )(xf, tgt)


def _to_padded(w_nat):
    parts = []
    for nme in PAD_ORDER:
        p = w_nat[..., NAT_OFF[nme]:NAT_OFF[nme] + NAT_SIZE[nme]]
        if SLAB[nme] > NAT_SIZE[nme]:
            p = jnp.pad(p, [(0, 0)] * (p.ndim - 1) + [(IN_SLAB[nme], SLAB[nme] - NAT_SIZE[nme] - IN_SLAB[nme])])
        parts.append(p)
    return jnp.concatenate(parts, axis=-1)


def _from_padded(w_pad):
    return jnp.concatenate([w_pad[..., PAD_OFF[nme] + IN_SLAB[nme]:PAD_OFF[nme] + IN_SLAB[nme] + NAT_SIZE[nme]]
                            for nme in IN_NAMES], axis=-1)


def _rope_tables(lc, l):
    half = MLA_ROPE // 2
    inv = ROPE_BASE ** (-jnp.arange(0, half, 2, dtype=F32) / half)
    tok = jnp.arange(l)
    ang_r = (tok // GRID_W).astype(F32)[:, None] * inv
    ang_c = (tok % GRID_W).astype(F32)[:, None] * inv
    ang = jnp.concatenate([ang_r, ang_r, ang_c, ang_c], axis=-1)
    cos = jnp.concatenate([jnp.ones((lc, MLA_ROPE), F32), jnp.cos(ang)], axis=0)
    sin = jnp.concatenate([jnp.zeros((lc, MLA_ROPE), F32), jnp.sin(ang)], axis=0)
    t = lc + l
    tail = MLA_HP - MLA_QK
    ck = jnp.concatenate([jnp.ones((t, MLA_NOPE), F32), cos, jnp.ones((t, tail), F32)], axis=1)
    sk = jnp.concatenate([jnp.zeros((t, MLA_NOPE), F32), sin, jnp.zeros((t, tail), F32)], axis=1)
    return jnp.tile(ck, (1, MLA_H)), jnp.tile(sk, (1, MLA_H)), ck, sk


def _pad_heads(w):
    lead = w.shape[:-1]
    w = w.reshape(lead + (MLA_H, MLA_QK))
    return jnp.pad(w, [(0, 0)] * len(lead) + [(0, 0), (0, MLA_HP - MLA_QK)]).reshape(lead + (MLA_H * MLA_HP,))


def _unpad_heads(w):
    lead = w.shape[:-1]
    return w.reshape(lead + (MLA_H, MLA_HP))[..., :MLA_QK].reshape(lead + (MLA_H * MLA_QK,))


def _local_step(x, c, ctx, tgt, wf):
    nb, l, _ = x.shape
    lc = ctx.shape[1]
    assert lc == TM and l % TM == 0
    t = lc + l
    tpe = t // TM
    n = nb * t
    nt = n // TM
    bf = lambda a: a.astype(BF16)

    xs = jnp.concatenate([ctx, x], axis=1).reshape(n, D)
    assert nb <= 4
    cv = jnp.concatenate([c, jnp.zeros((4 - nb, D), F32), wf['c_ctx'][None, :], jnp.zeros((3, D), F32)], axis=0)
    mod_w_b = bf(wf['mod_w'])
    mod_all = _mod_fwd(cv, mod_w_b, wf['mod_b'].reshape(DEPTH, 1, 3 * D))
    cq, sq, ck, sk = _rope_tables(lc, l)
    pos = jnp.concatenate([jnp.arange(lc), jnp.arange(l)]).astype(jnp.int32)[:, None]
    seglen = jnp.concatenate([jnp.full((lc,), lc), jnp.full((l,), l)]).astype(jnp.int32)[:, None]
    tiles = np.arange(nt)
    ntp = -(-nt // LANES) * LANES
    sel = np.zeros((8, ntp), np.float32)
    sel[np.where(tiles % tpe == 0, 4, tiles // tpe), tiles] = 1.0
    sel = jnp.asarray(sel)

    def tile_sums(st):
        return jnp.pad(st.transpose(1, 0, 2), ((0, 0), (0, ntp - nt), (0, 0)))

    lw = []
    for ly in range(DEPTH):
        w_in_p = _to_padded(bf(wf['w_in'][ly]))
        w_uq_p = _pad_heads(bf(wf['mla_w_uq'][ly]))
        w2 = jnp.pad(jnp.stack([bf(wf['gla_af_w2'][ly]), bf(wf['gla_ab_w2'][ly])]),
                     ((0, 0), (0, LANES - GLA_RANK), (0, 0)))
        lw.append(dict(
            w_in=w_in_p, w_in_t=w_in_p.T,
            w_uq=w_uq_p, w_uq_t=w_uq_p.T,
            w_ukv=bf(wf['mla_w_ukv'][ly]), w_ukv_t=bf(wf['mla_w_ukv'][ly]).T,
            pool_w=bf(wf['pool_w'][ly]), pool_w_t=bf(wf['pool_w'][ly]).transpose(0, 2, 1),
            w2=w2, w2_t=w2.transpose(0, 2, 1),
            b2=jnp.stack([wf['gla_af_b'][ly], wf['gla_ab_b'][ly]]).reshape(2, 1, GLA_H * GLA_DK),
            wbm=bf(wf['w_branch_mla'][ly]), wbp=bf(wf['w_branch_pool'][ly]), wbg=bf(wf['w_branch_gla'][ly]),
            wout=bf(wf['w_out'][ly]),
            wbm_t=bf(wf['w_branch_mla'][ly]).T, wbp_t=bf(wf['w_branch_pool'][ly]).T,
            wbg_t=bf(wf['w_branch_gla'][ly]).T, wout_t=bf(wf['w_out'][ly]).T,
            pre_g=wf['pre_norm'][ly][None, :], post_g=wf['post_norm'][ly][None, :],
            q_norm=wf['mla_q_norm'][ly][None, :], kv_norm=wf['mla_kv_norm'][ly][None, :],
            pool_scale=wf['pool_scale'][ly][None, :], gla_norm=wf['gla_norm'][ly][None, :]))

    saved = []
    xcur = xs
    for ly in range(DEPTH):
        w = lw[ly]
        z, h = _pre_fwd(xcur, mod_all[ly], w['pre_g'], w['w_in'], tpe, f"pre_fwd{ly}")
        qb, kvb, krb = _mla_pre(z, w['q_norm'], w['kv_norm'], w['w_uq'], w['w_ukv'], cq, sq, ck, sk, tpe, f"mla_pre{ly}")
        o_mla, lse = _attn_fwd(qb, kvb, krb, nb, lc, f"attn_fwd{ly}")
        y_pool = _pool_fwd(z, w['pool_w'], w['pool_scale'], pos, seglen, nb, f"pool_fwd{ly}")
        ogf, ogb, st_f, st_r = _gla_fwd(z, w['w2'], w['b2'], nb, lc, f"gla_fwd{ly}")
        xnew, out = _merge_fwd(xcur, z, o_mla, y_pool, ogf, ogb, w['gla_norm'], w['wbm'], w['wbp'], w['wbg'],
                               w['wout'], mod_all[ly], w['post_g'], tpe, f"merge_fwd{ly}")
        saved.append(dict(x=xcur, z=z, h=h, qb=qb, kvb=kvb, krb=krb, lse=lse, o_mla=o_mla, y_pool=y_pool,
                          st_f=st_f, st_r=st_r, ogf=ogf, ogb=ogb, out=out))
        xcur = xnew

    dxcur, lparts = _loss_grad(xcur, tgt.reshape(nb * l, D), nb, tpe)
    loss = jnp.sum(lparts[:, 0, 0])

    g = {k: [None] * DEPTH for k in WEIGHTS if k != 'c_ctx'}
    dcv = jnp.zeros((8, D), F32)
    dcc = None
    for ly in reversed(range(DEPTH)):
        w, s = lw[ly], saved[ly]
        (dzm, dom, dzgm, dyp, dog, dzgg, st_b, g['w_branch_mla'][ly], g['w_branch_pool'][ly], g['w_branch_gla'][ly],
         g['w_out'][ly], dgn) = _merge_bwd(
            dxcur, s['out'], s['z'], s['o_mla'], s['y_pool'], s['ogf'], s['ogb'], w['gla_norm'], w['wbm'], w['wbp'],
            w['wbg'], w['wbm_t'], w['wbp_t'], w['wbg_t'], w['wout_t'], mod_all[ly], w['post_g'], tpe,
            f"merge_bwd{ly}")
        g['gla_norm'][ly] = dgn[0]
        dq, dkv, dkr = _attn_bwd(s['qb'], s['kvb'], s['krb'], s['o_mla'], s['lse'], dom, nb, lc, f"attn_bwd{ly}")
        dzq, dzkv, dzkr, dwq, g['mla_w_ukv'][ly], dgq, dgkv = _mla_pre_bwd(
            s['z'], dq, dkv, dkr, w['q_norm'], w['kv_norm'], w['w_uq_t'], w['w_ukv_t'], cq, sq, ck, sk, tpe,
            f"mla_pre_bwd{ly}")
        g['mla_w_uq'][ly] = _unpad_heads(dwq)
        g['mla_q_norm'][ly], g['mla_kv_norm'][ly] = dgq[0], dgkv[0]
        dzpx, dzpg, g['pool_w'][ly], dps = _pool_bwd(s['z'], dyp, w['pool_w'], w['pool_w_t'], w['pool_scale'],
                                                     pos, seglen, nb, f"pool_bwd{ly}")
        g['pool_scale'][ly] = dps[0]
        (dq_f, dk_f, dv_f, da_f, dq_r, dk_r, dv_r, da_r, dw2_f, db2_f, dw2_r, db2_r) = _gla_bwd(
            s['z'], w['w2'], w['w2_t'], w['b2'], s['st_f'], s['st_r'], dog, nb, lc, f"gla_bwd{ly}")
        dzgq = _add_cast(dq_f, dq_r, f"gla_dq{ly}")
        dzgk = _add_cast(dk_f, dk_r, f"gla_dk{ly}")
        dzgv = _add_cast(dv_f, dv_r, f"gla_dv{ly}")
        g['gla_af_w2'][ly], g['gla_ab_w2'][ly] = dw2_f[:GLA_RANK], dw2_r[:GLA_RANK]
        g['gla_af_b'][ly], g['gla_ab_b'][ly] = db2_f[0], db2_r[0]
        parts = dict(merge=dzm, mla_gate=dzgm, mla_q=dzq, mla_kv=dzkv, mla_kr=dzkr, pool_x=dzpx, pool_gate=dzpg,
                     gla_v=dzgv, gla_gate=dzgg, gla_q=dzgq, gla_k=dzgk, gla_af=bf(da_f), gla_ab=bf(da_r))
        dz = jnp.concatenate([parts[nme] for nme in PAD_ORDER], axis=1)
        dxcur, st_a = _pre_bwd(dz, w['w_in_t'], s['x'], dxcur, mod_all[ly], w['pre_g'], tpe, f"pre_bwd{ly}")
        tk = next(k for k in (1024, 512, TM) if n % k == 0)
        g['w_in'][ly] = _from_padded(_matmul_tn(s['h'], dz, 768, tk, f"w_in_grad{ly}"))
        dmw, dmb, dcv, dcc, dpre, dpost = _mod_bwd(cv, sel, tile_sums(st_a), tile_sums(st_b),
                                                   mod_w_b[ly].T, dcv, f"mod_bwd{ly}")
        g['mod_w'][ly], g['mod_b'][ly] = dmw, dmb[0]
        g['pre_norm'][ly], g['post_norm'][ly] = dpre[0], dpost[0]

    grads = {k: jnp.stack(v) for k, v in g.items()}
    grads['c_ctx'] = dcc[4]
    grad_x = dxcur.reshape(nb, t, D)[:, lc:, :]
    return loss, grad_x, grads


def _place():
    x, y, c = lax.axis_index("x"), lax.axis_index("y"), lax.axis_index("c")
    chips = [(1 - x, y), (x, 1 - y), (1 - x, 1 - y)]
    return x, y, c, chips


def _hbm_call(body, name, out_shape, n_in, sems):
    any_spec = pl.BlockSpec(memory_space=pl.ANY)
    return pl.pallas_call(body, name=name, out_shape=out_shape, in_specs=[any_spec] * n_in,
                          out_specs=jax.tree.map(lambda _: any_spec, out_shape), scratch_shapes=sems)


def _all_gather_shards(ws):
    n = len(ws)

    def body(*refs):
        ins, outs, (send_sems, recv_sems) = refs[:n], refs[n:2 * n], refs[2 * n:]
        x, y, c, chips = _place()

        def copy(k, q, chip, half, to, src=None):
            dst = outs[k].at[2 * chip[0] + chip[1], half]
            return pltpu.make_async_remote_copy(src_ref=dst if src is None else src, dst_ref=dst,
                                                send_sem=send_sems.at[k, q], recv_sem=recv_sems.at[k, q],
                                                device_id=to, device_id_type=MESH)

        first = [copy(k, j, (x, y), c, (*chip, c), src=ins[k].at[c]) for k in range(n) for j, chip in enumerate(chips)]
        for cp in first:
            cp.start()
        passed = []
        for k in range(n):
            for j, chip in enumerate(chips):
                copy(k, j, chip, c, (x, y, c)).wait_recv()
                passed.append(copy(k, 3 + j, chip, c, (x, y, 1 - c)))
                passed[-1].start()
        for k in range(n):
            for j, chip in enumerate(chips):
                copy(k, 3 + j, chip, 1 - c, (x, y, 1 - c)).wait_recv()
        for cp in first + passed:
            cp.wait_send()

    shapes = tuple(jax.ShapeDtypeStruct((N_CHIPS,) + w.shape, w.dtype) for w in ws)
    return _hbm_call(body, "all_gather_shards", shapes, n,
                     [pltpu.SemaphoreType.DMA((n, 6)), pltpu.SemaphoreType.DMA((n, 6))])(*ws)


def _to_sibling(arrs, other_layer, name):
    n = len(arrs)

    def body(*refs):
        ins, outs, (send_sems, recv_sems) = refs[:n], refs[n:2 * n], refs[2 * n:]
        x, y, c, _ = _place()
        cps = [pltpu.make_async_remote_copy(src_ref=ins[k].at[1 - c] if other_layer else ins[k], dst_ref=outs[k],
                                            send_sem=send_sems.at[k], recv_sem=recv_sems.at[k],
                                            device_id=(x, y, 1 - c), device_id_type=MESH) for k in range(n)]
        for cp in cps:
            cp.start()
        for cp in cps:
            cp.wait()

    shapes = tuple(jax.ShapeDtypeStruct(a.shape[1:] if other_layer else a.shape, a.dtype) for a in arrs)
    return _hbm_call(body, name, shapes, n, [pltpu.SemaphoreType.DMA((n,)), pltpu.SemaphoreType.DMA((n,))])(*arrs)


def _scatter_to_chips(hs):
    n = len(hs)

    def body(*refs):
        ins, outs, (send_sems, recv_sems) = refs[:n], refs[n:2 * n], refs[2 * n:]
        x, y, c, chips = _place()
        me = 2 * x + y
        sends = []
        for k in range(n):
            for j, chip in enumerate(chips):
                cp = pltpu.make_async_remote_copy(src_ref=ins[k].at[2 * chip[0] + chip[1]], dst_ref=outs[k].at[me],
                                                  send_sem=send_sems.at[k, j], recv_sem=recv_sems.at[k, j],
                                                  device_id=(*chip, c), device_id_type=MESH)
                cp.start()
                sends.append(cp)
        for k in range(n):
            for j, chip in enumerate(chips):
                dst = outs[k].at[2 * chip[0] + chip[1]]
                pltpu.make_async_remote_copy(src_ref=dst, dst_ref=dst, send_sem=send_sems.at[k, j],
                                             recv_sem=recv_sems.at[k, j], device_id=(*chip, c),
                                             device_id_type=MESH).wait_recv()
        for cp in sends:
            cp.wait_send()

    shapes = tuple(jax.ShapeDtypeStruct(h.shape, h.dtype) for h in hs)
    return _hbm_call(body, "scatter_to_chips", shapes, n,
                     [pltpu.SemaphoreType.DMA((n, 3)), pltpu.SemaphoreType.DMA((n, 3))])(*hs)


BLOCK_BYTES = 2 * 1024 * 1024


def _row_block(r, cols):
    if r * cols * 4 <= BLOCK_BYTES or r % 8:
        return r
    br = 8
    while r % (2 * br) == 0 and 2 * br * cols * 4 <= BLOCK_BYTES:
        br *= 2
    return br


def _add_cores(b, got, name):
    _, ns, r, cols = b.shape
    br = _row_block(r, 2 * cols)

    def body(b_ref, g_ref, o_ref):
        mine = jnp.where(lax.axis_index("c") == 0, b_ref[0, 0], b_ref[1, 0])
        o_ref[0] = (mine + g_ref[0]).astype(BF16)

    spec = pl.BlockSpec((1, br, cols), lambda i, j: (i, j, 0))
    return pl.pallas_call(body, name=name, grid=(ns, r // br),
                          in_specs=[pl.BlockSpec((2, 1, br, cols), lambda i, j: (0, i, j, 0)), spec], out_specs=spec,
                          out_shape=jax.ShapeDtypeStruct((ns, r, cols), BF16))(b, got)


def _sum_chips(own, got, name):
    _, r, cols = own.shape
    br = _row_block(r, 4 * cols)

    def body(own_ref, got_ref, o_ref):
        me = 2 * lax.axis_index("x") + lax.axis_index("y")
        part = [jnp.where(me == j, own_ref[j], got_ref[j]).astype(F32) for j in range(N_CHIPS)]
        o_ref[...] = ((part[0] + part[1]) + part[2]) + part[3]

    spec = pl.BlockSpec((N_CHIPS, br, cols), lambda j: (0, j, 0))
    return pl.pallas_call(body, name=name, grid=(r // br,), in_specs=[spec, spec],
                          out_specs=pl.BlockSpec((br, cols), lambda j: (j, 0)),
                          out_shape=jax.ShapeDtypeStruct((r, cols), F32))(own, got)


def _adamw(w, g_mine, g_other, m, v, name):
    _, r, cols = w.shape
    br = _row_block(r, 4 * cols)

    def body(w_ref, gm_ref, go_ref, m_ref, v_ref, g_ref, d_ref, nm_ref, nv_ref):
        gv = jnp.where(pl.program_id(0) == lax.axis_index("c"), gm_ref[...], go_ref[...])
        m2 = ADAM_B1 * m_ref[0] + (1.0 - ADAM_B1) * gv
        v2 = ADAM_B2 * v_ref[0] + (1.0 - ADAM_B2) * jnp.square(gv)
        m_hat = m2 / (1.0 - ADAM_B1 ** ADAM_STEP)
        v_hat = v2 / (1.0 - ADAM_B2 ** ADAM_STEP)
        g_ref[0] = gv
        d_ref[0] = -ADAM_LR * (m_hat / (jnp.sqrt(v_hat) + ADAM_EPS) + ADAM_WD * w_ref[0])
        nm_ref[0] = m2
        nv_ref[0] = v2

    lay = pl.BlockSpec((1, br, cols), lambda l, j: (l, j, 0))
    flat = pl.BlockSpec((br, cols), lambda l, j: (j, 0))
    shp = jax.ShapeDtypeStruct(w.shape, F32)
    return pl.pallas_call(body, name=name, grid=(2, r // br), in_specs=[lay, flat, flat, lay, lay],
                          out_specs=[lay] * 4, out_shape=(shp,) * 4)(w, g_mine, g_other, m, v)


def _pack_small(ts):
    flat = jnp.concatenate([ts[k].reshape(DEPTH, -1) for k in REPLICATED], axis=1)
    return flat.reshape(DEPTH, flat.shape[1] // LANES, LANES)


def _unpack_small(packed, like):
    flat = packed.reshape(DEPTH, -1)
    out, off = {}, 0
    for k in REPLICATED:
        size = like[k].size // DEPTH
        out[k] = flat[:, off:off + size].reshape(like[k].shape)
        off += size
    return out


def _shard_major(a, axis):
    if axis == 1:
        return a.reshape(DEPTH, N_CHIPS, a.shape[1] // N_CHIPS, a.shape[2])
    return a.reshape(DEPTH, a.shape[1], N_CHIPS, a.shape[2] // N_CHIPS).transpose(0, 2, 1, 3)


def kernel(x, c, ctx, c_ctx, mod_w, mod_b, pre_norm, post_norm, w_in, mla_q_norm, mla_w_uq, mla_kv_norm, mla_w_ukv, pool_w, pool_scale, gla_af_w2, gla_af_b, gla_ab_w2, gla_ab_b, gla_norm, w_branch_mla, w_branch_pool, w_branch_gla, w_out, loss_target, m_c_ctx, m_mod_w, m_mod_b, m_pre_norm, m_post_norm, m_w_in, m_mla_q_norm, m_mla_w_uq, m_mla_kv_norm, m_mla_w_ukv, m_pool_w, m_pool_scale, m_gla_af_w2, m_gla_af_b, m_gla_ab_w2, m_gla_ab_b, m_gla_norm, m_w_branch_mla, m_w_branch_pool, m_w_branch_gla, m_w_out, v_c_ctx, v_mod_w, v_mod_b, v_pre_norm, v_post_norm, v_w_in, v_mla_q_norm, v_mla_w_uq, v_mla_kv_norm, v_mla_w_ukv, v_pool_w, v_pool_scale, v_gla_af_w2, v_gla_af_b, v_gla_ab_w2, v_gla_ab_b, v_gla_norm, v_w_branch_mla, v_w_branch_pool, v_w_branch_gla, v_w_out):
    given = dict(locals())
    wts = {k: given[k] for k in WEIGHTS}
    my_chip = 2 * lax.axis_index("x") + lax.axis_index("y")

    mine = [wts[k].astype(BF16) for k, _ in SHARDED]
    gathered = _all_gather_shards(mine)
    full = dict(wts)
    for (k, axis), own, got in zip(SHARDED, mine, gathered):
        full[k] = jnp.concatenate([jnp.where(my_chip == s, own, got[s]) for s in range(N_CHIPS)], axis=axis)

    loss_local, grad_x, grads = _local_step(x, c, ctx, loss_target, full)
    loss = lax.psum(loss_local, ("x", "y", "c"))

    small = _pack_small(grads)
    bufs = [_shard_major(grads[k], axis) for k, axis in SHARDED]
    bufs.append(jnp.broadcast_to(small[:, None], (DEPTH, N_CHIPS) + small.shape[1:]))
    got = _to_sibling(bufs, True, "swap_halves")
    chip_sum = [_add_cores(b, g, f"add_cores{i}") for i, (b, g) in enumerate(zip(bufs, got))]
    recv = _scatter_to_chips(chip_sum)
    mine_red = [_sum_chips(cs, rc, f"sum_chips{i}") for i, (cs, rc) in enumerate(zip(chip_sum, recv))]
    other_red = _to_sibling(mine_red, False, "join_halves")

    outs = {}
    for i, (k, _) in enumerate(SHARDED):
        outs[k] = _adamw(wts[k], mine_red[i], other_red[i], given['m_' + k], given['v_' + k], f"adamw{i}")
    packed = _adamw(_pack_small(wts), mine_red[-1], other_red[-1],
                    _pack_small({k: given['m_' + k] for k in REPLICATED}),
                    _pack_small({k: given['v_' + k] for k in REPLICATED}), "adamw_small")
    unpacked = [_unpack_small(p, wts) for p in packed]
    for k in REPLICATED:
        outs[k] = tuple(u[k] for u in unpacked)
    return (loss, grad_x, *[outs[k][q] for q in range(4) for k in WEIGHTS])
```

```python
import functools

import numpy as np
import jax
import jax.numpy as jnp
from jax import lax
from jax.experimental import pallas as pl
from jax.experimental.pallas import tpu as pltpu

F32 = jnp.float32
BF16 = jnp.bfloat16
HIGHEST = lax.Precision.HIGHEST
MESH = pl.DeviceIdType.MESH

D = 1024
DEPTH = 2
EPS = 1e-6
GRID_W = 64
MLA_H, MLA_NOPE, MLA_ROPE, MLA_V = 8, 64, 32, 64
MLA_QK = MLA_NOPE + MLA_ROPE
ROPE_BASE = 10000.0
POOL_WINDOWS = (2, 4, 8, 16)
GLA_H, GLA_DK, GLA_DV, GLA_RANK, GLA_TAU = 4, 64, 128, 16, 16.0
GLA_C = 128
EXP_CLAMP = 80.0
ADAM_LR, ADAM_B1, ADAM_B2, ADAM_EPS, ADAM_WD, ADAM_STEP = 0.001, 0.9, 0.999, 1e-08, 0.01, 10

TM = 256
LANES = 128
N_CHIPS = 4

IN_NAMES = ('mla_q', 'mla_kv', 'mla_kr', 'mla_gate', 'pool_x', 'pool_gate',
            'gla_q', 'gla_k', 'gla_v', 'gla_af', 'gla_ab', 'gla_gate', 'merge')
IN_SIZES = (256, 128, 32, 512, 512, 512, 256, 256, 512, 16, 16, 512, 3 * D)
NAT_OFF = dict(zip(IN_NAMES, [int(o) for o in np.cumsum((0,) + IN_SIZES[:-1])]))
NAT_SIZE = dict(zip(IN_NAMES, IN_SIZES))
PAD_ORDER = ('merge', 'mla_gate', 'mla_q', 'mla_kv', 'mla_kr', 'pool_x', 'pool_gate',
             'gla_v', 'gla_gate', 'gla_q', 'gla_k', 'gla_af', 'gla_ab')
SLAB = {n: max(NAT_SIZE[n], LANES) for n in IN_NAMES}
PAD_OFF = dict(zip(PAD_ORDER, [int(o) for o in np.cumsum([0] + [SLAB[n] for n in PAD_ORDER[:-1]])]))
D_PAD = sum(SLAB.values())
IN_SLAB = {n: 0 for n in IN_NAMES}
IN_SLAB['mla_kr'] = MLA_NOPE
MLA_HP = 128


def _blk(name):
    return PAD_OFF[name] // SLAB[name]


SHARDED = (('mod_w', 2), ('w_in', 2), ('mla_w_uq', 2), ('mla_w_ukv', 2), ('gla_af_w2', 2), ('gla_ab_w2', 2),
           ('w_branch_mla', 2), ('w_branch_pool', 2), ('w_branch_gla', 2), ('w_out', 1))
REPLICATED = ('c_ctx', 'mod_b', 'pre_norm', 'post_norm', 'mla_q_norm', 'mla_kv_norm', 'pool_w', 'pool_scale',
              'gla_af_b', 'gla_ab_b', 'gla_norm')
WEIGHTS = ('c_ctx', 'mod_w', 'mod_b', 'pre_norm', 'post_norm', 'w_in', 'mla_q_norm', 'mla_w_uq', 'mla_kv_norm',
           'mla_w_ukv', 'pool_w', 'pool_scale', 'gla_af_w2', 'gla_af_b', 'gla_ab_w2', 'gla_ab_b', 'gla_norm',
           'w_branch_mla', 'w_branch_pool', 'w_branch_gla', 'w_out')


def _cp(vmem_mb=None, sem=None):
    kw = {}
    if vmem_mb is not None:
        kw['vmem_limit_bytes'] = vmem_mb * 1024 * 1024
    if sem is not None:
        kw['dimension_semantics'] = sem
    return pltpu.CompilerParams(**kw)


def _bdot(a, b):
    return jnp.dot(a.astype(BF16), b.astype(BF16), preferred_element_type=F32)


def _bdot_nt(a, b):
    return lax.dot_general(a.astype(BF16), b.astype(BF16), (((1,), (1,)), ((), ())), preferred_element_type=F32)


def _bdot_tn(a, b):
    return lax.dot_general(a.astype(BF16), b.astype(BF16), (((0,), (0,)), ((), ())), preferred_element_type=F32)


def _xdot(a, b):
    return jnp.dot(a, b, precision=HIGHEST, preferred_element_type=F32)


def _xdot_tn(a, b):
    return lax.dot_general(a, b, (((0,), (0,)), ((), ())), precision=HIGHEST, preferred_element_type=F32)


NN = (((1,), (0,)), ((), ()))
NT = (((1,), (1,)), ((), ()))
TN = (((0,), (0,)), ((), ()))


def _split(a):
    hi = a.astype(BF16)
    return hi, (a - hi.astype(F32)).astype(BF16)


def _dot3(a, b, dims):
    (ah, al), (bh, bl) = a, b
    f = lambda u, v: lax.dot_general(u, v, dims, preferred_element_type=F32)
    return f(ah, bh) + (f(ah, bl) + f(al, bh))


def _sigmoid(x):
    return jax.nn.sigmoid(x)


def _silu(x):
    return x * _sigmoid(x)


def _dsilu(x):
    s = _sigmoid(x)
    return s * (1.0 + x * (1.0 - s))


def _rstd(x):
    return lax.rsqrt(jnp.mean(x * x, axis=-1, keepdims=True) + EPS)


def _norm_bwd(xhat, r, dy):
    return r * (dy - xhat * jnp.mean(xhat * dy, axis=-1, keepdims=True))


def _colsum(a):
    return jnp.sum(a, axis=0, keepdims=True)


def _modrow(i, tpe):
    return jnp.where(i % tpe == 0, 4, i // tpe)


def _rot(x):
    n = x.shape[-1]
    lane = lax.broadcasted_iota(jnp.int32, x.shape, x.ndim - 1)
    return jnp.where(lane % 16 < 8, -pltpu.roll(x, n - 8, x.ndim - 1), pltpu.roll(x, 8, x.ndim - 1))


def _mod_fwd(cv, mod_w, mod_b):
    def body(cv_ref, w_ref, b_ref, o_ref):
        s = _silu(cv_ref[...])
        for l in range(DEPTH):
            o_ref[l] = _bdot(s, w_ref[l]) + b_ref[l]

    return pl.pallas_call(body, name="mod_fwd", out_shape=jax.ShapeDtypeStruct((DEPTH, 8, 3 * D), F32),
                          compiler_params=_cp(40))(cv, mod_w, mod_b)


def _mod_bwd(cv, sel, st_a, st_b, w_t, dcv_in, name):
    def body(cv_ref, sel_ref, sa_ref, sb_ref, wt_ref, dcin_ref, dw_ref, db_ref, dcv_ref, dcc_ref, dpre_ref, dpost_ref):
        cvv = cv_ref[...]
        s = _silu(cvv)
        sel_v = sel_ref[...]
        dmod = jnp.concatenate([_xdot(sel_v, sa_ref[0]), _xdot(sel_v, sa_ref[1]), _xdot(sel_v, sb_ref[0])], axis=1)
        dw_ref[...] = _bdot_tn(s, dmod)
        db_ref[...] = _colsum(dmod)
        dcv = dcin_ref[...] + _bdot(dmod, wt_ref[...])
        dcv_ref[...] = dcv
        dcc_ref[...] = dcv * _dsilu(cvv)
        dpre_ref[...] = _colsum(sa_ref[2])
        dpost_ref[...] = _colsum(sb_ref[1])

    shapes = (jax.ShapeDtypeStruct((D, 3 * D), F32), jax.ShapeDtypeStruct((1, 3 * D), F32),
              jax.ShapeDtypeStruct((8, D), F32), jax.ShapeDtypeStruct((8, D), F32),
              jax.ShapeDtypeStruct((1, D), F32), jax.ShapeDtypeStruct((1, D), F32))
    return pl.pallas_call(body, name=name, out_shape=shapes, compiler_params=_cp(48))(cv, sel, st_a, st_b, w_t, dcv_in)


def _pre_fwd(x, modl, pre_g, w, tpe, name):
    n = x.shape[0]
    nt = n // TM
    ncb = 3
    tn = D_PAD // ncb
    tm = 2 * TM if n % (2 * TM) == 0 else TM

    def norm_body(x_ref, m_ref, g_ref, h_ref):
        xv = x_ref[...]
        m = m_ref[0]
        h_ref[...] = (xv * _rstd(xv) * g_ref[...] * (1.0 + m[:, D:2 * D]) + m[:, 0:D]).astype(BF16)

    h = pl.pallas_call(
        norm_body, name=name + "_norm", grid=(nt,),
        in_specs=[pl.BlockSpec((TM, D), lambda i: (i, 0)),
                  pl.BlockSpec((1, 1, 3 * D), lambda i: (_modrow(i, tpe), 0, 0)),
                  pl.BlockSpec((1, D), lambda i: (0, 0))],
        out_specs=pl.BlockSpec((TM, D), lambda i: (i, 0)),
        out_shape=jax.ShapeDtypeStruct((n, D), BF16),
        compiler_params=_cp(32, ("arbitrary",)),
    )(x, modl.reshape(8, 1, 3 * D), pre_g)

    def mm_body(h_ref, w_ref, z_ref):
        z_ref[...] = jnp.dot(h_ref[...], w_ref[...], preferred_element_type=F32)

    z = pl.pallas_call(
        mm_body, name=name, grid=(ncb, n // tm),
        in_specs=[pl.BlockSpec((tm, D), lambda j, i: (i, 0)), pl.BlockSpec((D, tn), lambda j, i: (0, j))],
        out_specs=pl.BlockSpec((tm, tn), lambda j, i: (i, j)),
        out_shape=jax.ShapeDtypeStruct((n, D_PAD), F32),
        compiler_params=_cp(48, ("arbitrary", "arbitrary")),
    )(h, w)
    return z, h


def _pre_bwd(dz, w_t, x, dxres, modl, pre_g, tpe, name):
    n = x.shape[0]
    nt = n // TM

    def body(dz_ref, wt_ref, x_ref, dr_ref, m_ref, g_ref, dx_ref, st_ref):
        dh = jnp.dot(dz_ref[...], wt_ref[...], preferred_element_type=F32)
        xv = x_ref[...]
        r = _rstd(xv)
        xn = xv * r
        m = m_ref[0]
        sc1 = 1.0 + m[:, D:2 * D]
        g = g_ref[...]
        st_ref[0, 0:1, :] = _colsum(dh)
        st_ref[0, 1:2, :] = _colsum(dh * xn * g)
        st_ref[0, 2:3, :] = _colsum(dh * xn * sc1)
        st_ref[0, 3:8, :] = jnp.zeros((5, D), F32)
        dx_ref[...] = dr_ref[...] + _norm_bwd(xn, r, dh * g * sc1)

    return pl.pallas_call(
        body, name=name, grid=(nt,),
        in_specs=[pl.BlockSpec((TM, D_PAD), lambda i: (i, 0)),
                  pl.BlockSpec((D_PAD, D), lambda i: (0, 0)),
                  pl.BlockSpec((TM, D), lambda i: (i, 0)),
                  pl.BlockSpec((TM, D), lambda i: (i, 0)),
                  pl.BlockSpec((1, 1, 3 * D), lambda i: (_modrow(i, tpe), 0, 0)),
                  pl.BlockSpec((1, D), lambda i: (0, 0))],
        out_specs=[pl.BlockSpec((TM, D), lambda i: (i, 0)),
                   pl.BlockSpec((1, 8, D), lambda i: (i, 0, 0))],
        out_shape=(jax.ShapeDtypeStruct((n, D), F32), jax.ShapeDtypeStruct((nt, 8, D), F32)),
        compiler_params=_cp(56, ("arbitrary",)),
    )(dz, w_t, x, dxres, modl.reshape(8, 1, 3 * D), pre_g)


def _matmul_tn(a, b, tm, tk, name):
    n, k1 = a.shape
    k2 = b.shape[1]

    def body(a_ref, b_ref, o_ref):
        @pl.when(pl.program_id(1) == 0)
        def _():
            o_ref[...] = jnp.zeros(o_ref.shape, F32)

        o_ref[...] += lax.dot_general(a_ref[...], b_ref[...], (((0,), (0,)), ((), ())), preferred_element_type=F32)

    return pl.pallas_call(
        body, name=name, grid=(k1 // tm, n // tk),
        in_specs=[pl.BlockSpec((tk, tm), lambda i, k: (k, i)), pl.BlockSpec((tk, k2), lambda i, k: (k, 0))],
        out_specs=pl.BlockSpec((tm, k2), lambda i, k: (i, 0)),
        out_shape=jax.ShapeDtypeStruct((k1, k2), F32),
        compiler_params=_cp(48, ("arbitrary", "arbitrary")),
    )(a, b)


def _mla_pre(z, q_norm, kv_norm, w_uq, w_ukv, cq, sq, ck, sk, tpe, name):
    n = z.shape[0]
    nt = n // TM
    scale = MLA_QK ** -0.5

    def body(zq_ref, zkv_ref, zkr_ref, gq_ref, gkv_ref, wq_ref, wkv_ref, cq_ref, sq_ref, ck_ref, sk_ref,
             q_ref, kv_ref, kr_ref):
        zq = zq_ref[...]
        qn = zq * _rstd(zq) * gq_ref[...]
        qraw = _bdot(qn, wq_ref[...])
        q_ref[...] = ((qraw * cq_ref[...] + _rot(qraw) * sq_ref[...]) * scale).astype(BF16)
        zkv = zkv_ref[...]
        kvn = zkv * _rstd(zkv) * gkv_ref[...]
        kv_ref[...] = _bdot(kvn, wkv_ref[...]).astype(BF16)
        zkr = zkr_ref[...]
        kr_ref[...] = (zkr * ck_ref[...] + _rot(zkr) * sk_ref[...]).astype(BF16)

    hq, hkv = MLA_H * MLA_HP, MLA_H * (MLA_NOPE + MLA_V)
    const = lambda i: (0, 0)
    tab = lambda i: (i % tpe, 0)
    return pl.pallas_call(
        body, name=name, grid=(nt,),
        in_specs=[pl.BlockSpec((TM, 256), lambda i: (i, _blk('mla_q'))),
                  pl.BlockSpec((TM, 128), lambda i: (i, _blk('mla_kv'))),
                  pl.BlockSpec((TM, 128), lambda i: (i, _blk('mla_kr'))),
                  pl.BlockSpec((1, 256), const), pl.BlockSpec((1, 128), const),
                  pl.BlockSpec((256, hq), const), pl.BlockSpec((128, hkv), const),
                  pl.BlockSpec((TM, hq), tab), pl.BlockSpec((TM, hq), tab),
                  pl.BlockSpec((TM, 128), tab), pl.BlockSpec((TM, 128), tab)],
        out_specs=[pl.BlockSpec((TM, hq), lambda i: (i, 0)), pl.BlockSpec((TM, hkv), lambda i: (i, 0)),
                   pl.BlockSpec((TM, 128), lambda i: (i, 0))],
        out_shape=(jax.ShapeDtypeStruct((n, hq), BF16), jax.ShapeDtypeStruct((n, hkv), BF16),
                   jax.ShapeDtypeStruct((n, 128), BF16)),
        compiler_params=_cp(32, ("arbitrary",)),
    )(z, z, z, q_norm, kv_norm, w_uq, w_ukv, cq, sq, ck, sk)


def _mla_pre_bwd(z, dq, dkv, dkr, q_norm, kv_norm, w_uq_t, w_ukv_t, cq, sq, ck, sk, tpe, name):
    n = z.shape[0]
    nt = n // TM
    scale = MLA_QK ** -0.5
    hq, hkv = MLA_H * MLA_HP, MLA_H * (MLA_NOPE + MLA_V)

    def body(zq_ref, zkv_ref, dq_ref, dkv_ref, dkr_ref, gq_ref, gkv_ref, wqt_ref, wkvt_ref, cq_ref, sq_ref,
             ck_ref, sk_ref, dzq_ref, dzkv_ref, dzkr_ref, dwq_ref, dwkv_ref, dgq_ref, dgkv_ref):
        @pl.when(pl.program_id(0) == 0)
        def _():
            dwq_ref[...] = jnp.zeros(dwq_ref.shape, F32)
            dwkv_ref[...] = jnp.zeros(dwkv_ref.shape, F32)
            dgq_ref[...] = jnp.zeros(dgq_ref.shape, F32)
            dgkv_ref[...] = jnp.zeros(dgkv_ref.shape, F32)

        zq = zq_ref[...]
        rq = _rstd(zq)
        qhat = zq * rq
        gq = gq_ref[...]
        dqs = dq_ref[...] * scale
        dqraw = dqs * cq_ref[...] - _rot(dqs * sq_ref[...])
        dwq_ref[...] += _bdot_tn(qhat * gq, dqraw)
        dqn = _bdot(dqraw, wqt_ref[...])
        dgq_ref[...] += _colsum(dqn * qhat)
        dzq_ref[...] = _norm_bwd(qhat, rq, dqn * gq).astype(BF16)

        zkv = zkv_ref[...]
        rkv = _rstd(zkv)
        khat = zkv * rkv
        gkv = gkv_ref[...]
        dkvv = dkv_ref[...]
        dwkv_ref[...] += _bdot_tn(khat * gkv, dkvv)
        dkvn = _bdot(dkvv, wkvt_ref[...])
        dgkv_ref[...] += _colsum(dkvn * khat)
        dzkv_ref[...] = _norm_bwd(khat, rkv, dkvn * gkv).astype(BF16)

        dkr = dkr_ref[...]
        dzkr_ref[...] = (dkr * ck_ref[...] - _rot(dkr * sk_ref[...])).astype(BF16)

    const = lambda i: (0, 0)
    tab = lambda i: (i % tpe, 0)
    row = lambda i: (i, 0)
    return pl.pallas_call(
        body, name=name, grid=(nt,),
        in_specs=[pl.BlockSpec((TM, 256), lambda i: (i, _blk('mla_q'))),
                  pl.BlockSpec((TM, 128), lambda i: (i, _blk('mla_kv'))),
                  pl.BlockSpec((TM, hq), row), pl.BlockSpec((TM, hkv), row), pl.BlockSpec((TM, 128), row),
                  pl.BlockSpec((1, 256), const), pl.BlockSpec((1, 128), const),
                  pl.BlockSpec((hq, 256), const), pl.BlockSpec((hkv, 128), const),
                  pl.BlockSpec((TM, hq), tab), pl.BlockSpec((TM, hq), tab),
                  pl.BlockSpec((TM, 128), tab), pl.BlockSpec((TM, 128), tab)],
        out_specs=[pl.BlockSpec((TM, 256), row), pl.BlockSpec((TM, 128), row), pl.BlockSpec((TM, 128), row),
                   pl.BlockSpec((256, hq), const), pl.BlockSpec((128, hkv), const),
                   pl.BlockSpec((1, 256), const), pl.BlockSpec((1, 128), const)],
        out_shape=(jax.ShapeDtypeStruct((n, 256), BF16), jax.ShapeDtypeStruct((n, 128), BF16),
                   jax.ShapeDtypeStruct((n, 128), BF16), jax.ShapeDtypeStruct((256, hq), F32),
                   jax.ShapeDtypeStruct((128, hkv), F32), jax.ShapeDtypeStruct((1, 256), F32),
                   jax.ShapeDtypeStruct((1, 128), F32)),
        compiler_params=_cp(32, ("arbitrary",)),
    )(z, z, dq, dkv, dkr, q_norm, kv_norm, w_uq_t, w_ukv_t, cq, sq, ck, sk)


def _attn_head(q_ref, kv_ref, kr_ref, hh, nk):
    kvh = kv_ref[0:nk, hh * MLA_HP:(hh + 1) * MLA_HP]
    lane = lax.broadcasted_iota(jnp.int32, kvh.shape, 1)
    kh = jnp.where(lane < MLA_NOPE, kvh, kr_ref[0:nk, :])
    qh = q_ref[:, hh * MLA_HP:(hh + 1) * MLA_HP]
    return kvh, kh, qh, lax.dot_general(qh, kh, (((1,), (1,)), ((), ())), preferred_element_type=F32)


def _by_segment(j, lc, t, fn):
    pl.when(j == 0)(functools.partial(fn, lc))
    pl.when(j != 0)(functools.partial(fn, t))


def _attn_specs(nb, tpe, t):
    tile = lambda b, p, j: (b * tpe + j, p)
    return [pl.BlockSpec((TM, 2 * MLA_HP), tile),
            pl.BlockSpec((t, 2 * MLA_HP), lambda b, p, j: (b, p)),
            pl.BlockSpec((t, MLA_HP), lambda b, p, j: (b, 0))]


def _attn_fwd(q, kv, kr, nb, lc, name):
    n = q.shape[0]
    t = n // nb
    tpe = t // TM

    def body(q_ref, kv_ref, kr_ref, o_ref, lse_ref):
        def run(nk):
            lane = lax.broadcasted_iota(jnp.int32, (TM, MLA_HP), 1)
            res, lses = [], []
            for hh in range(2):
                kvh, _, _, s = _attn_head(q_ref, kv_ref, kr_ref, hh, nk)
                m = jnp.max(s, axis=-1, keepdims=True)
                p = jnp.exp(s - m)
                l = jnp.sum(p, axis=-1, keepdims=True)
                res.append(jnp.dot(p.astype(BF16), kvh, preferred_element_type=F32) / l)
                lses.append(m + jnp.log(l))
            o_ref[...] = jnp.where(lane < MLA_V, pltpu.roll(res[0], MLA_V, 1), res[1])
            lane2 = lax.broadcasted_iota(jnp.int32, (TM, 2), 1)
            lse_ref[0] = jnp.where(lane2 == 0, lses[0], lses[1])

        _by_segment(pl.program_id(2), lc, t, run)

    return pl.pallas_call(
        body, name=name, grid=(nb, MLA_H // 2, tpe),
        in_specs=_attn_specs(nb, tpe, t),
        out_specs=[pl.BlockSpec((TM, 2 * MLA_V), lambda b, p, j: (b * tpe + j, p)),
                   pl.BlockSpec((1, TM, 2), lambda b, p, j: (p, b * tpe + j, 0))],
        out_shape=(jax.ShapeDtypeStruct((n, MLA_H * MLA_V), F32), jax.ShapeDtypeStruct((MLA_H // 2, n, 2), F32)),
        compiler_params=_cp(48, ("arbitrary", "arbitrary", "arbitrary")),
    )(q, kv, kr)


def _attn_bwd(q, kv, kr, o, lse, do, nb, lc, name):
    n = q.shape[0]
    t = n // nb
    tpe = t // TM

    def body(q_ref, kv_ref, kr_ref, o_ref, lse_ref, do_ref, dq_ref, dkv_ref, dkr_ref):
        p_id, j = pl.program_id(1), pl.program_id(2)

        @pl.when(j == 0)
        def _():
            dkv_ref[...] = jnp.zeros(dkv_ref.shape, F32)

        @pl.when((j == 0) & (p_id == 0))
        def _():
            dkr_ref[...] = jnp.zeros(dkr_ref.shape, F32)

        def run(nk):
            lane = lax.broadcasted_iota(jnp.int32, (TM, MLA_HP), 1)
            lane_t = lax.broadcasted_iota(jnp.int32, (nk, MLA_HP), 1)
            lane2 = lax.broadcasted_iota(jnp.int32, (TM, 2), 1)
            lse = lse_ref[0]
            dov, ov = do_ref[...], o_ref[...]
            dkr = jnp.zeros((nk, MLA_HP), F32)
            for hh in range(2):
                kvh, kh, qh, s = _attn_head(q_ref, kv_ref, kr_ref, hh, nk)
                p = jnp.exp(s - jnp.sum(jnp.where(lane2 == hh, lse, 0.0), axis=1, keepdims=True))
                do_pos = jnp.where(lane >= MLA_NOPE, pltpu.roll(dov, MLA_V, 1) if hh == 0 else dov, 0.0)
                o_pos = jnp.where(lane >= MLA_NOPE, pltpu.roll(ov, MLA_V, 1) if hh == 0 else ov, 0.0)
                delta = jnp.sum(do_pos * o_pos, axis=-1, keepdims=True)
                dob = do_pos.astype(BF16)
                dp = lax.dot_general(dob, kvh, (((1,), (1,)), ((), ())), preferred_element_type=F32)
                ds = (p * (dp - delta)).astype(BF16)
                dq_ref[:, hh * MLA_HP:(hh + 1) * MLA_HP] = jnp.dot(ds, kh, preferred_element_type=F32)
                dkf = lax.dot_general(ds, qh, (((0,), (0,)), ((), ())), preferred_element_type=F32)
                dvp = lax.dot_general(p.astype(BF16), dob, (((0,), (0,)), ((), ())), preferred_element_type=F32)
                dkv_ref[0:nk, hh * MLA_HP:(hh + 1) * MLA_HP] += jnp.where(lane_t < MLA_NOPE, dkf, dvp)
                dkr = dkr + jnp.where(lane_t >= MLA_NOPE, dkf, 0.0)
            dkr_ref[0:nk, :] += dkr

        _by_segment(j, lc, t, run)

    tile = lambda b, p, j: (b * tpe + j, p)
    return pl.pallas_call(
        body, name=name, grid=(nb, MLA_H // 2, tpe),
        in_specs=_attn_specs(nb, tpe, t) + [pl.BlockSpec((TM, 2 * MLA_V), tile),
                                            pl.BlockSpec((1, TM, 2), lambda b, p, j: (p, b * tpe + j, 0)),
                                            pl.BlockSpec((TM, 2 * MLA_V), tile)],
        out_specs=[pl.BlockSpec((TM, 2 * MLA_HP), tile),
                   pl.BlockSpec((t, 2 * MLA_HP), lambda b, p, j: (b, p)),
                   pl.BlockSpec((t, MLA_HP), lambda b, p, j: (b, 0))],
        out_shape=(jax.ShapeDtypeStruct((n, MLA_H * MLA_HP), F32), jax.ShapeDtypeStruct((n, MLA_H * MLA_HP), F32),
                   jax.ShapeDtypeStruct((n, MLA_HP), F32)),
        compiler_params=_cp(56, ("arbitrary", "arbitrary", "arbitrary")),
    )(q, kv, kr, o, lse, do)


def _pool_window(ug, pos, seglen, w, transpose):
    t = ug.shape[0]
    cnt = (jnp.minimum(pos + w // 2, seglen) - jnp.maximum(pos - w // 2, 0)).astype(F32)
    if transpose:
        ug = ug / cnt
    acc = jnp.zeros_like(ug)
    for j in range(-(w // 2), w // 2):
        jj = -j if transpose else j
        src = pos + jj
        valid = (src >= 0) & (src < seglen)
        acc = acc + jnp.where(valid, pltpu.roll(ug, (-jj) % t, 0), 0.0)
    return acc if transpose else acc / cnt


def _by_group(g, fn):
    for k, w in enumerate(POOL_WINDOWS):
        pl.when(g == k)(functools.partial(fn, w))


def _pool_specs(t):
    px, pg = PAD_OFF['pool_x'] // LANES, PAD_OFF['pool_gate'] // LANES
    return [pl.BlockSpec((t, LANES), lambda g, b: (b, px + g)),
            pl.BlockSpec((t, LANES), lambda g, b: (b, pg + g)),
            pl.BlockSpec((1, LANES, LANES), lambda g, b: (g, 0, 0)),
            pl.BlockSpec((1, LANES), lambda g, b: (0, g)),
            pl.BlockSpec((t, 1), lambda g, b: (0, 0)), pl.BlockSpec((t, 1), lambda g, b: (0, 0))]


def _pool_fwd(z, pool_w, pool_scale, pos, seglen, nb, name):
    n = z.shape[0]
    t = n // nb

    def body(u_ref, zg_ref, pw_ref, ps_ref, pos_ref, sl_ref, y_ref):
        def run(w):
            u = u_ref[...]
            pooled = _pool_window(u, pos_ref[...], sl_ref[...], w, False) - u
            y_ref[...] = (_bdot(pooled, pw_ref[0]) * ps_ref[...] * _silu(zg_ref[...])).astype(BF16)

        _by_group(pl.program_id(0), run)

    return pl.pallas_call(
        body, name=name, grid=(4, nb), in_specs=_pool_specs(t),
        out_specs=pl.BlockSpec((t, LANES), lambda g, b: (b, g)),
        out_shape=jax.ShapeDtypeStruct((n, 512), BF16),
        compiler_params=_cp(48, ("arbitrary", "arbitrary")),
    )(z, z, pool_w, pool_scale, pos, seglen)


def _pool_bwd(z, dy, pool_w, pool_w_t, pool_scale, pos, seglen, nb, name):
    n = z.shape[0]
    t = n // nb

    def body(u_ref, zg_ref, pw_ref, ps_ref, pos_ref, sl_ref, dy_ref, pwt_ref, du_ref, dg_ref, dpw_ref, dps_ref):
        @pl.when(pl.program_id(1) == 0)
        def _():
            dpw_ref[...] = jnp.zeros(dpw_ref.shape, F32)
            dps_ref[...] = jnp.zeros(dps_ref.shape, F32)

        def run(w):
            u = u_ref[...]
            pos_v, sl_v = pos_ref[...], sl_ref[...]
            pooled = _pool_window(u, pos_v, sl_v, w, False) - u
            mixed = _bdot(pooled, pw_ref[0])
            zg = zg_ref[...]
            sg = _silu(zg)
            ps = ps_ref[...]
            dyv = dy_ref[...]
            dps_ref[...] += _colsum(dyv * mixed * sg)
            dg_ref[...] = (dyv * mixed * ps * _dsilu(zg)).astype(BF16)
            dmixed = dyv * ps * sg
            dpw_ref[0] += _bdot_tn(pooled, dmixed)
            dpooled = _bdot(dmixed, pwt_ref[0])
            du_ref[...] = (_pool_window(dpooled, pos_v, sl_v, w, True) - dpooled).astype(BF16)

        _by_group(pl.program_id(0), run)

    blk = pl.BlockSpec((t, LANES), lambda g, b: (b, g))
    return pl.pallas_call(
        body, name=name, grid=(4, nb),
        in_specs=_pool_specs(t) + [blk, pl.BlockSpec((1, LANES, LANES), lambda g, b: (g, 0, 0))],
        out_specs=[blk, blk, pl.BlockSpec((1, LANES, LANES), lambda g, b: (g, 0, 0)),
                   pl.BlockSpec((1, LANES), lambda g, b: (0, g))],
        out_shape=(jax.ShapeDtypeStruct((n, 512), BF16), jax.ShapeDtypeStruct((n, 512), BF16),
                   jax.ShapeDtypeStruct((4, 128, 128), F32), jax.ShapeDtypeStruct((1, 512), F32)),
        compiler_params=_cp(48, ("arbitrary", "arbitrary")),
    )(z, z, pool_w, pool_scale, pos, seglen, dy, pool_w_t)


def _gla_chunk(q_ref, k_ref, a_ref, w2_ref, b2_ref, reverse):
    c = GLA_C
    x = _bdot(a_ref[...], w2_ref[0]) + b2_ref[0]
    la = (jnp.minimum(x, 0.0) - jnp.log(1.0 + jnp.exp(-jnp.abs(x)))) * (1.0 / GLA_TAU)
    row = lax.broadcasted_iota(jnp.int32, (c, c), 0)
    col = lax.broadcasted_iota(jnp.int32, (c, c), 1)
    tri = (col >= row) if reverse else (col <= row)
    tri_t = (col <= row) if reverse else (col >= row)
    b = _xdot(tri.astype(F32), la)
    tok = lax.broadcasted_iota(jnp.int32, la.shape, 0)
    bref = _colsum(jnp.where((tok >= c // 2) if reverse else (tok < c // 2), la, 0.0))
    blast = _colsum(la)
    eq = jnp.exp(jnp.minimum(b - bref, EXP_CLAMP))
    ek = jnp.exp(jnp.minimum(bref - b, EXP_CLAMP))
    qs = q_ref[...] * (GLA_DK ** -0.5)
    kk = k_ref[...]
    eb = jnp.exp(b)
    etail = jnp.exp(blast - b)
    return dict(x=x, la=la, tri=tri, tri_t=tri_t, eq=eq, ek=ek, qs=qs, kk=kk, qd=qs * eq, kd=kk * ek, qe=qs * eb,
                kl=kk * etail, eb=eb, etail=etail)


def _pair(a, p):
    return a[:, p * LANES:(p + 1) * LANES]


def _head_masks():
    lane = lax.broadcasted_iota(jnp.int32, (GLA_C, LANES), 1)
    return (lane < GLA_DK, lane >= GLA_DK)


def _state_decay(la, p):
    return jnp.exp(_xdot_tn(_pair(la, p), jnp.ones((GLA_C, GLA_DV), F32)))


def _gla_chunk_maps(nb, nc, ncc, order):
    def rmap(j):
        return jnp.where(j < ncc, ncc - 1 - j, nc - 1 - (j - ncc))

    if order == 'scan':
        return (lambda b, j: b * nc + j), (lambda b, j: b * nc + rmap(j))
    return (lambda b, j: b * nc + nc - 1 - j), (lambda b, j: b * nc + rmap(nc - 1 - j))


def _gla_in_specs(maps):
    specs = []
    for d, cm in enumerate(maps):
        gate = 'gla_af' if d == 0 else 'gla_ab'
        specs += [pl.BlockSpec((GLA_C, 256), lambda b, j, cm=cm: (cm(b, j), _blk('gla_q'))),
                  pl.BlockSpec((GLA_C, 256), lambda b, j, cm=cm: (cm(b, j), _blk('gla_k'))),
                  pl.BlockSpec((GLA_C, 512), lambda b, j, cm=cm: (cm(b, j), _blk('gla_v'))),
                  pl.BlockSpec((GLA_C, LANES), lambda b, j, cm=cm, gate=gate: (cm(b, j), _blk(gate))),
                  pl.BlockSpec((1, LANES, 256), lambda b, j, d=d: (d, 0, 0)),
                  pl.BlockSpec((1, 1, 256), lambda b, j, d=d: (d, 0, 0))]
    return specs


def _gla_fwd(z, w2, b2, nb, lc, name):
    n = z.shape[0]
    nc = n // nb // GLA_C
    maps = _gla_chunk_maps(nb, nc, lc // GLA_C, 'scan')

    def body(*refs):
        ins, (of_ref, ob_ref, sf_ref, sb_ref, s_sc) = refs[:12], refs[12:]

        @pl.when(pl.program_id(1) == 0)
        def _():
            s_sc[...] = jnp.zeros(s_sc.shape, F32)

        masks = _head_masks()
        for d, (o_ref, st_ref) in enumerate(((of_ref, sf_ref), (ob_ref, sb_ref))):
            q_ref, k_ref, v_ref, a_ref, w2_ref, b2_ref = ins[6 * d:6 * d + 6]
            ch = _gla_chunk(q_ref, k_ref, a_ref, w2_ref, b2_ref, d == 1)
            for p in range(2):
                s_prev = s_sc[d, p]
                st_ref[0, p] = s_prev
                s_new = _state_decay(ch['la'], p) * s_prev
                kd_p = _pair(ch['kd'], p)
                for hh in range(2):
                    h = 2 * p + hh
                    vv = v_ref[:, h * GLA_DV:(h + 1) * GLA_DV]
                    att = jnp.where(ch['tri'], _bdot_nt(jnp.where(masks[hh], _pair(ch['qd'], p), 0.0), kd_p), 0.0)
                    o_ref[:, h * GLA_DV:(h + 1) * GLA_DV] = (
                        _bdot(att, vv) + _bdot(jnp.where(masks[hh], _pair(ch['qe'], p), 0.0), s_prev))
                    s_new = s_new + _dot3(_split(jnp.where(masks[hh], _pair(ch['kl'], p), 0.0)), _split(vv), TN)
                s_sc[d, p] = s_new

    o_shape = jax.ShapeDtypeStruct((n, 512), F32)
    st_shape = jax.ShapeDtypeStruct((n // GLA_C, 2, LANES, GLA_DV), F32)
    return pl.pallas_call(
        body, name=name, grid=(nb, nc),
        in_specs=_gla_in_specs(maps),
        out_specs=[pl.BlockSpec((GLA_C, 512), lambda b, j: (maps[0](b, j), 0)),
                   pl.BlockSpec((GLA_C, 512), lambda b, j: (maps[1](b, j), 0)),
                   pl.BlockSpec((1, 2, LANES, GLA_DV), lambda b, j: (maps[0](b, j), 0, 0, 0)),
                   pl.BlockSpec((1, 2, LANES, GLA_DV), lambda b, j: (maps[1](b, j), 0, 0, 0))],
        out_shape=(o_shape, o_shape, st_shape, st_shape),
        scratch_shapes=[pltpu.VMEM((2, 2, LANES, GLA_DV), F32)],
        compiler_params=_cp(32, ("arbitrary", "arbitrary")),
    )(z, z, z, z, w2, b2, z, z, z, z, w2, b2)


def _gla_bwd(z, w2, w2_t, b2, st_f, st_b, dog, nb, lc, name):
    n = z.shape[0]
    nc = n // nb // GLA_C
    maps = _gla_chunk_maps(nb, nc, lc // GLA_C, 'back')

    def body(*refs):
        ins, extra, outs, (ds_sc, sfx_sc) = refs[:12], refs[12:18], refs[18:30], refs[30:]

        @pl.when(pl.program_id(1) == 0)
        def _():
            ds_sc[...] = jnp.zeros(ds_sc.shape, F32)
            sfx_sc[...] = jnp.zeros(sfx_sc.shape, F32)

        @pl.when((pl.program_id(0) == 0) & (pl.program_id(1) == 0))
        def _():
            for r in outs[8:12]:
                r[...] = jnp.zeros(r.shape, F32)

        masks = _head_masks()
        for d in range(2):
            q_ref, k_ref, v_ref, a_ref, w2_ref, b2_ref = ins[6 * d:6 * d + 6]
            w2t_ref, st_ref, do_ref = extra[3 * d:3 * d + 3]
            dq_ref, dk_ref, dv_ref, da_ref = outs[4 * d:4 * d + 4]
            dw2_ref, db2_ref = outs[8 + 2 * d], outs[9 + 2 * d]
            ch = _gla_chunk(q_ref, k_ref, a_ref, w2_ref, b2_ref, d == 1)
            dqs, dks, dbs = [], [], []
            for p in range(2):
                s_prev = st_ref[0, p]
                ds_new = ds_sc[d, p]
                qd_p, kd_p, qe_p, kl_p = (_pair(ch[nme], p) for nme in ('qd', 'kd', 'qe', 'kl'))
                ds_prev = _state_decay(ch['la'], p) * ds_new
                qd_b, kd_b = qd_p.astype(BF16), kd_p.astype(BF16)
                sp_s, dsn_s = _split(s_prev), _split(ds_new)
                dq_h, dk_h, db_h = [], [], []
                for hh in range(2):
                    h = 2 * p + hh
                    vv = v_ref[:, h * GLA_DV:(h + 1) * GLA_DV]
                    dov = do_ref[:, h * GLA_DV:(h + 1) * GLA_DV]
                    att = jnp.where(ch['tri'], _bdot_nt(jnp.where(masks[hh], qd_p, 0.0), kd_p), 0.0)
                    dv_ref[:, h * GLA_DV:(h + 1) * GLA_DV] = (
                        _bdot_tn(att, dov) + _bdot(jnp.where(masks[hh], kl_p, 0.0), ds_new))
                    vv_s, dov_s = _split(vv), _split(dov)
                    datt_b = jnp.where(ch['tri'], lax.dot_general(dov_s[0], vv_s[0], NT, preferred_element_type=F32),
                                       0.0).astype(BF16)
                    dq_in = lax.dot_general(datt_b, kd_b, NN, preferred_element_type=F32)
                    dk_in = lax.dot_general(datt_b, qd_b, TN, preferred_element_type=F32)
                    dq_st = _dot3(dov_s, sp_s, NT) * _pair(ch['eb'], p)
                    dk_st = _dot3(vv_s, dsn_s, NT) * _pair(ch['etail'], p)
                    dq_h.append(dq_in * _pair(ch['eq'], p) + dq_st)
                    dk_h.append(dk_in * _pair(ch['ek'], p) + dk_st)
                    db_h.append((qd_b.astype(F32) * dq_in - kd_b.astype(F32) * dk_in)
                                + (_pair(ch['qs'], p) * dq_st - _pair(ch['kk'], p) * dk_st))
                    ds_prev = ds_prev + _dot3(_split(jnp.where(masks[hh], qe_p, 0.0)), dov_s, TN)
                ds_sc[d, p] = ds_prev
                dqs.append(jnp.where(masks[0], dq_h[0], dq_h[1]))
                dks.append(jnp.where(masks[0], dk_h[0], dk_h[1]))
                dbs.append(jnp.where(masks[0], db_h[0], db_h[1]))
            dq_ref[...] = jnp.concatenate(dqs, axis=1) * (GLA_DK ** -0.5)
            dk_ref[...] = jnp.concatenate(dks, axis=1)
            db = jnp.concatenate(dbs, axis=1)
            dla = _xdot(ch['tri_t'].astype(F32), db) + sfx_sc[d]
            sfx_sc[d] = sfx_sc[d] + _colsum(db)
            dx = dla * (1.0 / GLA_TAU) * _sigmoid(-ch['x'])
            da_ref[...] = _bdot(dx, w2t_ref[0])
            dw2_ref[...] += _bdot_tn(a_ref[...], dx)
            db2_ref[...] += _colsum(dx)

    extra_specs, out_specs = [], []
    for d, cm in enumerate(maps):
        extra_specs += [pl.BlockSpec((1, 256, LANES), lambda b, j, d=d: (d, 0, 0)),
                        pl.BlockSpec((1, 2, LANES, GLA_DV), lambda b, j, cm=cm: (cm(b, j), 0, 0, 0)),
                        pl.BlockSpec((GLA_C, 512), lambda b, j, cm=cm: (cm(b, j), 0))]
        out_specs += [pl.BlockSpec((GLA_C, 256), lambda b, j, cm=cm: (cm(b, j), 0)),
                      pl.BlockSpec((GLA_C, 256), lambda b, j, cm=cm: (cm(b, j), 0)),
                      pl.BlockSpec((GLA_C, 512), lambda b, j, cm=cm: (cm(b, j), 0)),
                      pl.BlockSpec((GLA_C, LANES), lambda b, j, cm=cm: (cm(b, j), 0))]
    const2 = lambda b, j: (0, 0)
    out_specs += [pl.BlockSpec((LANES, 256), const2), pl.BlockSpec((1, 256), const2)] * 2
    per_dir = (jax.ShapeDtypeStruct((n, 256), F32), jax.ShapeDtypeStruct((n, 256), F32),
               jax.ShapeDtypeStruct((n, 512), F32), jax.ShapeDtypeStruct((n, LANES), F32))
    wshape = (jax.ShapeDtypeStruct((LANES, 256), F32), jax.ShapeDtypeStruct((1, 256), F32))
    return pl.pallas_call(
        body, name=name, grid=(nb, nc),
        in_specs=_gla_in_specs(maps) + extra_specs,
        out_specs=out_specs,
        out_shape=per_dir + per_dir + wshape + wshape,
        scratch_shapes=[pltpu.VMEM((2, 2, LANES, GLA_DV), F32), pltpu.VMEM((2, 1, 256), F32)],
        compiler_params=_cp(32, ("arbitrary", "arbitrary")),
    )(z, z, z, z, w2, b2, z, z, z, z, w2, b2, w2_t, st_f, dog, w2_t, st_b, dog)


def _add_cast(a, b, name):
    n, w = a.shape

    def body(a_ref, b_ref, o_ref):
        o_ref[...] = (a_ref[...] + b_ref[...]).astype(BF16)

    return pl.pallas_call(
        body, name=name, grid=(n // TM,),
        in_specs=[pl.BlockSpec((TM, w), lambda i: (i, 0)), pl.BlockSpec((TM, w), lambda i: (i, 0))],
        out_specs=pl.BlockSpec((TM, w), lambda i: (i, 0)),
        out_shape=jax.ShapeDtypeStruct((n, w), BF16),
        compiler_params=_cp(32, ("arbitrary",)),
    )(a, b)


def _gla_out_norm(og):
    hats, rs = [], []
    for h in range(GLA_H):
        seg = og[:, h * GLA_DV:(h + 1) * GLA_DV]
        r = _rstd(seg)
        hats.append(seg * r)
        rs.append(r)
    return hats, rs


def _merge_branches(zm_ref, zgm_ref, zgg_ref, om_ref, yp_ref, ogf_ref, ogb_ref, gn_ref, wbm_ref, wbp_ref, wbg_ref):
    zgm, zgg = zgm_ref[...], zgg_ref[...]
    om = om_ref[...]
    y_mla = om * _silu(zgm)
    hats, rs = _gla_out_norm(ogf_ref[...] + ogb_ref[...])
    gn = gn_ref[...]
    sgg = _silu(zgg)
    y_gla = jnp.concatenate([hats[h] * gn for h in range(GLA_H)], axis=1) * sgg
    ys = (y_mla, yp_ref[...], y_gla)
    ps = (_bdot(y_mla, wbm_ref[...]), jnp.dot(yp_ref[...], wbp_ref[...], preferred_element_type=F32),
          _bdot(y_gla, wbg_ref[...]))
    zm = zm_ref[...]
    gs = tuple(_sigmoid(zm[:, a * D:(a + 1) * D]) for a in range(3))
    merged = gs[0] * ps[0] + gs[1] * ps[1] + gs[2] * ps[2]
    return dict(zgm=zgm, zgg=zgg, om=om, hats=hats, rs=rs, gn=gn, sgg=sgg, ys=ys, ps=ps, gs=gs, merged=merged)


def _merge_in_specs(tpe):
    row = lambda i: (i, 0)
    const = lambda i: (0, 0)
    return [pl.BlockSpec((TM, 3 * D), lambda i: (i, _blk('merge'))),
            pl.BlockSpec((TM, 512), lambda i: (i, _blk('mla_gate'))),
            pl.BlockSpec((TM, 512), lambda i: (i, _blk('gla_gate'))),
            pl.BlockSpec((TM, 512), row), pl.BlockSpec((TM, 512), row), pl.BlockSpec((TM, 512), row),
            pl.BlockSpec((TM, 512), row), pl.BlockSpec((1, 128), const),
            pl.BlockSpec((512, D), const), pl.BlockSpec((512, D), const), pl.BlockSpec((512, D), const),
            pl.BlockSpec((1, 1, 3 * D), lambda i: (_modrow(i, tpe), 0, 0)), pl.BlockSpec((1, D), const)]


def _merge_fwd(x, z, o_mla, y_pool, ogf, ogb, gla_n, wbm, wbp, wbg, wout, modl, post_g, tpe, name):
    n = x.shape[0]

    def body(zm_ref, zgm_ref, zgg_ref, om_ref, yp_ref, ogf_ref, ogb_ref, gn_ref, wbm_ref, wbp_ref, wbg_ref,
             m_ref, pg_ref, x_ref, wo_ref, xn_ref, out_ref):
        mb = _merge_branches(zm_ref, zgm_ref, zgg_ref, om_ref, yp_ref, ogf_ref, ogb_ref, gn_ref,
                             wbm_ref, wbp_ref, wbg_ref)
        out = _bdot(mb['merged'], wo_ref[...])
        gate = m_ref[0][:, 2 * D:3 * D]
        xn_ref[...] = x_ref[...] + gate * (out * _rstd(out) * pg_ref[...])
        out_ref[...] = out

    row = lambda i: (i, 0)
    return pl.pallas_call(
        body, name=name, grid=(n // TM,),
        in_specs=_merge_in_specs(tpe) + [pl.BlockSpec((TM, D), row), pl.BlockSpec((D, D), lambda i: (0, 0))],
        out_specs=[pl.BlockSpec((TM, D), row), pl.BlockSpec((TM, D), row)],
        out_shape=(jax.ShapeDtypeStruct((n, D), F32), jax.ShapeDtypeStruct((n, D), F32)),
        compiler_params=_cp(48, ("arbitrary",)),
    )(z, z, z, o_mla, y_pool, ogf, ogb, gla_n, wbm, wbp, wbg, modl.reshape(8, 1, 3 * D), post_g, x, wout)


def _merge_bwd(dxn, out, z, o_mla, y_pool, ogf, ogb, gla_n, wbm, wbp, wbg, wbm_t, wbp_t, wbg_t, wout_t,
               modl, post_g, tpe, name):
    n = out.shape[0]
    nt = n // TM

    def body(zm_ref, zgm_ref, zgg_ref, om_ref, yp_ref, ogf_ref, ogb_ref, gn_ref, wbm_ref, wbp_ref, wbg_ref,
             m_ref, pg_ref, dxn_ref, out_ref, wbmt_ref, wbpt_ref, wbgt_ref, wot_ref,
             dzm_ref, dom_ref, dzgm_ref, dyp_ref, dog_ref, dzgg_ref, st_ref,
             dwbm_ref, dwbp_ref, dwbg_ref, dwo_ref, dgn_ref):
        @pl.when(pl.program_id(0) == 0)
        def _():
            for r in (dwbm_ref, dwbp_ref, dwbg_ref, dwo_ref, dgn_ref):
                r[...] = jnp.zeros(r.shape, F32)

        mb = _merge_branches(zm_ref, zgm_ref, zgg_ref, om_ref, yp_ref, ogf_ref, ogb_ref, gn_ref,
                             wbm_ref, wbp_ref, wbg_ref)
        out = out_ref[...]
        r2 = _rstd(out)
        on = out * r2
        pg = pg_ref[...]
        gate = m_ref[0][:, 2 * D:3 * D]
        dxn_v = dxn_ref[...]
        st_ref[0, 0:1, :] = _colsum(dxn_v * on * pg)
        st_ref[0, 1:2, :] = _colsum(dxn_v * gate * on)
        st_ref[0, 2:8, :] = jnp.zeros((6, D), F32)
        dout = _norm_bwd(on, r2, dxn_v * gate * pg)
        dwo_ref[...] += _bdot_tn(mb['merged'], dout)
        dmerged = _bdot(dout, wot_ref[...])
        dys = []
        for a, (dw_ref, wt_ref) in enumerate(((dwbm_ref, wbmt_ref), (dwbp_ref, wbpt_ref), (dwbg_ref, wbgt_ref))):
            g = mb['gs'][a]
            dzm_ref[:, a * D:(a + 1) * D] = (dmerged * mb['ps'][a] * g * (1.0 - g)).astype(BF16)
            dp = dmerged * g
            dw_ref[...] += _bdot_tn(mb['ys'][a], dp)
            dys.append(_bdot(dp, wt_ref[...]))
        dom_ref[...] = dys[0] * _silu(mb['zgm'])
        dzgm_ref[...] = (dys[0] * mb['om'] * _dsilu(mb['zgm'])).astype(BF16)
        dyp_ref[...] = dys[1]
        gn = mb['gn']
        dgn = jnp.zeros((1, GLA_DV), F32)
        dzgg, dog = [], []
        for h in range(GLA_H):
            sl = slice(h * GLA_DV, (h + 1) * GLA_DV)
            dyg = dys[2][:, sl]
            hat = mb['hats'][h]
            dzgg.append(dyg * hat * gn * _dsilu(mb['zgg'][:, sl]))
            dn = dyg * mb['sgg'][:, sl]
            dgn = dgn + _colsum(dn * hat)
            dog.append(_norm_bwd(hat, mb['rs'][h], dn * gn))
        dgn_ref[...] += dgn
        dzgg_ref[...] = jnp.concatenate(dzgg, axis=1).astype(BF16)
        dog_ref[...] = jnp.concatenate(dog, axis=1)

    row = lambda i: (i, 0)
    const = lambda i: (0, 0)
    wspec = pl.BlockSpec((512, D), const)
    wtspec = pl.BlockSpec((D, 512), const)
    return pl.pallas_call(
        body, name=name, grid=(nt,),
        in_specs=_merge_in_specs(tpe) + [pl.BlockSpec((TM, D), row), pl.BlockSpec((TM, D), row),
                                         wtspec, wtspec, wtspec,
                                         pl.BlockSpec((D, D), const)],
        out_specs=[pl.BlockSpec((TM, 3 * D), row), pl.BlockSpec((TM, 512), row), pl.BlockSpec((TM, 512), row),
                   pl.BlockSpec((TM, 512), row), pl.BlockSpec((TM, 512), row), pl.BlockSpec((TM, 512), row),
                   pl.BlockSpec((1, 8, D), lambda i: (i, 0, 0)),
                   wspec, wspec, wspec, pl.BlockSpec((D, D), const), pl.BlockSpec((1, 128), const)],
        out_shape=(jax.ShapeDtypeStruct((n, 3 * D), BF16), jax.ShapeDtypeStruct((n, 512), F32),
                   jax.ShapeDtypeStruct((n, 512), BF16), jax.ShapeDtypeStruct((n, 512), F32),
                   jax.ShapeDtypeStruct((n, 512), F32), jax.ShapeDtypeStruct((n, 512), BF16),
                   jax.ShapeDtypeStruct((nt, 8, D), F32),
                   jax.ShapeDtypeStruct((512, D), F32), jax.ShapeDtypeStruct((512, D), F32),
                   jax.ShapeDtypeStruct((512, D), F32), jax.ShapeDtypeStruct((D, D), F32),
                   jax.ShapeDtypeStruct((1, 128), F32)),
        compiler_params=_cp(56, ("arbitrary",)),
    )(z, z, z, o_mla, y_pool, ogf, ogb, gla_n, wbm, wbp, wbg, modl.reshape(8, 1, 3 * D), post_g,
      dxn, out, wbm_t, wbp_t, wbg_t, wout_t)


def _loss_grad(xf, tgt, nb, tpe):
    n = xf.shape[0]

    def body(x_ref, t_ref, dx_ref, l_ref):
        j = pl.program_id(1)
        d = x_ref[...] - t_ref[...]
        latent = j > 0
        dx_ref[...] = jnp.where(latent, d * (1.0 / D), 0.0)
        l_ref[...] = jnp.full(l_ref.shape, jnp.where(latent, 0.5 / D * jnp.sum(d * d), 0.0), F32)

    return pl.pallas_call(
        body, name="loss_grad", grid=(nb, tpe),
        in_specs=[pl.BlockSpec((TM, D), lambda b, j: (b * tpe + j, 0)),
                  pl.BlockSpec((TM, D), lambda b, j: (b * (tpe - 1) + jnp.maximum(j - 1, 0), 0))],
        out_specs=[pl.BlockSpec((TM, D), lambda b, j: (b * tpe + j, 0)),
                   pl.BlockSpec((1, 8, 128), lambda b, j: (b * tpe + j, 0, 0))],
        out_shape=(jax.ShapeDtypeStruct((n, D), F32), jax.ShapeDtypeStruct((n // TM, 8, 128), F32)),
        compiler_params=_cp(32, ("arbitrary", "arbitrary")),
    )(xf, tgt)


def _to_padded(w_nat):
    parts = []
    for nme in PAD_ORDER:
        p = w_nat[NAT_OFF[nme]:NAT_OFF[nme] + NAT_SIZE[nme]]
        if SLAB[nme] > NAT_SIZE[nme]:
            p = jnp.pad(p, [(IN_SLAB[nme], SLAB[nme] - NAT_SIZE[nme] - IN_SLAB[nme]), (0, 0)])
        parts.append(p)
    return jnp.concatenate(parts, axis=0)


def _from_padded(w_pad):
    return jnp.concatenate([w_pad[PAD_OFF[nme] + IN_SLAB[nme]:PAD_OFF[nme] + IN_SLAB[nme] + NAT_SIZE[nme]]
                            for nme in IN_NAMES], axis=0)


def _rope_tables(lc, l):
    half = MLA_ROPE // 2
    inv = ROPE_BASE ** (-jnp.arange(0, half, 2, dtype=F32) / half)
    tok = jnp.arange(l)
    ang_r = (tok // GRID_W).astype(F32)[:, None] * inv
    ang_c = (tok % GRID_W).astype(F32)[:, None] * inv
    ang = jnp.concatenate([ang_r, ang_r, ang_c, ang_c], axis=-1)
    cos = jnp.concatenate([jnp.ones((lc, MLA_ROPE), F32), jnp.cos(ang)], axis=0)
    sin = jnp.concatenate([jnp.zeros((lc, MLA_ROPE), F32), jnp.sin(ang)], axis=0)
    t = lc + l
    tail = MLA_HP - MLA_QK
    ck = jnp.concatenate([jnp.ones((t, MLA_NOPE), F32), cos, jnp.ones((t, tail), F32)], axis=1)
    sk = jnp.concatenate([jnp.zeros((t, MLA_NOPE), F32), sin, jnp.zeros((t, tail), F32)], axis=1)
    return jnp.tile(ck, (1, MLA_H)), jnp.tile(sk, (1, MLA_H)), ck, sk


def _pad_heads(w):
    lead = w.shape[:-1]
    w = w.reshape(lead + (MLA_H, MLA_QK))
    return jnp.pad(w, [(0, 0)] * len(lead) + [(0, 0), (0, MLA_HP - MLA_QK)]).reshape(lead + (MLA_H * MLA_HP,))


def _unpad_heads(w):
    lead = w.shape[:-1]
    return w.reshape(lead + (MLA_H, MLA_HP))[..., :MLA_QK].reshape(lead + (MLA_H * MLA_QK,))


def _local_step(x, c, ctx, tgt, wf):
    nb, l, _ = x.shape
    lc = ctx.shape[1]
    assert lc == TM and l % TM == 0
    t = lc + l
    tpe = t // TM
    n = nb * t
    nt = n // TM
    bf = lambda a: a.astype(BF16)

    xs = jnp.concatenate([ctx, x], axis=1).reshape(n, D)
    assert nb <= 4
    cv = jnp.concatenate([c, jnp.zeros((4 - nb, D), F32), wf['c_ctx'][None, :], jnp.zeros((3, D), F32)], axis=0)
    mod_w_b = bf(wf['mod_w'])
    mod_all = _mod_fwd(cv, mod_w_b, wf['mod_b'].reshape(DEPTH, 1, 3 * D))
    cq, sq, ck, sk = _rope_tables(lc, l)
    pos = jnp.concatenate([jnp.arange(lc), jnp.arange(l)]).astype(jnp.int32)[:, None]
    seglen = jnp.concatenate([jnp.full((lc,), lc), jnp.full((l,), l)]).astype(jnp.int32)[:, None]
    tiles = np.arange(nt)
    ntp = -(-nt // LANES) * LANES
    sel = np.zeros((8, ntp), np.float32)
    sel[np.where(tiles % tpe == 0, 4, tiles // tpe), tiles] = 1.0
    sel = jnp.asarray(sel)

    def tile_sums(st):
        return jnp.pad(st.transpose(1, 0, 2), ((0, 0), (0, ntp - nt), (0, 0)))

    lw = []
    for ly in range(DEPTH):
        w_in_t = _to_padded(bf(wf['w_in'][ly]))
        w_uq_p = _pad_heads(bf(wf['mla_w_uq'][ly]))
        w2 = jnp.pad(jnp.stack([bf(wf['gla_af_w2'][ly]), bf(wf['gla_ab_w2'][ly])]),
                     ((0, 0), (0, LANES - GLA_RANK), (0, 0)))
        lw.append(dict(
            w_in=w_in_t.T, w_in_t=w_in_t,
            w_uq=w_uq_p, w_uq_t=w_uq_p.T,
            w_ukv=bf(wf['mla_w_ukv'][ly]), w_ukv_t=bf(wf['mla_w_ukv'][ly]).T,
            pool_w=bf(wf['pool_w'][ly]), pool_w_t=bf(wf['pool_w'][ly]).transpose(0, 2, 1),
            w2=w2, w2_t=w2.transpose(0, 2, 1),
            b2=jnp.stack([wf['gla_af_b'][ly], wf['gla_ab_b'][ly]]).reshape(2, 1, GLA_H * GLA_DK),
            wbm=bf(wf['w_branch_mla'][ly]), wbp=bf(wf['w_branch_pool'][ly]), wbg=bf(wf['w_branch_gla'][ly]),
            wout=bf(wf['w_out'][ly]),
            wbm_t=bf(wf['w_branch_mla'][ly]).T, wbp_t=bf(wf['w_branch_pool'][ly]).T,
            wbg_t=bf(wf['w_branch_gla'][ly]).T, wout_t=bf(wf['w_out'][ly]).T,
            pre_g=wf['pre_norm'][ly][None, :], post_g=wf['post_norm'][ly][None, :],
            q_norm=wf['mla_q_norm'][ly][None, :], kv_norm=wf['mla_kv_norm'][ly][None, :],
            pool_scale=wf['pool_scale'][ly][None, :], gla_norm=wf['gla_norm'][ly][None, :]))

    saved = []
    xcur = xs
    for ly in range(DEPTH):
        w = lw[ly]
        z, h = _pre_fwd(xcur, mod_all[ly], w['pre_g'], w['w_in'], tpe, f"pre_fwd{ly}")
        qb, kvb, krb = _mla_pre(z, w['q_norm'], w['kv_norm'], w['w_uq'], w['w_ukv'], cq, sq, ck, sk, tpe, f"mla_pre{ly}")
        o_mla, lse = _attn_fwd(qb, kvb, krb, nb, lc, f"attn_fwd{ly}")
        y_pool = _pool_fwd(z, w['pool_w'], w['pool_scale'], pos, seglen, nb, f"pool_fwd{ly}")
        ogf, ogb, st_f, st_r = _gla_fwd(z, w['w2'], w['b2'], nb, lc, f"gla_fwd{ly}")
        xnew, out = _merge_fwd(xcur, z, o_mla, y_pool, ogf, ogb, w['gla_norm'], w['wbm'], w['wbp'], w['wbg'],
                               w['wout'], mod_all[ly], w['post_g'], tpe, f"merge_fwd{ly}")
        saved.append(dict(x=xcur, z=z, h=h, qb=qb, kvb=kvb, krb=krb, lse=lse, o_mla=o_mla, y_pool=y_pool,
                          st_f=st_f, st_r=st_r, ogf=ogf, ogb=ogb, out=out))
        xcur = xnew

    dxcur, lparts = _loss_grad(xcur, tgt.reshape(nb * l, D), nb, tpe)
    loss = jnp.sum(lparts[:, 0, 0])

    g = {k: [None] * DEPTH for k in WEIGHTS if k != 'c_ctx'}
    dcv = jnp.zeros((8, D), F32)
    dcc = None
    for ly in reversed(range(DEPTH)):
        w, s = lw[ly], saved[ly]
        (dzm, dom, dzgm, dyp, dog, dzgg, st_b, g['w_branch_mla'][ly], g['w_branch_pool'][ly], g['w_branch_gla'][ly],
         g['w_out'][ly], dgn) = _merge_bwd(
            dxcur, s['out'], s['z'], s['o_mla'], s['y_pool'], s['ogf'], s['ogb'], w['gla_norm'], w['wbm'], w['wbp'],
            w['wbg'], w['wbm_t'], w['wbp_t'], w['wbg_t'], w['wout_t'], mod_all[ly], w['post_g'], tpe,
            f"merge_bwd{ly}")
        g['gla_norm'][ly] = dgn[0]
        dq, dkv, dkr = _attn_bwd(s['qb'], s['kvb'], s['krb'], s['o_mla'], s['lse'], dom, nb, lc, f"attn_bwd{ly}")
        dzq, dzkv, dzkr, dwq, g['mla_w_ukv'][ly], dgq, dgkv = _mla_pre_bwd(
            s['z'], dq, dkv, dkr, w['q_norm'], w['kv_norm'], w['w_uq_t'], w['w_ukv_t'], cq, sq, ck, sk, tpe,
            f"mla_pre_bwd{ly}")
        g['mla_w_uq'][ly] = _unpad_heads(dwq)
        g['mla_q_norm'][ly], g['mla_kv_norm'][ly] = dgq[0], dgkv[0]
        dzpx, dzpg, g['pool_w'][ly], dps = _pool_bwd(s['z'], dyp, w['pool_w'], w['pool_w_t'], w['pool_scale'],
                                                     pos, seglen, nb, f"pool_bwd{ly}")
        g['pool_scale'][ly] = dps[0]
        (dq_f, dk_f, dv_f, da_f, dq_r, dk_r, dv_r, da_r, dw2_f, db2_f, dw2_r, db2_r) = _gla_bwd(
            s['z'], w['w2'], w['w2_t'], w['b2'], s['st_f'], s['st_r'], dog, nb, lc, f"gla_bwd{ly}")
        dzgq = _add_cast(dq_f, dq_r, f"gla_dq{ly}")
        dzgk = _add_cast(dk_f, dk_r, f"gla_dk{ly}")
        dzgv = _add_cast(dv_f, dv_r, f"gla_dv{ly}")
        g['gla_af_w2'][ly], g['gla_ab_w2'][ly] = dw2_f[:GLA_RANK], dw2_r[:GLA_RANK]
        g['gla_af_b'][ly], g['gla_ab_b'][ly] = db2_f[0], db2_r[0]
        parts = dict(merge=dzm, mla_gate=dzgm, mla_q=dzq, mla_kv=dzkv, mla_kr=dzkr, pool_x=dzpx, pool_gate=dzpg,
                     gla_v=dzgv, gla_gate=dzgg, gla_q=dzgq, gla_k=dzgk, gla_af=bf(da_f), gla_ab=bf(da_r))
        dz = jnp.concatenate([parts[nme] for nme in PAD_ORDER], axis=1)
        dxcur, st_a = _pre_bwd(dz, w['w_in_t'], s['x'], dxcur, mod_all[ly], w['pre_g'], tpe, f"pre_bwd{ly}")
        tk = next(k for k in (1024, 512, TM) if n % k == 0)
        g['w_in'][ly] = _from_padded(_matmul_tn(dz, s['h'], 768, tk, f"w_in_grad{ly}"))
        dmw, dmb, dcv, dcc, dpre, dpost = _mod_bwd(cv, sel, tile_sums(st_a), tile_sums(st_b),
                                                   mod_w_b[ly].T, dcv, f"mod_bwd{ly}")
        g['mod_w'][ly], g['mod_b'][ly] = dmw, dmb[0]
        g['pre_norm'][ly], g['post_norm'][ly] = dpre[0], dpost[0]

    grads = {k: jnp.stack(v) for k, v in g.items()}
    grads['c_ctx'] = dcc[4]
    grad_x = dxcur.reshape(nb, t, D)[:, lc:, :]
    return loss, grad_x, grads


def _place():
    x, y, c = lax.axis_index("x"), lax.axis_index("y"), lax.axis_index("c")
    chips = [(1 - x, y), (x, 1 - y), (1 - x, 1 - y)]
    return x, y, c, chips


def _hbm_call(body, name, out_shape, n_in, sems):
    any_spec = pl.BlockSpec(memory_space=pl.ANY)
    return pl.pallas_call(body, name=name, out_shape=out_shape, in_specs=[any_spec] * n_in,
                          out_specs=jax.tree.map(lambda _: any_spec, out_shape), scratch_shapes=sems)


def _all_gather_shards(ws):
    n = len(ws)

    def body(*refs):
        ins, outs, (send_sems, recv_sems) = refs[:n], refs[n:2 * n], refs[2 * n:]
        x, y, c, chips = _place()

        def copy(k, q, chip, half, to, src=None):
            dst = outs[k].at[2 * chip[0] + chip[1], half]
            return pltpu.make_async_remote_copy(src_ref=dst if src is None else src, dst_ref=dst,
                                                send_sem=send_sems.at[k, q], recv_sem=recv_sems.at[k, q],
                                                device_id=to, device_id_type=MESH)

        first = [copy(k, j, (x, y), c, (*chip, c), src=ins[k].at[c]) for k in range(n) for j, chip in enumerate(chips)]
        for cp in first:
            cp.start()
        passed = []
        for k in range(n):
            for j, chip in enumerate(chips):
                copy(k, j, chip, c, (x, y, c)).wait_recv()
                passed.append(copy(k, 3 + j, chip, c, (x, y, 1 - c)))
                passed[-1].start()
        for k in range(n):
            for j, chip in enumerate(chips):
                copy(k, 3 + j, chip, 1 - c, (x, y, 1 - c)).wait_recv()
        for cp in first + passed:
            cp.wait_send()

    shapes = tuple(jax.ShapeDtypeStruct((N_CHIPS,) + w.shape, w.dtype) for w in ws)
    return _hbm_call(body, "all_gather_shards", shapes, n,
                     [pltpu.SemaphoreType.DMA((n, 6)), pltpu.SemaphoreType.DMA((n, 6))])(*ws)


def _to_sibling(arrs, other_layer, name):
    n = len(arrs)

    def body(*refs):
        ins, outs, (send_sems, recv_sems) = refs[:n], refs[n:2 * n], refs[2 * n:]
        x, y, c, _ = _place()
        cps = [pltpu.make_async_remote_copy(src_ref=ins[k].at[1 - c] if other_layer else ins[k], dst_ref=outs[k],
                                            send_sem=send_sems.at[k], recv_sem=recv_sems.at[k],
                                            device_id=(x, y, 1 - c), device_id_type=MESH) for k in range(n)]
        for cp in cps:
            cp.start()
        for cp in cps:
            cp.wait()

    shapes = tuple(jax.ShapeDtypeStruct(a.shape[1:] if other_layer else a.shape, a.dtype) for a in arrs)
    return _hbm_call(body, name, shapes, n, [pltpu.SemaphoreType.DMA((n,)), pltpu.SemaphoreType.DMA((n,))])(*arrs)


def _scatter_to_chips(hs):
    n = len(hs)

    def body(*refs):
        ins, outs, (send_sems, recv_sems) = refs[:n], refs[n:2 * n], refs[2 * n:]
        x, y, c, chips = _place()
        me = 2 * x + y
        sends = []
        for k in range(n):
            for j, chip in enumerate(chips):
                cp = pltpu.make_async_remote_copy(src_ref=ins[k].at[2 * chip[0] + chip[1]], dst_ref=outs[k].at[me],
                                                  send_sem=send_sems.at[k, j], recv_sem=recv_sems.at[k, j],
                                                  device_id=(*chip, c), device_id_type=MESH)
                cp.start()
                sends.append(cp)
        for k in range(n):
            for j, chip in enumerate(chips):
                dst = outs[k].at[2 * chip[0] + chip[1]]
                pltpu.make_async_remote_copy(src_ref=dst, dst_ref=dst, send_sem=send_sems.at[k, j],
                                             recv_sem=recv_sems.at[k, j], device_id=(*chip, c),
                                             device_id_type=MESH).wait_recv()
        for cp in sends:
            cp.wait_send()

    shapes = tuple(jax.ShapeDtypeStruct(h.shape, h.dtype) for h in hs)
    return _hbm_call(body, "scatter_to_chips", shapes, n,
                     [pltpu.SemaphoreType.DMA((n, 3)), pltpu.SemaphoreType.DMA((n, 3))])(*hs)


BLOCK_BYTES = 2 * 1024 * 1024


def _row_block(r, cols):
    if r * cols * 4 <= BLOCK_BYTES or r % 8:
        return r
    br = 8
    while r % (2 * br) == 0 and 2 * br * cols * 4 <= BLOCK_BYTES:
        br *= 2
    return br


def _add_cores(b, got, name):
    _, ns, r, cols = b.shape
    br = _row_block(r, 2 * cols)

    def body(b_ref, g_ref, o_ref):
        mine = jnp.where(lax.axis_index("c") == 0, b_ref[0, 0], b_ref[1, 0])
        o_ref[0] = (mine + g_ref[0]).astype(BF16)

    spec = pl.BlockSpec((1, br, cols), lambda i, j: (i, j, 0))
    return pl.pallas_call(body, name=name, grid=(ns, r // br),
                          in_specs=[pl.BlockSpec((2, 1, br, cols), lambda i, j: (0, i, j, 0)), spec], out_specs=spec,
                          out_shape=jax.ShapeDtypeStruct((ns, r, cols), BF16))(b, got)


def _sum_chips(own, got, name):
    _, r, cols = own.shape
    br = _row_block(r, 4 * cols)

    def body(own_ref, got_ref, o_ref):
        me = 2 * lax.axis_index("x") + lax.axis_index("y")
        part = [jnp.where(me == j, own_ref[j], got_ref[j]).astype(F32) for j in range(N_CHIPS)]
        o_ref[...] = ((part[0] + part[1]) + part[2]) + part[3]

    spec = pl.BlockSpec((N_CHIPS, br, cols), lambda j: (0, j, 0))
    return pl.pallas_call(body, name=name, grid=(r // br,), in_specs=[spec, spec],
                          out_specs=pl.BlockSpec((br, cols), lambda j: (j, 0)),
                          out_shape=jax.ShapeDtypeStruct((r, cols), F32))(own, got)


def _adamw(w, g_mine, g_other, m, v, name):
    _, r, cols = w.shape
    br = _row_block(r, 4 * cols)

    def body(w_ref, gm_ref, go_ref, m_ref, v_ref, g_ref, d_ref, nm_ref, nv_ref):
        gv = jnp.where(pl.program_id(0) == lax.axis_index("c"), gm_ref[...], go_ref[...])
        m2 = ADAM_B1 * m_ref[0] + (1.0 - ADAM_B1) * gv
        v2 = ADAM_B2 * v_ref[0] + (1.0 - ADAM_B2) * jnp.square(gv)
        m_hat = m2 / (1.0 - ADAM_B1 ** ADAM_STEP)
        v_hat = v2 / (1.0 - ADAM_B2 ** ADAM_STEP)
        g_ref[0] = gv
        d_ref[0] = -ADAM_LR * (m_hat / (jnp.sqrt(v_hat) + ADAM_EPS) + ADAM_WD * w_ref[0])
        nm_ref[0] = m2
        nv_ref[0] = v2

    lay = pl.BlockSpec((1, br, cols), lambda l, j: (l, j, 0))
    flat = pl.BlockSpec((br, cols), lambda l, j: (j, 0))
    shp = jax.ShapeDtypeStruct(w.shape, F32)
    return pl.pallas_call(body, name=name, grid=(2, r // br), in_specs=[lay, flat, flat, lay, lay],
                          out_specs=[lay] * 4, out_shape=(shp,) * 4)(w, g_mine, g_other, m, v)


def _pack_small(ts):
    flat = jnp.concatenate([ts[k].reshape(DEPTH, -1) for k in REPLICATED], axis=1)
    return flat.reshape(DEPTH, flat.shape[1] // LANES, LANES)


def _unpack_small(packed, like):
    flat = packed.reshape(DEPTH, -1)
    out, off = {}, 0
    for k in REPLICATED:
        size = like[k].size // DEPTH
        out[k] = flat[:, off:off + size].reshape(like[k].shape)
        off += size
    return out


def _shard_major(a, axis):
    if axis == 1:
        return a.reshape(DEPTH, N_CHIPS, a.shape[1] // N_CHIPS, a.shape[2])
    return a.reshape(DEPTH, a.shape[1], N_CHIPS, a.shape[2] // N_CHIPS).transpose(0, 2, 1, 3)


def kernel(x, c, ctx, c_ctx, mod_w, mod_b, pre_norm, post_norm, w_in, mla_q_norm, mla_w_uq, mla_kv_norm, mla_w_ukv, pool_w, pool_scale, gla_af_w2, gla_af_b, gla_ab_w2, gla_ab_b, gla_norm, w_branch_mla, w_branch_pool, w_branch_gla, w_out, loss_target, m_c_ctx, m_mod_w, m_mod_b, m_pre_norm, m_post_norm, m_w_in, m_mla_q_norm, m_mla_w_uq, m_mla_kv_norm, m_mla_w_ukv, m_pool_w, m_pool_scale, m_gla_af_w2, m_gla_af_b, m_gla_ab_w2, m_gla_ab_b, m_gla_norm, m_w_branch_mla, m_w_branch_pool, m_w_branch_gla, m_w_out, v_c_ctx, v_mod_w, v_mod_b, v_pre_norm, v_post_norm, v_w_in, v_mla_q_norm, v_mla_w_uq, v_mla_kv_norm, v_mla_w_ukv, v_pool_w, v_pool_scale, v_gla_af_w2, v_gla_af_b, v_gla_ab_w2, v_gla_ab_b, v_gla_norm, v_w_branch_mla, v_w_branch_pool, v_w_branch_gla, v_w_out):
    given = dict(locals())
    wts = {k: given[k] for k in WEIGHTS}
    my_chip = 2 * lax.axis_index("x") + lax.axis_index("y")

    view = lambda k, a: jnp.swapaxes(a, 1, 2) if k == 'w_in' else a
    axes = {k: (3 - axis if k == 'w_in' else axis) for k, axis in SHARDED}

    mine = [view(k, wts[k]).astype(BF16) for k, _ in SHARDED]
    gathered = _all_gather_shards(mine)
    full = dict(wts)
    for (k, _), own, got in zip(SHARDED, mine, gathered):
        full[k] = jnp.concatenate([jnp.where(my_chip == s, own, got[s]) for s in range(N_CHIPS)], axis=axes[k])

    loss_local, grad_x, grads = _local_step(x, c, ctx, loss_target, full)
    loss = lax.psum(loss_local, ("x", "y", "c"))

    small = _pack_small(grads)
    bufs = [_shard_major(grads[k], axes[k]) for k, _ in SHARDED]
    bufs.append(jnp.broadcast_to(small[:, None], (DEPTH, N_CHIPS) + small.shape[1:]))
    got = _to_sibling(bufs, True, "swap_halves")
    chip_sum = [_add_cores(b, g, f"add_cores{i}") for i, (b, g) in enumerate(zip(bufs, got))]
    recv = _scatter_to_chips(chip_sum)
    mine_red = [_sum_chips(cs, rc, f"sum_chips{i}") for i, (cs, rc) in enumerate(zip(chip_sum, recv))]
    other_red = _to_sibling(mine_red, False, "join_halves")

    outs = {}
    for i, (k, _) in enumerate(SHARDED):
        res = _adamw(view(k, wts[k]), mine_red[i], other_red[i], view(k, given['m_' + k]), view(k, given['v_' + k]),
                     f"adamw{i}")
        outs[k] = tuple(view(k, r) for r in res)
    packed = _adamw(_pack_small(wts), mine_red[-1], other_red[-1],
                    _pack_small({k: given['m_' + k] for k in REPLICATED}),
                    _pack_small({k: given['v_' + k] for k in REPLICATED}), "adamw_small")
    unpacked = [_unpack_small(p, wts) for p in packed]
    for k in REPLICATED:
        outs[k] = tuple(u[k] for u in unpacked)
    return (loss, grad_x, *[outs[k][q] for q in range(4) for k in WEIGHTS])
```

```python
import functools

import numpy as np
import jax
import jax.numpy as jnp
from jax import lax
from jax.experimental import pallas as pl
from jax.experimental.pallas import tpu as pltpu

F32 = jnp.float32
BF16 = jnp.bfloat16
HIGHEST = lax.Precision.HIGHEST
MESH = pl.DeviceIdType.MESH

D = 1024
DEPTH = 2
EPS = 1e-6
GRID_W = 64
MLA_H, MLA_NOPE, MLA_ROPE, MLA_V = 8, 64, 32, 64
MLA_QK = MLA_NOPE + MLA_ROPE
ROPE_BASE = 10000.0
POOL_WINDOWS = (2, 4, 8, 16)
GLA_H, GLA_DK, GLA_DV, GLA_RANK, GLA_TAU = 4, 64, 128, 16, 16.0
GLA_C = 128
EXP_CLAMP = 80.0
ADAM_LR, ADAM_B1, ADAM_B2, ADAM_EPS, ADAM_WD, ADAM_STEP = 0.001, 0.9, 0.999, 1e-08, 0.01, 10

TM = 256
LANES = 128
N_CHIPS = 4

IN_NAMES = ('mla_q', 'mla_kv', 'mla_kr', 'mla_gate', 'pool_x', 'pool_gate',
            'gla_q', 'gla_k', 'gla_v', 'gla_af', 'gla_ab', 'gla_gate', 'merge')
IN_SIZES = (256, 128, 32, 512, 512, 512, 256, 256, 512, 16, 16, 512, 3 * D)
NAT_OFF = dict(zip(IN_NAMES, [int(o) for o in np.cumsum((0,) + IN_SIZES[:-1])]))
NAT_SIZE = dict(zip(IN_NAMES, IN_SIZES))
PAD_ORDER = ('merge', 'mla_gate', 'mla_q', 'mla_kv', 'mla_kr', 'pool_x', 'pool_gate',
             'gla_v', 'gla_gate', 'gla_q', 'gla_k', 'gla_af', 'gla_ab')
SLAB = {n: max(NAT_SIZE[n], LANES) for n in IN_NAMES}
PAD_OFF = dict(zip(PAD_ORDER, [int(o) for o in np.cumsum([0] + [SLAB[n] for n in PAD_ORDER[:-1]])]))
D_PAD = sum(SLAB.values())
IN_SLAB = {n: 0 for n in IN_NAMES}
IN_SLAB['mla_kr'] = MLA_NOPE
MLA_HP = 128


def _blk(name):
    return PAD_OFF[name] // SLAB[name]


SHARDED = (('mod_w', 2), ('w_in', 2), ('mla_w_uq', 2), ('mla_w_ukv', 2), ('gla_af_w2', 2), ('gla_ab_w2', 2),
           ('w_branch_mla', 2), ('w_branch_pool', 2), ('w_branch_gla', 2), ('w_out', 1))
REPLICATED = ('c_ctx', 'mod_b', 'pre_norm', 'post_norm', 'mla_q_norm', 'mla_kv_norm', 'pool_w', 'pool_scale',
              'gla_af_b', 'gla_ab_b', 'gla_norm')
WEIGHTS = ('c_ctx', 'mod_w', 'mod_b', 'pre_norm', 'post_norm', 'w_in', 'mla_q_norm', 'mla_w_uq', 'mla_kv_norm',
           'mla_w_ukv', 'pool_w', 'pool_scale', 'gla_af_w2', 'gla_af_b', 'gla_ab_w2', 'gla_ab_b', 'gla_norm',
           'w_branch_mla', 'w_branch_pool', 'w_branch_gla', 'w_out')


def _cp(vmem_mb=None, sem=None):
    kw = {}
    if vmem_mb is not None:
        kw['vmem_limit_bytes'] = vmem_mb * 1024 * 1024
    if sem is not None:
        kw['dimension_semantics'] = sem
    return pltpu.CompilerParams(**kw)


def _bdot(a, b):
    return jnp.dot(a.astype(BF16), b.astype(BF16), preferred_element_type=F32)


def _bdot_nt(a, b):
    return lax.dot_general(a.astype(BF16), b.astype(BF16), (((1,), (1,)), ((), ())), preferred_element_type=F32)


def _bdot_tn(a, b):
    return lax.dot_general(a.astype(BF16), b.astype(BF16), (((0,), (0,)), ((), ())), preferred_element_type=F32)


def _xdot(a, b):
    return jnp.dot(a, b, precision=HIGHEST, preferred_element_type=F32)


def _xdot_tn(a, b):
    return lax.dot_general(a, b, (((0,), (0,)), ((), ())), precision=HIGHEST, preferred_element_type=F32)


NN = (((1,), (0,)), ((), ()))
NT = (((1,), (1,)), ((), ()))
TN = (((0,), (0,)), ((), ()))


def _split(a):
    hi = a.astype(BF16)
    return hi, (a - hi.astype(F32)).astype(BF16)


def _dot3(a, b, dims):
    (ah, al), (bh, bl) = a, b
    f = lambda u, v: lax.dot_general(u, v, dims, preferred_element_type=F32)
    return f(ah, bh) + (f(ah, bl) + f(al, bh))


def _sigmoid(x):
    return jax.nn.sigmoid(x)


def _silu(x):
    return x * _sigmoid(x)


def _dsilu(x):
    s = _sigmoid(x)
    return s * (1.0 + x * (1.0 - s))


def _rstd(x):
    return lax.rsqrt(jnp.mean(x * x, axis=-1, keepdims=True) + EPS)


def _norm_bwd(xhat, r, dy):
    return r * (dy - xhat * jnp.mean(xhat * dy, axis=-1, keepdims=True))


def _colsum(a):
    return jnp.sum(a, axis=0, keepdims=True)


def _modrow(i, tpe):
    return jnp.where(i % tpe == 0, 4, i // tpe)


def _rot(x):
    n = x.shape[-1]
    lane = lax.broadcasted_iota(jnp.int32, x.shape, x.ndim - 1)
    return jnp.where(lane % 16 < 8, -pltpu.roll(x, n - 8, x.ndim - 1), pltpu.roll(x, 8, x.ndim - 1))


def _mod_fwd(cv, mod_w, mod_b):
    def body(cv_ref, w_ref, b_ref, o_ref):
        s = _silu(cv_ref[...])
        for l in range(DEPTH):
            o_ref[l] = _bdot(s, w_ref[l]) + b_ref[l]

    return pl.pallas_call(body, name="mod_fwd", out_shape=jax.ShapeDtypeStruct((DEPTH, 8, 3 * D), F32),
                          compiler_params=_cp(40))(cv, mod_w, mod_b)


def _mod_bwd(cv, sel, st_a, st_b, w_t, dcv_in, name):
    def body(cv_ref, sel_ref, sa_ref, sb_ref, wt_ref, dcin_ref, dw_ref, db_ref, dcv_ref, dcc_ref, dpre_ref, dpost_ref):
        cvv = cv_ref[...]
        s = _silu(cvv)
        sel_v = sel_ref[...]
        dmod = jnp.concatenate([_xdot(sel_v, sa_ref[0]), _xdot(sel_v, sa_ref[1]), _xdot(sel_v, sb_ref[0])], axis=1)
        dw_ref[...] = _bdot_tn(s, dmod)
        db_ref[...] = _colsum(dmod)
        dcv = dcin_ref[...] + _bdot(dmod, wt_ref[...])
        dcv_ref[...] = dcv
        dcc_ref[...] = dcv * _dsilu(cvv)
        dpre_ref[...] = _colsum(sa_ref[2])
        dpost_ref[...] = _colsum(sb_ref[1])

    shapes = (jax.ShapeDtypeStruct((D, 3 * D), F32), jax.ShapeDtypeStruct((1, 3 * D), F32),
              jax.ShapeDtypeStruct((8, D), F32), jax.ShapeDtypeStruct((8, D), F32),
              jax.ShapeDtypeStruct((1, D), F32), jax.ShapeDtypeStruct((1, D), F32))
    return pl.pallas_call(body, name=name, out_shape=shapes, compiler_params=_cp(48))(cv, sel, st_a, st_b, w_t, dcv_in)


def _pre_fwd(x, modl, pre_g, w_t, tpe, name):
    n = x.shape[0]
    nt = n // TM
    ncb = 3
    tn = D_PAD // ncb
    tm = 2 * TM if n % (2 * TM) == 0 else TM

    def norm_body(x_ref, m_ref, g_ref, h_ref):
        xv = x_ref[...]
        m = m_ref[0]
        h_ref[...] = (xv * _rstd(xv) * g_ref[...] * (1.0 + m[:, D:2 * D]) + m[:, 0:D]).astype(BF16)

    h = pl.pallas_call(
        norm_body, name=name + "_norm", grid=(nt,),
        in_specs=[pl.BlockSpec((TM, D), lambda i: (i, 0)),
                  pl.BlockSpec((1, 1, 3 * D), lambda i: (_modrow(i, tpe), 0, 0)),
                  pl.BlockSpec((1, D), lambda i: (0, 0))],
        out_specs=pl.BlockSpec((TM, D), lambda i: (i, 0)),
        out_shape=jax.ShapeDtypeStruct((n, D), BF16),
        compiler_params=_cp(32, ("arbitrary",)),
    )(x, modl.reshape(8, 1, 3 * D), pre_g)

    def mm_body(h_ref, wt_ref, z_ref):
        z_ref[...] = lax.dot_general(h_ref[...], wt_ref[...], NT, preferred_element_type=F32)

    z = pl.pallas_call(
        mm_body, name=name, grid=(ncb, n // tm),
        in_specs=[pl.BlockSpec((tm, D), lambda j, i: (i, 0)), pl.BlockSpec((tn, D), lambda j, i: (j, 0))],
        out_specs=pl.BlockSpec((tm, tn), lambda j, i: (i, j)),
        out_shape=jax.ShapeDtypeStruct((n, D_PAD), F32),
        compiler_params=_cp(48, ("arbitrary", "arbitrary")),
    )(h, w_t)
    return z, h


def _pre_bwd(dz, w_t, x, dxres, modl, pre_g, tpe, name):
    n = x.shape[0]
    nt = n // TM

    def body(dz_ref, wt_ref, x_ref, dr_ref, m_ref, g_ref, dx_ref, st_ref):
        dh = jnp.dot(dz_ref[...], wt_ref[...], preferred_element_type=F32)
        xv = x_ref[...]
        r = _rstd(xv)
        xn = xv * r
        m = m_ref[0]
        sc1 = 1.0 + m[:, D:2 * D]
        g = g_ref[...]
        st_ref[0, 0:1, :] = _colsum(dh)
        st_ref[0, 1:2, :] = _colsum(dh * xn * g)
        st_ref[0, 2:3, :] = _colsum(dh * xn * sc1)
        st_ref[0, 3:8, :] = jnp.zeros((5, D), F32)
        dx_ref[...] = dr_ref[...] + _norm_bwd(xn, r, dh * g * sc1)

    return pl.pallas_call(
        body, name=name, grid=(nt,),
        in_specs=[pl.BlockSpec((TM, D_PAD), lambda i: (i, 0)),
                  pl.BlockSpec((D_PAD, D), lambda i: (0, 0)),
                  pl.BlockSpec((TM, D), lambda i: (i, 0)),
                  pl.BlockSpec((TM, D), lambda i: (i, 0)),
                  pl.BlockSpec((1, 1, 3 * D), lambda i: (_modrow(i, tpe), 0, 0)),
                  pl.BlockSpec((1, D), lambda i: (0, 0))],
        out_specs=[pl.BlockSpec((TM, D), lambda i: (i, 0)),
                   pl.BlockSpec((1, 8, D), lambda i: (i, 0, 0))],
        out_shape=(jax.ShapeDtypeStruct((n, D), F32), jax.ShapeDtypeStruct((nt, 8, D), F32)),
        compiler_params=_cp(56, ("arbitrary",)),
    )(dz, w_t, x, dxres, modl.reshape(8, 1, 3 * D), pre_g)


def _matmul_tn(a, b, tm, tk, name):
    n, k1 = a.shape
    k2 = b.shape[1]

    def body(a_ref, b_ref, o_ref):
        @pl.when(pl.program_id(1) == 0)
        def _():
            o_ref[...] = jnp.zeros(o_ref.shape, F32)

        o_ref[...] += lax.dot_general(a_ref[...], b_ref[...], (((0,), (0,)), ((), ())), preferred_element_type=F32)

    return pl.pallas_call(
        body, name=name, grid=(k1 // tm, n // tk),
        in_specs=[pl.BlockSpec((tk, tm), lambda i, k: (k, i)), pl.BlockSpec((tk, k2), lambda i, k: (k, 0))],
        out_specs=pl.BlockSpec((tm, k2), lambda i, k: (i, 0)),
        out_shape=jax.ShapeDtypeStruct((k1, k2), F32),
        compiler_params=_cp(48, ("arbitrary", "arbitrary")),
    )(a, b)


def _mla_pre(z, q_norm, kv_norm, w_uq, w_ukv, cq, sq, ck, sk, tpe, name):
    n = z.shape[0]
    nt = n // TM
    scale = MLA_QK ** -0.5

    def body(zq_ref, zkv_ref, zkr_ref, gq_ref, gkv_ref, wq_ref, wkv_ref, cq_ref, sq_ref, ck_ref, sk_ref,
             q_ref, kv_ref, kr_ref):
        zq = zq_ref[...]
        qn = zq * _rstd(zq) * gq_ref[...]
        qraw = _bdot(qn, wq_ref[...])
        q_ref[...] = ((qraw * cq_ref[...] + _rot(qraw) * sq_ref[...]) * scale).astype(BF16)
        zkv = zkv_ref[...]
        kvn = zkv * _rstd(zkv) * gkv_ref[...]
        kv_ref[...] = _bdot(kvn, wkv_ref[...]).astype(BF16)
        zkr = zkr_ref[...]
        kr_ref[...] = (zkr * ck_ref[...] + _rot(zkr) * sk_ref[...]).astype(BF16)

    hq, hkv = MLA_H * MLA_HP, MLA_H * (MLA_NOPE + MLA_V)
    const = lambda i: (0, 0)
    tab = lambda i: (i % tpe, 0)
    return pl.pallas_call(
        body, name=name, grid=(nt,),
        in_specs=[pl.BlockSpec((TM, 256), lambda i: (i, _blk('mla_q'))),
                  pl.BlockSpec((TM, 128), lambda i: (i, _blk('mla_kv'))),
                  pl.BlockSpec((TM, 128), lambda i: (i, _blk('mla_kr'))),
                  pl.BlockSpec((1, 256), const), pl.BlockSpec((1, 128), const),
                  pl.BlockSpec((256, hq), const), pl.BlockSpec((128, hkv), const),
                  pl.BlockSpec((TM, hq), tab), pl.BlockSpec((TM, hq), tab),
                  pl.BlockSpec((TM, 128), tab), pl.BlockSpec((TM, 128), tab)],
        out_specs=[pl.BlockSpec((TM, hq), lambda i: (i, 0)), pl.BlockSpec((TM, hkv), lambda i: (i, 0)),
                   pl.BlockSpec((TM, 128), lambda i: (i, 0))],
        out_shape=(jax.ShapeDtypeStruct((n, hq), BF16), jax.ShapeDtypeStruct((n, hkv), BF16),
                   jax.ShapeDtypeStruct((n, 128), BF16)),
        compiler_params=_cp(32, ("arbitrary",)),
    )(z, z, z, q_norm, kv_norm, w_uq, w_ukv, cq, sq, ck, sk)


def _mla_pre_bwd(z, dq, dkv, dkr, q_norm, kv_norm, w_uq_t, w_ukv_t, cq, sq, ck, sk, tpe, name):
    n = z.shape[0]
    nt = n // TM
    scale = MLA_QK ** -0.5
    hq, hkv = MLA_H * MLA_HP, MLA_H * (MLA_NOPE + MLA_V)

    def body(zq_ref, zkv_ref, dq_ref, dkv_ref, dkr_ref, gq_ref, gkv_ref, wqt_ref, wkvt_ref, cq_ref, sq_ref,
             ck_ref, sk_ref, dzq_ref, dzkv_ref, dzkr_ref, dwq_ref, dwkv_ref, dgq_ref, dgkv_ref):
        @pl.when(pl.program_id(0) == 0)
        def _():
            dwq_ref[...] = jnp.zeros(dwq_ref.shape, F32)
            dwkv_ref[...] = jnp.zeros(dwkv_ref.shape, F32)
            dgq_ref[...] = jnp.zeros(dgq_ref.shape, F32)
            dgkv_ref[...] = jnp.zeros(dgkv_ref.shape, F32)

        zq = zq_ref[...]
        rq = _rstd(zq)
        qhat = zq * rq
        gq = gq_ref[...]
        dqs = dq_ref[...] * scale
        dqraw = dqs * cq_ref[...] - _rot(dqs * sq_ref[...])
        dwq_ref[...] += _bdot_tn(qhat * gq, dqraw)
        dqn = _bdot(dqraw, wqt_ref[...])
        dgq_ref[...] += _colsum(dqn * qhat)
        dzq_ref[...] = _norm_bwd(qhat, rq, dqn * gq).astype(BF16)

        zkv = zkv_ref[...]
        rkv = _rstd(zkv)
        khat = zkv * rkv
        gkv = gkv_ref[...]
        dkvv = dkv_ref[...]
        dwkv_ref[...] += _bdot_tn(khat * gkv, dkvv)
        dkvn = _bdot(dkvv, wkvt_ref[...])
        dgkv_ref[...] += _colsum(dkvn * khat)
        dzkv_ref[...] = _norm_bwd(khat, rkv, dkvn * gkv).astype(BF16)

        dkr = dkr_ref[...]
        dzkr_ref[...] = (dkr * ck_ref[...] - _rot(dkr * sk_ref[...])).astype(BF16)

    const = lambda i: (0, 0)
    tab = lambda i: (i % tpe, 0)
    row = lambda i: (i, 0)
    return pl.pallas_call(
        body, name=name, grid=(nt,),
        in_specs=[pl.BlockSpec((TM, 256), lambda i: (i, _blk('mla_q'))),
                  pl.BlockSpec((TM, 128), lambda i: (i, _blk('mla_kv'))),
                  pl.BlockSpec((TM, hq), row), pl.BlockSpec((TM, hkv), row), pl.BlockSpec((TM, 128), row),
                  pl.BlockSpec((1, 256), const), pl.BlockSpec((1, 128), const),
                  pl.BlockSpec((hq, 256), const), pl.BlockSpec((hkv, 128), const),
                  pl.BlockSpec((TM, hq), tab), pl.BlockSpec((TM, hq), tab),
                  pl.BlockSpec((TM, 128), tab), pl.BlockSpec((TM, 128), tab)],
        out_specs=[pl.BlockSpec((TM, 256), row), pl.BlockSpec((TM, 128), row), pl.BlockSpec((TM, 128), row),
                   pl.BlockSpec((256, hq), const), pl.BlockSpec((128, hkv), const),
                   pl.BlockSpec((1, 256), const), pl.BlockSpec((1, 128), const)],
        out_shape=(jax.ShapeDtypeStruct((n, 256), BF16), jax.ShapeDtypeStruct((n, 128), BF16),
                   jax.ShapeDtypeStruct((n, 128), BF16), jax.ShapeDtypeStruct((256, hq), F32),
                   jax.ShapeDtypeStruct((128, hkv), F32), jax.ShapeDtypeStruct((1, 256), F32),
                   jax.ShapeDtypeStruct((1, 128), F32)),
        compiler_params=_cp(32, ("arbitrary",)),
    )(z, z, dq, dkv, dkr, q_norm, kv_norm, w_uq_t, w_ukv_t, cq, sq, ck, sk)


def _attn_head(q_ref, kv_ref, kr_ref, hh, nk):
    kvh = kv_ref[0:nk, hh * MLA_HP:(hh + 1) * MLA_HP]
    lane = lax.broadcasted_iota(jnp.int32, kvh.shape, 1)
    kh = jnp.where(lane < MLA_NOPE, kvh, kr_ref[0:nk, :])
    qh = q_ref[:, hh * MLA_HP:(hh + 1) * MLA_HP]
    return kvh, kh, qh, lax.dot_general(qh, kh, (((1,), (1,)), ((), ())), preferred_element_type=F32)


def _by_segment(j, lc, t, fn):
    pl.when(j == 0)(functools.partial(fn, lc))
    pl.when(j != 0)(functools.partial(fn, t))


def _attn_specs(nb, tpe, t):
    tile = lambda b, p, j: (b * tpe + j, p)
    return [pl.BlockSpec((TM, 2 * MLA_HP), tile),
            pl.BlockSpec((t, 2 * MLA_HP), lambda b, p, j: (b, p)),
            pl.BlockSpec((t, MLA_HP), lambda b, p, j: (b, 0))]


def _attn_fwd(q, kv, kr, nb, lc, name):
    n = q.shape[0]
    t = n // nb
    tpe = t // TM

    def body(q_ref, kv_ref, kr_ref, o_ref, lse_ref):
        def run(nk):
            lane = lax.broadcasted_iota(jnp.int32, (TM, MLA_HP), 1)
            res, lses = [], []
            for hh in range(2):
                kvh, _, _, s = _attn_head(q_ref, kv_ref, kr_ref, hh, nk)
                m = jnp.max(s, axis=-1, keepdims=True)
                p = jnp.exp(s - m)
                l = jnp.sum(p, axis=-1, keepdims=True)
                res.append(jnp.dot(p.astype(BF16), kvh, preferred_element_type=F32) / l)
                lses.append(m + jnp.log(l))
            o_ref[...] = jnp.where(lane < MLA_V, pltpu.roll(res[0], MLA_V, 1), res[1])
            lane2 = lax.broadcasted_iota(jnp.int32, (TM, 2), 1)
            lse_ref[0] = jnp.where(lane2 == 0, lses[0], lses[1])

        _by_segment(pl.program_id(2), lc, t, run)

    return pl.pallas_call(
        body, name=name, grid=(nb, MLA_H // 2, tpe),
        in_specs=_attn_specs(nb, tpe, t),
        out_specs=[pl.BlockSpec((TM, 2 * MLA_V), lambda b, p, j: (b * tpe + j, p)),
                   pl.BlockSpec((1, TM, 2), lambda b, p, j: (p, b * tpe + j, 0))],
        out_shape=(jax.ShapeDtypeStruct((n, MLA_H * MLA_V), F32), jax.ShapeDtypeStruct((MLA_H // 2, n, 2), F32)),
        compiler_params=_cp(48, ("arbitrary", "arbitrary", "arbitrary")),
    )(q, kv, kr)


def _attn_bwd(q, kv, kr, o, lse, do, nb, lc, name):
    n = q.shape[0]
    t = n // nb
    tpe = t // TM

    def body(q_ref, kv_ref, kr_ref, o_ref, lse_ref, do_ref, dq_ref, dkv_ref, dkr_ref):
        p_id, j = pl.program_id(1), pl.program_id(2)

        @pl.when(j == 0)
        def _():
            dkv_ref[...] = jnp.zeros(dkv_ref.shape, F32)

        @pl.when((j == 0) & (p_id == 0))
        def _():
            dkr_ref[...] = jnp.zeros(dkr_ref.shape, F32)

        def run(nk):
            lane = lax.broadcasted_iota(jnp.int32, (TM, MLA_HP), 1)
            lane_t = lax.broadcasted_iota(jnp.int32, (nk, MLA_HP), 1)
            lane2 = lax.broadcasted_iota(jnp.int32, (TM, 2), 1)
            lse = lse_ref[0]
            dov, ov = do_ref[...], o_ref[...]
            dkr = jnp.zeros((nk, MLA_HP), F32)
            for hh in range(2):
                kvh, kh, qh, s = _attn_head(q_ref, kv_ref, kr_ref, hh, nk)
                p = jnp.exp(s - jnp.sum(jnp.where(lane2 == hh, lse, 0.0), axis=1, keepdims=True))
                do_pos = jnp.where(lane >= MLA_NOPE, pltpu.roll(dov, MLA_V, 1) if hh == 0 else dov, 0.0)
                o_pos = jnp.where(lane >= MLA_NOPE, pltpu.roll(ov, MLA_V, 1) if hh == 0 else ov, 0.0)
                delta = jnp.sum(do_pos * o_pos, axis=-1, keepdims=True)
                dob = do_pos.astype(BF16)
                dp = lax.dot_general(dob, kvh, (((1,), (1,)), ((), ())), preferred_element_type=F32)
                ds = (p * (dp - delta)).astype(BF16)
                dq_ref[:, hh * MLA_HP:(hh + 1) * MLA_HP] = jnp.dot(ds, kh, preferred_element_type=F32)
                dkf = lax.dot_general(ds, qh, (((0,), (0,)), ((), ())), preferred_element_type=F32)
                dvp = lax.dot_general(p.astype(BF16), dob, (((0,), (0,)), ((), ())), preferred_element_type=F32)
                dkv_ref[0:nk, hh * MLA_HP:(hh + 1) * MLA_HP] += jnp.where(lane_t < MLA_NOPE, dkf, dvp)
                dkr = dkr + jnp.where(lane_t >= MLA_NOPE, dkf, 0.0)
            dkr_ref[0:nk, :] += dkr

        _by_segment(j, lc, t, run)

    tile = lambda b, p, j: (b * tpe + j, p)
    return pl.pallas_call(
        body, name=name, grid=(nb, MLA_H // 2, tpe),
        in_specs=_attn_specs(nb, tpe, t) + [pl.BlockSpec((TM, 2 * MLA_V), tile),
                                            pl.BlockSpec((1, TM, 2), lambda b, p, j: (p, b * tpe + j, 0)),
                                            pl.BlockSpec((TM, 2 * MLA_V), tile)],
        out_specs=[pl.BlockSpec((TM, 2 * MLA_HP), tile),
                   pl.BlockSpec((t, 2 * MLA_HP), lambda b, p, j: (b, p)),
                   pl.BlockSpec((t, MLA_HP), lambda b, p, j: (b, 0))],
        out_shape=(jax.ShapeDtypeStruct((n, MLA_H * MLA_HP), F32), jax.ShapeDtypeStruct((n, MLA_H * MLA_HP), F32),
                   jax.ShapeDtypeStruct((n, MLA_HP), F32)),
        compiler_params=_cp(56, ("arbitrary", "arbitrary", "arbitrary")),
    )(q, kv, kr, o, lse, do)


def _pool_window(ug, pos, seglen, w, transpose):
    t = ug.shape[0]
    cnt = (jnp.minimum(pos + w // 2, seglen) - jnp.maximum(pos - w // 2, 0)).astype(F32)
    if transpose:
        ug = ug / cnt
    acc = jnp.zeros_like(ug)
    for j in range(-(w // 2), w // 2):
        jj = -j if transpose else j
        src = pos + jj
        valid = (src >= 0) & (src < seglen)
        acc = acc + jnp.where(valid, pltpu.roll(ug, (-jj) % t, 0), 0.0)
    return acc if transpose else acc / cnt


def _by_group(g, fn):
    for k, w in enumerate(POOL_WINDOWS):
        pl.when(g == k)(functools.partial(fn, w))


def _pool_specs(t):
    px, pg = PAD_OFF['pool_x'] // LANES, PAD_OFF['pool_gate'] // LANES
    return [pl.BlockSpec((t, LANES), lambda g, b: (b, px + g)),
            pl.BlockSpec((t, LANES), lambda g, b: (b, pg + g)),
            pl.BlockSpec((1, LANES, LANES), lambda g, b: (g, 0, 0)),
            pl.BlockSpec((1, LANES), lambda g, b: (0, g)),
            pl.BlockSpec((t, 1), lambda g, b: (0, 0)), pl.BlockSpec((t, 1), lambda g, b: (0, 0))]


def _pool_fwd(z, pool_w, pool_scale, pos, seglen, nb, name):
    n = z.shape[0]
    t = n // nb

    def body(u_ref, zg_ref, pw_ref, ps_ref, pos_ref, sl_ref, y_ref):
        def run(w):
            u = u_ref[...]
            pooled = _pool_window(u, pos_ref[...], sl_ref[...], w, False) - u
            y_ref[...] = (_bdot(pooled, pw_ref[0]) * ps_ref[...] * _silu(zg_ref[...])).astype(BF16)

        _by_group(pl.program_id(0), run)

    return pl.pallas_call(
        body, name=name, grid=(4, nb), in_specs=_pool_specs(t),
        out_specs=pl.BlockSpec((t, LANES), lambda g, b: (b, g)),
        out_shape=jax.ShapeDtypeStruct((n, 512), BF16),
        compiler_params=_cp(48, ("arbitrary", "arbitrary")),
    )(z, z, pool_w, pool_scale, pos, seglen)


def _pool_bwd(z, dy, pool_w, pool_w_t, pool_scale, pos, seglen, nb, name):
    n = z.shape[0]
    t = n // nb

    def body(u_ref, zg_ref, pw_ref, ps_ref, pos_ref, sl_ref, dy_ref, pwt_ref, du_ref, dg_ref, dpw_ref, dps_ref):
        @pl.when(pl.program_id(1) == 0)
        def _():
            dpw_ref[...] = jnp.zeros(dpw_ref.shape, F32)
            dps_ref[...] = jnp.zeros(dps_ref.shape, F32)

        def run(w):
            u = u_ref[...]
            pos_v, sl_v = pos_ref[...], sl_ref[...]
            pooled = _pool_window(u, pos_v, sl_v, w, False) - u
            mixed = _bdot(pooled, pw_ref[0])
            zg = zg_ref[...]
            sg = _silu(zg)
            ps = ps_ref[...]
            dyv = dy_ref[...]
            dps_ref[...] += _colsum(dyv * mixed * sg)
            dg_ref[...] = (dyv * mixed * ps * _dsilu(zg)).astype(BF16)
            dmixed = dyv * ps * sg
            dpw_ref[0] += _bdot_tn(pooled, dmixed)
            dpooled = _bdot(dmixed, pwt_ref[0])
            du_ref[...] = (_pool_window(dpooled, pos_v, sl_v, w, True) - dpooled).astype(BF16)

        _by_group(pl.program_id(0), run)

    blk = pl.BlockSpec((t, LANES), lambda g, b: (b, g))
    return pl.pallas_call(
        body, name=name, grid=(4, nb),
        in_specs=_pool_specs(t) + [blk, pl.BlockSpec((1, LANES, LANES), lambda g, b: (g, 0, 0))],
        out_specs=[blk, blk, pl.BlockSpec((1, LANES, LANES), lambda g, b: (g, 0, 0)),
                   pl.BlockSpec((1, LANES), lambda g, b: (0, g))],
        out_shape=(jax.ShapeDtypeStruct((n, 512), BF16), jax.ShapeDtypeStruct((n, 512), BF16),
                   jax.ShapeDtypeStruct((4, 128, 128), F32), jax.ShapeDtypeStruct((1, 512), F32)),
        compiler_params=_cp(48, ("arbitrary", "arbitrary")),
    )(z, z, pool_w, pool_scale, pos, seglen, dy, pool_w_t)


def _gla_chunk(q_ref, k_ref, a_ref, w2_ref, b2_ref, reverse):
    c = GLA_C
    x = _bdot(a_ref[...], w2_ref[0]) + b2_ref[0]
    la = (jnp.minimum(x, 0.0) - jnp.log(1.0 + jnp.exp(-jnp.abs(x)))) * (1.0 / GLA_TAU)
    row = lax.broadcasted_iota(jnp.int32, (c, c), 0)
    col = lax.broadcasted_iota(jnp.int32, (c, c), 1)
    tri = (col >= row) if reverse else (col <= row)
    tri_t = (col <= row) if reverse else (col >= row)
    b = _xdot(tri.astype(F32), la)
    tok = lax.broadcasted_iota(jnp.int32, la.shape, 0)
    bref = _colsum(jnp.where((tok >= c // 2) if reverse else (tok < c // 2), la, 0.0))
    blast = _colsum(la)
    eq = jnp.exp(jnp.minimum(b - bref, EXP_CLAMP))
    ek = jnp.exp(jnp.minimum(bref - b, EXP_CLAMP))
    qs = q_ref[...] * (GLA_DK ** -0.5)
    kk = k_ref[...]
    eb = jnp.exp(b)
    etail = jnp.exp(blast - b)
    return dict(x=x, la=la, tri=tri, tri_t=tri_t, eq=eq, ek=ek, qs=qs, kk=kk, qd=qs * eq, kd=kk * ek, qe=qs * eb,
                kl=kk * etail, eb=eb, etail=etail)


def _pair(a, p):
    return a[:, p * LANES:(p + 1) * LANES]


def _head_masks():
    lane = lax.broadcasted_iota(jnp.int32, (GLA_C, LANES), 1)
    return (lane < GLA_DK, lane >= GLA_DK)


def _state_decay(la, p):
    return jnp.exp(_xdot_tn(_pair(la, p), jnp.ones((GLA_C, GLA_DV), F32)))


def _gla_chunk_maps(nb, nc, ncc, order):
    def rmap(j):
        return jnp.where(j < ncc, ncc - 1 - j, nc - 1 - (j - ncc))

    if order == 'scan':
        return (lambda b, j: b * nc + j), (lambda b, j: b * nc + rmap(j))
    return (lambda b, j: b * nc + nc - 1 - j), (lambda b, j: b * nc + rmap(nc - 1 - j))


def _gla_in_specs(maps):
    specs = []
    for d, cm in enumerate(maps):
        gate = 'gla_af' if d == 0 else 'gla_ab'
        specs += [pl.BlockSpec((GLA_C, 256), lambda b, j, cm=cm: (cm(b, j), _blk('gla_q'))),
                  pl.BlockSpec((GLA_C, 256), lambda b, j, cm=cm: (cm(b, j), _blk('gla_k'))),
                  pl.BlockSpec((GLA_C, 512), lambda b, j, cm=cm: (cm(b, j), _blk('gla_v'))),
                  pl.BlockSpec((GLA_C, LANES), lambda b, j, cm=cm, gate=gate: (cm(b, j), _blk(gate))),
                  pl.BlockSpec((1, LANES, 256), lambda b, j, d=d: (d, 0, 0)),
                  pl.BlockSpec((1, 1, 256), lambda b, j, d=d: (d, 0, 0))]
    return specs


def _gla_fwd(z, w2, b2, nb, lc, name):
    n = z.shape[0]
    nc = n // nb // GLA_C
    maps = _gla_chunk_maps(nb, nc, lc // GLA_C, 'scan')

    def body(*refs):
        ins, (of_ref, ob_ref, sf_ref, sb_ref, s_sc) = refs[:12], refs[12:]

        @pl.when(pl.program_id(1) == 0)
        def _():
            s_sc[...] = jnp.zeros(s_sc.shape, F32)

        masks = _head_masks()
        for d, (o_ref, st_ref) in enumerate(((of_ref, sf_ref), (ob_ref, sb_ref))):
            q_ref, k_ref, v_ref, a_ref, w2_ref, b2_ref = ins[6 * d:6 * d + 6]
            ch = _gla_chunk(q_ref, k_ref, a_ref, w2_ref, b2_ref, d == 1)
            for p in range(2):
                s_prev = s_sc[d, p]
                st_ref[0, p] = s_prev
                s_new = _state_decay(ch['la'], p) * s_prev
                kd_p = _pair(ch['kd'], p)
                for hh in range(2):
                    h = 2 * p + hh
                    vv = v_ref[:, h * GLA_DV:(h + 1) * GLA_DV]
                    att = jnp.where(ch['tri'], _bdot_nt(jnp.where(masks[hh], _pair(ch['qd'], p), 0.0), kd_p), 0.0)
                    o_ref[:, h * GLA_DV:(h + 1) * GLA_DV] = (
                        _bdot(att, vv) + _bdot(jnp.where(masks[hh], _pair(ch['qe'], p), 0.0), s_prev))
                    s_new = s_new + _dot3(_split(jnp.where(masks[hh], _pair(ch['kl'], p), 0.0)), _split(vv), TN)
                s_sc[d, p] = s_new

    o_shape = jax.ShapeDtypeStruct((n, 512), F32)
    st_shape = jax.ShapeDtypeStruct((n // GLA_C, 2, LANES, GLA_DV), F32)
    return pl.pallas_call(
        body, name=name, grid=(nb, nc),
        in_specs=_gla_in_specs(maps),
        out_specs=[pl.BlockSpec((GLA_C, 512), lambda b, j: (maps[0](b, j), 0)),
                   pl.BlockSpec((GLA_C, 512), lambda b, j: (maps[1](b, j), 0)),
                   pl.BlockSpec((1, 2, LANES, GLA_DV), lambda b, j: (maps[0](b, j), 0, 0, 0)),
                   pl.BlockSpec((1, 2, LANES, GLA_DV), lambda b, j: (maps[1](b, j), 0, 0, 0))],
        out_shape=(o_shape, o_shape, st_shape, st_shape),
        scratch_shapes=[pltpu.VMEM((2, 2, LANES, GLA_DV), F32)],
        compiler_params=_cp(32, ("arbitrary", "arbitrary")),
    )(z, z, z, z, w2, b2, z, z, z, z, w2, b2)


def _gla_bwd(z, w2, w2_t, b2, st_f, st_b, dog, nb, lc, name):
    n = z.shape[0]
    nc = n // nb // GLA_C
    maps = _gla_chunk_maps(nb, nc, lc // GLA_C, 'back')

    def body(*refs):
        ins, extra, outs, (ds_sc, sfx_sc) = refs[:12], refs[12:18], refs[18:30], refs[30:]

        @pl.when(pl.program_id(1) == 0)
        def _():
            ds_sc[...] = jnp.zeros(ds_sc.shape, F32)
            sfx_sc[...] = jnp.zeros(sfx_sc.shape, F32)

        @pl.when((pl.program_id(0) == 0) & (pl.program_id(1) == 0))
        def _():
            for r in outs[8:12]:
                r[...] = jnp.zeros(r.shape, F32)

        masks = _head_masks()
        for d in range(2):
            q_ref, k_ref, v_ref, a_ref, w2_ref, b2_ref = ins[6 * d:6 * d + 6]
            w2t_ref, st_ref, do_ref = extra[3 * d:3 * d + 3]
            dq_ref, dk_ref, dv_ref, da_ref = outs[4 * d:4 * d + 4]
            dw2_ref, db2_ref = outs[8 + 2 * d], outs[9 + 2 * d]
            ch = _gla_chunk(q_ref, k_ref, a_ref, w2_ref, b2_ref, d == 1)
            dqs, dks, dbs = [], [], []
            for p in range(2):
                s_prev = st_ref[0, p]
                ds_new = ds_sc[d, p]
                qd_p, kd_p, qe_p, kl_p = (_pair(ch[nme], p) for nme in ('qd', 'kd', 'qe', 'kl'))
                ds_prev = _state_decay(ch['la'], p) * ds_new
                qd_b, kd_b = qd_p.astype(BF16), kd_p.astype(BF16)
                sp_s, dsn_s = _split(s_prev), _split(ds_new)
                dq_h, dk_h, db_h = [], [], []
                for hh in range(2):
                    h = 2 * p + hh
                    vv = v_ref[:, h * GLA_DV:(h + 1) * GLA_DV]
                    dov = do_ref[:, h * GLA_DV:(h + 1) * GLA_DV]
                    att = jnp.where(ch['tri'], _bdot_nt(jnp.where(masks[hh], qd_p, 0.0), kd_p), 0.0)
                    dv_ref[:, h * GLA_DV:(h + 1) * GLA_DV] = (
                        _bdot_tn(att, dov) + _bdot(jnp.where(masks[hh], kl_p, 0.0), ds_new))
                    vv_s, dov_s = _split(vv), _split(dov)
                    datt_b = jnp.where(ch['tri'], lax.dot_general(dov_s[0], vv_s[0], NT, preferred_element_type=F32),
                                       0.0).astype(BF16)
                    dq_in = lax.dot_general(datt_b, kd_b, NN, preferred_element_type=F32)
                    dk_in = lax.dot_general(datt_b, qd_b, TN, preferred_element_type=F32)
                    dq_st = _dot3(dov_s, sp_s, NT) * _pair(ch['eb'], p)
                    dk_st = _dot3(vv_s, dsn_s, NT) * _pair(ch['etail'], p)
                    dq_h.append(dq_in * _pair(ch['eq'], p) + dq_st)
                    dk_h.append(dk_in * _pair(ch['ek'], p) + dk_st)
                    db_h.append((qd_b.astype(F32) * dq_in - kd_b.astype(F32) * dk_in)
                                + (_pair(ch['qs'], p) * dq_st - _pair(ch['kk'], p) * dk_st))
                    ds_prev = ds_prev + _dot3(_split(jnp.where(masks[hh], qe_p, 0.0)), dov_s, TN)
                ds_sc[d, p] = ds_prev
                dqs.append(jnp.where(masks[0], dq_h[0], dq_h[1]))
                dks.append(jnp.where(masks[0], dk_h[0], dk_h[1]))
                dbs.append(jnp.where(masks[0], db_h[0], db_h[1]))
            dq_ref[...] = jnp.concatenate(dqs, axis=1) * (GLA_DK ** -0.5)
            dk_ref[...] = jnp.concatenate(dks, axis=1)
            db = jnp.concatenate(dbs, axis=1)
            dla = _xdot(ch['tri_t'].astype(F32), db) + sfx_sc[d]
            sfx_sc[d] = sfx_sc[d] + _colsum(db)
            dx = dla * (1.0 / GLA_TAU) * _sigmoid(-ch['x'])
            da_ref[...] = _bdot(dx, w2t_ref[0])
            dw2_ref[...] += _bdot_tn(a_ref[...], dx)
            db2_ref[...] += _colsum(dx)

    extra_specs, out_specs = [], []
    for d, cm in enumerate(maps):
        extra_specs += [pl.BlockSpec((1, 256, LANES), lambda b, j, d=d: (d, 0, 0)),
                        pl.BlockSpec((1, 2, LANES, GLA_DV), lambda b, j, cm=cm: (cm(b, j), 0, 0, 0)),
                        pl.BlockSpec((GLA_C, 512), lambda b, j, cm=cm: (cm(b, j), 0))]
        out_specs += [pl.BlockSpec((GLA_C, 256), lambda b, j, cm=cm: (cm(b, j), 0)),
                      pl.BlockSpec((GLA_C, 256), lambda b, j, cm=cm: (cm(b, j), 0)),
                      pl.BlockSpec((GLA_C, 512), lambda b, j, cm=cm: (cm(b, j), 0)),
                      pl.BlockSpec((GLA_C, LANES), lambda b, j, cm=cm: (cm(b, j), 0))]
    const2 = lambda b, j: (0, 0)
    out_specs += [pl.BlockSpec((LANES, 256), const2), pl.BlockSpec((1, 256), const2)] * 2
    per_dir = (jax.ShapeDtypeStruct((n, 256), F32), jax.ShapeDtypeStruct((n, 256), F32),
               jax.ShapeDtypeStruct((n, 512), F32), jax.ShapeDtypeStruct((n, LANES), F32))
    wshape = (jax.ShapeDtypeStruct((LANES, 256), F32), jax.ShapeDtypeStruct((1, 256), F32))
    return pl.pallas_call(
        body, name=name, grid=(nb, nc),
        in_specs=_gla_in_specs(maps) + extra_specs,
        out_specs=out_specs,
        out_shape=per_dir + per_dir + wshape + wshape,
        scratch_shapes=[pltpu.VMEM((2, 2, LANES, GLA_DV), F32), pltpu.VMEM((2, 1, 256), F32)],
        compiler_params=_cp(32, ("arbitrary", "arbitrary")),
    )(z, z, z, z, w2, b2, z, z, z, z, w2, b2, w2_t, st_f, dog, w2_t, st_b, dog)


def _add_cast(a, b, name):
    n, w = a.shape

    def body(a_ref, b_ref, o_ref):
        o_ref[...] = (a_ref[...] + b_ref[...]).astype(BF16)

    return pl.pallas_call(
        body, name=name, grid=(n // TM,),
        in_specs=[pl.BlockSpec((TM, w), lambda i: (i, 0)), pl.BlockSpec((TM, w), lambda i: (i, 0))],
        out_specs=pl.BlockSpec((TM, w), lambda i: (i, 0)),
        out_shape=jax.ShapeDtypeStruct((n, w), BF16),
        compiler_params=_cp(32, ("arbitrary",)),
    )(a, b)


def _gla_out_norm(og):
    hats, rs = [], []
    for h in range(GLA_H):
        seg = og[:, h * GLA_DV:(h + 1) * GLA_DV]
        r = _rstd(seg)
        hats.append(seg * r)
        rs.append(r)
    return hats, rs


def _merge_branches(zm_ref, zgm_ref, zgg_ref, om_ref, yp_ref, ogf_ref, ogb_ref, gn_ref, wbm_ref, wbp_ref, wbg_ref):
    zgm, zgg = zgm_ref[...], zgg_ref[...]
    om = om_ref[...]
    y_mla = om * _silu(zgm)
    hats, rs = _gla_out_norm(ogf_ref[...] + ogb_ref[...])
    gn = gn_ref[...]
    sgg = _silu(zgg)
    y_gla = jnp.concatenate([hats[h] * gn for h in range(GLA_H)], axis=1) * sgg
    ys = (y_mla, yp_ref[...], y_gla)
    ps = (_bdot(y_mla, wbm_ref[...]), jnp.dot(yp_ref[...], wbp_ref[...], preferred_element_type=F32),
          _bdot(y_gla, wbg_ref[...]))
    zm = zm_ref[...]
    gs = tuple(_sigmoid(zm[:, a * D:(a + 1) * D]) for a in range(3))
    merged = gs[0] * ps[0] + gs[1] * ps[1] + gs[2] * ps[2]
    return dict(zgm=zgm, zgg=zgg, om=om, hats=hats, rs=rs, gn=gn, sgg=sgg, ys=ys, ps=ps, gs=gs, merged=merged)


def _merge_in_specs(tpe):
    row = lambda i: (i, 0)
    const = lambda i: (0, 0)
    return [pl.BlockSpec((TM, 3 * D), lambda i: (i, _blk('merge'))),
            pl.BlockSpec((TM, 512), lambda i: (i, _blk('mla_gate'))),
            pl.BlockSpec((TM, 512), lambda i: (i, _blk('gla_gate'))),
            pl.BlockSpec((TM, 512), row), pl.BlockSpec((TM, 512), row), pl.BlockSpec((TM, 512), row),
            pl.BlockSpec((TM, 512), row), pl.BlockSpec((1, 128), const),
            pl.BlockSpec((512, D), const), pl.BlockSpec((512, D), const), pl.BlockSpec((512, D), const),
            pl.BlockSpec((1, 1, 3 * D), lambda i: (_modrow(i, tpe), 0, 0)), pl.BlockSpec((1, D), const)]


def _merge_fwd(x, z, o_mla, y_pool, ogf, ogb, gla_n, wbm, wbp, wbg, wout, modl, post_g, tpe, name):
    n = x.shape[0]

    def body(zm_ref, zgm_ref, zgg_ref, om_ref, yp_ref, ogf_ref, ogb_ref, gn_ref, wbm_ref, wbp_ref, wbg_ref,
             m_ref, pg_ref, x_ref, wo_ref, xn_ref, out_ref):
        mb = _merge_branches(zm_ref, zgm_ref, zgg_ref, om_ref, yp_ref, ogf_ref, ogb_ref, gn_ref,
                             wbm_ref, wbp_ref, wbg_ref)
        out = _bdot(mb['merged'], wo_ref[...])
        gate = m_ref[0][:, 2 * D:3 * D]
        xn_ref[...] = x_ref[...] + gate * (out * _rstd(out) * pg_ref[...])
        out_ref[...] = out

    row = lambda i: (i, 0)
    return pl.pallas_call(
        body, name=name, grid=(n // TM,),
        in_specs=_merge_in_specs(tpe) + [pl.BlockSpec((TM, D), row), pl.BlockSpec((D, D), lambda i: (0, 0))],
        out_specs=[pl.BlockSpec((TM, D), row), pl.BlockSpec((TM, D), row)],
        out_shape=(jax.ShapeDtypeStruct((n, D), F32), jax.ShapeDtypeStruct((n, D), F32)),
        compiler_params=_cp(48, ("arbitrary",)),
    )(z, z, z, o_mla, y_pool, ogf, ogb, gla_n, wbm, wbp, wbg, modl.reshape(8, 1, 3 * D), post_g, x, wout)


def _merge_bwd(dxn, out, z, o_mla, y_pool, ogf, ogb, gla_n, wbm, wbp, wbg, wbm_t, wbp_t, wbg_t, wout_t,
               modl, post_g, tpe, name):
    n = out.shape[0]
    nt = n // TM

    def body(zm_ref, zgm_ref, zgg_ref, om_ref, yp_ref, ogf_ref, ogb_ref, gn_ref, wbm_ref, wbp_ref, wbg_ref,
             m_ref, pg_ref, dxn_ref, out_ref, wbmt_ref, wbpt_ref, wbgt_ref, wot_ref,
             dzm_ref, dom_ref, dzgm_ref, dyp_ref, dog_ref, dzgg_ref, st_ref,
             dwbm_ref, dwbp_ref, dwbg_ref, dwo_ref, dgn_ref):
        @pl.when(pl.program_id(0) == 0)
        def _():
            for r in (dwbm_ref, dwbp_ref, dwbg_ref, dwo_ref, dgn_ref):
                r[...] = jnp.zeros(r.shape, F32)

        mb = _merge_branches(zm_ref, zgm_ref, zgg_ref, om_ref, yp_ref, ogf_ref, ogb_ref, gn_ref,
                             wbm_ref, wbp_ref, wbg_ref)
        out = out_ref[...]
        r2 = _rstd(out)
        on = out * r2
        pg = pg_ref[...]
        gate = m_ref[0][:, 2 * D:3 * D]
        dxn_v = dxn_ref[...]
        st_ref[0, 0:1, :] = _colsum(dxn_v * on * pg)
        st_ref[0, 1:2, :] = _colsum(dxn_v * gate * on)
        st_ref[0, 2:8, :] = jnp.zeros((6, D), F32)
        dout = _norm_bwd(on, r2, dxn_v * gate * pg)
        dwo_ref[...] += _bdot_tn(mb['merged'], dout)
        dmerged = _bdot(dout, wot_ref[...])
        dys = []
        for a, (dw_ref, wt_ref) in enumerate(((dwbm_ref, wbmt_ref), (dwbp_ref, wbpt_ref), (dwbg_ref, wbgt_ref))):
            g = mb['gs'][a]
            dzm_ref[:, a * D:(a + 1) * D] = (dmerged * mb['ps'][a] * g * (1.0 - g)).astype(BF16)
            dp = dmerged * g
            dw_ref[...] += _bdot_tn(mb['ys'][a], dp)
            dys.append(_bdot(dp, wt_ref[...]))
        dom_ref[...] = dys[0] * _silu(mb['zgm'])
        dzgm_ref[...] = (dys[0] * mb['om'] * _dsilu(mb['zgm'])).astype(BF16)
        dyp_ref[...] = dys[1]
        gn = mb['gn']
        dgn = jnp.zeros((1, GLA_DV), F32)
        dzgg, dog = [], []
        for h in range(GLA_H):
            sl = slice(h * GLA_DV, (h + 1) * GLA_DV)
            dyg = dys[2][:, sl]
            hat = mb['hats'][h]
            dzgg.append(dyg * hat * gn * _dsilu(mb['zgg'][:, sl]))
            dn = dyg * mb['sgg'][:, sl]
            dgn = dgn + _colsum(dn * hat)
            dog.append(_norm_bwd(hat, mb['rs'][h], dn * gn))
        dgn_ref[...] += dgn
        dzgg_ref[...] = jnp.concatenate(dzgg, axis=1).astype(BF16)
        dog_ref[...] = jnp.concatenate(dog, axis=1)

    row = lambda i: (i, 0)
    const = lambda i: (0, 0)
    wspec = pl.BlockSpec((512, D), const)
    wtspec = pl.BlockSpec((D, 512), const)
    return pl.pallas_call(
        body, name=name, grid=(nt,),
        in_specs=_merge_in_specs(tpe) + [pl.BlockSpec((TM, D), row), pl.BlockSpec((TM, D), row),
                                         wtspec, wtspec, wtspec,
                                         pl.BlockSpec((D, D), const)],
        out_specs=[pl.BlockSpec((TM, 3 * D), row), pl.BlockSpec((TM, 512), row), pl.BlockSpec((TM, 512), row),
                   pl.BlockSpec((TM, 512), row), pl.BlockSpec((TM, 512), row), pl.BlockSpec((TM, 512), row),
                   pl.BlockSpec((1, 8, D), lambda i: (i, 0, 0)),
                   wspec, wspec, wspec, pl.BlockSpec((D, D), const), pl.BlockSpec((1, 128), const)],
        out_shape=(jax.ShapeDtypeStruct((n, 3 * D), BF16), jax.ShapeDtypeStruct((n, 512), F32),
                   jax.ShapeDtypeStruct((n, 512), BF16), jax.ShapeDtypeStruct((n, 512), F32),
                   jax.ShapeDtypeStruct((n, 512), F32), jax.ShapeDtypeStruct((n, 512), BF16),
                   jax.ShapeDtypeStruct((nt, 8, D), F32),
                   jax.ShapeDtypeStruct((512, D), F32), jax.ShapeDtypeStruct((512, D), F32),
                   jax.ShapeDtypeStruct((512, D), F32), jax.ShapeDtypeStruct((D, D), F32),
                   jax.ShapeDtypeStruct((1, 128), F32)),
        compiler_params=_cp(56, ("arbitrary",)),
    )(z, z, z, o_mla, y_pool, ogf, ogb, gla_n, wbm, wbp, wbg, modl.reshape(8, 1, 3 * D), post_g,
      dxn, out, wbm_t, wbp_t, wbg_t, wout_t)


def _loss_grad(xf, tgt, nb, tpe):
    n = xf.shape[0]

    def body(x_ref, t_ref, dx_ref, l_ref):
        j = pl.program_id(1)
        d = x_ref[...] - t_ref[...]
        latent = j > 0
        dx_ref[...] = jnp.where(latent, d * (1.0 / D), 0.0)
        l_ref[...] = jnp.full(l_ref.shape, jnp.where(latent, 0.5 / D * jnp.sum(d * d), 0.0), F32)

    return pl.pallas_call(
        body, name="loss_grad", grid=(nb, tpe),
        in_specs=[pl.BlockSpec((TM, D), lambda b, j: (b * tpe + j, 0)),
                  pl.BlockSpec((TM, D), lambda b, j: (b * (tpe - 1) + jnp.maximum(j - 1, 0), 0))],
        out_specs=[pl.BlockSpec((TM, D), lambda b, j: (b * tpe + j, 0)),
                   pl.BlockSpec((1, 8, 128), lambda b, j: (b * tpe + j, 0, 0))],
        out_shape=(jax.ShapeDtypeStruct((n, D), F32), jax.ShapeDtypeStruct((n // TM, 8, 128), F32)),
        compiler_params=_cp(32, ("arbitrary", "arbitrary")),
    )(xf, tgt)


def _to_padded(w_nat):
    parts = []
    for nme in PAD_ORDER:
        p = w_nat[NAT_OFF[nme]:NAT_OFF[nme] + NAT_SIZE[nme]]
        if SLAB[nme] > NAT_SIZE[nme]:
            p = jnp.pad(p, [(IN_SLAB[nme], SLAB[nme] - NAT_SIZE[nme] - IN_SLAB[nme]), (0, 0)])
        parts.append(p)
    return jnp.concatenate(parts, axis=0)


def _from_padded(w_pad):
    return jnp.concatenate([w_pad[PAD_OFF[nme] + IN_SLAB[nme]:PAD_OFF[nme] + IN_SLAB[nme] + NAT_SIZE[nme]]
                            for nme in IN_NAMES], axis=0)


def _rope_tables(lc, l):
    half = MLA_ROPE // 2
    inv = ROPE_BASE ** (-jnp.arange(0, half, 2, dtype=F32) / half)
    tok = jnp.arange(l)
    ang_r = (tok // GRID_W).astype(F32)[:, None] * inv
    ang_c = (tok % GRID_W).astype(F32)[:, None] * inv
    ang = jnp.concatenate([ang_r, ang_r, ang_c, ang_c], axis=-1)
    cos = jnp.concatenate([jnp.ones((lc, MLA_ROPE), F32), jnp.cos(ang)], axis=0)
    sin = jnp.concatenate([jnp.zeros((lc, MLA_ROPE), F32), jnp.sin(ang)], axis=0)
    t = lc + l
    tail = MLA_HP - MLA_QK
    ck = jnp.concatenate([jnp.ones((t, MLA_NOPE), F32), cos, jnp.ones((t, tail), F32)], axis=1)
    sk = jnp.concatenate([jnp.zeros((t, MLA_NOPE), F32), sin, jnp.zeros((t, tail), F32)], axis=1)
    return jnp.tile(ck, (1, MLA_H)), jnp.tile(sk, (1, MLA_H)), ck, sk


def _pad_heads(w):
    lead = w.shape[:-1]
    w = w.reshape(lead + (MLA_H, MLA_QK))
    return jnp.pad(w, [(0, 0)] * len(lead) + [(0, 0), (0, MLA_HP - MLA_QK)]).reshape(lead + (MLA_H * MLA_HP,))


def _unpad_heads(w):
    lead = w.shape[:-1]
    return w.reshape(lead + (MLA_H, MLA_HP))[..., :MLA_QK].reshape(lead + (MLA_H * MLA_QK,))


def _local_step(x, c, ctx, tgt, wf):
    nb, l, _ = x.shape
    lc = ctx.shape[1]
    assert lc == TM and l % TM == 0
    t = lc + l
    tpe = t // TM
    n = nb * t
    nt = n // TM
    bf = lambda a: a.astype(BF16)

    xs = jnp.concatenate([ctx, x], axis=1).reshape(n, D)
    assert nb <= 4
    cv = jnp.concatenate([c, jnp.zeros((4 - nb, D), F32), wf['c_ctx'][None, :], jnp.zeros((3, D), F32)], axis=0)
    mod_w_b = bf(wf['mod_w'])
    mod_all = _mod_fwd(cv, mod_w_b, wf['mod_b'].reshape(DEPTH, 1, 3 * D))
    cq, sq, ck, sk = _rope_tables(lc, l)
    pos = jnp.concatenate([jnp.arange(lc), jnp.arange(l)]).astype(jnp.int32)[:, None]
    seglen = jnp.concatenate([jnp.full((lc,), lc), jnp.full((l,), l)]).astype(jnp.int32)[:, None]
    tiles = np.arange(nt)
    ntp = -(-nt // LANES) * LANES
    sel = np.zeros((8, ntp), np.float32)
    sel[np.where(tiles % tpe == 0, 4, tiles // tpe), tiles] = 1.0
    sel = jnp.asarray(sel)

    def tile_sums(st):
        return jnp.pad(st.transpose(1, 0, 2), ((0, 0), (0, ntp - nt), (0, 0)))

    lw = []
    for ly in range(DEPTH):
        w_in_t = _to_padded(bf(wf['w_in'][ly]))
        w_uq_p = _pad_heads(bf(wf['mla_w_uq'][ly]))
        w2 = jnp.pad(jnp.stack([bf(wf['gla_af_w2'][ly]), bf(wf['gla_ab_w2'][ly])]),
                     ((0, 0), (0, LANES - GLA_RANK), (0, 0)))
        lw.append(dict(
            w_in_t=w_in_t,
            w_uq=w_uq_p, w_uq_t=w_uq_p.T,
            w_ukv=bf(wf['mla_w_ukv'][ly]), w_ukv_t=bf(wf['mla_w_ukv'][ly]).T,
            pool_w=bf(wf['pool_w'][ly]), pool_w_t=bf(wf['pool_w'][ly]).transpose(0, 2, 1),
            w2=w2, w2_t=w2.transpose(0, 2, 1),
            b2=jnp.stack([wf['gla_af_b'][ly], wf['gla_ab_b'][ly]]).reshape(2, 1, GLA_H * GLA_DK),
            wbm=bf(wf['w_branch_mla'][ly]), wbp=bf(wf['w_branch_pool'][ly]), wbg=bf(wf['w_branch_gla'][ly]),
            wout=bf(wf['w_out'][ly]),
            wbm_t=bf(wf['w_branch_mla'][ly]).T, wbp_t=bf(wf['w_branch_pool'][ly]).T,
            wbg_t=bf(wf['w_branch_gla'][ly]).T, wout_t=bf(wf['w_out'][ly]).T,
            pre_g=wf['pre_norm'][ly][None, :], post_g=wf['post_norm'][ly][None, :],
            q_norm=wf['mla_q_norm'][ly][None, :], kv_norm=wf['mla_kv_norm'][ly][None, :],
            pool_scale=wf['pool_scale'][ly][None, :], gla_norm=wf['gla_norm'][ly][None, :]))

    saved = []
    xcur = xs
    for ly in range(DEPTH):
        w = lw[ly]
        z, h = _pre_fwd(xcur, mod_all[ly], w['pre_g'], w['w_in_t'], tpe, f"pre_fwd{ly}")
        qb, kvb, krb = _mla_pre(z, w['q_norm'], w['kv_norm'], w['w_uq'], w['w_ukv'], cq, sq, ck, sk, tpe, f"mla_pre{ly}")
        o_mla, lse = _attn_fwd(qb, kvb, krb, nb, lc, f"attn_fwd{ly}")
        y_pool = _pool_fwd(z, w['pool_w'], w['pool_scale'], pos, seglen, nb, f"pool_fwd{ly}")
        ogf, ogb, st_f, st_r = _gla_fwd(z, w['w2'], w['b2'], nb, lc, f"gla_fwd{ly}")
        xnew, out = _merge_fwd(xcur, z, o_mla, y_pool, ogf, ogb, w['gla_norm'], w['wbm'], w['wbp'], w['wbg'],
                               w['wout'], mod_all[ly], w['post_g'], tpe, f"merge_fwd{ly}")
        saved.append(dict(x=xcur, z=z, h=h, qb=qb, kvb=kvb, krb=krb, lse=lse, o_mla=o_mla, y_pool=y_pool,
                          st_f=st_f, st_r=st_r, ogf=ogf, ogb=ogb, out=out))
        xcur = xnew

    dxcur, lparts = _loss_grad(xcur, tgt.reshape(nb * l, D), nb, tpe)
    loss = jnp.sum(lparts[:, 0, 0])

    g = {k: [None] * DEPTH for k in WEIGHTS if k != 'c_ctx'}
    dcv = jnp.zeros((8, D), F32)
    dcc = None
    for ly in reversed(range(DEPTH)):
        w, s = lw[ly], saved[ly]
        (dzm, dom, dzgm, dyp, dog, dzgg, st_b, g['w_branch_mla'][ly], g['w_branch_pool'][ly], g['w_branch_gla'][ly],
         g['w_out'][ly], dgn) = _merge_bwd(
            dxcur, s['out'], s['z'], s['o_mla'], s['y_pool'], s['ogf'], s['ogb'], w['gla_norm'], w['wbm'], w['wbp'],
            w['wbg'], w['wbm_t'], w['wbp_t'], w['wbg_t'], w['wout_t'], mod_all[ly], w['post_g'], tpe,
            f"merge_bwd{ly}")
        g['gla_norm'][ly] = dgn[0]
        dq, dkv, dkr = _attn_bwd(s['qb'], s['kvb'], s['krb'], s['o_mla'], s['lse'], dom, nb, lc, f"attn_bwd{ly}")
        dzq, dzkv, dzkr, dwq, g['mla_w_ukv'][ly], dgq, dgkv = _mla_pre_bwd(
            s['z'], dq, dkv, dkr, w['q_norm'], w['kv_norm'], w['w_uq_t'], w['w_ukv_t'], cq, sq, ck, sk, tpe,
            f"mla_pre_bwd{ly}")
        g['mla_w_uq'][ly] = _unpad_heads(dwq)
        g['mla_q_norm'][ly], g['mla_kv_norm'][ly] = dgq[0], dgkv[0]
        dzpx, dzpg, g['pool_w'][ly], dps = _pool_bwd(s['z'], dyp, w['pool_w'], w['pool_w_t'], w['pool_scale'],
                                                     pos, seglen, nb, f"pool_bwd{ly}")
        g['pool_scale'][ly] = dps[0]
        (dq_f, dk_f, dv_f, da_f, dq_r, dk_r, dv_r, da_r, dw2_f, db2_f, dw2_r, db2_r) = _gla_bwd(
            s['z'], w['w2'], w['w2_t'], w['b2'], s['st_f'], s['st_r'], dog, nb, lc, f"gla_bwd{ly}")
        dzgq = _add_cast(dq_f, dq_r, f"gla_dq{ly}")
        dzgk = _add_cast(dk_f, dk_r, f"gla_dk{ly}")
        dzgv = _add_cast(dv_f, dv_r, f"gla_dv{ly}")
        g['gla_af_w2'][ly], g['gla_ab_w2'][ly] = dw2_f[:GLA_RANK], dw2_r[:GLA_RANK]
        g['gla_af_b'][ly], g['gla_ab_b'][ly] = db2_f[0], db2_r[0]
        parts = dict(merge=dzm, mla_gate=dzgm, mla_q=dzq, mla_kv=dzkv, mla_kr=dzkr, pool_x=dzpx, pool_gate=dzpg,
                     gla_v=dzgv, gla_gate=dzgg, gla_q=dzgq, gla_k=dzgk, gla_af=bf(da_f), gla_ab=bf(da_r))
        dz = jnp.concatenate([parts[nme] for nme in PAD_ORDER], axis=1)
        dxcur, st_a = _pre_bwd(dz, w['w_in_t'], s['x'], dxcur, mod_all[ly], w['pre_g'], tpe, f"pre_bwd{ly}")
        tk = next(k for k in (1024, 512, TM) if n % k == 0)
        g['w_in'][ly] = _from_padded(_matmul_tn(dz, s['h'], 768, tk, f"w_in_grad{ly}"))
        dmw, dmb, dcv, dcc, dpre, dpost = _mod_bwd(cv, sel, tile_sums(st_a), tile_sums(st_b),
                                                   mod_w_b[ly].T, dcv, f"mod_bwd{ly}")
        g['mod_w'][ly], g['mod_b'][ly] = dmw, dmb[0]
        g['pre_norm'][ly], g['post_norm'][ly] = dpre[0], dpost[0]

    grads = {k: jnp.stack(v) for k, v in g.items()}
    grads['c_ctx'] = dcc[4]
    grad_x = dxcur.reshape(nb, t, D)[:, lc:, :]
    return loss, grad_x, grads


def _place():
    x, y, c = lax.axis_index("x"), lax.axis_index("y"), lax.axis_index("c")
    chips = [(1 - x, y), (x, 1 - y), (1 - x, 1 - y)]
    return x, y, c, chips


def _hbm_call(body, name, out_shape, n_in, sems):
    any_spec = pl.BlockSpec(memory_space=pl.ANY)
    return pl.pallas_call(body, name=name, out_shape=out_shape, in_specs=[any_spec] * n_in,
                          out_specs=jax.tree.map(lambda _: any_spec, out_shape), scratch_shapes=sems)


def _all_gather_shards(ws):
    n = len(ws)

    def body(*refs):
        ins, outs, (send_sems, recv_sems) = refs[:n], refs[n:2 * n], refs[2 * n:]
        x, y, c, chips = _place()

        def copy(k, q, chip, half, to, src=None):
            dst = outs[k].at[2 * chip[0] + chip[1], half]
            return pltpu.make_async_remote_copy(src_ref=dst if src is None else src, dst_ref=dst,
                                                send_sem=send_sems.at[k, q], recv_sem=recv_sems.at[k, q],
                                                device_id=to, device_id_type=MESH)

        first = [copy(k, j, (x, y), c, (*chip, c), src=ins[k].at[c]) for k in range(n) for j, chip in enumerate(chips)]
        for cp in first:
            cp.start()
        passed = []
        for k in range(n):
            for j, chip in enumerate(chips):
                copy(k, j, chip, c, (x, y, c)).wait_recv()
                passed.append(copy(k, 3 + j, chip, c, (x, y, 1 - c)))
                passed[-1].start()
        for k in range(n):
            for j, chip in enumerate(chips):
                copy(k, 3 + j, chip, 1 - c, (x, y, 1 - c)).wait_recv()
        for cp in first + passed:
            cp.wait_send()

    shapes = tuple(jax.ShapeDtypeStruct((N_CHIPS,) + w.shape, w.dtype) for w in ws)
    return _hbm_call(body, "all_gather_shards", shapes, n,
                     [pltpu.SemaphoreType.DMA((n, 6)), pltpu.SemaphoreType.DMA((n, 6))])(*ws)


def _to_sibling(arrs, other_layer, name):
    n = len(arrs)

    def body(*refs):
        ins, outs, (send_sems, recv_sems) = refs[:n], refs[n:2 * n], refs[2 * n:]
        x, y, c, _ = _place()
        cps = [pltpu.make_async_remote_copy(src_ref=ins[k].at[1 - c] if other_layer else ins[k], dst_ref=outs[k],
                                            send_sem=send_sems.at[k], recv_sem=recv_sems.at[k],
                                            device_id=(x, y, 1 - c), device_id_type=MESH) for k in range(n)]
        for cp in cps:
            cp.start()
        for cp in cps:
            cp.wait()

    shapes = tuple(jax.ShapeDtypeStruct(a.shape[1:] if other_layer else a.shape, a.dtype) for a in arrs)
    return _hbm_call(body, name, shapes, n, [pltpu.SemaphoreType.DMA((n,)), pltpu.SemaphoreType.DMA((n,))])(*arrs)


def _scatter_to_chips(hs):
    n = len(hs)

    def body(*refs):
        ins, outs, (send_sems, recv_sems) = refs[:n], refs[n:2 * n], refs[2 * n:]
        x, y, c, chips = _place()
        me = 2 * x + y
        sends = []
        for k in range(n):
            for j, chip in enumerate(chips):
                cp = pltpu.make_async_remote_copy(src_ref=ins[k].at[2 * chip[0] + chip[1]], dst_ref=outs[k].at[me],
                                                  send_sem=send_sems.at[k, j], recv_sem=recv_sems.at[k, j],
                                                  device_id=(*chip, c), device_id_type=MESH)
                cp.start()
                sends.append(cp)
        for k in range(n):
            for j, chip in enumerate(chips):
                dst = outs[k].at[2 * chip[0] + chip[1]]
                pltpu.make_async_remote_copy(src_ref=dst, dst_ref=dst, send_sem=send_sems.at[k, j],
                                             recv_sem=recv_sems.at[k, j], device_id=(*chip, c),
                                             device_id_type=MESH).wait_recv()
        for cp in sends:
            cp.wait_send()

    shapes = tuple(jax.ShapeDtypeStruct(h.shape, h.dtype) for h in hs)
    return _hbm_call(body, "scatter_to_chips", shapes, n,
                     [pltpu.SemaphoreType.DMA((n, 3)), pltpu.SemaphoreType.DMA((n, 3))])(*hs)


BLOCK_BYTES = 10 * 1024 * 1024


def _blocks(r, cols, pos_bytes):
    rows = sorted({d for d in range(8, r + 1, 8) if r % d == 0} | {r})
    wide = sorted({d for d in range(LANES, cols + 1, LANES) if cols % d == 0} | {cols})
    fits = [(br * bc, bc, br) for br in rows for bc in wide if br * bc * pos_bytes <= BLOCK_BYTES]
    if not fits:
        return rows[0], wide[0]
    _, bc, br = max(fits)
    return br, bc


def _add_cores(b, got, name):
    _, ns, r, cols = b.shape
    br, bc = _blocks(r, cols, 2 * 4 + 4 + 2)

    def body(b_ref, g_ref, o_ref):
        mine = jnp.where(lax.axis_index("c") == 0, b_ref[0, 0], b_ref[1, 0])
        o_ref[0] = (mine + g_ref[0]).astype(BF16)

    spec = pl.BlockSpec((1, br, bc), lambda i, j, k: (i, j, k))
    return pl.pallas_call(body, name=name, grid=(ns, r // br, cols // bc),
                          in_specs=[pl.BlockSpec((2, 1, br, bc), lambda i, j, k: (0, i, j, k)), spec], out_specs=spec,
                          out_shape=jax.ShapeDtypeStruct((ns, r, cols), BF16), compiler_params=_cp(40))(b, got)


def _sum_chips(own, got, name):
    _, r, cols = own.shape
    br, bc = _blocks(r, cols, 2 * N_CHIPS * 2 + 4)

    def body(own_ref, got_ref, o_ref):
        me = 2 * lax.axis_index("x") + lax.axis_index("y")
        part = [jnp.where(me == j, own_ref[j], got_ref[j]).astype(F32) for j in range(N_CHIPS)]
        o_ref[...] = ((part[0] + part[1]) + part[2]) + part[3]

    spec = pl.BlockSpec((N_CHIPS, br, bc), lambda j, k: (0, j, k))
    return pl.pallas_call(body, name=name, grid=(r // br, cols // bc), in_specs=[spec, spec],
                          out_specs=pl.BlockSpec((br, bc), lambda j, k: (j, k)),
                          out_shape=jax.ShapeDtypeStruct((r, cols), F32), compiler_params=_cp(40))(own, got)


def _adamw(w, g_mine, g_other, m, v, name):
    _, r, cols = w.shape
    br, bc = _blocks(r, cols, 9 * 4)

    def body(w_ref, gm_ref, go_ref, m_ref, v_ref, g_ref, d_ref, nm_ref, nv_ref):
        gv = jnp.where(pl.program_id(0) == lax.axis_index("c"), gm_ref[...], go_ref[...])
        m2 = ADAM_B1 * m_ref[0] + (1.0 - ADAM_B1) * gv
        v2 = ADAM_B2 * v_ref[0] + (1.0 - ADAM_B2) * jnp.square(gv)
        m_hat = m2 / (1.0 - ADAM_B1 ** ADAM_STEP)
        v_hat = v2 / (1.0 - ADAM_B2 ** ADAM_STEP)
        g_ref[0] = gv
        d_ref[0] = -ADAM_LR * (m_hat / (jnp.sqrt(v_hat) + ADAM_EPS) + ADAM_WD * w_ref[0])
        nm_ref[0] = m2
        nv_ref[0] = v2

    lay = pl.BlockSpec((1, br, bc), lambda l, j, k: (l, j, k))
    flat = pl.BlockSpec((br, bc), lambda l, j, k: (j, k))
    shp = jax.ShapeDtypeStruct(w.shape, F32)
    return pl.pallas_call(body, name=name, grid=(2, r // br, cols // bc), in_specs=[lay, flat, flat, lay, lay],
                          out_specs=[lay] * 4, out_shape=(shp,) * 4, compiler_params=_cp(40))(w, g_mine, g_other, m, v)


def _pack_small(ts):
    flat = jnp.concatenate([ts[k].reshape(DEPTH, -1) for k in REPLICATED], axis=1)
    return flat.reshape(DEPTH, flat.shape[1] // LANES, LANES)


def _unpack_small(packed, like):
    flat = packed.reshape(DEPTH, -1)
    out, off = {}, 0
    for k in REPLICATED:
        size = like[k].size // DEPTH
        out[k] = flat[:, off:off + size].reshape(like[k].shape)
        off += size
    return out


def _shard_major(a, axis):
    if axis == 1:
        return a.reshape(DEPTH, N_CHIPS, a.shape[1] // N_CHIPS, a.shape[2])
    return a.reshape(DEPTH, a.shape[1], N_CHIPS, a.shape[2] // N_CHIPS).transpose(0, 2, 1, 3)


def kernel(x, c, ctx, c_ctx, mod_w, mod_b, pre_norm, post_norm, w_in, mla_q_norm, mla_w_uq, mla_kv_norm, mla_w_ukv, pool_w, pool_scale, gla_af_w2, gla_af_b, gla_ab_w2, gla_ab_b, gla_norm, w_branch_mla, w_branch_pool, w_branch_gla, w_out, loss_target, m_c_ctx, m_mod_w, m_mod_b, m_pre_norm, m_post_norm, m_w_in, m_mla_q_norm, m_mla_w_uq, m_mla_kv_norm, m_mla_w_ukv, m_pool_w, m_pool_scale, m_gla_af_w2, m_gla_af_b, m_gla_ab_w2, m_gla_ab_b, m_gla_norm, m_w_branch_mla, m_w_branch_pool, m_w_branch_gla, m_w_out, v_c_ctx, v_mod_w, v_mod_b, v_pre_norm, v_post_norm, v_w_in, v_mla_q_norm, v_mla_w_uq, v_mla_kv_norm, v_mla_w_ukv, v_pool_w, v_pool_scale, v_gla_af_w2, v_gla_af_b, v_gla_ab_w2, v_gla_ab_b, v_gla_norm, v_w_branch_mla, v_w_branch_pool, v_w_branch_gla, v_w_out):
    given = dict(locals())
    wts = {k: given[k] for k in WEIGHTS}
    my_chip = 2 * lax.axis_index("x") + lax.axis_index("y")

    view = lambda k, a: jnp.swapaxes(a, 1, 2) if k == 'w_in' else a
    axes = {k: (3 - axis if k == 'w_in' else axis) for k, axis in SHARDED}

    mine = [view(k, wts[k]).astype(BF16) for k, _ in SHARDED]
    gathered = _all_gather_shards(mine)
    full = dict(wts)
    for (k, _), own, got in zip(SHARDED, mine, gathered):
        full[k] = jnp.concatenate([jnp.where(my_chip == s, own, got[s]) for s in range(N_CHIPS)], axis=axes[k])

    loss_local, grad_x, grads = _local_step(x, c, ctx, loss_target, full)
    loss = lax.psum(loss_local, ("x", "y", "c"))

    small = _pack_small(grads)
    bufs = [_shard_major(grads[k], axes[k]) for k, _ in SHARDED]
    bufs.append(jnp.broadcast_to(small[:, None], (DEPTH, N_CHIPS) + small.shape[1:]))
    got = _to_sibling(bufs, True, "swap_halves")
    chip_sum = [_add_cores(b, g, f"add_cores{i}") for i, (b, g) in enumerate(zip(bufs, got))]
    recv = _scatter_to_chips(chip_sum)
    mine_red = [_sum_chips(cs, rc, f"sum_chips{i}") for i, (cs, rc) in enumerate(zip(chip_sum, recv))]
    other_red = _to_sibling(mine_red, False, "join_halves")

    outs = {}
    for i, (k, _) in enumerate(SHARDED):
        res = _adamw(view(k, wts[k]), mine_red[i], other_red[i], view(k, given['m_' + k]), view(k, given['v_' + k]),
                     f"adamw{i}")
        outs[k] = tuple(view(k, r) for r in res)
    packed = _adamw(_pack_small(wts), mine_red[-1], other_red[-1],
                    _pack_small({k: given['m_' + k] for k in REPLICATED}),
                    _pack_small({k: given['v_' + k] for k in REPLICATED}), "adamw_small")
    unpacked = [_unpack_small(p, wts) for p in packed]
    for k in REPLICATED:
        outs[k] = tuple(u[k] for u in unpacked)
    return (loss, grad_x, *[outs[k][q] for q in range(4) for k in WEIGHTS])
```

```python
import functools

import numpy as np
import jax
import jax.numpy as jnp
from jax import lax
from jax.experimental import pallas as pl
from jax.experimental.pallas import tpu as pltpu

F32 = jnp.float32
BF16 = jnp.bfloat16
HIGHEST = lax.Precision.HIGHEST
MESH = pl.DeviceIdType.MESH

D = 1024
DEPTH = 2
EPS = 1e-6
GRID_W = 64
MLA_H, MLA_NOPE, MLA_ROPE, MLA_V = 8, 64, 32, 64
MLA_QK = MLA_NOPE + MLA_ROPE
ROPE_BASE = 10000.0
POOL_WINDOWS = (2, 4, 8, 16)
GLA_H, GLA_DK, GLA_DV, GLA_RANK, GLA_TAU = 4, 64, 128, 16, 16.0
GLA_C = 128
EXP_CLAMP = 80.0
ADAM_LR, ADAM_B1, ADAM_B2, ADAM_EPS, ADAM_WD, ADAM_STEP = 0.001, 0.9, 0.999, 1e-08, 0.01, 10

TM = 256
LANES = 128
N_CHIPS = 4

IN_NAMES = ('mla_q', 'mla_kv', 'mla_kr', 'mla_gate', 'pool_x', 'pool_gate',
            'gla_q', 'gla_k', 'gla_v', 'gla_af', 'gla_ab', 'gla_gate', 'merge')
IN_SIZES = (256, 128, 32, 512, 512, 512, 256, 256, 512, 16, 16, 512, 3 * D)
NAT_OFF = dict(zip(IN_NAMES, [int(o) for o in np.cumsum((0,) + IN_SIZES[:-1])]))
NAT_SIZE = dict(zip(IN_NAMES, IN_SIZES))
PAD_ORDER = ('merge', 'mla_gate', 'mla_q', 'mla_kv', 'mla_kr', 'pool_x', 'pool_gate',
             'gla_v', 'gla_gate', 'gla_q', 'gla_k', 'gla_af', 'gla_ab')
SLAB = {n: max(NAT_SIZE[n], LANES) for n in IN_NAMES}
PAD_OFF = dict(zip(PAD_ORDER, [int(o) for o in np.cumsum([0] + [SLAB[n] for n in PAD_ORDER[:-1]])]))
D_PAD = sum(SLAB.values())
IN_SLAB = {n: 0 for n in IN_NAMES}
IN_SLAB['mla_kr'] = MLA_NOPE
MLA_HP = 128
DZ_OFF = dict(merge=0, mla_gate=3072, gla_gate=3584, mla_q=4096, mla_kv=4352, mla_kr=4480, pool=4608,
              gla_v=5632, gla_q=6144, gla_k=6400, gla_af=6656, gla_ab=6784)
DZ_PARTS = ([(n, 0, NAT_SIZE[n]) for n in ('merge', 'mla_gate', 'gla_gate', 'mla_q', 'mla_kv', 'mla_kr')]
            + [(n, g * LANES, LANES) for g in range(4) for n in ('pool_x', 'pool_gate')]
            + [(n, 0, NAT_SIZE[n]) for n in ('gla_v', 'gla_q', 'gla_k', 'gla_af', 'gla_ab')])


def _blk(name):
    return PAD_OFF[name] // SLAB[name]


SHARDED = (('mod_w', 2), ('w_in', 2), ('mla_w_uq', 2), ('mla_w_ukv', 2), ('gla_af_w2', 2), ('gla_ab_w2', 2),
           ('w_branch_mla', 2), ('w_branch_pool', 2), ('w_branch_gla', 2), ('w_out', 1))
REPLICATED = ('c_ctx', 'mod_b', 'pre_norm', 'post_norm', 'mla_q_norm', 'mla_kv_norm', 'pool_w', 'pool_scale',
              'gla_af_b', 'gla_ab_b', 'gla_norm')
WEIGHTS = ('c_ctx', 'mod_w', 'mod_b', 'pre_norm', 'post_norm', 'w_in', 'mla_q_norm', 'mla_w_uq', 'mla_kv_norm',
           'mla_w_ukv', 'pool_w', 'pool_scale', 'gla_af_w2', 'gla_af_b', 'gla_ab_w2', 'gla_ab_b', 'gla_norm',
           'w_branch_mla', 'w_branch_pool', 'w_branch_gla', 'w_out')


def _cp(vmem_mb=None, sem=None):
    kw = {}
    if vmem_mb is not None:
        kw['vmem_limit_bytes'] = vmem_mb * 1024 * 1024
    if sem is not None:
        kw['dimension_semantics'] = sem
    return pltpu.CompilerParams(**kw)


DZ_ANY = pl.BlockSpec(memory_space=pl.ANY)


def _bdot(a, b):
    return jnp.dot(a.astype(BF16), b.astype(BF16), preferred_element_type=F32)


def _bdot_nt(a, b):
    return lax.dot_general(a.astype(BF16), b.astype(BF16), (((1,), (1,)), ((), ())), preferred_element_type=F32)


def _bdot_tn(a, b):
    return lax.dot_general(a.astype(BF16), b.astype(BF16), (((0,), (0,)), ((), ())), preferred_element_type=F32)


def _xdot(a, b):
    return jnp.dot(a, b, precision=HIGHEST, preferred_element_type=F32)


def _xdot_tn(a, b):
    return lax.dot_general(a, b, (((0,), (0,)), ((), ())), precision=HIGHEST, preferred_element_type=F32)


NN = (((1,), (0,)), ((), ()))
NT = (((1,), (1,)), ((), ()))
TN = (((0,), (0,)), ((), ()))


def _split(a):
    hi = a.astype(BF16)
    return hi, (a - hi.astype(F32)).astype(BF16)


def _dot3(a, b, dims):
    (ah, al), (bh, bl) = a, b
    f = lambda u, v: lax.dot_general(u, v, dims, preferred_element_type=F32)
    return f(ah, bh) + (f(ah, bl) + f(al, bh))


def _sigmoid(x):
    return jax.nn.sigmoid(x)


def _silu(x):
    return x * _sigmoid(x)


def _dsilu(x):
    s = _sigmoid(x)
    return s * (1.0 + x * (1.0 - s))


def _rstd(x):
    return lax.rsqrt(jnp.mean(x * x, axis=-1, keepdims=True) + EPS)


def _norm_bwd(xhat, r, dy):
    return r * (dy - xhat * jnp.mean(xhat * dy, axis=-1, keepdims=True))


def _colsum(a):
    return jnp.sum(a, axis=0, keepdims=True)


def _modrow(i, tpe):
    return jnp.where(i % tpe == 0, 4, i // tpe)


def _rot(x):
    n = x.shape[-1]
    lane = lax.broadcasted_iota(jnp.int32, x.shape, x.ndim - 1)
    return jnp.where(lane % 16 < 8, -pltpu.roll(x, n - 8, x.ndim - 1), pltpu.roll(x, 8, x.ndim - 1))


def _mod_fwd(cv, mod_w, mod_b):
    def body(cv_ref, w_ref, b_ref, o_ref):
        s = _silu(cv_ref[...])
        for l in range(DEPTH):
            o_ref[l] = _bdot(s, w_ref[l]) + b_ref[l]

    return pl.pallas_call(body, name="mod_fwd", out_shape=jax.ShapeDtypeStruct((DEPTH, 8, 3 * D), F32),
                          compiler_params=_cp(40))(cv, mod_w, mod_b)


def _mod_bwd(cv, sel, st_a, st_b, w_t, dcv_in, name):
    def body(cv_ref, sel_ref, sa_ref, sb_ref, wt_ref, dcin_ref, dw_ref, db_ref, dcv_ref, dcc_ref, dpre_ref, dpost_ref):
        cvv = cv_ref[...]
        s = _silu(cvv)
        sel_v = sel_ref[...]
        dmod = jnp.concatenate([_xdot(sel_v, sa_ref[0]), _xdot(sel_v, sa_ref[1]), _xdot(sel_v, sb_ref[0])], axis=1)
        dw_ref[...] = _bdot_tn(s, dmod)
        db_ref[...] = _colsum(dmod)
        dcv = dcin_ref[...] + _bdot(dmod, wt_ref[...])
        dcv_ref[...] = dcv
        dcc_ref[...] = dcv * _dsilu(cvv)
        dpre_ref[...] = _colsum(sa_ref[2])
        dpost_ref[...] = _colsum(sb_ref[1])

    shapes = (jax.ShapeDtypeStruct((D, 3 * D), F32), jax.ShapeDtypeStruct((1, 3 * D), F32),
              jax.ShapeDtypeStruct((8, D), F32), jax.ShapeDtypeStruct((8, D), F32),
              jax.ShapeDtypeStruct((1, D), F32), jax.ShapeDtypeStruct((1, D), F32))
    return pl.pallas_call(body, name=name, out_shape=shapes, compiler_params=_cp(48))(cv, sel, st_a, st_b, w_t, dcv_in)


def _pre_fwd(x, modl, pre_g, w_t, tpe, name):
    n = x.shape[0]
    nt = n // TM
    ncb = 3
    tn = D_PAD // ncb
    tm = 2 * TM if n % (2 * TM) == 0 else TM

    def norm_body(x_ref, m_ref, g_ref, h_ref):
        xv = x_ref[...]
        m = m_ref[0]
        h_ref[...] = (xv * _rstd(xv) * g_ref[...] * (1.0 + m[:, D:2 * D]) + m[:, 0:D]).astype(BF16)

    h = pl.pallas_call(
        norm_body, name=name + "_norm", grid=(nt,),
        in_specs=[pl.BlockSpec((TM, D), lambda i: (i, 0)),
                  pl.BlockSpec((1, 1, 3 * D), lambda i: (_modrow(i, tpe), 0, 0)),
                  pl.BlockSpec((1, D), lambda i: (0, 0))],
        out_specs=pl.BlockSpec((TM, D), lambda i: (i, 0)),
        out_shape=jax.ShapeDtypeStruct((n, D), BF16),
        compiler_params=_cp(32, ("arbitrary",)),
    )(x, modl.reshape(8, 1, 3 * D), pre_g)

    def mm_body(h_ref, wt_ref, z_ref):
        z_ref[...] = lax.dot_general(h_ref[...], wt_ref[...], NT, preferred_element_type=F32)

    z = pl.pallas_call(
        mm_body, name=name, grid=(ncb, n // tm),
        in_specs=[pl.BlockSpec((tm, D), lambda j, i: (i, 0)), pl.BlockSpec((tn, D), lambda j, i: (j, 0))],
        out_specs=pl.BlockSpec((tm, tn), lambda j, i: (i, j)),
        out_shape=jax.ShapeDtypeStruct((n, D_PAD), F32),
        compiler_params=_cp(48, ("arbitrary", "arbitrary")),
    )(h, w_t)
    return z, h


def _pre_bwd(dz, w_t, x, dxres, modl, pre_g, tpe, name):
    n = x.shape[0]
    nt = n // TM

    def body(dz_ref, wt_ref, x_ref, dr_ref, m_ref, g_ref, dx_ref, st_ref):
        dh = jnp.dot(dz_ref[...], wt_ref[...], preferred_element_type=F32)
        xv = x_ref[...]
        r = _rstd(xv)
        xn = xv * r
        m = m_ref[0]
        sc1 = 1.0 + m[:, D:2 * D]
        g = g_ref[...]
        st_ref[0, 0:1, :] = _colsum(dh)
        st_ref[0, 1:2, :] = _colsum(dh * xn * g)
        st_ref[0, 2:3, :] = _colsum(dh * xn * sc1)
        st_ref[0, 3:8, :] = jnp.zeros((5, D), F32)
        dx_ref[...] = dr_ref[...] + _norm_bwd(xn, r, dh * g * sc1)

    return pl.pallas_call(
        body, name=name, grid=(nt,),
        in_specs=[pl.BlockSpec((TM, D_PAD), lambda i: (i, 0)),
                  pl.BlockSpec((D_PAD, D), lambda i: (0, 0)),
                  pl.BlockSpec((TM, D), lambda i: (i, 0)),
                  pl.BlockSpec((TM, D), lambda i: (i, 0)),
                  pl.BlockSpec((1, 1, 3 * D), lambda i: (_modrow(i, tpe), 0, 0)),
                  pl.BlockSpec((1, D), lambda i: (0, 0))],
        out_specs=[pl.BlockSpec((TM, D), lambda i: (i, 0)),
                   pl.BlockSpec((1, 8, D), lambda i: (i, 0, 0))],
        out_shape=(jax.ShapeDtypeStruct((n, D), F32), jax.ShapeDtypeStruct((nt, 8, D), F32)),
        compiler_params=_cp(56, ("arbitrary",)),
    )(dz, w_t, x, dxres, modl.reshape(8, 1, 3 * D), pre_g)


def _matmul_tn(a, b, tm, tk, name):
    n, k1 = a.shape
    k2 = b.shape[1]

    def body(a_ref, b_ref, o_ref):
        @pl.when(pl.program_id(1) == 0)
        def _():
            o_ref[...] = jnp.zeros(o_ref.shape, F32)

        o_ref[...] += lax.dot_general(a_ref[...], b_ref[...], (((0,), (0,)), ((), ())), preferred_element_type=F32)

    return pl.pallas_call(
        body, name=name, grid=(k1 // tm, n // tk),
        in_specs=[pl.BlockSpec((tk, tm), lambda i, k: (k, i)), pl.BlockSpec((tk, k2), lambda i, k: (k, 0))],
        out_specs=pl.BlockSpec((tm, k2), lambda i, k: (i, 0)),
        out_shape=jax.ShapeDtypeStruct((k1, k2), F32),
        compiler_params=_cp(48, ("arbitrary", "arbitrary")),
    )(a, b)


def _mla_pre(z, q_norm, kv_norm, w_uq, w_ukv, cq, sq, ck, sk, tpe, name):
    n = z.shape[0]
    nt = n // TM
    scale = MLA_QK ** -0.5

    def body(zq_ref, zkv_ref, zkr_ref, gq_ref, gkv_ref, wq_ref, wkv_ref, cq_ref, sq_ref, ck_ref, sk_ref,
             q_ref, kv_ref, kr_ref):
        zq = zq_ref[...]
        qn = zq * _rstd(zq) * gq_ref[...]
        qraw = _bdot(qn, wq_ref[...])
        q_ref[...] = ((qraw * cq_ref[...] + _rot(qraw) * sq_ref[...]) * scale).astype(BF16)
        zkv = zkv_ref[...]
        kvn = zkv * _rstd(zkv) * gkv_ref[...]
        kv_ref[...] = _bdot(kvn, wkv_ref[...]).astype(BF16)
        zkr = zkr_ref[...]
        kr_ref[...] = (zkr * ck_ref[...] + _rot(zkr) * sk_ref[...]).astype(BF16)

    hq, hkv = MLA_H * MLA_HP, MLA_H * (MLA_NOPE + MLA_V)
    const = lambda i: (0, 0)
    tab = lambda i: (i % tpe, 0)
    return pl.pallas_call(
        body, name=name, grid=(nt,),
        in_specs=[pl.BlockSpec((TM, 256), lambda i: (i, _blk('mla_q'))),
                  pl.BlockSpec((TM, 128), lambda i: (i, _blk('mla_kv'))),
                  pl.BlockSpec((TM, 128), lambda i: (i, _blk('mla_kr'))),
                  pl.BlockSpec((1, 256), const), pl.BlockSpec((1, 128), const),
                  pl.BlockSpec((256, hq), const), pl.BlockSpec((128, hkv), const),
                  pl.BlockSpec((TM, hq), tab), pl.BlockSpec((TM, hq), tab),
                  pl.BlockSpec((TM, 128), tab), pl.BlockSpec((TM, 128), tab)],
        out_specs=[pl.BlockSpec((TM, hq), lambda i: (i, 0)), pl.BlockSpec((TM, hkv), lambda i: (i, 0)),
                   pl.BlockSpec((TM, 128), lambda i: (i, 0))],
        out_shape=(jax.ShapeDtypeStruct((n, hq), BF16), jax.ShapeDtypeStruct((n, hkv), BF16),
                   jax.ShapeDtypeStruct((n, 128), BF16)),
        compiler_params=_cp(32, ("arbitrary",)),
    )(z, z, z, q_norm, kv_norm, w_uq, w_ukv, cq, sq, ck, sk)


def _mla_pre_bwd(z, dq, dkv, dkr, q_norm, kv_norm, w_uq_t, w_ukv_t, cq, sq, ck, sk, dz, tpe, name):
    n = z.shape[0]
    nt = n // TM
    scale = MLA_QK ** -0.5
    hq, hkv = MLA_H * MLA_HP, MLA_H * (MLA_NOPE + MLA_V)

    def body(zq_ref, zkv_ref, dq_ref, dkv_ref, dkr_ref, gq_ref, gkv_ref, wqt_ref, wkvt_ref, cq_ref, sq_ref,
             ck_ref, sk_ref, dz_in, dz_ref, dwq_ref, dwkv_ref, dgq_ref, dgkv_ref):
        @pl.when(pl.program_id(0) == 0)
        def _():
            dwq_ref[...] = jnp.zeros(dwq_ref.shape, F32)
            dwkv_ref[...] = jnp.zeros(dwkv_ref.shape, F32)
            dgq_ref[...] = jnp.zeros(dgq_ref.shape, F32)
            dgkv_ref[...] = jnp.zeros(dgkv_ref.shape, F32)

        zq = zq_ref[...]
        rq = _rstd(zq)
        qhat = zq * rq
        gq = gq_ref[...]
        dqs = dq_ref[...] * scale
        dqraw = dqs * cq_ref[...] - _rot(dqs * sq_ref[...])
        dwq_ref[...] += _bdot_tn(qhat * gq, dqraw)
        dqn = _bdot(dqraw, wqt_ref[...])
        dgq_ref[...] += _colsum(dqn * qhat)
        dz_ref[:, 0:256] = _norm_bwd(qhat, rq, dqn * gq).astype(BF16)

        zkv = zkv_ref[...]
        rkv = _rstd(zkv)
        khat = zkv * rkv
        gkv = gkv_ref[...]
        dkvv = dkv_ref[...]
        dwkv_ref[...] += _bdot_tn(khat * gkv, dkvv)
        dkvn = _bdot(dkvv, wkvt_ref[...])
        dgkv_ref[...] += _colsum(dkvn * khat)
        dz_ref[:, 256:384] = _norm_bwd(khat, rkv, dkvn * gkv).astype(BF16)

        dkr = dkr_ref[...]
        dz_ref[:, 384:512] = (dkr * ck_ref[...] - _rot(dkr * sk_ref[...])).astype(BF16)

    const = lambda i: (0, 0)
    tab = lambda i: (i % tpe, 0)
    row = lambda i: (i, 0)
    return pl.pallas_call(
        body, name=name, grid=(nt,),
        in_specs=[pl.BlockSpec((TM, 256), lambda i: (i, _blk('mla_q'))),
                  pl.BlockSpec((TM, 128), lambda i: (i, _blk('mla_kv'))),
                  pl.BlockSpec((TM, hq), row), pl.BlockSpec((TM, hkv), row), pl.BlockSpec((TM, 128), row),
                  pl.BlockSpec((1, 256), const), pl.BlockSpec((1, 128), const),
                  pl.BlockSpec((hq, 256), const), pl.BlockSpec((hkv, 128), const),
                  pl.BlockSpec((TM, hq), tab), pl.BlockSpec((TM, hq), tab),
                  pl.BlockSpec((TM, 128), tab), pl.BlockSpec((TM, 128), tab), DZ_ANY],
        out_specs=[pl.BlockSpec((TM, 512), lambda i: (i, DZ_OFF['mla_q'] // 512)),
                   pl.BlockSpec((256, hq), const), pl.BlockSpec((128, hkv), const),
                   pl.BlockSpec((1, 256), const), pl.BlockSpec((1, 128), const)],
        out_shape=(jax.ShapeDtypeStruct(dz.shape, dz.dtype), jax.ShapeDtypeStruct((256, hq), F32),
                   jax.ShapeDtypeStruct((128, hkv), F32), jax.ShapeDtypeStruct((1, 256), F32),
                   jax.ShapeDtypeStruct((1, 128), F32)),
        input_output_aliases={13: 0},
        compiler_params=_cp(32, ("arbitrary",)),
    )(z, z, dq, dkv, dkr, q_norm, kv_norm, w_uq_t, w_ukv_t, cq, sq, ck, sk, dz)


def _attn_head(q_ref, kv_ref, kr_ref, hh, nk):
    kvh = kv_ref[0:nk, hh * MLA_HP:(hh + 1) * MLA_HP]
    lane = lax.broadcasted_iota(jnp.int32, kvh.shape, 1)
    kh = jnp.where(lane < MLA_NOPE, kvh, kr_ref[0:nk, :])
    qh = q_ref[:, hh * MLA_HP:(hh + 1) * MLA_HP]
    return kvh, kh, qh, lax.dot_general(qh, kh, (((1,), (1,)), ((), ())), preferred_element_type=F32)


def _by_segment(j, lc, t, fn):
    pl.when(j == 0)(functools.partial(fn, lc))
    pl.when(j != 0)(functools.partial(fn, t))


def _attn_specs(nb, tpe, t):
    tile = lambda b, p, j: (b * tpe + j, p)
    return [pl.BlockSpec((TM, 2 * MLA_HP), tile),
            pl.BlockSpec((t, 2 * MLA_HP), lambda b, p, j: (b, p)),
            pl.BlockSpec((t, MLA_HP), lambda b, p, j: (b, 0))]


def _attn_fwd(q, kv, kr, nb, lc, name):
    n = q.shape[0]
    t = n // nb
    tpe = t // TM

    def body(q_ref, kv_ref, kr_ref, o_ref, lse_ref):
        def run(nk):
            lane = lax.broadcasted_iota(jnp.int32, (TM, MLA_HP), 1)
            res, lses = [], []
            for hh in range(2):
                kvh, _, _, s = _attn_head(q_ref, kv_ref, kr_ref, hh, nk)
                m = jnp.max(s, axis=-1, keepdims=True)
                p = jnp.exp(s - m)
                l = jnp.sum(p, axis=-1, keepdims=True)
                res.append(jnp.dot(p.astype(BF16), kvh, preferred_element_type=F32) / l)
                lses.append(m + jnp.log(l))
            o_ref[...] = jnp.where(lane < MLA_V, pltpu.roll(res[0], MLA_V, 1), res[1])
            lane2 = lax.broadcasted_iota(jnp.int32, (TM, 2), 1)
            lse_ref[0] = jnp.where(lane2 == 0, lses[0], lses[1])

        _by_segment(pl.program_id(2), lc, t, run)

    return pl.pallas_call(
        body, name=name, grid=(nb, MLA_H // 2, tpe),
        in_specs=_attn_specs(nb, tpe, t),
        out_specs=[pl.BlockSpec((TM, 2 * MLA_V), lambda b, p, j: (b * tpe + j, p)),
                   pl.BlockSpec((1, TM, 2), lambda b, p, j: (p, b * tpe + j, 0))],
        out_shape=(jax.ShapeDtypeStruct((n, MLA_H * MLA_V), F32), jax.ShapeDtypeStruct((MLA_H // 2, n, 2), F32)),
        compiler_params=_cp(48, ("arbitrary", "arbitrary", "arbitrary")),
    )(q, kv, kr)


def _attn_bwd(q, kv, kr, o, lse, do, nb, lc, name):
    n = q.shape[0]
    t = n // nb
    tpe = t // TM

    def body(q_ref, kv_ref, kr_ref, o_ref, lse_ref, do_ref, dq_ref, dkv_ref, dkr_ref):
        p_id, j = pl.program_id(1), pl.program_id(2)

        @pl.when(j == 0)
        def _():
            dkv_ref[...] = jnp.zeros(dkv_ref.shape, F32)

        @pl.when((j == 0) & (p_id == 0))
        def _():
            dkr_ref[...] = jnp.zeros(dkr_ref.shape, F32)

        def run(nk):
            lane = lax.broadcasted_iota(jnp.int32, (TM, MLA_HP), 1)
            lane_t = lax.broadcasted_iota(jnp.int32, (nk, MLA_HP), 1)
            lane2 = lax.broadcasted_iota(jnp.int32, (TM, 2), 1)
            lse = lse_ref[0]
            dov, ov = do_ref[...], o_ref[...]
            dkr = jnp.zeros((nk, MLA_HP), F32)
            for hh in range(2):
                kvh, kh, qh, s = _attn_head(q_ref, kv_ref, kr_ref, hh, nk)
                p = jnp.exp(s - jnp.sum(jnp.where(lane2 == hh, lse, 0.0), axis=1, keepdims=True))
                do_pos = jnp.where(lane >= MLA_NOPE, pltpu.roll(dov, MLA_V, 1) if hh == 0 else dov, 0.0)
                o_pos = jnp.where(lane >= MLA_NOPE, pltpu.roll(ov, MLA_V, 1) if hh == 0 else ov, 0.0)
                delta = jnp.sum(do_pos * o_pos, axis=-1, keepdims=True)
                dob = do_pos.astype(BF16)
                dp = lax.dot_general(dob, kvh, (((1,), (1,)), ((), ())), preferred_element_type=F32)
                ds = (p * (dp - delta)).astype(BF16)
                dq_ref[:, hh * MLA_HP:(hh + 1) * MLA_HP] = jnp.dot(ds, kh, preferred_element_type=F32)
                dkf = lax.dot_general(ds, qh, (((0,), (0,)), ((), ())), preferred_element_type=F32)
                dvp = lax.dot_general(p.astype(BF16), dob, (((0,), (0,)), ((), ())), preferred_element_type=F32)
                dkv_ref[0:nk, hh * MLA_HP:(hh + 1) * MLA_HP] += jnp.where(lane_t < MLA_NOPE, dkf, dvp)
                dkr = dkr + jnp.where(lane_t >= MLA_NOPE, dkf, 0.0)
            dkr_ref[0:nk, :] += dkr

        _by_segment(j, lc, t, run)

    tile = lambda b, p, j: (b * tpe + j, p)
    return pl.pallas_call(
        body, name=name, grid=(nb, MLA_H // 2, tpe),
        in_specs=_attn_specs(nb, tpe, t) + [pl.BlockSpec((TM, 2 * MLA_V), tile),
                                            pl.BlockSpec((1, TM, 2), lambda b, p, j: (p, b * tpe + j, 0)),
                                            pl.BlockSpec((TM, 2 * MLA_V), tile)],
        out_specs=[pl.BlockSpec((TM, 2 * MLA_HP), tile),
                   pl.BlockSpec((t, 2 * MLA_HP), lambda b, p, j: (b, p)),
                   pl.BlockSpec((t, MLA_HP), lambda b, p, j: (b, 0))],
        out_shape=(jax.ShapeDtypeStruct((n, MLA_H * MLA_HP), F32), jax.ShapeDtypeStruct((n, MLA_H * MLA_HP), F32),
                   jax.ShapeDtypeStruct((n, MLA_HP), F32)),
        compiler_params=_cp(56, ("arbitrary", "arbitrary", "arbitrary")),
    )(q, kv, kr, o, lse, do)


def _pool_window(ug, pos, seglen, w, transpose):
    t = ug.shape[0]
    cnt = (jnp.minimum(pos + w // 2, seglen) - jnp.maximum(pos - w // 2, 0)).astype(F32)
    if transpose:
        ug = ug / cnt
    acc = jnp.zeros_like(ug)
    for j in range(-(w // 2), w // 2):
        jj = -j if transpose else j
        src = pos + jj
        valid = (src >= 0) & (src < seglen)
        acc = acc + jnp.where(valid, pltpu.roll(ug, (-jj) % t, 0), 0.0)
    return acc if transpose else acc / cnt


def _by_group(g, fn):
    for k, w in enumerate(POOL_WINDOWS):
        pl.when(g == k)(functools.partial(fn, w))


def _pool_specs(t):
    px, pg = PAD_OFF['pool_x'] // LANES, PAD_OFF['pool_gate'] // LANES
    return [pl.BlockSpec((t, LANES), lambda g, b: (b, px + g)),
            pl.BlockSpec((t, LANES), lambda g, b: (b, pg + g)),
            pl.BlockSpec((1, LANES, LANES), lambda g, b: (g, 0, 0)),
            pl.BlockSpec((1, LANES), lambda g, b: (0, g)),
            pl.BlockSpec((t, 1), lambda g, b: (0, 0)), pl.BlockSpec((t, 1), lambda g, b: (0, 0))]


def _pool_fwd(z, pool_w, pool_scale, pos, seglen, nb, name):
    n = z.shape[0]
    t = n // nb

    def body(u_ref, zg_ref, pw_ref, ps_ref, pos_ref, sl_ref, y_ref):
        def run(w):
            u = u_ref[...]
            pooled = _pool_window(u, pos_ref[...], sl_ref[...], w, False) - u
            y_ref[...] = (_bdot(pooled, pw_ref[0]) * ps_ref[...] * _silu(zg_ref[...])).astype(BF16)

        _by_group(pl.program_id(0), run)

    return pl.pallas_call(
        body, name=name, grid=(4, nb), in_specs=_pool_specs(t),
        out_specs=pl.BlockSpec((t, LANES), lambda g, b: (b, g)),
        out_shape=jax.ShapeDtypeStruct((n, 512), BF16),
        compiler_params=_cp(48, ("arbitrary", "arbitrary")),
    )(z, z, pool_w, pool_scale, pos, seglen)


def _pool_bwd(z, dy, pool_w, pool_w_t, pool_scale, pos, seglen, dz, nb, name):
    n = z.shape[0]
    t = n // nb

    def body(u_ref, zg_ref, pw_ref, ps_ref, pos_ref, sl_ref, dy_ref, pwt_ref, dz_in, dz_ref, dpw_ref, dps_ref):
        @pl.when(pl.program_id(1) == 0)
        def _():
            dpw_ref[...] = jnp.zeros(dpw_ref.shape, F32)
            dps_ref[...] = jnp.zeros(dps_ref.shape, F32)

        def run(w):
            u = u_ref[...]
            pos_v, sl_v = pos_ref[...], sl_ref[...]
            pooled = _pool_window(u, pos_v, sl_v, w, False) - u
            mixed = _bdot(pooled, pw_ref[0])
            zg = zg_ref[...]
            sg = _silu(zg)
            ps = ps_ref[...]
            dyv = dy_ref[...]
            dps_ref[...] += _colsum(dyv * mixed * sg)
            dz_ref[:, LANES:2 * LANES] = (dyv * mixed * ps * _dsilu(zg)).astype(BF16)
            dmixed = dyv * ps * sg
            dpw_ref[0] += _bdot_tn(pooled, dmixed)
            dpooled = _bdot(dmixed, pwt_ref[0])
            dz_ref[:, 0:LANES] = (_pool_window(dpooled, pos_v, sl_v, w, True) - dpooled).astype(BF16)

        _by_group(pl.program_id(0), run)

    blk = pl.BlockSpec((t, LANES), lambda g, b: (b, g))
    return pl.pallas_call(
        body, name=name, grid=(4, nb),
        in_specs=_pool_specs(t) + [blk, pl.BlockSpec((1, LANES, LANES), lambda g, b: (g, 0, 0)), DZ_ANY],
        out_specs=[pl.BlockSpec((t, 2 * LANES), lambda g, b: (b, DZ_OFF['pool'] // (2 * LANES) + g)),
                   pl.BlockSpec((1, LANES, LANES), lambda g, b: (g, 0, 0)),
                   pl.BlockSpec((1, LANES), lambda g, b: (0, g))],
        out_shape=(jax.ShapeDtypeStruct(dz.shape, dz.dtype),
                   jax.ShapeDtypeStruct((4, 128, 128), F32), jax.ShapeDtypeStruct((1, 512), F32)),
        input_output_aliases={8: 0},
        compiler_params=_cp(48, ("arbitrary", "arbitrary")),
    )(z, z, pool_w, pool_scale, pos, seglen, dy, pool_w_t, dz)


def _gla_chunk(q_ref, k_ref, a_ref, w2_ref, b2_ref, reverse):
    c = GLA_C
    x = _bdot(a_ref[...], w2_ref[0]) + b2_ref[0]
    la = (jnp.minimum(x, 0.0) - jnp.log(1.0 + jnp.exp(-jnp.abs(x)))) * (1.0 / GLA_TAU)
    row = lax.broadcasted_iota(jnp.int32, (c, c), 0)
    col = lax.broadcasted_iota(jnp.int32, (c, c), 1)
    tri = (col >= row) if reverse else (col <= row)
    tri_t = (col <= row) if reverse else (col >= row)
    b = _xdot(tri.astype(F32), la)
    tok = lax.broadcasted_iota(jnp.int32, la.shape, 0)
    bref = _colsum(jnp.where((tok >= c // 2) if reverse else (tok < c // 2), la, 0.0))
    blast = _colsum(la)
    eq = jnp.exp(jnp.minimum(b - bref, EXP_CLAMP))
    ek = jnp.exp(jnp.minimum(bref - b, EXP_CLAMP))
    qs = q_ref[...] * (GLA_DK ** -0.5)
    kk = k_ref[...]
    eb = jnp.exp(b)
    etail = jnp.exp(blast - b)
    return dict(x=x, la=la, tri=tri, tri_t=tri_t, eq=eq, ek=ek, qs=qs, kk=kk, qd=qs * eq, kd=kk * ek, qe=qs * eb,
                kl=kk * etail, eb=eb, etail=etail)


def _pair(a, p):
    return a[:, p * LANES:(p + 1) * LANES]


def _head_masks():
    lane = lax.broadcasted_iota(jnp.int32, (GLA_C, LANES), 1)
    return (lane < GLA_DK, lane >= GLA_DK)


def _state_decay(la, p):
    return jnp.exp(_xdot_tn(_pair(la, p), jnp.ones((GLA_C, GLA_DV), F32)))


def _gla_chunk_maps(nb, nc, ncc, order):
    def rmap(j):
        return jnp.where(j < ncc, ncc - 1 - j, nc - 1 - (j - ncc))

    if order == 'scan':
        return (lambda b, j: b * nc + j), (lambda b, j: b * nc + rmap(j))
    return (lambda b, j: b * nc + nc - 1 - j), (lambda b, j: b * nc + rmap(nc - 1 - j))


def _gla_in_specs(maps):
    specs = []
    for d, cm in enumerate(maps):
        gate = 'gla_af' if d == 0 else 'gla_ab'
        specs += [pl.BlockSpec((GLA_C, 256), lambda b, j, cm=cm: (cm(b, j), _blk('gla_q'))),
                  pl.BlockSpec((GLA_C, 256), lambda b, j, cm=cm: (cm(b, j), _blk('gla_k'))),
                  pl.BlockSpec((GLA_C, 512), lambda b, j, cm=cm: (cm(b, j), _blk('gla_v'))),
                  pl.BlockSpec((GLA_C, LANES), lambda b, j, cm=cm, gate=gate: (cm(b, j), _blk(gate))),
                  pl.BlockSpec((1, LANES, 256), lambda b, j, d=d: (d, 0, 0)),
                  pl.BlockSpec((1, 1, 256), lambda b, j, d=d: (d, 0, 0))]
    return specs


def _gla_fwd(z, w2, b2, nb, lc, name):
    n = z.shape[0]
    nc = n // nb // GLA_C
    maps = _gla_chunk_maps(nb, nc, lc // GLA_C, 'scan')

    def body(*refs):
        ins, (of_ref, ob_ref, sf_ref, sb_ref, s_sc) = refs[:12], refs[12:]

        @pl.when(pl.program_id(1) == 0)
        def _():
            s_sc[...] = jnp.zeros(s_sc.shape, F32)

        masks = _head_masks()
        for d, (o_ref, st_ref) in enumerate(((of_ref, sf_ref), (ob_ref, sb_ref))):
            q_ref, k_ref, v_ref, a_ref, w2_ref, b2_ref = ins[6 * d:6 * d + 6]
            ch = _gla_chunk(q_ref, k_ref, a_ref, w2_ref, b2_ref, d == 1)
            for p in range(2):
                s_prev = s_sc[d, p]
                st_ref[0, p] = s_prev
                s_new = _state_decay(ch['la'], p) * s_prev
                kd_p = _pair(ch['kd'], p)
                for hh in range(2):
                    h = 2 * p + hh
                    vv = v_ref[:, h * GLA_DV:(h + 1) * GLA_DV]
                    att = jnp.where(ch['tri'], _bdot_nt(jnp.where(masks[hh], _pair(ch['qd'], p), 0.0), kd_p), 0.0)
                    o_ref[:, h * GLA_DV:(h + 1) * GLA_DV] = (
                        _bdot(att, vv) + _bdot(jnp.where(masks[hh], _pair(ch['qe'], p), 0.0), s_prev))
                    s_new = s_new + _dot3(_split(jnp.where(masks[hh], _pair(ch['kl'], p), 0.0)), _split(vv), TN)
                s_sc[d, p] = s_new

    o_shape = jax.ShapeDtypeStruct((n, 512), F32)
    st_shape = jax.ShapeDtypeStruct((n // GLA_C, 2, LANES, GLA_DV), F32)
    return pl.pallas_call(
        body, name=name, grid=(nb, nc),
        in_specs=_gla_in_specs(maps),
        out_specs=[pl.BlockSpec((GLA_C, 512), lambda b, j: (maps[0](b, j), 0)),
                   pl.BlockSpec((GLA_C, 512), lambda b, j: (maps[1](b, j), 0)),
                   pl.BlockSpec((1, 2, LANES, GLA_DV), lambda b, j: (maps[0](b, j), 0, 0, 0)),
                   pl.BlockSpec((1, 2, LANES, GLA_DV), lambda b, j: (maps[1](b, j), 0, 0, 0))],
        out_shape=(o_shape, o_shape, st_shape, st_shape),
        scratch_shapes=[pltpu.VMEM((2, 2, LANES, GLA_DV), F32)],
        compiler_params=_cp(32, ("arbitrary", "arbitrary")),
    )(z, z, z, z, w2, b2, z, z, z, z, w2, b2)


def _gla_bwd(z, w2, w2_t, b2, st_f, st_b, dog, nb, lc, name):
    n = z.shape[0]
    nc = n // nb // GLA_C
    maps = _gla_chunk_maps(nb, nc, lc // GLA_C, 'back')

    def body(*refs):
        ins, extra, outs, (ds_sc, sfx_sc) = refs[:12], refs[12:18], refs[18:30], refs[30:]

        @pl.when(pl.program_id(1) == 0)
        def _():
            ds_sc[...] = jnp.zeros(ds_sc.shape, F32)
            sfx_sc[...] = jnp.zeros(sfx_sc.shape, F32)

        @pl.when((pl.program_id(0) == 0) & (pl.program_id(1) == 0))
        def _():
            for r in outs[8:12]:
                r[...] = jnp.zeros(r.shape, F32)

        masks = _head_masks()
        for d in range(2):
            q_ref, k_ref, v_ref, a_ref, w2_ref, b2_ref = ins[6 * d:6 * d + 6]
            w2t_ref, st_ref, do_ref = extra[3 * d:3 * d + 3]
            dq_ref, dk_ref, dv_ref, da_ref = outs[4 * d:4 * d + 4]
            dw2_ref, db2_ref = outs[8 + 2 * d], outs[9 + 2 * d]
            ch = _gla_chunk(q_ref, k_ref, a_ref, w2_ref, b2_ref, d == 1)
            dqs, dks, dbs = [], [], []
            for p in range(2):
                s_prev = st_ref[0, p]
                ds_new = ds_sc[d, p]
                qd_p, kd_p, qe_p, kl_p = (_pair(ch[nme], p) for nme in ('qd', 'kd', 'qe', 'kl'))
                ds_prev = _state_decay(ch['la'], p) * ds_new
                qd_b, kd_b = qd_p.astype(BF16), kd_p.astype(BF16)
                sp_s, dsn_s = _split(s_prev), _split(ds_new)
                dq_h, dk_h, db_h = [], [], []
                for hh in range(2):
                    h = 2 * p + hh
                    vv = v_ref[:, h * GLA_DV:(h + 1) * GLA_DV]
                    dov = do_ref[:, h * GLA_DV:(h + 1) * GLA_DV]
                    att = jnp.where(ch['tri'], _bdot_nt(jnp.where(masks[hh], qd_p, 0.0), kd_p), 0.0)
                    dv_ref[:, h * GLA_DV:(h + 1) * GLA_DV] = (
                        _bdot_tn(att, dov) + _bdot(jnp.where(masks[hh], kl_p, 0.0), ds_new))
                    vv_s, dov_s = _split(vv), _split(dov)
                    datt_b = jnp.where(ch['tri'], lax.dot_general(dov_s[0], vv_s[0], NT, preferred_element_type=F32),
                                       0.0).astype(BF16)
                    dq_in = lax.dot_general(datt_b, kd_b, NN, preferred_element_type=F32)
                    dk_in = lax.dot_general(datt_b, qd_b, TN, preferred_element_type=F32)
                    dq_st = _dot3(dov_s, sp_s, NT) * _pair(ch['eb'], p)
                    dk_st = _dot3(vv_s, dsn_s, NT) * _pair(ch['etail'], p)
                    dq_h.append(dq_in * _pair(ch['eq'], p) + dq_st)
                    dk_h.append(dk_in * _pair(ch['ek'], p) + dk_st)
                    db_h.append((qd_b.astype(F32) * dq_in - kd_b.astype(F32) * dk_in)
                                + (_pair(ch['qs'], p) * dq_st - _pair(ch['kk'], p) * dk_st))
                    ds_prev = ds_prev + _dot3(_split(jnp.where(masks[hh], qe_p, 0.0)), dov_s, TN)
                ds_sc[d, p] = ds_prev
                dqs.append(jnp.where(masks[0], dq_h[0], dq_h[1]))
                dks.append(jnp.where(masks[0], dk_h[0], dk_h[1]))
                dbs.append(jnp.where(masks[0], db_h[0], db_h[1]))
            dq_ref[...] = jnp.concatenate(dqs, axis=1) * (GLA_DK ** -0.5)
            dk_ref[...] = jnp.concatenate(dks, axis=1)
            db = jnp.concatenate(dbs, axis=1)
            dla = _xdot(ch['tri_t'].astype(F32), db) + sfx_sc[d]
            sfx_sc[d] = sfx_sc[d] + _colsum(db)
            dx = dla * (1.0 / GLA_TAU) * _sigmoid(-ch['x'])
            da_ref[...] = _bdot(dx, w2t_ref[0])
            dw2_ref[...] += _bdot_tn(a_ref[...], dx)
            db2_ref[...] += _colsum(dx)

    extra_specs, out_specs = [], []
    for d, cm in enumerate(maps):
        extra_specs += [pl.BlockSpec((1, 256, LANES), lambda b, j, d=d: (d, 0, 0)),
                        pl.BlockSpec((1, 2, LANES, GLA_DV), lambda b, j, cm=cm: (cm(b, j), 0, 0, 0)),
                        pl.BlockSpec((GLA_C, 512), lambda b, j, cm=cm: (cm(b, j), 0))]
        out_specs += [pl.BlockSpec((GLA_C, 256), lambda b, j, cm=cm: (cm(b, j), 0)),
                      pl.BlockSpec((GLA_C, 256), lambda b, j, cm=cm: (cm(b, j), 0)),
                      pl.BlockSpec((GLA_C, 512), lambda b, j, cm=cm: (cm(b, j), 0)),
                      pl.BlockSpec((GLA_C, LANES), lambda b, j, cm=cm: (cm(b, j), 0))]
    const2 = lambda b, j: (0, 0)
    out_specs += [pl.BlockSpec((LANES, 256), const2), pl.BlockSpec((1, 256), const2)] * 2
    per_dir = (jax.ShapeDtypeStruct((n, 256), F32), jax.ShapeDtypeStruct((n, 256), F32),
               jax.ShapeDtypeStruct((n, 512), F32), jax.ShapeDtypeStruct((n, LANES), F32))
    wshape = (jax.ShapeDtypeStruct((LANES, 256), F32), jax.ShapeDtypeStruct((1, 256), F32))
    return pl.pallas_call(
        body, name=name, grid=(nb, nc),
        in_specs=_gla_in_specs(maps) + extra_specs,
        out_specs=out_specs,
        out_shape=per_dir + per_dir + wshape + wshape,
        scratch_shapes=[pltpu.VMEM((2, 2, LANES, GLA_DV), F32), pltpu.VMEM((2, 1, 256), F32)],
        compiler_params=_cp(32, ("arbitrary", "arbitrary")),
    )(z, z, z, z, w2, b2, z, z, z, z, w2, b2, w2_t, st_f, dog, w2_t, st_b, dog)


def _gla_into_dz(dz, dq_f, dq_r, dk_f, dk_r, dv_f, dv_r, da_f, da_r, name):
    n = dq_f.shape[0]
    row = lambda i: (i, 0)
    w256, w512, w128 = (pl.BlockSpec((TM, w), row) for w in (256, 512, 128))
    shp = jax.ShapeDtypeStruct(dz.shape, dz.dtype)

    def v_body(dvf, dvr, dz_in, o_ref):
        o_ref[...] = (dvf[...] + dvr[...]).astype(BF16)

    dz = pl.pallas_call(
        v_body, name=name + "_v", grid=(n // TM,), in_specs=[w512, w512, DZ_ANY],
        out_specs=pl.BlockSpec((TM, 512), lambda i: (i, DZ_OFF['gla_v'] // 512)), out_shape=shp,
        input_output_aliases={2: 0}, compiler_params=_cp(32, ("arbitrary",)))(dv_f, dv_r, dz)

    def qk_body(dqf, dqr, dkf, dkr, daf, dar, dz_in, o_ref):
        o_ref[:, 0:256] = (dqf[...] + dqr[...]).astype(BF16)
        o_ref[:, 256:512] = (dkf[...] + dkr[...]).astype(BF16)
        o_ref[:, 512:640] = daf[...].astype(BF16)
        o_ref[:, 640:768] = dar[...].astype(BF16)

    return pl.pallas_call(
        qk_body, name=name + "_qk", grid=(n // TM,), in_specs=[w256] * 4 + [w128] * 2 + [DZ_ANY],
        out_specs=pl.BlockSpec((TM, 768), lambda i: (i, DZ_OFF['gla_q'] // 768)), out_shape=shp,
        input_output_aliases={6: 0}, compiler_params=_cp(32, ("arbitrary",)))(dq_f, dq_r, dk_f, dk_r, da_f, da_r, dz)


def _gla_out_norm(og):
    hats, rs = [], []
    for h in range(GLA_H):
        seg = og[:, h * GLA_DV:(h + 1) * GLA_DV]
        r = _rstd(seg)
        hats.append(seg * r)
        rs.append(r)
    return hats, rs


def _merge_branches(zm_ref, zgm_ref, zgg_ref, om_ref, yp_ref, ogf_ref, ogb_ref, gn_ref, wbm_ref, wbp_ref, wbg_ref):
    zgm, zgg = zgm_ref[...], zgg_ref[...]
    om = om_ref[...]
    y_mla = om * _silu(zgm)
    hats, rs = _gla_out_norm(ogf_ref[...] + ogb_ref[...])
    gn = gn_ref[...]
    sgg = _silu(zgg)
    y_gla = jnp.concatenate([hats[h] * gn for h in range(GLA_H)], axis=1) * sgg
    ys = (y_mla, yp_ref[...], y_gla)
    ps = (_bdot(y_mla, wbm_ref[...]), jnp.dot(yp_ref[...], wbp_ref[...], preferred_element_type=F32),
          _bdot(y_gla, wbg_ref[...]))
    zm = zm_ref[...]
    gs = tuple(_sigmoid(zm[:, a * D:(a + 1) * D]) for a in range(3))
    merged = gs[0] * ps[0] + gs[1] * ps[1] + gs[2] * ps[2]
    return dict(zgm=zgm, zgg=zgg, om=om, hats=hats, rs=rs, gn=gn, sgg=sgg, ys=ys, ps=ps, gs=gs, merged=merged)


def _merge_in_specs(tpe):
    row = lambda i: (i, 0)
    const = lambda i: (0, 0)
    return [pl.BlockSpec((TM, 3 * D), lambda i: (i, _blk('merge'))),
            pl.BlockSpec((TM, 512), lambda i: (i, _blk('mla_gate'))),
            pl.BlockSpec((TM, 512), lambda i: (i, _blk('gla_gate'))),
            pl.BlockSpec((TM, 512), row), pl.BlockSpec((TM, 512), row), pl.BlockSpec((TM, 512), row),
            pl.BlockSpec((TM, 512), row), pl.BlockSpec((1, 128), const),
            pl.BlockSpec((512, D), const), pl.BlockSpec((512, D), const), pl.BlockSpec((512, D), const),
            pl.BlockSpec((1, 1, 3 * D), lambda i: (_modrow(i, tpe), 0, 0)), pl.BlockSpec((1, D), const)]


def _merge_fwd(x, z, o_mla, y_pool, ogf, ogb, gla_n, wbm, wbp, wbg, wout, modl, post_g, tpe, name):
    n = x.shape[0]

    def body(zm_ref, zgm_ref, zgg_ref, om_ref, yp_ref, ogf_ref, ogb_ref, gn_ref, wbm_ref, wbp_ref, wbg_ref,
             m_ref, pg_ref, x_ref, wo_ref, xn_ref, out_ref):
        mb = _merge_branches(zm_ref, zgm_ref, zgg_ref, om_ref, yp_ref, ogf_ref, ogb_ref, gn_ref,
                             wbm_ref, wbp_ref, wbg_ref)
        out = _bdot(mb['merged'], wo_ref[...])
        gate = m_ref[0][:, 2 * D:3 * D]
        xn_ref[...] = x_ref[...] + gate * (out * _rstd(out) * pg_ref[...])
        out_ref[...] = out

    row = lambda i: (i, 0)
    return pl.pallas_call(
        body, name=name, grid=(n // TM,),
        in_specs=_merge_in_specs(tpe) + [pl.BlockSpec((TM, D), row), pl.BlockSpec((D, D), lambda i: (0, 0))],
        out_specs=[pl.BlockSpec((TM, D), row), pl.BlockSpec((TM, D), row)],
        out_shape=(jax.ShapeDtypeStruct((n, D), F32), jax.ShapeDtypeStruct((n, D), F32)),
        compiler_params=_cp(48, ("arbitrary",)),
    )(z, z, z, o_mla, y_pool, ogf, ogb, gla_n, wbm, wbp, wbg, modl.reshape(8, 1, 3 * D), post_g, x, wout)


def _merge_bwd(dxn, out, z, o_mla, y_pool, ogf, ogb, gla_n, wbm, wbp, wbg, wbm_t, wbp_t, wbg_t, wout_t,
               modl, post_g, tpe, name):
    n = out.shape[0]
    nt = n // TM

    def body(zm_ref, zgm_ref, zgg_ref, om_ref, yp_ref, ogf_ref, ogb_ref, gn_ref, wbm_ref, wbp_ref, wbg_ref,
             m_ref, pg_ref, dxn_ref, out_ref, wbmt_ref, wbpt_ref, wbgt_ref, wot_ref,
             dz_ref, dom_ref, dyp_ref, dog_ref, st_ref,
             dwbm_ref, dwbp_ref, dwbg_ref, dwo_ref, dgn_ref):
        @pl.when(pl.program_id(0) == 0)
        def _():
            for r in (dwbm_ref, dwbp_ref, dwbg_ref, dwo_ref, dgn_ref):
                r[...] = jnp.zeros(r.shape, F32)

        mb = _merge_branches(zm_ref, zgm_ref, zgg_ref, om_ref, yp_ref, ogf_ref, ogb_ref, gn_ref,
                             wbm_ref, wbp_ref, wbg_ref)
        out = out_ref[...]
        r2 = _rstd(out)
        on = out * r2
        pg = pg_ref[...]
        gate = m_ref[0][:, 2 * D:3 * D]
        dxn_v = dxn_ref[...]
        st_ref[0, 0:1, :] = _colsum(dxn_v * on * pg)
        st_ref[0, 1:2, :] = _colsum(dxn_v * gate * on)
        st_ref[0, 2:8, :] = jnp.zeros((6, D), F32)
        dout = _norm_bwd(on, r2, dxn_v * gate * pg)
        dwo_ref[...] += _bdot_tn(mb['merged'], dout)
        dmerged = _bdot(dout, wot_ref[...])
        dys = []
        for a, (dw_ref, wt_ref) in enumerate(((dwbm_ref, wbmt_ref), (dwbp_ref, wbpt_ref), (dwbg_ref, wbgt_ref))):
            g = mb['gs'][a]
            dz_ref[:, a * D:(a + 1) * D] = (dmerged * mb['ps'][a] * g * (1.0 - g)).astype(BF16)
            dp = dmerged * g
            dw_ref[...] += _bdot_tn(mb['ys'][a], dp)
            dys.append(_bdot(dp, wt_ref[...]))
        dom_ref[...] = dys[0] * _silu(mb['zgm'])
        dz_ref[:, 3 * D:3 * D + 512] = (dys[0] * mb['om'] * _dsilu(mb['zgm'])).astype(BF16)
        dyp_ref[...] = dys[1]
        gn = mb['gn']
        dgn = jnp.zeros((1, GLA_DV), F32)
        dzgg, dog = [], []
        for h in range(GLA_H):
            sl = slice(h * GLA_DV, (h + 1) * GLA_DV)
            dyg = dys[2][:, sl]
            hat = mb['hats'][h]
            dzgg.append(dyg * hat * gn * _dsilu(mb['zgg'][:, sl]))
            dn = dyg * mb['sgg'][:, sl]
            dgn = dgn + _colsum(dn * hat)
            dog.append(_norm_bwd(hat, mb['rs'][h], dn * gn))
        dgn_ref[...] += dgn
        dz_ref[:, 3 * D + 512:4 * D] = jnp.concatenate(dzgg, axis=1).astype(BF16)
        dog_ref[...] = jnp.concatenate(dog, axis=1)

    row = lambda i: (i, 0)
    const = lambda i: (0, 0)
    wspec = pl.BlockSpec((512, D), const)
    wtspec = pl.BlockSpec((D, 512), const)
    return pl.pallas_call(
        body, name=name, grid=(nt,),
        in_specs=_merge_in_specs(tpe) + [pl.BlockSpec((TM, D), row), pl.BlockSpec((TM, D), row),
                                         wtspec, wtspec, wtspec,
                                         pl.BlockSpec((D, D), const)],
        out_specs=[pl.BlockSpec((TM, 4 * D), row), pl.BlockSpec((TM, 512), row),
                   pl.BlockSpec((TM, 512), row), pl.BlockSpec((TM, 512), row),
                   pl.BlockSpec((1, 8, D), lambda i: (i, 0, 0)),
                   wspec, wspec, wspec, pl.BlockSpec((D, D), const), pl.BlockSpec((1, 128), const)],
        out_shape=(jax.ShapeDtypeStruct((n, D_PAD), BF16), jax.ShapeDtypeStruct((n, 512), F32),
                   jax.ShapeDtypeStruct((n, 512), F32), jax.ShapeDtypeStruct((n, 512), F32),
                   jax.ShapeDtypeStruct((nt, 8, D), F32),
                   jax.ShapeDtypeStruct((512, D), F32), jax.ShapeDtypeStruct((512, D), F32),
                   jax.ShapeDtypeStruct((512, D), F32), jax.ShapeDtypeStruct((D, D), F32),
                   jax.ShapeDtypeStruct((1, 128), F32)),
        compiler_params=_cp(56, ("arbitrary",)),
    )(z, z, z, o_mla, y_pool, ogf, ogb, gla_n, wbm, wbp, wbg, modl.reshape(8, 1, 3 * D), post_g,
      dxn, out, wbm_t, wbp_t, wbg_t, wout_t)


def _loss_grad(xf, tgt, nb, tpe):
    n = xf.shape[0]

    def body(x_ref, t_ref, dx_ref, l_ref):
        j = pl.program_id(1)
        d = x_ref[...] - t_ref[...]
        latent = j > 0
        dx_ref[...] = jnp.where(latent, d * (1.0 / D), 0.0)
        l_ref[...] = jnp.full(l_ref.shape, jnp.where(latent, 0.5 / D * jnp.sum(d * d), 0.0), F32)

    return pl.pallas_call(
        body, name="loss_grad", grid=(nb, tpe),
        in_specs=[pl.BlockSpec((TM, D), lambda b, j: (b * tpe + j, 0)),
                  pl.BlockSpec((TM, D), lambda b, j: (b * (tpe - 1) + jnp.maximum(j - 1, 0), 0))],
        out_specs=[pl.BlockSpec((TM, D), lambda b, j: (b * tpe + j, 0)),
                   pl.BlockSpec((1, 8, 128), lambda b, j: (b * tpe + j, 0, 0))],
        out_shape=(jax.ShapeDtypeStruct((n, D), F32), jax.ShapeDtypeStruct((n // TM, 8, 128), F32)),
        compiler_params=_cp(32, ("arbitrary", "arbitrary")),
    )(xf, tgt)


def _to_padded(w_nat):
    parts = []
    for nme in PAD_ORDER:
        p = w_nat[NAT_OFF[nme]:NAT_OFF[nme] + NAT_SIZE[nme]]
        if SLAB[nme] > NAT_SIZE[nme]:
            p = jnp.pad(p, [(IN_SLAB[nme], SLAB[nme] - NAT_SIZE[nme] - IN_SLAB[nme]), (0, 0)])
        parts.append(p)
    return jnp.concatenate(parts, axis=0)


def _to_dz(w_nat):
    parts = []
    for nme, off, size in DZ_PARTS:
        p = w_nat[NAT_OFF[nme] + off:NAT_OFF[nme] + off + size]
        if size < LANES:
            p = jnp.pad(p, [(IN_SLAB[nme], LANES - size - IN_SLAB[nme]), (0, 0)])
        parts.append(p)
    return jnp.concatenate(parts, axis=0)


def _from_dz(dw):
    found, pos = {}, 0
    for nme, off, size in DZ_PARTS:
        start = pos + (IN_SLAB[nme] if size < LANES else 0)
        found.setdefault(nme, []).append(dw[start:start + size])
        pos += max(size, LANES)
    return jnp.concatenate([p for nme in IN_NAMES for p in found[nme]], axis=0)


def _rope_tables(lc, l):
    half = MLA_ROPE // 2
    inv = ROPE_BASE ** (-jnp.arange(0, half, 2, dtype=F32) / half)
    tok = jnp.arange(l)
    ang_r = (tok // GRID_W).astype(F32)[:, None] * inv
    ang_c = (tok % GRID_W).astype(F32)[:, None] * inv
    ang = jnp.concatenate([ang_r, ang_r, ang_c, ang_c], axis=-1)
    cos = jnp.concatenate([jnp.ones((lc, MLA_ROPE), F32), jnp.cos(ang)], axis=0)
    sin = jnp.concatenate([jnp.zeros((lc, MLA_ROPE), F32), jnp.sin(ang)], axis=0)
    t = lc + l
    tail = MLA_HP - MLA_QK
    ck = jnp.concatenate([jnp.ones((t, MLA_NOPE), F32), cos, jnp.ones((t, tail), F32)], axis=1)
    sk = jnp.concatenate([jnp.zeros((t, MLA_NOPE), F32), sin, jnp.zeros((t, tail), F32)], axis=1)
    return jnp.tile(ck, (1, MLA_H)), jnp.tile(sk, (1, MLA_H)), ck, sk


def _pad_heads(w):
    lead = w.shape[:-1]
    w = w.reshape(lead + (MLA_H, MLA_QK))
    return jnp.pad(w, [(0, 0)] * len(lead) + [(0, 0), (0, MLA_HP - MLA_QK)]).reshape(lead + (MLA_H * MLA_HP,))


def _unpad_heads(w):
    lead = w.shape[:-1]
    return w.reshape(lead + (MLA_H, MLA_HP))[..., :MLA_QK].reshape(lead + (MLA_H * MLA_QK,))


def _local_step(x, c, ctx, tgt, wf):
    nb, l, _ = x.shape
    lc = ctx.shape[1]
    assert lc == TM and l % TM == 0
    t = lc + l
    tpe = t // TM
    n = nb * t
    nt = n // TM
    bf = lambda a: a.astype(BF16)

    xs = jnp.concatenate([ctx, x], axis=1).reshape(n, D)
    assert nb <= 4
    cv = jnp.concatenate([c, jnp.zeros((4 - nb, D), F32), wf['c_ctx'][None, :], jnp.zeros((3, D), F32)], axis=0)
    mod_w_b = bf(wf['mod_w'])
    mod_all = _mod_fwd(cv, mod_w_b, wf['mod_b'].reshape(DEPTH, 1, 3 * D))
    cq, sq, ck, sk = _rope_tables(lc, l)
    pos = jnp.concatenate([jnp.arange(lc), jnp.arange(l)]).astype(jnp.int32)[:, None]
    seglen = jnp.concatenate([jnp.full((lc,), lc), jnp.full((l,), l)]).astype(jnp.int32)[:, None]
    tiles = np.arange(nt)
    ntp = -(-nt // LANES) * LANES
    sel = np.zeros((8, ntp), np.float32)
    sel[np.where(tiles % tpe == 0, 4, tiles // tpe), tiles] = 1.0
    sel = jnp.asarray(sel)

    def tile_sums(st):
        return jnp.pad(st.transpose(1, 0, 2), ((0, 0), (0, ntp - nt), (0, 0)))

    lw = []
    for ly in range(DEPTH):
        w_in_t = _to_padded(bf(wf['w_in'][ly]))
        w_in_dz = _to_dz(bf(wf['w_in'][ly]))
        w_uq_p = _pad_heads(bf(wf['mla_w_uq'][ly]))
        w2 = jnp.pad(jnp.stack([bf(wf['gla_af_w2'][ly]), bf(wf['gla_ab_w2'][ly])]),
                     ((0, 0), (0, LANES - GLA_RANK), (0, 0)))
        lw.append(dict(
            w_in_t=w_in_t, w_in_dz=w_in_dz,
            w_uq=w_uq_p, w_uq_t=w_uq_p.T,
            w_ukv=bf(wf['mla_w_ukv'][ly]), w_ukv_t=bf(wf['mla_w_ukv'][ly]).T,
            pool_w=bf(wf['pool_w'][ly]), pool_w_t=bf(wf['pool_w'][ly]).transpose(0, 2, 1),
            w2=w2, w2_t=w2.transpose(0, 2, 1),
            b2=jnp.stack([wf['gla_af_b'][ly], wf['gla_ab_b'][ly]]).reshape(2, 1, GLA_H * GLA_DK),
            wbm=bf(wf['w_branch_mla'][ly]), wbp=bf(wf['w_branch_pool'][ly]), wbg=bf(wf['w_branch_gla'][ly]),
            wout=bf(wf['w_out'][ly]),
            wbm_t=bf(wf['w_branch_mla'][ly]).T, wbp_t=bf(wf['w_branch_pool'][ly]).T,
            wbg_t=bf(wf['w_branch_gla'][ly]).T, wout_t=bf(wf['w_out'][ly]).T,
            pre_g=wf['pre_norm'][ly][None, :], post_g=wf['post_norm'][ly][None, :],
            q_norm=wf['mla_q_norm'][ly][None, :], kv_norm=wf['mla_kv_norm'][ly][None, :],
            pool_scale=wf['pool_scale'][ly][None, :], gla_norm=wf['gla_norm'][ly][None, :]))

    saved = []
    xcur = xs
    for ly in range(DEPTH):
        w = lw[ly]
        z, h = _pre_fwd(xcur, mod_all[ly], w['pre_g'], w['w_in_t'], tpe, f"pre_fwd{ly}")
        qb, kvb, krb = _mla_pre(z, w['q_norm'], w['kv_norm'], w['w_uq'], w['w_ukv'], cq, sq, ck, sk, tpe, f"mla_pre{ly}")
        o_mla, lse = _attn_fwd(qb, kvb, krb, nb, lc, f"attn_fwd{ly}")
        y_pool = _pool_fwd(z, w['pool_w'], w['pool_scale'], pos, seglen, nb, f"pool_fwd{ly}")
        ogf, ogb, st_f, st_r = _gla_fwd(z, w['w2'], w['b2'], nb, lc, f"gla_fwd{ly}")
        xnew, out = _merge_fwd(xcur, z, o_mla, y_pool, ogf, ogb, w['gla_norm'], w['wbm'], w['wbp'], w['wbg'],
                               w['wout'], mod_all[ly], w['post_g'], tpe, f"merge_fwd{ly}")
        saved.append(dict(x=xcur, z=z, h=h, qb=qb, kvb=kvb, krb=krb, lse=lse, o_mla=o_mla, y_pool=y_pool,
                          st_f=st_f, st_r=st_r, ogf=ogf, ogb=ogb, out=out))
        xcur = xnew

    dxcur, lparts = _loss_grad(xcur, tgt.reshape(nb * l, D), nb, tpe)
    loss = jnp.sum(lparts[:, 0, 0])

    g = {k: [None] * DEPTH for k in WEIGHTS if k != 'c_ctx'}
    dcv = jnp.zeros((8, D), F32)
    dcc = None
    for ly in reversed(range(DEPTH)):
        w, s = lw[ly], saved[ly]
        (dz, dom, dyp, dog, st_b, g['w_branch_mla'][ly], g['w_branch_pool'][ly], g['w_branch_gla'][ly],
         g['w_out'][ly], dgn) = _merge_bwd(
            dxcur, s['out'], s['z'], s['o_mla'], s['y_pool'], s['ogf'], s['ogb'], w['gla_norm'], w['wbm'], w['wbp'],
            w['wbg'], w['wbm_t'], w['wbp_t'], w['wbg_t'], w['wout_t'], mod_all[ly], w['post_g'], tpe,
            f"merge_bwd{ly}")
        g['gla_norm'][ly] = dgn[0]
        dq, dkv, dkr = _attn_bwd(s['qb'], s['kvb'], s['krb'], s['o_mla'], s['lse'], dom, nb, lc, f"attn_bwd{ly}")
        dz, dwq, g['mla_w_ukv'][ly], dgq, dgkv = _mla_pre_bwd(
            s['z'], dq, dkv, dkr, w['q_norm'], w['kv_norm'], w['w_uq_t'], w['w_ukv_t'], cq, sq, ck, sk, dz, tpe,
            f"mla_pre_bwd{ly}")
        g['mla_w_uq'][ly] = _unpad_heads(dwq)
        g['mla_q_norm'][ly], g['mla_kv_norm'][ly] = dgq[0], dgkv[0]
        dz, g['pool_w'][ly], dps = _pool_bwd(s['z'], dyp, w['pool_w'], w['pool_w_t'], w['pool_scale'],
                                             pos, seglen, dz, nb, f"pool_bwd{ly}")
        g['pool_scale'][ly] = dps[0]
        (dq_f, dk_f, dv_f, da_f, dq_r, dk_r, dv_r, da_r, dw2_f, db2_f, dw2_r, db2_r) = _gla_bwd(
            s['z'], w['w2'], w['w2_t'], w['b2'], s['st_f'], s['st_r'], dog, nb, lc, f"gla_bwd{ly}")
        dz = _gla_into_dz(dz, dq_f, dq_r, dk_f, dk_r, dv_f, dv_r, da_f, da_r, f"gla_dz{ly}")
        g['gla_af_w2'][ly], g['gla_ab_w2'][ly] = dw2_f[:GLA_RANK], dw2_r[:GLA_RANK]
        g['gla_af_b'][ly], g['gla_ab_b'][ly] = db2_f[0], db2_r[0]
        dxcur, st_a = _pre_bwd(dz, w['w_in_dz'], s['x'], dxcur, mod_all[ly], w['pre_g'], tpe, f"pre_bwd{ly}")
        tk = next(k for k in (3072, 1024, 512, TM) if n % k == 0)
        g['w_in'][ly] = _from_dz(_matmul_tn(dz, s['h'], 768, tk, f"w_in_grad{ly}"))
        dmw, dmb, dcv, dcc, dpre, dpost = _mod_bwd(cv, sel, tile_sums(st_a), tile_sums(st_b),
                                                   mod_w_b[ly].T, dcv, f"mod_bwd{ly}")
        g['mod_w'][ly], g['mod_b'][ly] = dmw, dmb[0]
        g['pre_norm'][ly], g['post_norm'][ly] = dpre[0], dpost[0]

    grads = {k: jnp.stack(v) for k, v in g.items()}
    grads['c_ctx'] = dcc[4]
    grad_x = dxcur.reshape(nb, t, D)[:, lc:, :]
    return loss, grad_x, grads


def _place():
    x, y, c = lax.axis_index("x"), lax.axis_index("y"), lax.axis_index("c")
    chips = [(1 - x, y), (x, 1 - y), (1 - x, 1 - y)]
    return x, y, c, chips


def _hbm_call(body, name, out_shape, n_in, sems):
    any_spec = pl.BlockSpec(memory_space=pl.ANY)
    return pl.pallas_call(body, name=name, out_shape=out_shape, in_specs=[any_spec] * n_in,
                          out_specs=jax.tree.map(lambda _: any_spec, out_shape), scratch_shapes=sems)


def _all_gather_shards(ws):
    n = len(ws)

    def body(*refs):
        ins, outs, (send_sems, recv_sems) = refs[:n], refs[n:2 * n], refs[2 * n:]
        x, y, c, chips = _place()

        def copy(k, q, chip, half, to, src=None):
            dst = outs[k].at[2 * chip[0] + chip[1], half]
            return pltpu.make_async_remote_copy(src_ref=dst if src is None else src, dst_ref=dst,
                                                send_sem=send_sems.at[k, q], recv_sem=recv_sems.at[k, q],
                                                device_id=to, device_id_type=MESH)

        first = [copy(k, j, (x, y), c, (*chip, c), src=ins[k].at[c]) for k in range(n) for j, chip in enumerate(chips)]
        for cp in first:
            cp.start()
        passed = []
        for k in range(n):
            for j, chip in enumerate(chips):
                copy(k, j, chip, c, (x, y, c)).wait_recv()
                passed.append(copy(k, 3 + j, chip, c, (x, y, 1 - c)))
                passed[-1].start()
        for k in range(n):
            for j, chip in enumerate(chips):
                copy(k, 3 + j, chip, 1 - c, (x, y, 1 - c)).wait_recv()
        for cp in first + passed:
            cp.wait_send()

    shapes = tuple(jax.ShapeDtypeStruct((N_CHIPS,) + w.shape, w.dtype) for w in ws)
    return _hbm_call(body, "all_gather_shards", shapes, n,
                     [pltpu.SemaphoreType.DMA((n, 6)), pltpu.SemaphoreType.DMA((n, 6))])(*ws)


def _to_sibling(arrs, other_layer, name):
    n = len(arrs)

    def body(*refs):
        ins, outs, (send_sems, recv_sems) = refs[:n], refs[n:2 * n], refs[2 * n:]
        x, y, c, _ = _place()
        cps = [pltpu.make_async_remote_copy(src_ref=ins[k].at[1 - c] if other_layer else ins[k], dst_ref=outs[k],
                                            send_sem=send_sems.at[k], recv_sem=recv_sems.at[k],
                                            device_id=(x, y, 1 - c), device_id_type=MESH) for k in range(n)]
        for cp in cps:
            cp.start()
        for cp in cps:
            cp.wait()

    shapes = tuple(jax.ShapeDtypeStruct(a.shape[1:] if other_layer else a.shape, a.dtype) for a in arrs)
    return _hbm_call(body, name, shapes, n, [pltpu.SemaphoreType.DMA((n,)), pltpu.SemaphoreType.DMA((n,))])(*arrs)


def _scatter_to_chips(hs):
    n = len(hs)

    def body(*refs):
        ins, outs, (send_sems, recv_sems) = refs[:n], refs[n:2 * n], refs[2 * n:]
        x, y, c, chips = _place()
        me = 2 * x + y
        sends = []
        for k in range(n):
            for j, chip in enumerate(chips):
                cp = pltpu.make_async_remote_copy(src_ref=ins[k].at[2 * chip[0] + chip[1]], dst_ref=outs[k].at[me],
                                                  send_sem=send_sems.at[k, j], recv_sem=recv_sems.at[k, j],
                                                  device_id=(*chip, c), device_id_type=MESH)
                cp.start()
                sends.append(cp)
        for k in range(n):
            for j, chip in enumerate(chips):
                dst = outs[k].at[2 * chip[0] + chip[1]]
                pltpu.make_async_remote_copy(src_ref=dst, dst_ref=dst, send_sem=send_sems.at[k, j],
                                             recv_sem=recv_sems.at[k, j], device_id=(*chip, c),
                                             device_id_type=MESH).wait_recv()
        for cp in sends:
            cp.wait_send()

    shapes = tuple(jax.ShapeDtypeStruct(h.shape, h.dtype) for h in hs)
    return _hbm_call(body, "scatter_to_chips", shapes, n,
                     [pltpu.SemaphoreType.DMA((n, 3)), pltpu.SemaphoreType.DMA((n, 3))])(*hs)


BLOCK_BYTES = 10 * 1024 * 1024


def _blocks(r, cols, pos_bytes):
    rows = sorted({d for d in range(8, r + 1, 8) if r % d == 0} | {r})
    wide = sorted({d for d in range(LANES, cols + 1, LANES) if cols % d == 0} | {cols})
    fits = [(br * bc, bc, br) for br in rows for bc in wide if br * bc * pos_bytes <= BLOCK_BYTES]
    if not fits:
        return rows[0], wide[0]
    _, bc, br = max(fits)
    return br, bc


def _add_cores(b, got, name):
    _, ns, r, cols = b.shape
    br, bc = _blocks(r, cols, 2 * 4 + 4 + 2)

    def body(b_ref, g_ref, o_ref):
        mine = jnp.where(lax.axis_index("c") == 0, b_ref[0, 0], b_ref[1, 0])
        o_ref[0] = (mine + g_ref[0]).astype(BF16)

    spec = pl.BlockSpec((1, br, bc), lambda i, j, k: (i, j, k))
    return pl.pallas_call(body, name=name, grid=(ns, r // br, cols // bc),
                          in_specs=[pl.BlockSpec((2, 1, br, bc), lambda i, j, k: (0, i, j, k)), spec], out_specs=spec,
                          out_shape=jax.ShapeDtypeStruct((ns, r, cols), BF16), compiler_params=_cp(40))(b, got)


def _sum_chips(own, got, name):
    _, r, cols = own.shape
    br, bc = _blocks(r, cols, 2 * N_CHIPS * 2 + 4)

    def body(own_ref, got_ref, o_ref):
        me = 2 * lax.axis_index("x") + lax.axis_index("y")
        part = [jnp.where(me == j, own_ref[j], got_ref[j]).astype(F32) for j in range(N_CHIPS)]
        o_ref[...] = ((part[0] + part[1]) + part[2]) + part[3]

    spec = pl.BlockSpec((N_CHIPS, br, bc), lambda j, k: (0, j, k))
    return pl.pallas_call(body, name=name, grid=(r // br, cols // bc), in_specs=[spec, spec],
                          out_specs=pl.BlockSpec((br, bc), lambda j, k: (j, k)),
                          out_shape=jax.ShapeDtypeStruct((r, cols), F32), compiler_params=_cp(40))(own, got)


def _adamw(w, g_mine, g_other, m, v, name):
    _, r, cols = w.shape
    br, bc = _blocks(r, cols, 9 * 4)

    def body(w_ref, gm_ref, go_ref, m_ref, v_ref, g_ref, d_ref, nm_ref, nv_ref):
        gv = jnp.where(pl.program_id(0) == lax.axis_index("c"), gm_ref[...], go_ref[...])
        m2 = ADAM_B1 * m_ref[0] + (1.0 - ADAM_B1) * gv
        v2 = ADAM_B2 * v_ref[0] + (1.0 - ADAM_B2) * jnp.square(gv)
        m_hat = m2 / (1.0 - ADAM_B1 ** ADAM_STEP)
        v_hat = v2 / (1.0 - ADAM_B2 ** ADAM_STEP)
        g_ref[0] = gv
        d_ref[0] = -ADAM_LR * (m_hat / (jnp.sqrt(v_hat) + ADAM_EPS) + ADAM_WD * w_ref[0])
        nm_ref[0] = m2
        nv_ref[0] = v2

    lay = pl.BlockSpec((1, br, bc), lambda l, j, k: (l, j, k))
    flat = pl.BlockSpec((br, bc), lambda l, j, k: (j, k))
    shp = jax.ShapeDtypeStruct(w.shape, F32)
    return pl.pallas_call(body, name=name, grid=(2, r // br, cols // bc), in_specs=[lay, flat, flat, lay, lay],
                          out_specs=[lay] * 4, out_shape=(shp,) * 4, compiler_params=_cp(40))(w, g_mine, g_other, m, v)


def _pack_small(ts):
    flat = jnp.concatenate([ts[k].reshape(DEPTH, -1) for k in REPLICATED], axis=1)
    return flat.reshape(DEPTH, flat.shape[1] // LANES, LANES)


def _unpack_small(packed, like):
    flat = packed.reshape(DEPTH, -1)
    out, off = {}, 0
    for k in REPLICATED:
        size = like[k].size // DEPTH
        out[k] = flat[:, off:off + size].reshape(like[k].shape)
        off += size
    return out


def _shard_major(a, axis):
    if axis == 1:
        return a.reshape(DEPTH, N_CHIPS, a.shape[1] // N_CHIPS, a.shape[2])
    return a.reshape(DEPTH, a.shape[1], N_CHIPS, a.shape[2] // N_CHIPS).transpose(0, 2, 1, 3)


def kernel(x, c, ctx, c_ctx, mod_w, mod_b, pre_norm, post_norm, w_in, mla_q_norm, mla_w_uq, mla_kv_norm, mla_w_ukv, pool_w, pool_scale, gla_af_w2, gla_af_b, gla_ab_w2, gla_ab_b, gla_norm, w_branch_mla, w_branch_pool, w_branch_gla, w_out, loss_target, m_c_ctx, m_mod_w, m_mod_b, m_pre_norm, m_post_norm, m_w_in, m_mla_q_norm, m_mla_w_uq, m_mla_kv_norm, m_mla_w_ukv, m_pool_w, m_pool_scale, m_gla_af_w2, m_gla_af_b, m_gla_ab_w2, m_gla_ab_b, m_gla_norm, m_w_branch_mla, m_w_branch_pool, m_w_branch_gla, m_w_out, v_c_ctx, v_mod_w, v_mod_b, v_pre_norm, v_post_norm, v_w_in, v_mla_q_norm, v_mla_w_uq, v_mla_kv_norm, v_mla_w_ukv, v_pool_w, v_pool_scale, v_gla_af_w2, v_gla_af_b, v_gla_ab_w2, v_gla_ab_b, v_gla_norm, v_w_branch_mla, v_w_branch_pool, v_w_branch_gla, v_w_out):
    given = dict(locals())
    wts = {k: given[k] for k in WEIGHTS}
    my_chip = 2 * lax.axis_index("x") + lax.axis_index("y")

    view = lambda k, a: jnp.swapaxes(a, 1, 2) if k == 'w_in' else a
    axes = {k: (3 - axis if k == 'w_in' else axis) for k, axis in SHARDED}

    mine = [view(k, wts[k]).astype(BF16) for k, _ in SHARDED]
    gathered = _all_gather_shards(mine)
    full = dict(wts)
    for (k, _), own, got in zip(SHARDED, mine, gathered):
        full[k] = jnp.concatenate([jnp.where(my_chip == s, own, got[s]) for s in range(N_CHIPS)], axis=axes[k])

    loss_local, grad_x, grads = _local_step(x, c, ctx, loss_target, full)
    loss = lax.psum(loss_local, ("x", "y", "c"))

    small = _pack_small(grads)
    bufs = [_shard_major(grads[k], axes[k]) for k, _ in SHARDED]
    bufs.append(jnp.broadcast_to(small[:, None], (DEPTH, N_CHIPS) + small.shape[1:]))
    got = _to_sibling(bufs, True, "swap_halves")
    chip_sum = [_add_cores(b, g, f"add_cores{i}") for i, (b, g) in enumerate(zip(bufs, got))]
    recv = _scatter_to_chips(chip_sum)
    mine_red = [_sum_chips(cs, rc, f"sum_chips{i}") for i, (cs, rc) in enumerate(zip(chip_sum, recv))]
    other_red = _to_sibling(mine_red, False, "join_halves")

    outs = {}
    for i, (k, _) in enumerate(SHARDED):
        res = _adamw(view(k, wts[k]), mine_red[i], other_red[i], view(k, given['m_' + k]), view(k, given['v_' + k]),
                     f"adamw{i}")
        outs[k] = tuple(view(k, r) for r in res)
    packed = _adamw(_pack_small(wts), mine_red[-1], other_red[-1],
                    _pack_small({k: given['m_' + k] for k in REPLICATED}),
                    _pack_small({k: given['v_' + k] for k in REPLICATED}), "adamw_small")
    unpacked = [_unpack_small(p, wts) for p in packed]
    for k in REPLICATED:
        outs[k] = tuple(u[k] for u in unpacked)
    return (loss, grad_x, *[outs[k][q] for q in range(4) for k in WEIGHTS])
```

```python
import functools

import numpy as np
import jax
import jax.numpy as jnp
from jax import lax
from jax.experimental import pallas as pl
from jax.experimental.pallas import tpu as pltpu

F32 = jnp.float32
BF16 = jnp.bfloat16
HIGHEST = lax.Precision.HIGHEST
MESH = pl.DeviceIdType.MESH

D = 1024
DEPTH = 2
EPS = 1e-6
GRID_W = 64
MLA_H, MLA_NOPE, MLA_ROPE, MLA_V = 8, 64, 32, 64
MLA_QK = MLA_NOPE + MLA_ROPE
ROPE_BASE = 10000.0
POOL_WINDOWS = (2, 4, 8, 16)
GLA_H, GLA_DK, GLA_DV, GLA_RANK, GLA_TAU = 4, 64, 128, 16, 16.0
GLA_C = 128
EXP_CLAMP = 80.0
ADAM_LR, ADAM_B1, ADAM_B2, ADAM_EPS, ADAM_WD, ADAM_STEP = 0.001, 0.9, 0.999, 1e-08, 0.01, 10

TM = 256
LANES = 128
N_CHIPS = 4

IN_NAMES = ('mla_q', 'mla_kv', 'mla_kr', 'mla_gate', 'pool_x', 'pool_gate',
            'gla_q', 'gla_k', 'gla_v', 'gla_af', 'gla_ab', 'gla_gate', 'merge')
IN_SIZES = (256, 128, 32, 512, 512, 512, 256, 256, 512, 16, 16, 512, 3 * D)
NAT_OFF = dict(zip(IN_NAMES, [int(o) for o in np.cumsum((0,) + IN_SIZES[:-1])]))
NAT_SIZE = dict(zip(IN_NAMES, IN_SIZES))
PAD_ORDER = ('merge', 'mla_gate', 'mla_q', 'mla_kv', 'mla_kr', 'pool_x', 'pool_gate',
             'gla_v', 'gla_gate', 'gla_q', 'gla_k', 'gla_af', 'gla_ab')
SLAB = {n: max(NAT_SIZE[n], LANES) for n in IN_NAMES}
PAD_OFF = dict(zip(PAD_ORDER, [int(o) for o in np.cumsum([0] + [SLAB[n] for n in PAD_ORDER[:-1]])]))
D_PAD = sum(SLAB.values())
IN_SLAB = {n: 0 for n in IN_NAMES}
IN_SLAB['mla_kr'] = MLA_NOPE
MLA_HP = 128
DZ_OFF = dict(merge=0, mla_gate=3072, gla_gate=3584, mla_q=4096, mla_kv=4352, mla_kr=4480, pool=4608,
              gla_v=5632, gla_q=6144, gla_k=6400, gla_af=6656, gla_ab=6784)
DZ_PARTS = ([(n, 0, NAT_SIZE[n]) for n in ('merge', 'mla_gate', 'gla_gate', 'mla_q', 'mla_kv', 'mla_kr')]
            + [(n, g * LANES, LANES) for g in range(4) for n in ('pool_x', 'pool_gate')]
            + [(n, 0, NAT_SIZE[n]) for n in ('gla_v', 'gla_q', 'gla_k', 'gla_af', 'gla_ab')])


def _blk(name):
    return PAD_OFF[name] // SLAB[name]


SHARDED = (('mod_w', 2), ('w_in', 2), ('mla_w_uq', 2), ('mla_w_ukv', 2), ('gla_af_w2', 2), ('gla_ab_w2', 2),
           ('w_branch_mla', 2), ('w_branch_pool', 2), ('w_branch_gla', 2), ('w_out', 1))
REPLICATED = ('c_ctx', 'mod_b', 'pre_norm', 'post_norm', 'mla_q_norm', 'mla_kv_norm', 'pool_w', 'pool_scale',
              'gla_af_b', 'gla_ab_b', 'gla_norm')
WEIGHTS = ('c_ctx', 'mod_w', 'mod_b', 'pre_norm', 'post_norm', 'w_in', 'mla_q_norm', 'mla_w_uq', 'mla_kv_norm',
           'mla_w_ukv', 'pool_w', 'pool_scale', 'gla_af_w2', 'gla_af_b', 'gla_ab_w2', 'gla_ab_b', 'gla_norm',
           'w_branch_mla', 'w_branch_pool', 'w_branch_gla', 'w_out')


def _cp(vmem_mb=None, sem=None):
    kw = {}
    if vmem_mb is not None:
        kw['vmem_limit_bytes'] = vmem_mb * 1024 * 1024
    if sem is not None:
        kw['dimension_semantics'] = sem
    return pltpu.CompilerParams(**kw)


DZ_ANY = pl.BlockSpec(memory_space=pl.ANY)


def _bdot(a, b):
    return jnp.dot(a.astype(BF16), b.astype(BF16), preferred_element_type=F32)


def _bdot_nt(a, b):
    return lax.dot_general(a.astype(BF16), b.astype(BF16), (((1,), (1,)), ((), ())), preferred_element_type=F32)


def _bdot_tn(a, b):
    return lax.dot_general(a.astype(BF16), b.astype(BF16), (((0,), (0,)), ((), ())), preferred_element_type=F32)


def _xdot(a, b):
    return jnp.dot(a, b, precision=HIGHEST, preferred_element_type=F32)


def _xdot_tn(a, b):
    return lax.dot_general(a, b, (((0,), (0,)), ((), ())), precision=HIGHEST, preferred_element_type=F32)


NN = (((1,), (0,)), ((), ()))
NT = (((1,), (1,)), ((), ()))
TN = (((0,), (0,)), ((), ()))


def _split(a):
    hi = a.astype(BF16)
    return hi, (a - hi.astype(F32)).astype(BF16)


def _dot2(a, b, dims):
    f = lambda u, v: lax.dot_general(u, v, dims, preferred_element_type=F32)
    if isinstance(a, tuple):
        return f(a[0], b) + f(a[1], b)
    return f(a, b[0]) + f(a, b[1])


def _sigmoid(x):
    return jax.nn.sigmoid(x)


def _silu(x):
    return x * _sigmoid(x)


def _dsilu(x):
    s = _sigmoid(x)
    return s * (1.0 + x * (1.0 - s))


def _rstd(x):
    return lax.rsqrt(jnp.mean(x * x, axis=-1, keepdims=True) + EPS)


def _norm_bwd(xhat, r, dy):
    return r * (dy - xhat * jnp.mean(xhat * dy, axis=-1, keepdims=True))


def _colsum(a):
    return jnp.sum(a, axis=0, keepdims=True)


def _modrow(i, tpe):
    return jnp.where(i % tpe == 0, 4, i // tpe)


def _rot(x):
    n = x.shape[-1]
    lane = lax.broadcasted_iota(jnp.int32, x.shape, x.ndim - 1)
    return jnp.where(lane % 16 < 8, -pltpu.roll(x, n - 8, x.ndim - 1), pltpu.roll(x, 8, x.ndim - 1))


def _mod_fwd(cv, mod_w, mod_b):
    def body(cv_ref, w_ref, b_ref, o_ref):
        s = _silu(cv_ref[...])
        for l in range(DEPTH):
            o_ref[l] = _bdot(s, w_ref[l]) + b_ref[l]

    return pl.pallas_call(body, name="mod_fwd", out_shape=jax.ShapeDtypeStruct((DEPTH, 8, 3 * D), F32),
                          compiler_params=_cp(40))(cv, mod_w, mod_b)


def _mod_bwd(cv, sel, st_a, st_b, w_t, dcv_in, name):
    def body(cv_ref, sel_ref, sa_ref, sb_ref, wt_ref, dcin_ref, dw_ref, db_ref, dcv_ref, dcc_ref, dpre_ref, dpost_ref):
        cvv = cv_ref[...]
        s = _silu(cvv)
        sel_v = sel_ref[...]
        dmod = jnp.concatenate([_xdot(sel_v, sa_ref[0]), _xdot(sel_v, sa_ref[1]), _xdot(sel_v, sb_ref[0])], axis=1)
        dw_ref[...] = _bdot_tn(s, dmod)
        db_ref[...] = _colsum(dmod)
        dcv = dcin_ref[...] + _bdot(dmod, wt_ref[...])
        dcv_ref[...] = dcv
        dcc_ref[...] = dcv * _dsilu(cvv)
        dpre_ref[...] = _colsum(sa_ref[2])
        dpost_ref[...] = _colsum(sb_ref[1])

    shapes = (jax.ShapeDtypeStruct((D, 3 * D), F32), jax.ShapeDtypeStruct((1, 3 * D), F32),
              jax.ShapeDtypeStruct((8, D), F32), jax.ShapeDtypeStruct((8, D), F32),
              jax.ShapeDtypeStruct((1, D), F32), jax.ShapeDtypeStruct((1, D), F32))
    return pl.pallas_call(body, name=name, out_shape=shapes, compiler_params=_cp(48))(cv, sel, st_a, st_b, w_t, dcv_in)


def _pre_fwd(x, modl, pre_g, w_t, tpe, name):
    n = x.shape[0]
    nt = n // TM
    ncb = 3
    tn = D_PAD // ncb
    tm = 2 * TM if n % (2 * TM) == 0 else TM

    def norm_body(x_ref, m_ref, g_ref, h_ref):
        xv = x_ref[...]
        m = m_ref[0]
        h_ref[...] = (xv * _rstd(xv) * g_ref[...] * (1.0 + m[:, D:2 * D]) + m[:, 0:D]).astype(BF16)

    h = pl.pallas_call(
        norm_body, name=name + "_norm", grid=(nt,),
        in_specs=[pl.BlockSpec((TM, D), lambda i: (i, 0)),
                  pl.BlockSpec((1, 1, 3 * D), lambda i: (_modrow(i, tpe), 0, 0)),
                  pl.BlockSpec((1, D), lambda i: (0, 0))],
        out_specs=pl.BlockSpec((TM, D), lambda i: (i, 0)),
        out_shape=jax.ShapeDtypeStruct((n, D), BF16),
        compiler_params=_cp(32, ("arbitrary",)),
    )(x, modl.reshape(8, 1, 3 * D), pre_g)

    def mm_body(h_ref, wt_ref, z_ref):
        z_ref[...] = lax.dot_general(h_ref[...], wt_ref[...], NT, preferred_element_type=F32)

    z = pl.pallas_call(
        mm_body, name=name, grid=(ncb, n // tm),
        in_specs=[pl.BlockSpec((tm, D), lambda j, i: (i, 0)), pl.BlockSpec((tn, D), lambda j, i: (j, 0))],
        out_specs=pl.BlockSpec((tm, tn), lambda j, i: (i, j)),
        out_shape=jax.ShapeDtypeStruct((n, D_PAD), F32),
        compiler_params=_cp(48, ("arbitrary", "arbitrary")),
    )(h, w_t)
    return z, h


def _pre_bwd(dz, w_t, x, dxres, modl, pre_g, tpe, name):
    n = x.shape[0]
    nt = n // TM

    def body(dz_ref, wt_ref, x_ref, dr_ref, m_ref, g_ref, dx_ref, st_ref):
        dh = jnp.dot(dz_ref[...], wt_ref[...], preferred_element_type=F32)
        xv = x_ref[...]
        r = _rstd(xv)
        xn = xv * r
        m = m_ref[0]
        sc1 = 1.0 + m[:, D:2 * D]
        g = g_ref[...]
        st_ref[0, 0:1, :] = _colsum(dh)
        st_ref[0, 1:2, :] = _colsum(dh * xn * g)
        st_ref[0, 2:3, :] = _colsum(dh * xn * sc1)
        st_ref[0, 3:8, :] = jnp.zeros((5, D), F32)
        dx_ref[...] = dr_ref[...] + _norm_bwd(xn, r, dh * g * sc1)

    return pl.pallas_call(
        body, name=name, grid=(nt,),
        in_specs=[pl.BlockSpec((TM, D_PAD), lambda i: (i, 0)),
                  pl.BlockSpec((D_PAD, D), lambda i: (0, 0)),
                  pl.BlockSpec((TM, D), lambda i: (i, 0)),
                  pl.BlockSpec((TM, D), lambda i: (i, 0)),
                  pl.BlockSpec((1, 1, 3 * D), lambda i: (_modrow(i, tpe), 0, 0)),
                  pl.BlockSpec((1, D), lambda i: (0, 0))],
        out_specs=[pl.BlockSpec((TM, D), lambda i: (i, 0)),
                   pl.BlockSpec((1, 8, D), lambda i: (i, 0, 0))],
        out_shape=(jax.ShapeDtypeStruct((n, D), F32), jax.ShapeDtypeStruct((nt, 8, D), F32)),
        compiler_params=_cp(56, ("arbitrary",)),
    )(dz, w_t, x, dxres, modl.reshape(8, 1, 3 * D), pre_g)


def _matmul_tn(a, b, tm, tk, name):
    n, k1 = a.shape
    k2 = b.shape[1]

    def body(a_ref, b_ref, o_ref):
        @pl.when(pl.program_id(1) == 0)
        def _():
            o_ref[...] = jnp.zeros(o_ref.shape, F32)

        o_ref[...] += lax.dot_general(a_ref[...], b_ref[...], (((0,), (0,)), ((), ())), preferred_element_type=F32)

    return pl.pallas_call(
        body, name=name, grid=(k1 // tm, n // tk),
        in_specs=[pl.BlockSpec((tk, tm), lambda i, k: (k, i)), pl.BlockSpec((tk, k2), lambda i, k: (k, 0))],
        out_specs=pl.BlockSpec((tm, k2), lambda i, k: (i, 0)),
        out_shape=jax.ShapeDtypeStruct((k1, k2), F32),
        compiler_params=_cp(48, ("arbitrary", "arbitrary")),
    )(a, b)


def _mla_pre(z, q_norm, kv_norm, w_uq, w_ukv, cq, sq, ck, sk, tpe, name):
    n = z.shape[0]
    nt = n // TM
    scale = MLA_QK ** -0.5

    def body(zq_ref, zkv_ref, zkr_ref, gq_ref, gkv_ref, wq_ref, wkv_ref, cq_ref, sq_ref, ck_ref, sk_ref,
             q_ref, kv_ref, kr_ref):
        zq = zq_ref[...]
        qn = zq * _rstd(zq) * gq_ref[...]
        qraw = _bdot(qn, wq_ref[...])
        q_ref[...] = ((qraw * cq_ref[...] + _rot(qraw) * sq_ref[...]) * scale).astype(BF16)
        zkv = zkv_ref[...]
        kvn = zkv * _rstd(zkv) * gkv_ref[...]
        kv_ref[...] = _bdot(kvn, wkv_ref[...]).astype(BF16)
        zkr = zkr_ref[...]
        kr_ref[...] = (zkr * ck_ref[...] + _rot(zkr) * sk_ref[...]).astype(BF16)

    hq, hkv = MLA_H * MLA_HP, MLA_H * (MLA_NOPE + MLA_V)
    const = lambda i: (0, 0)
    tab = lambda i: (i % tpe, 0)
    return pl.pallas_call(
        body, name=name, grid=(nt,),
        in_specs=[pl.BlockSpec((TM, 256), lambda i: (i, _blk('mla_q'))),
                  pl.BlockSpec((TM, 128), lambda i: (i, _blk('mla_kv'))),
                  pl.BlockSpec((TM, 128), lambda i: (i, _blk('mla_kr'))),
                  pl.BlockSpec((1, 256), const), pl.BlockSpec((1, 128), const),
                  pl.BlockSpec((256, hq), const), pl.BlockSpec((128, hkv), const),
                  pl.BlockSpec((TM, hq), tab), pl.BlockSpec((TM, hq), tab),
                  pl.BlockSpec((TM, 128), tab), pl.BlockSpec((TM, 128), tab)],
        out_specs=[pl.BlockSpec((TM, hq), lambda i: (i, 0)), pl.BlockSpec((TM, hkv), lambda i: (i, 0)),
                   pl.BlockSpec((TM, 128), lambda i: (i, 0))],
        out_shape=(jax.ShapeDtypeStruct((n, hq), BF16), jax.ShapeDtypeStruct((n, hkv), BF16),
                   jax.ShapeDtypeStruct((n, 128), BF16)),
        compiler_params=_cp(32, ("arbitrary",)),
    )(z, z, z, q_norm, kv_norm, w_uq, w_ukv, cq, sq, ck, sk)


def _mla_pre_bwd(z, dq, dkv, dkr, q_norm, kv_norm, w_uq_t, w_ukv_t, cq, sq, ck, sk, dz, tpe, name):
    n = z.shape[0]
    nt = n // TM
    scale = MLA_QK ** -0.5
    hq, hkv = MLA_H * MLA_HP, MLA_H * (MLA_NOPE + MLA_V)

    def body(zq_ref, zkv_ref, dq_ref, dkv_ref, dkr_ref, gq_ref, gkv_ref, wqt_ref, wkvt_ref, cq_ref, sq_ref,
             ck_ref, sk_ref, dz_in, dz_ref, dwq_ref, dwkv_ref, dgq_ref, dgkv_ref):
        @pl.when(pl.program_id(0) == 0)
        def _():
            dwq_ref[...] = jnp.zeros(dwq_ref.shape, F32)
            dwkv_ref[...] = jnp.zeros(dwkv_ref.shape, F32)
            dgq_ref[...] = jnp.zeros(dgq_ref.shape, F32)
            dgkv_ref[...] = jnp.zeros(dgkv_ref.shape, F32)

        zq = zq_ref[...]
        rq = _rstd(zq)
        qhat = zq * rq
        gq = gq_ref[...]
        dqs = dq_ref[...] * scale
        dqraw = dqs * cq_ref[...] - _rot(dqs * sq_ref[...])
        dwq_ref[...] += _bdot_tn(qhat * gq, dqraw)
        dqn = _bdot(dqraw, wqt_ref[...])
        dgq_ref[...] += _colsum(dqn * qhat)
        dz_ref[:, 0:256] = _norm_bwd(qhat, rq, dqn * gq).astype(BF16)

        zkv = zkv_ref[...]
        rkv = _rstd(zkv)
        khat = zkv * rkv
        gkv = gkv_ref[...]
        dkvv = dkv_ref[...]
        dwkv_ref[...] += _bdot_tn(khat * gkv, dkvv)
        dkvn = _bdot(dkvv, wkvt_ref[...])
        dgkv_ref[...] += _colsum(dkvn * khat)
        dz_ref[:, 256:384] = _norm_bwd(khat, rkv, dkvn * gkv).astype(BF16)

        dkr = dkr_ref[...]
        dz_ref[:, 384:512] = (dkr * ck_ref[...] - _rot(dkr * sk_ref[...])).astype(BF16)

    const = lambda i: (0, 0)
    tab = lambda i: (i % tpe, 0)
    row = lambda i: (i, 0)
    return pl.pallas_call(
        body, name=name, grid=(nt,),
        in_specs=[pl.BlockSpec((TM, 256), lambda i: (i, _blk('mla_q'))),
                  pl.BlockSpec((TM, 128), lambda i: (i, _blk('mla_kv'))),
                  pl.BlockSpec((TM, hq), row), pl.BlockSpec((TM, hkv), row), pl.BlockSpec((TM, 128), row),
                  pl.BlockSpec((1, 256), const), pl.BlockSpec((1, 128), const),
                  pl.BlockSpec((hq, 256), const), pl.BlockSpec((hkv, 128), const),
                  pl.BlockSpec((TM, hq), tab), pl.BlockSpec((TM, hq), tab),
                  pl.BlockSpec((TM, 128), tab), pl.BlockSpec((TM, 128), tab), DZ_ANY],
        out_specs=[pl.BlockSpec((TM, 512), lambda i: (i, DZ_OFF['mla_q'] // 512)),
                   pl.BlockSpec((256, hq), const), pl.BlockSpec((128, hkv), const),
                   pl.BlockSpec((1, 256), const), pl.BlockSpec((1, 128), const)],
        out_shape=(jax.ShapeDtypeStruct(dz.shape, dz.dtype), jax.ShapeDtypeStruct((256, hq), F32),
                   jax.ShapeDtypeStruct((128, hkv), F32), jax.ShapeDtypeStruct((1, 256), F32),
                   jax.ShapeDtypeStruct((1, 128), F32)),
        input_output_aliases={13: 0},
        compiler_params=_cp(32, ("arbitrary",)),
    )(z, z, dq, dkv, dkr, q_norm, kv_norm, w_uq_t, w_ukv_t, cq, sq, ck, sk, dz)


def _attn_head(q_ref, kv_ref, kr_ref, hh, nk):
    kvh = kv_ref[0:nk, hh * MLA_HP:(hh + 1) * MLA_HP]
    lane = lax.broadcasted_iota(jnp.int32, kvh.shape, 1)
    kh = jnp.where(lane < MLA_NOPE, kvh, kr_ref[0:nk, :])
    qh = q_ref[:, hh * MLA_HP:(hh + 1) * MLA_HP]
    return kvh, kh, qh, lax.dot_general(qh, kh, (((1,), (1,)), ((), ())), preferred_element_type=F32)


def _by_segment(j, lc, t, fn):
    pl.when(j == 0)(functools.partial(fn, lc))
    pl.when(j != 0)(functools.partial(fn, t))


def _attn_specs(nb, tpe, t):
    tile = lambda b, p, j: (b * tpe + j, p)
    return [pl.BlockSpec((TM, 2 * MLA_HP), tile),
            pl.BlockSpec((t, 2 * MLA_HP), lambda b, p, j: (b, p)),
            pl.BlockSpec((t, MLA_HP), lambda b, p, j: (b, 0))]


def _attn_fwd(q, kv, kr, nb, lc, name):
    n = q.shape[0]
    t = n // nb
    tpe = t // TM

    def body(q_ref, kv_ref, kr_ref, o_ref, lse_ref):
        def run(nk):
            lane = lax.broadcasted_iota(jnp.int32, (TM, MLA_HP), 1)
            res, lses = [], []
            for hh in range(2):
                kvh, _, _, s = _attn_head(q_ref, kv_ref, kr_ref, hh, nk)
                m = jnp.max(s, axis=-1, keepdims=True)
                p = jnp.exp(s - m)
                l = jnp.sum(p, axis=-1, keepdims=True)
                res.append(jnp.dot(p.astype(BF16), kvh, preferred_element_type=F32) / l)
                lses.append(m + jnp.log(l))
            o_ref[...] = jnp.where(lane < MLA_V, pltpu.roll(res[0], MLA_V, 1), res[1])
            lane2 = lax.broadcasted_iota(jnp.int32, (TM, 2), 1)
            lse_ref[0] = jnp.where(lane2 == 0, lses[0], lses[1])

        _by_segment(pl.program_id(2), lc, t, run)

    return pl.pallas_call(
        body, name=name, grid=(nb, MLA_H // 2, tpe),
        in_specs=_attn_specs(nb, tpe, t),
        out_specs=[pl.BlockSpec((TM, 2 * MLA_V), lambda b, p, j: (b * tpe + j, p)),
                   pl.BlockSpec((1, TM, 2), lambda b, p, j: (p, b * tpe + j, 0))],
        out_shape=(jax.ShapeDtypeStruct((n, MLA_H * MLA_V), F32), jax.ShapeDtypeStruct((MLA_H // 2, n, 2), F32)),
        compiler_params=_cp(48, ("arbitrary", "arbitrary", "arbitrary")),
    )(q, kv, kr)


def _attn_bwd(q, kv, kr, o, lse, do, nb, lc, name):
    n = q.shape[0]
    t = n // nb
    tpe = t // TM

    def body(q_ref, kv_ref, kr_ref, o_ref, lse_ref, do_ref, dq_ref, dkv_ref, dkr_ref):
        p_id, j = pl.program_id(1), pl.program_id(2)

        @pl.when(j == 0)
        def _():
            dkv_ref[...] = jnp.zeros(dkv_ref.shape, F32)

        @pl.when((j == 0) & (p_id == 0))
        def _():
            dkr_ref[...] = jnp.zeros(dkr_ref.shape, F32)

        def run(nk):
            lane = lax.broadcasted_iota(jnp.int32, (TM, MLA_HP), 1)
            lane_t = lax.broadcasted_iota(jnp.int32, (nk, MLA_HP), 1)
            lane2 = lax.broadcasted_iota(jnp.int32, (TM, 2), 1)
            lse = lse_ref[0]
            dov, ov = do_ref[...], o_ref[...]
            dkr = jnp.zeros((nk, MLA_HP), F32)
            for hh in range(2):
                kvh, kh, qh, s = _attn_head(q_ref, kv_ref, kr_ref, hh, nk)
                p = jnp.exp(s - jnp.sum(jnp.where(lane2 == hh, lse, 0.0), axis=1, keepdims=True))
                do_pos = jnp.where(lane >= MLA_NOPE, pltpu.roll(dov, MLA_V, 1) if hh == 0 else dov, 0.0)
                o_pos = jnp.where(lane >= MLA_NOPE, pltpu.roll(ov, MLA_V, 1) if hh == 0 else ov, 0.0)
                delta = jnp.sum(do_pos * o_pos, axis=-1, keepdims=True)
                dob = do_pos.astype(BF16)
                dp = lax.dot_general(dob, kvh, (((1,), (1,)), ((), ())), preferred_element_type=F32)
                ds = (p * (dp - delta)).astype(BF16)
                dq_ref[:, hh * MLA_HP:(hh + 1) * MLA_HP] = jnp.dot(ds, kh, preferred_element_type=F32)
                dkf = lax.dot_general(ds, qh, (((0,), (0,)), ((), ())), preferred_element_type=F32)
                dvp = lax.dot_general(p.astype(BF16), dob, (((0,), (0,)), ((), ())), preferred_element_type=F32)
                dkv_ref[0:nk, hh * MLA_HP:(hh + 1) * MLA_HP] += jnp.where(lane_t < MLA_NOPE, dkf, dvp)
                dkr = dkr + jnp.where(lane_t >= MLA_NOPE, dkf, 0.0)
            dkr_ref[0:nk, :] += dkr

        _by_segment(j, lc, t, run)

    tile = lambda b, p, j: (b * tpe + j, p)
    return pl.pallas_call(
        body, name=name, grid=(nb, MLA_H // 2, tpe),
        in_specs=_attn_specs(nb, tpe, t) + [pl.BlockSpec((TM, 2 * MLA_V), tile),
                                            pl.BlockSpec((1, TM, 2), lambda b, p, j: (p, b * tpe + j, 0)),
                                            pl.BlockSpec((TM, 2 * MLA_V), tile)],
        out_specs=[pl.BlockSpec((TM, 2 * MLA_HP), tile),
                   pl.BlockSpec((t, 2 * MLA_HP), lambda b, p, j: (b, p)),
                   pl.BlockSpec((t, MLA_HP), lambda b, p, j: (b, 0))],
        out_shape=(jax.ShapeDtypeStruct((n, MLA_H * MLA_HP), F32), jax.ShapeDtypeStruct((n, MLA_H * MLA_HP), F32),
                   jax.ShapeDtypeStruct((n, MLA_HP), F32)),
        compiler_params=_cp(56, ("arbitrary", "arbitrary", "arbitrary")),
    )(q, kv, kr, o, lse, do)


def _pool_window(ug, pos, seglen, w, transpose):
    t = ug.shape[0]
    cnt = (jnp.minimum(pos + w // 2, seglen) - jnp.maximum(pos - w // 2, 0)).astype(F32)
    if transpose:
        ug = ug / cnt
    acc = jnp.zeros_like(ug)
    for j in range(-(w // 2), w // 2):
        jj = -j if transpose else j
        src = pos + jj
        valid = (src >= 0) & (src < seglen)
        acc = acc + jnp.where(valid, pltpu.roll(ug, (-jj) % t, 0), 0.0)
    return acc if transpose else acc / cnt


def _by_group(g, fn):
    for k, w in enumerate(POOL_WINDOWS):
        pl.when(g == k)(functools.partial(fn, w))


def _pool_specs(t):
    px, pg = PAD_OFF['pool_x'] // LANES, PAD_OFF['pool_gate'] // LANES
    return [pl.BlockSpec((t, LANES), lambda g, b: (b, px + g)),
            pl.BlockSpec((t, LANES), lambda g, b: (b, pg + g)),
            pl.BlockSpec((1, LANES, LANES), lambda g, b: (g, 0, 0)),
            pl.BlockSpec((1, LANES), lambda g, b: (0, g)),
            pl.BlockSpec((t, 1), lambda g, b: (0, 0)), pl.BlockSpec((t, 1), lambda g, b: (0, 0))]


def _pool_fwd(z, pool_w, pool_scale, pos, seglen, nb, name):
    n = z.shape[0]
    t = n // nb

    def body(u_ref, zg_ref, pw_ref, ps_ref, pos_ref, sl_ref, y_ref):
        def run(w):
            u = u_ref[...]
            pooled = _pool_window(u, pos_ref[...], sl_ref[...], w, False) - u
            y_ref[...] = (_bdot(pooled, pw_ref[0]) * ps_ref[...] * _silu(zg_ref[...])).astype(BF16)

        _by_group(pl.program_id(0), run)

    return pl.pallas_call(
        body, name=name, grid=(4, nb), in_specs=_pool_specs(t),
        out_specs=pl.BlockSpec((t, LANES), lambda g, b: (b, g)),
        out_shape=jax.ShapeDtypeStruct((n, 512), BF16),
        compiler_params=_cp(48, ("arbitrary", "arbitrary")),
    )(z, z, pool_w, pool_scale, pos, seglen)


def _pool_bwd(z, dy, pool_w, pool_w_t, pool_scale, pos, seglen, dz, nb, name):
    n = z.shape[0]
    t = n // nb

    def body(u_ref, zg_ref, pw_ref, ps_ref, pos_ref, sl_ref, dy_ref, pwt_ref, dz_in, dz_ref, dpw_ref, dps_ref):
        @pl.when(pl.program_id(1) == 0)
        def _():
            dpw_ref[...] = jnp.zeros(dpw_ref.shape, F32)
            dps_ref[...] = jnp.zeros(dps_ref.shape, F32)

        def run(w):
            u = u_ref[...]
            pos_v, sl_v = pos_ref[...], sl_ref[...]
            pooled = _pool_window(u, pos_v, sl_v, w, False) - u
            mixed = _bdot(pooled, pw_ref[0])
            zg = zg_ref[...]
            sg = _silu(zg)
            ps = ps_ref[...]
            dyv = dy_ref[...]
            dps_ref[...] += _colsum(dyv * mixed * sg)
            dz_ref[:, LANES:2 * LANES] = (dyv * mixed * ps * _dsilu(zg)).astype(BF16)
            dmixed = dyv * ps * sg
            dpw_ref[0] += _bdot_tn(pooled, dmixed)
            dpooled = _bdot(dmixed, pwt_ref[0])
            dz_ref[:, 0:LANES] = (_pool_window(dpooled, pos_v, sl_v, w, True) - dpooled).astype(BF16)

        _by_group(pl.program_id(0), run)

    blk = pl.BlockSpec((t, LANES), lambda g, b: (b, g))
    return pl.pallas_call(
        body, name=name, grid=(4, nb),
        in_specs=_pool_specs(t) + [blk, pl.BlockSpec((1, LANES, LANES), lambda g, b: (g, 0, 0)), DZ_ANY],
        out_specs=[pl.BlockSpec((t, 2 * LANES), lambda g, b: (b, DZ_OFF['pool'] // (2 * LANES) + g)),
                   pl.BlockSpec((1, LANES, LANES), lambda g, b: (g, 0, 0)),
                   pl.BlockSpec((1, LANES), lambda g, b: (0, g))],
        out_shape=(jax.ShapeDtypeStruct(dz.shape, dz.dtype),
                   jax.ShapeDtypeStruct((4, 128, 128), F32), jax.ShapeDtypeStruct((1, 512), F32)),
        input_output_aliases={8: 0},
        compiler_params=_cp(48, ("arbitrary", "arbitrary")),
    )(z, z, pool_w, pool_scale, pos, seglen, dy, pool_w_t, dz)


def _gla_chunk(q_ref, k_ref, a_ref, w2_ref, b2_ref, reverse):
    c = GLA_C
    x = _bdot(a_ref[...], w2_ref[0]) + b2_ref[0]
    la = (jnp.minimum(x, 0.0) - jnp.log(1.0 + jnp.exp(-jnp.abs(x)))) * (1.0 / GLA_TAU)
    row = lax.broadcasted_iota(jnp.int32, (c, c), 0)
    col = lax.broadcasted_iota(jnp.int32, (c, c), 1)
    tri = (col >= row) if reverse else (col <= row)
    tri_t = (col <= row) if reverse else (col >= row)
    b = _xdot(tri.astype(F32), la)
    tok = lax.broadcasted_iota(jnp.int32, la.shape, 0)
    bref = _colsum(jnp.where((tok >= c // 2) if reverse else (tok < c // 2), la, 0.0))
    blast = _colsum(la)
    eq = jnp.exp(jnp.minimum(b - bref, EXP_CLAMP))
    ek = jnp.exp(jnp.minimum(bref - b, EXP_CLAMP))
    qs = q_ref[...] * (GLA_DK ** -0.5)
    kk = k_ref[...]
    eb = jnp.exp(b)
    etail = jnp.exp(blast - b)
    return dict(x=x, la=la, tri=tri, tri_t=tri_t, eq=eq, ek=ek, qs=qs, kk=kk, qd=qs * eq, kd=kk * ek, qe=qs * eb,
                kl=kk * etail, eb=eb, etail=etail)


def _pair(a, p):
    return a[:, p * LANES:(p + 1) * LANES]


def _head_masks():
    lane = lax.broadcasted_iota(jnp.int32, (GLA_C, LANES), 1)
    return (lane < GLA_DK, lane >= GLA_DK)


def _state_decay(la, p):
    return jnp.exp(_xdot_tn(_pair(la, p), jnp.ones((GLA_C, GLA_DV), F32)))


def _gla_chunk_maps(nb, nc, ncc, order):
    def rmap(j):
        return jnp.where(j < ncc, ncc - 1 - j, nc - 1 - (j - ncc))

    if order == 'scan':
        return (lambda b, j: b * nc + j), (lambda b, j: b * nc + rmap(j))
    return (lambda b, j: b * nc + nc - 1 - j), (lambda b, j: b * nc + rmap(nc - 1 - j))


def _gla_in_specs(maps):
    specs = []
    for d, cm in enumerate(maps):
        gate = 'gla_af' if d == 0 else 'gla_ab'
        specs += [pl.BlockSpec((GLA_C, 256), lambda b, j, cm=cm: (cm(b, j), _blk('gla_q'))),
                  pl.BlockSpec((GLA_C, 256), lambda b, j, cm=cm: (cm(b, j), _blk('gla_k'))),
                  pl.BlockSpec((GLA_C, 512), lambda b, j, cm=cm: (cm(b, j), _blk('gla_v'))),
                  pl.BlockSpec((GLA_C, LANES), lambda b, j, cm=cm, gate=gate: (cm(b, j), _blk(gate))),
                  pl.BlockSpec((1, LANES, 256), lambda b, j, d=d: (d, 0, 0)),
                  pl.BlockSpec((1, 1, 256), lambda b, j, d=d: (d, 0, 0))]
    return specs


def _gla_fwd(z, w2, b2, nb, lc, name):
    n = z.shape[0]
    nc = n // nb // GLA_C
    maps = _gla_chunk_maps(nb, nc, lc // GLA_C, 'scan')

    def body(*refs):
        ins, (of_ref, ob_ref, sf_ref, sb_ref, s_sc) = refs[:12], refs[12:]

        @pl.when(pl.program_id(1) == 0)
        def _():
            s_sc[...] = jnp.zeros(s_sc.shape, F32)

        masks = _head_masks()
        for d, (o_ref, st_ref) in enumerate(((of_ref, sf_ref), (ob_ref, sb_ref))):
            q_ref, k_ref, v_ref, a_ref, w2_ref, b2_ref = ins[6 * d:6 * d + 6]
            ch = _gla_chunk(q_ref, k_ref, a_ref, w2_ref, b2_ref, d == 1)
            for p in range(2):
                s_prev = s_sc[d, p]
                st_ref[0, p] = s_prev
                s_new = _state_decay(ch['la'], p) * s_prev
                kd_p = _pair(ch['kd'], p)
                for hh in range(2):
                    h = 2 * p + hh
                    vv = v_ref[:, h * GLA_DV:(h + 1) * GLA_DV]
                    att = jnp.where(ch['tri'], _bdot_nt(jnp.where(masks[hh], _pair(ch['qd'], p), 0.0), kd_p), 0.0)
                    o_ref[:, h * GLA_DV:(h + 1) * GLA_DV] = (
                        _bdot(att, vv) + _bdot(jnp.where(masks[hh], _pair(ch['qe'], p), 0.0), s_prev))
                    s_new = s_new + _dot2(_split(jnp.where(masks[hh], _pair(ch['kl'], p), 0.0)), vv.astype(BF16), TN)
                s_sc[d, p] = s_new

    o_shape = jax.ShapeDtypeStruct((n, 512), F32)
    st_shape = jax.ShapeDtypeStruct((n // GLA_C, 2, LANES, GLA_DV), F32)
    return pl.pallas_call(
        body, name=name, grid=(nb, nc),
        in_specs=_gla_in_specs(maps),
        out_specs=[pl.BlockSpec((GLA_C, 512), lambda b, j: (maps[0](b, j), 0)),
                   pl.BlockSpec((GLA_C, 512), lambda b, j: (maps[1](b, j), 0)),
                   pl.BlockSpec((1, 2, LANES, GLA_DV), lambda b, j: (maps[0](b, j), 0, 0, 0)),
                   pl.BlockSpec((1, 2, LANES, GLA_DV), lambda b, j: (maps[1](b, j), 0, 0, 0))],
        out_shape=(o_shape, o_shape, st_shape, st_shape),
        scratch_shapes=[pltpu.VMEM((2, 2, LANES, GLA_DV), F32)],
        compiler_params=_cp(32, ("arbitrary", "arbitrary")),
    )(z, z, z, z, w2, b2, z, z, z, z, w2, b2)


def _gla_bwd(z, w2, w2_t, b2, st_f, st_b, dog, nb, lc, name):
    n = z.shape[0]
    nc = n // nb // GLA_C
    maps = _gla_chunk_maps(nb, nc, lc // GLA_C, 'back')

    def body(*refs):
        ins, extra, outs, (ds_sc, sfx_sc) = refs[:12], refs[12:18], refs[18:30], refs[30:]

        @pl.when(pl.program_id(1) == 0)
        def _():
            ds_sc[...] = jnp.zeros(ds_sc.shape, F32)
            sfx_sc[...] = jnp.zeros(sfx_sc.shape, F32)

        @pl.when((pl.program_id(0) == 0) & (pl.program_id(1) == 0))
        def _():
            for r in outs[8:12]:
                r[...] = jnp.zeros(r.shape, F32)

        masks = _head_masks()
        for d in range(2):
            q_ref, k_ref, v_ref, a_ref, w2_ref, b2_ref = ins[6 * d:6 * d + 6]
            w2t_ref, st_ref, do_ref = extra[3 * d:3 * d + 3]
            dq_ref, dk_ref, dv_ref, da_ref = outs[4 * d:4 * d + 4]
            dw2_ref, db2_ref = outs[8 + 2 * d], outs[9 + 2 * d]
            ch = _gla_chunk(q_ref, k_ref, a_ref, w2_ref, b2_ref, d == 1)
            dqs, dks, dbs = [], [], []
            for p in range(2):
                s_prev = st_ref[0, p]
                ds_new = ds_sc[d, p]
                qd_p, kd_p, qe_p, kl_p = (_pair(ch[nme], p) for nme in ('qd', 'kd', 'qe', 'kl'))
                ds_prev = _state_decay(ch['la'], p) * ds_new
                qd_b, kd_b = qd_p.astype(BF16), kd_p.astype(BF16)
                sp_s, dsn_s = _split(s_prev), _split(ds_new)
                dq_h, dk_h, db_h = [], [], []
                for hh in range(2):
                    h = 2 * p + hh
                    vv = v_ref[:, h * GLA_DV:(h + 1) * GLA_DV]
                    dov = do_ref[:, h * GLA_DV:(h + 1) * GLA_DV]
                    att = jnp.where(ch['tri'], _bdot_nt(jnp.where(masks[hh], qd_p, 0.0), kd_p), 0.0)
                    dv_ref[:, h * GLA_DV:(h + 1) * GLA_DV] = (
                        _bdot_tn(att, dov) + _bdot(jnp.where(masks[hh], kl_p, 0.0), ds_new))
                    vv_b, dov_b = vv.astype(BF16), dov.astype(BF16)
                    datt_b = jnp.where(ch['tri'], lax.dot_general(dov_b, vv_b, NT, preferred_element_type=F32),
                                       0.0).astype(BF16)
                    dq_in = lax.dot_general(datt_b, kd_b, NN, preferred_element_type=F32)
                    dk_in = lax.dot_general(datt_b, qd_b, TN, preferred_element_type=F32)
                    dq_st = _dot2(dov_b, sp_s, NT) * _pair(ch['eb'], p)
                    dk_st = _dot2(vv_b, dsn_s, NT) * _pair(ch['etail'], p)
                    dq_h.append(dq_in * _pair(ch['eq'], p) + dq_st)
                    dk_h.append(dk_in * _pair(ch['ek'], p) + dk_st)
                    db_h.append((qd_b.astype(F32) * dq_in - kd_b.astype(F32) * dk_in)
                                + (_pair(ch['qs'], p) * dq_st - _pair(ch['kk'], p) * dk_st))
                    ds_prev = ds_prev + _dot2(_split(jnp.where(masks[hh], qe_p, 0.0)), dov_b, TN)
                ds_sc[d, p] = ds_prev
                dqs.append(jnp.where(masks[0], dq_h[0], dq_h[1]))
                dks.append(jnp.where(masks[0], dk_h[0], dk_h[1]))
                dbs.append(jnp.where(masks[0], db_h[0], db_h[1]))
            dq_ref[...] = jnp.concatenate(dqs, axis=1) * (GLA_DK ** -0.5)
            dk_ref[...] = jnp.concatenate(dks, axis=1)
            db = jnp.concatenate(dbs, axis=1)
            dla = _xdot(ch['tri_t'].astype(F32), db) + sfx_sc[d]
            sfx_sc[d] = sfx_sc[d] + _colsum(db)
            dx = dla * (1.0 / GLA_TAU) * _sigmoid(-ch['x'])
            da_ref[...] = _bdot(dx, w2t_ref[0])
            dw2_ref[...] += _bdot_tn(a_ref[...], dx)
            db2_ref[...] += _colsum(dx)

    extra_specs, out_specs = [], []
    for d, cm in enumerate(maps):
        extra_specs += [pl.BlockSpec((1, 256, LANES), lambda b, j, d=d: (d, 0, 0)),
                        pl.BlockSpec((1, 2, LANES, GLA_DV), lambda b, j, cm=cm: (cm(b, j), 0, 0, 0)),
                        pl.BlockSpec((GLA_C, 512), lambda b, j, cm=cm: (cm(b, j), 0))]
        out_specs += [pl.BlockSpec((GLA_C, 256), lambda b, j, cm=cm: (cm(b, j), 0)),
                      pl.BlockSpec((GLA_C, 256), lambda b, j, cm=cm: (cm(b, j), 0)),
                      pl.BlockSpec((GLA_C, 512), lambda b, j, cm=cm: (cm(b, j), 0)),
                      pl.BlockSpec((GLA_C, LANES), lambda b, j, cm=cm: (cm(b, j), 0))]
    const2 = lambda b, j: (0, 0)
    out_specs += [pl.BlockSpec((LANES, 256), const2), pl.BlockSpec((1, 256), const2)] * 2
    per_dir = (jax.ShapeDtypeStruct((n, 256), F32), jax.ShapeDtypeStruct((n, 256), F32),
               jax.ShapeDtypeStruct((n, 512), F32), jax.ShapeDtypeStruct((n, LANES), F32))
    wshape = (jax.ShapeDtypeStruct((LANES, 256), F32), jax.ShapeDtypeStruct((1, 256), F32))
    return pl.pallas_call(
        body, name=name, grid=(nb, nc),
        in_specs=_gla_in_specs(maps) + extra_specs,
        out_specs=out_specs,
        out_shape=per_dir + per_dir + wshape + wshape,
        scratch_shapes=[pltpu.VMEM((2, 2, LANES, GLA_DV), F32), pltpu.VMEM((2, 1, 256), F32)],
        compiler_params=_cp(32, ("arbitrary", "arbitrary")),
    )(z, z, z, z, w2, b2, z, z, z, z, w2, b2, w2_t, st_f, dog, w2_t, st_b, dog)


def _gla_into_dz(dz, dq_f, dq_r, dk_f, dk_r, dv_f, dv_r, da_f, da_r, name):
    n = dq_f.shape[0]
    row = lambda i: (i, 0)
    w256, w512, w128 = (pl.BlockSpec((TM, w), row) for w in (256, 512, 128))
    shp = jax.ShapeDtypeStruct(dz.shape, dz.dtype)

    def v_body(dvf, dvr, dz_in, o_ref):
        o_ref[...] = (dvf[...] + dvr[...]).astype(BF16)

    dz = pl.pallas_call(
        v_body, name=name + "_v", grid=(n // TM,), in_specs=[w512, w512, DZ_ANY],
        out_specs=pl.BlockSpec((TM, 512), lambda i: (i, DZ_OFF['gla_v'] // 512)), out_shape=shp,
        input_output_aliases={2: 0}, compiler_params=_cp(32, ("arbitrary",)))(dv_f, dv_r, dz)

    def qk_body(dqf, dqr, dkf, dkr, daf, dar, dz_in, o_ref):
        o_ref[:, 0:256] = (dqf[...] + dqr[...]).astype(BF16)
        o_ref[:, 256:512] = (dkf[...] + dkr[...]).astype(BF16)
        o_ref[:, 512:640] = daf[...].astype(BF16)
        o_ref[:, 640:768] = dar[...].astype(BF16)

    return pl.pallas_call(
        qk_body, name=name + "_qk", grid=(n // TM,), in_specs=[w256] * 4 + [w128] * 2 + [DZ_ANY],
        out_specs=pl.BlockSpec((TM, 768), lambda i: (i, DZ_OFF['gla_q'] // 768)), out_shape=shp,
        input_output_aliases={6: 0}, compiler_params=_cp(32, ("arbitrary",)))(dq_f, dq_r, dk_f, dk_r, da_f, da_r, dz)


def _gla_out_norm(og):
    hats, rs = [], []
    for h in range(GLA_H):
        seg = og[:, h * GLA_DV:(h + 1) * GLA_DV]
        r = _rstd(seg)
        hats.append(seg * r)
        rs.append(r)
    return hats, rs


def _merge_branches(zm_ref, zgm_ref, zgg_ref, om_ref, yp_ref, ogf_ref, ogb_ref, gn_ref, wbm_ref, wbp_ref, wbg_ref):
    zgm, zgg = zgm_ref[...], zgg_ref[...]
    om = om_ref[...]
    y_mla = om * _silu(zgm)
    hats, rs = _gla_out_norm(ogf_ref[...] + ogb_ref[...])
    gn = gn_ref[...]
    sgg = _silu(zgg)
    y_gla = jnp.concatenate([hats[h] * gn for h in range(GLA_H)], axis=1) * sgg
    ys = (y_mla, yp_ref[...], y_gla)
    ps = (_bdot(y_mla, wbm_ref[...]), jnp.dot(yp_ref[...], wbp_ref[...], preferred_element_type=F32),
          _bdot(y_gla, wbg_ref[...]))
    zm = zm_ref[...]
    gs = tuple(_sigmoid(zm[:, a * D:(a + 1) * D]) for a in range(3))
    merged = gs[0] * ps[0] + gs[1] * ps[1] + gs[2] * ps[2]
    return dict(zgm=zgm, zgg=zgg, om=om, hats=hats, rs=rs, gn=gn, sgg=sgg, ys=ys, ps=ps, gs=gs, merged=merged)


def _merge_in_specs(tpe):
    row = lambda i: (i, 0)
    const = lambda i: (0, 0)
    return [pl.BlockSpec((TM, 3 * D), lambda i: (i, _blk('merge'))),
            pl.BlockSpec((TM, 512), lambda i: (i, _blk('mla_gate'))),
            pl.BlockSpec((TM, 512), lambda i: (i, _blk('gla_gate'))),
            pl.BlockSpec((TM, 512), row), pl.BlockSpec((TM, 512), row), pl.BlockSpec((TM, 512), row),
            pl.BlockSpec((TM, 512), row), pl.BlockSpec((1, 128), const),
            pl.BlockSpec((512, D), const), pl.BlockSpec((512, D), const), pl.BlockSpec((512, D), const),
            pl.BlockSpec((1, 1, 3 * D), lambda i: (_modrow(i, tpe), 0, 0)), pl.BlockSpec((1, D), const)]


def _merge_fwd(x, z, o_mla, y_pool, ogf, ogb, gla_n, wbm, wbp, wbg, wout, modl, post_g, tpe, name):
    n = x.shape[0]

    def body(zm_ref, zgm_ref, zgg_ref, om_ref, yp_ref, ogf_ref, ogb_ref, gn_ref, wbm_ref, wbp_ref, wbg_ref,
             m_ref, pg_ref, x_ref, wo_ref, xn_ref, out_ref):
        mb = _merge_branches(zm_ref, zgm_ref, zgg_ref, om_ref, yp_ref, ogf_ref, ogb_ref, gn_ref,
                             wbm_ref, wbp_ref, wbg_ref)
        out = _bdot(mb['merged'], wo_ref[...])
        gate = m_ref[0][:, 2 * D:3 * D]
        xn_ref[...] = x_ref[...] + gate * (out * _rstd(out) * pg_ref[...])
        out_ref[...] = out

    row = lambda i: (i, 0)
    return pl.pallas_call(
        body, name=name, grid=(n // TM,),
        in_specs=_merge_in_specs(tpe) + [pl.BlockSpec((TM, D), row), pl.BlockSpec((D, D), lambda i: (0, 0))],
        out_specs=[pl.BlockSpec((TM, D), row), pl.BlockSpec((TM, D), row)],
        out_shape=(jax.ShapeDtypeStruct((n, D), F32), jax.ShapeDtypeStruct((n, D), F32)),
        compiler_params=_cp(48, ("arbitrary",)),
    )(z, z, z, o_mla, y_pool, ogf, ogb, gla_n, wbm, wbp, wbg, modl.reshape(8, 1, 3 * D), post_g, x, wout)


def _merge_bwd(dxn, out, z, o_mla, y_pool, ogf, ogb, gla_n, wbm, wbp, wbg, wbm_t, wbp_t, wbg_t, wout_t,
               modl, post_g, tpe, name):
    n = out.shape[0]
    nt = n // TM

    def body(zm_ref, zgm_ref, zgg_ref, om_ref, yp_ref, ogf_ref, ogb_ref, gn_ref, wbm_ref, wbp_ref, wbg_ref,
             m_ref, pg_ref, dxn_ref, out_ref, wbmt_ref, wbpt_ref, wbgt_ref, wot_ref,
             dz_ref, dom_ref, dyp_ref, dog_ref, st_ref,
             dwbm_ref, dwbp_ref, dwbg_ref, dwo_ref, dgn_ref):
        @pl.when(pl.program_id(0) == 0)
        def _():
            for r in (dwbm_ref, dwbp_ref, dwbg_ref, dwo_ref, dgn_ref):
                r[...] = jnp.zeros(r.shape, F32)

        mb = _merge_branches(zm_ref, zgm_ref, zgg_ref, om_ref, yp_ref, ogf_ref, ogb_ref, gn_ref,
                             wbm_ref, wbp_ref, wbg_ref)
        out = out_ref[...]
        r2 = _rstd(out)
        on = out * r2
        pg = pg_ref[...]
        gate = m_ref[0][:, 2 * D:3 * D]
        dxn_v = dxn_ref[...]
        st_ref[0, 0:1, :] = _colsum(dxn_v * on * pg)
        st_ref[0, 1:2, :] = _colsum(dxn_v * gate * on)
        st_ref[0, 2:8, :] = jnp.zeros((6, D), F32)
        dout = _norm_bwd(on, r2, dxn_v * gate * pg)
        dwo_ref[...] += _bdot_tn(mb['merged'], dout)
        dmerged = _bdot(dout, wot_ref[...])
        dys = []
        for a, (dw_ref, wt_ref) in enumerate(((dwbm_ref, wbmt_ref), (dwbp_ref, wbpt_ref), (dwbg_ref, wbgt_ref))):
            g = mb['gs'][a]
            dz_ref[:, a * D:(a + 1) * D] = (dmerged * mb['ps'][a] * g * (1.0 - g)).astype(BF16)
            dp = dmerged * g
            dw_ref[...] += _bdot_tn(mb['ys'][a], dp)
            dys.append(_bdot(dp, wt_ref[...]))
        dom_ref[...] = dys[0] * _silu(mb['zgm'])
        dz_ref[:, 3 * D:3 * D + 512] = (dys[0] * mb['om'] * _dsilu(mb['zgm'])).astype(BF16)
        dyp_ref[...] = dys[1]
        gn = mb['gn']
        dgn = jnp.zeros((1, GLA_DV), F32)
        dzgg, dog = [], []
        for h in range(GLA_H):
            sl = slice(h * GLA_DV, (h + 1) * GLA_DV)
            dyg = dys[2][:, sl]
            hat = mb['hats'][h]
            dzgg.append(dyg * hat * gn * _dsilu(mb['zgg'][:, sl]))
            dn = dyg * mb['sgg'][:, sl]
            dgn = dgn + _colsum(dn * hat)
            dog.append(_norm_bwd(hat, mb['rs'][h], dn * gn))
        dgn_ref[...] += dgn
        dz_ref[:, 3 * D + 512:4 * D] = jnp.concatenate(dzgg, axis=1).astype(BF16)
        dog_ref[...] = jnp.concatenate(dog, axis=1)

    row = lambda i: (i, 0)
    const = lambda i: (0, 0)
    wspec = pl.BlockSpec((512, D), const)
    wtspec = pl.BlockSpec((D, 512), const)
    return pl.pallas_call(
        body, name=name, grid=(nt,),
        in_specs=_merge_in_specs(tpe) + [pl.BlockSpec((TM, D), row), pl.BlockSpec((TM, D), row),
                                         wtspec, wtspec, wtspec,
                                         pl.BlockSpec((D, D), const)],
        out_specs=[pl.BlockSpec((TM, 4 * D), row), pl.BlockSpec((TM, 512), row),
                   pl.BlockSpec((TM, 512), row), pl.BlockSpec((TM, 512), row),
                   pl.BlockSpec((1, 8, D), lambda i: (i, 0, 0)),
                   wspec, wspec, wspec, pl.BlockSpec((D, D), const), pl.BlockSpec((1, 128), const)],
        out_shape=(jax.ShapeDtypeStruct((n, D_PAD), BF16), jax.ShapeDtypeStruct((n, 512), F32),
                   jax.ShapeDtypeStruct((n, 512), F32), jax.ShapeDtypeStruct((n, 512), F32),
                   jax.ShapeDtypeStruct((nt, 8, D), F32),
                   jax.ShapeDtypeStruct((512, D), F32), jax.ShapeDtypeStruct((512, D), F32),
                   jax.ShapeDtypeStruct((512, D), F32), jax.ShapeDtypeStruct((D, D), F32),
                   jax.ShapeDtypeStruct((1, 128), F32)),
        compiler_params=_cp(56, ("arbitrary",)),
    )(z, z, z, o_mla, y_pool, ogf, ogb, gla_n, wbm, wbp, wbg, modl.reshape(8, 1, 3 * D), post_g,
      dxn, out, wbm_t, wbp_t, wbg_t, wout_t)


def _loss_grad(xf, tgt, nb, tpe):
    n = xf.shape[0]

    def body(x_ref, t_ref, dx_ref, l_ref):
        j = pl.program_id(1)
        d = x_ref[...] - t_ref[...]
        latent = j > 0
        dx_ref[...] = jnp.where(latent, d * (1.0 / D), 0.0)
        l_ref[...] = jnp.full(l_ref.shape, jnp.where(latent, 0.5 / D * jnp.sum(d * d), 0.0), F32)

    return pl.pallas_call(
        body, name="loss_grad", grid=(nb, tpe),
        in_specs=[pl.BlockSpec((TM, D), lambda b, j: (b * tpe + j, 0)),
                  pl.BlockSpec((TM, D), lambda b, j: (b * (tpe - 1) + jnp.maximum(j - 1, 0), 0))],
        out_specs=[pl.BlockSpec((TM, D), lambda b, j: (b * tpe + j, 0)),
                   pl.BlockSpec((1, 8, 128), lambda b, j: (b * tpe + j, 0, 0))],
        out_shape=(jax.ShapeDtypeStruct((n, D), F32), jax.ShapeDtypeStruct((n // TM, 8, 128), F32)),
        compiler_params=_cp(32, ("arbitrary", "arbitrary")),
    )(xf, tgt)


def _to_padded(w_nat):
    parts = []
    for nme in PAD_ORDER:
        p = w_nat[NAT_OFF[nme]:NAT_OFF[nme] + NAT_SIZE[nme]]
        if SLAB[nme] > NAT_SIZE[nme]:
            p = jnp.pad(p, [(IN_SLAB[nme], SLAB[nme] - NAT_SIZE[nme] - IN_SLAB[nme]), (0, 0)])
        parts.append(p)
    return jnp.concatenate(parts, axis=0)


def _to_dz(w_nat):
    parts = []
    for nme, off, size in DZ_PARTS:
        p = w_nat[NAT_OFF[nme] + off:NAT_OFF[nme] + off + size]
        if size < LANES:
            p = jnp.pad(p, [(IN_SLAB[nme], LANES - size - IN_SLAB[nme]), (0, 0)])
        parts.append(p)
    return jnp.concatenate(parts, axis=0)


def _from_dz(dw):
    found, pos = {}, 0
    for nme, off, size in DZ_PARTS:
        start = pos + (IN_SLAB[nme] if size < LANES else 0)
        found.setdefault(nme, []).append(dw[start:start + size])
        pos += max(size, LANES)
    return jnp.concatenate([p for nme in IN_NAMES for p in found[nme]], axis=0)


def _rope_tables(lc, l):
    half = MLA_ROPE // 2
    inv = ROPE_BASE ** (-jnp.arange(0, half, 2, dtype=F32) / half)
    tok = jnp.arange(l)
    ang_r = (tok // GRID_W).astype(F32)[:, None] * inv
    ang_c = (tok % GRID_W).astype(F32)[:, None] * inv
    ang = jnp.concatenate([ang_r, ang_r, ang_c, ang_c], axis=-1)
    cos = jnp.concatenate([jnp.ones((lc, MLA_ROPE), F32), jnp.cos(ang)], axis=0)
    sin = jnp.concatenate([jnp.zeros((lc, MLA_ROPE), F32), jnp.sin(ang)], axis=0)
    t = lc + l
    tail = MLA_HP - MLA_QK
    ck = jnp.concatenate([jnp.ones((t, MLA_NOPE), F32), cos, jnp.ones((t, tail), F32)], axis=1)
    sk = jnp.concatenate([jnp.zeros((t, MLA_NOPE), F32), sin, jnp.zeros((t, tail), F32)], axis=1)
    return jnp.tile(ck, (1, MLA_H)), jnp.tile(sk, (1, MLA_H)), ck, sk


def _pad_heads(w):
    lead = w.shape[:-1]
    w = w.reshape(lead + (MLA_H, MLA_QK))
    return jnp.pad(w, [(0, 0)] * len(lead) + [(0, 0), (0, MLA_HP - MLA_QK)]).reshape(lead + (MLA_H * MLA_HP,))


def _unpad_heads(w):
    lead = w.shape[:-1]
    return w.reshape(lead + (MLA_H, MLA_HP))[..., :MLA_QK].reshape(lead + (MLA_H * MLA_QK,))


def _local_step(x, c, ctx, tgt, wf):
    nb, l, _ = x.shape
    lc = ctx.shape[1]
    assert lc == TM and l % TM == 0
    t = lc + l
    tpe = t // TM
    n = nb * t
    nt = n // TM
    bf = lambda a: a.astype(BF16)

    xs = jnp.concatenate([ctx, x], axis=1).reshape(n, D)
    assert nb <= 4
    cv = jnp.concatenate([c, jnp.zeros((4 - nb, D), F32), wf['c_ctx'][None, :], jnp.zeros((3, D), F32)], axis=0)
    mod_w_b = bf(wf['mod_w'])
    mod_all = _mod_fwd(cv, mod_w_b, wf['mod_b'].reshape(DEPTH, 1, 3 * D))
    cq, sq, ck, sk = _rope_tables(lc, l)
    pos = jnp.concatenate([jnp.arange(lc), jnp.arange(l)]).astype(jnp.int32)[:, None]
    seglen = jnp.concatenate([jnp.full((lc,), lc), jnp.full((l,), l)]).astype(jnp.int32)[:, None]
    tiles = np.arange(nt)
    ntp = -(-nt // LANES) * LANES
    sel = np.zeros((8, ntp), np.float32)
    sel[np.where(tiles % tpe == 0, 4, tiles // tpe), tiles] = 1.0
    sel = jnp.asarray(sel)

    def tile_sums(st):
        return jnp.pad(st.transpose(1, 0, 2), ((0, 0), (0, ntp - nt), (0, 0)))

    lw = []
    for ly in range(DEPTH):
        w_in_t = _to_padded(bf(wf['w_in'][ly]))
        w_in_dz = _to_dz(bf(wf['w_in'][ly]))
        w_uq_p = _pad_heads(bf(wf['mla_w_uq'][ly]))
        w2 = jnp.pad(jnp.stack([bf(wf['gla_af_w2'][ly]), bf(wf['gla_ab_w2'][ly])]),
                     ((0, 0), (0, LANES - GLA_RANK), (0, 0)))
        lw.append(dict(
            w_in_t=w_in_t, w_in_dz=w_in_dz,
            w_uq=w_uq_p, w_uq_t=w_uq_p.T,
            w_ukv=bf(wf['mla_w_ukv'][ly]), w_ukv_t=bf(wf['mla_w_ukv'][ly]).T,
            pool_w=bf(wf['pool_w'][ly]), pool_w_t=bf(wf['pool_w'][ly]).transpose(0, 2, 1),
            w2=w2, w2_t=w2.transpose(0, 2, 1),
            b2=jnp.stack([wf['gla_af_b'][ly], wf['gla_ab_b'][ly]]).reshape(2, 1, GLA_H * GLA_DK),
            wbm=bf(wf['w_branch_mla'][ly]), wbp=bf(wf['w_branch_pool'][ly]), wbg=bf(wf['w_branch_gla'][ly]),
            wout=bf(wf['w_out'][ly]),
            wbm_t=bf(wf['w_branch_mla'][ly]).T, wbp_t=bf(wf['w_branch_pool'][ly]).T,
            wbg_t=bf(wf['w_branch_gla'][ly]).T, wout_t=bf(wf['w_out'][ly]).T,
            pre_g=wf['pre_norm'][ly][None, :], post_g=wf['post_norm'][ly][None, :],
            q_norm=wf['mla_q_norm'][ly][None, :], kv_norm=wf['mla_kv_norm'][ly][None, :],
            pool_scale=wf['pool_scale'][ly][None, :], gla_norm=wf['gla_norm'][ly][None, :]))

    saved = []
    xcur = xs
    for ly in range(DEPTH):
        w = lw[ly]
        z, h = _pre_fwd(xcur, mod_all[ly], w['pre_g'], w['w_in_t'], tpe, f"pre_fwd{ly}")
        qb, kvb, krb = _mla_pre(z, w['q_norm'], w['kv_norm'], w['w_uq'], w['w_ukv'], cq, sq, ck, sk, tpe, f"mla_pre{ly}")
        o_mla, lse = _attn_fwd(qb, kvb, krb, nb, lc, f"attn_fwd{ly}")
        y_pool = _pool_fwd(z, w['pool_w'], w['pool_scale'], pos, seglen, nb, f"pool_fwd{ly}")
        ogf, ogb, st_f, st_r = _gla_fwd(z, w['w2'], w['b2'], nb, lc, f"gla_fwd{ly}")
        xnew, out = _merge_fwd(xcur, z, o_mla, y_pool, ogf, ogb, w['gla_norm'], w['wbm'], w['wbp'], w['wbg'],
                               w['wout'], mod_all[ly], w['post_g'], tpe, f"merge_fwd{ly}")
        saved.append(dict(x=xcur, z=z, h=h, qb=qb, kvb=kvb, krb=krb, lse=lse, o_mla=o_mla, y_pool=y_pool,
                          st_f=st_f, st_r=st_r, ogf=ogf, ogb=ogb, out=out))
        xcur = xnew

    dxcur, lparts = _loss_grad(xcur, tgt.reshape(nb * l, D), nb, tpe)
    loss = jnp.sum(lparts[:, 0, 0])

    g = {k: [None] * DEPTH for k in WEIGHTS if k != 'c_ctx'}
    dcv = jnp.zeros((8, D), F32)
    dcc = None
    for ly in reversed(range(DEPTH)):
        w, s = lw[ly], saved[ly]
        (dz, dom, dyp, dog, st_b, g['w_branch_mla'][ly], g['w_branch_pool'][ly], g['w_branch_gla'][ly],
         g['w_out'][ly], dgn) = _merge_bwd(
            dxcur, s['out'], s['z'], s['o_mla'], s['y_pool'], s['ogf'], s['ogb'], w['gla_norm'], w['wbm'], w['wbp'],
            w['wbg'], w['wbm_t'], w['wbp_t'], w['wbg_t'], w['wout_t'], mod_all[ly], w['post_g'], tpe,
            f"merge_bwd{ly}")
        g['gla_norm'][ly] = dgn[0]
        dq, dkv, dkr = _attn_bwd(s['qb'], s['kvb'], s['krb'], s['o_mla'], s['lse'], dom, nb, lc, f"attn_bwd{ly}")
        dz, dwq, g['mla_w_ukv'][ly], dgq, dgkv = _mla_pre_bwd(
            s['z'], dq, dkv, dkr, w['q_norm'], w['kv_norm'], w['w_uq_t'], w['w_ukv_t'], cq, sq, ck, sk, dz, tpe,
            f"mla_pre_bwd{ly}")
        g['mla_w_uq'][ly] = _unpad_heads(dwq)
        g['mla_q_norm'][ly], g['mla_kv_norm'][ly] = dgq[0], dgkv[0]
        dz, g['pool_w'][ly], dps = _pool_bwd(s['z'], dyp, w['pool_w'], w['pool_w_t'], w['pool_scale'],
                                             pos, seglen, dz, nb, f"pool_bwd{ly}")
        g['pool_scale'][ly] = dps[0]
        (dq_f, dk_f, dv_f, da_f, dq_r, dk_r, dv_r, da_r, dw2_f, db2_f, dw2_r, db2_r) = _gla_bwd(
            s['z'], w['w2'], w['w2_t'], w['b2'], s['st_f'], s['st_r'], dog, nb, lc, f"gla_bwd{ly}")
        dz = _gla_into_dz(dz, dq_f, dq_r, dk_f, dk_r, dv_f, dv_r, da_f, da_r, f"gla_dz{ly}")
        g['gla_af_w2'][ly], g['gla_ab_w2'][ly] = dw2_f[:GLA_RANK], dw2_r[:GLA_RANK]
        g['gla_af_b'][ly], g['gla_ab_b'][ly] = db2_f[0], db2_r[0]
        dxcur, st_a = _pre_bwd(dz, w['w_in_dz'], s['x'], dxcur, mod_all[ly], w['pre_g'], tpe, f"pre_bwd{ly}")
        tk = next(k for k in (3072, 1024, 512, TM) if n % k == 0)
        g['w_in'][ly] = _from_dz(_matmul_tn(dz, s['h'], 768, tk, f"w_in_grad{ly}"))
        dmw, dmb, dcv, dcc, dpre, dpost = _mod_bwd(cv, sel, tile_sums(st_a), tile_sums(st_b),
                                                   mod_w_b[ly].T, dcv, f"mod_bwd{ly}")
        g['mod_w'][ly], g['mod_b'][ly] = dmw, dmb[0]
        g['pre_norm'][ly], g['post_norm'][ly] = dpre[0], dpost[0]

    grads = {k: jnp.stack(v) for k, v in g.items()}
    grads['c_ctx'] = dcc[4]
    grad_x = dxcur.reshape(nb, t, D)[:, lc:, :]
    return loss, grad_x, grads


def _place():
    x, y, c = lax.axis_index("x"), lax.axis_index("y"), lax.axis_index("c")
    chips = [(1 - x, y), (x, 1 - y), (1 - x, 1 - y)]
    return x, y, c, chips


def _hbm_call(body, name, out_shape, n_in, sems):
    any_spec = pl.BlockSpec(memory_space=pl.ANY)
    return pl.pallas_call(body, name=name, out_shape=out_shape, in_specs=[any_spec] * n_in,
                          out_specs=jax.tree.map(lambda _: any_spec, out_shape), scratch_shapes=sems)


def _all_gather_shards(ws):
    n = len(ws)

    def body(*refs):
        ins, outs, (send_sems, recv_sems) = refs[:n], refs[n:2 * n], refs[2 * n:]
        x, y, c, chips = _place()

        def copy(k, q, chip, half, to, src=None):
            dst = outs[k].at[2 * chip[0] + chip[1], half]
            return pltpu.make_async_remote_copy(src_ref=dst if src is None else src, dst_ref=dst,
                                                send_sem=send_sems.at[k, q], recv_sem=recv_sems.at[k, q],
                                                device_id=to, device_id_type=MESH)

        first = [copy(k, j, (x, y), c, (*chip, c), src=ins[k].at[c]) for k in range(n) for j, chip in enumerate(chips)]
        for cp in first:
            cp.start()
        passed = []
        for k in range(n):
            for j, chip in enumerate(chips):
                copy(k, j, chip, c, (x, y, c)).wait_recv()
                passed.append(copy(k, 3 + j, chip, c, (x, y, 1 - c)))
                passed[-1].start()
        for k in range(n):
            for j, chip in enumerate(chips):
                copy(k, 3 + j, chip, 1 - c, (x, y, 1 - c)).wait_recv()
        for cp in first + passed:
            cp.wait_send()

    shapes = tuple(jax.ShapeDtypeStruct((N_CHIPS,) + w.shape, w.dtype) for w in ws)
    return _hbm_call(body, "all_gather_shards", shapes, n,
                     [pltpu.SemaphoreType.DMA((n, 6)), pltpu.SemaphoreType.DMA((n, 6))])(*ws)


def _to_sibling(arrs, other_layer, name):
    n = len(arrs)

    def body(*refs):
        ins, outs, (send_sems, recv_sems) = refs[:n], refs[n:2 * n], refs[2 * n:]
        x, y, c, _ = _place()
        cps = [pltpu.make_async_remote_copy(src_ref=ins[k].at[1 - c] if other_layer else ins[k], dst_ref=outs[k],
                                            send_sem=send_sems.at[k], recv_sem=recv_sems.at[k],
                                            device_id=(x, y, 1 - c), device_id_type=MESH) for k in range(n)]
        for cp in cps:
            cp.start()
        for cp in cps:
            cp.wait()

    shapes = tuple(jax.ShapeDtypeStruct(a.shape[1:] if other_layer else a.shape, a.dtype) for a in arrs)
    return _hbm_call(body, name, shapes, n, [pltpu.SemaphoreType.DMA((n,)), pltpu.SemaphoreType.DMA((n,))])(*arrs)


def _scatter_to_chips(hs):
    n = len(hs)

    def body(*refs):
        ins, outs, (send_sems, recv_sems) = refs[:n], refs[n:2 * n], refs[2 * n:]
        x, y, c, chips = _place()
        me = 2 * x + y
        sends = []
        for k in range(n):
            for j, chip in enumerate(chips):
                cp = pltpu.make_async_remote_copy(src_ref=ins[k].at[2 * chip[0] + chip[1]], dst_ref=outs[k].at[me],
                                                  send_sem=send_sems.at[k, j], recv_sem=recv_sems.at[k, j],
                                                  device_id=(*chip, c), device_id_type=MESH)
                cp.start()
                sends.append(cp)
        for k in range(n):
            for j, chip in enumerate(chips):
                dst = outs[k].at[2 * chip[0] + chip[1]]
                pltpu.make_async_remote_copy(src_ref=dst, dst_ref=dst, send_sem=send_sems.at[k, j],
                                             recv_sem=recv_sems.at[k, j], device_id=(*chip, c),
                                             device_id_type=MESH).wait_recv()
        for cp in sends:
            cp.wait_send()

    shapes = tuple(jax.ShapeDtypeStruct(h.shape, h.dtype) for h in hs)
    return _hbm_call(body, "scatter_to_chips", shapes, n,
                     [pltpu.SemaphoreType.DMA((n, 3)), pltpu.SemaphoreType.DMA((n, 3))])(*hs)


BLOCK_BYTES = 10 * 1024 * 1024


def _blocks(r, cols, pos_bytes):
    rows = sorted({d for d in range(8, r + 1, 8) if r % d == 0} | {r})
    wide = sorted({d for d in range(LANES, cols + 1, LANES) if cols % d == 0} | {cols})
    fits = [(br * bc, bc, br) for br in rows for bc in wide if br * bc * pos_bytes <= BLOCK_BYTES]
    if not fits:
        return rows[0], wide[0]
    _, bc, br = max(fits)
    return br, bc


def _add_cores(b, got, name):
    _, ns, r, cols = b.shape
    br, bc = _blocks(r, cols, 2 * 4 + 4 + 2)

    def body(b_ref, g_ref, o_ref):
        mine = jnp.where(lax.axis_index("c") == 0, b_ref[0, 0], b_ref[1, 0])
        o_ref[0] = (mine + g_ref[0]).astype(BF16)

    spec = pl.BlockSpec((1, br, bc), lambda i, j, k: (i, j, k))
    return pl.pallas_call(body, name=name, grid=(ns, r // br, cols // bc),
                          in_specs=[pl.BlockSpec((2, 1, br, bc), lambda i, j, k: (0, i, j, k)), spec], out_specs=spec,
                          out_shape=jax.ShapeDtypeStruct((ns, r, cols), BF16), compiler_params=_cp(40))(b, got)


def _sum_chips(own, got, name):
    _, r, cols = own.shape
    br, bc = _blocks(r, cols, 2 * N_CHIPS * 2 + 4)

    def body(own_ref, got_ref, o_ref):
        me = 2 * lax.axis_index("x") + lax.axis_index("y")
        part = [jnp.where(me == j, own_ref[j], got_ref[j]).astype(F32) for j in range(N_CHIPS)]
        o_ref[...] = ((part[0] + part[1]) + part[2]) + part[3]

    spec = pl.BlockSpec((N_CHIPS, br, bc), lambda j, k: (0, j, k))
    return pl.pallas_call(body, name=name, grid=(r // br, cols // bc), in_specs=[spec, spec],
                          out_specs=pl.BlockSpec((br, bc), lambda j, k: (j, k)),
                          out_shape=jax.ShapeDtypeStruct((r, cols), F32), compiler_params=_cp(40))(own, got)


def _adamw(w, g_mine, g_other, m, v, name):
    _, r, cols = w.shape
    br, bc = _blocks(r, cols, 9 * 4)

    def body(w_ref, gm_ref, go_ref, m_ref, v_ref, g_ref, d_ref, nm_ref, nv_ref):
        gv = jnp.where(pl.program_id(0) == lax.axis_index("c"), gm_ref[...], go_ref[...])
        m2 = ADAM_B1 * m_ref[0] + (1.0 - ADAM_B1) * gv
        v2 = ADAM_B2 * v_ref[0] + (1.0 - ADAM_B2) * jnp.square(gv)
        m_hat = m2 / (1.0 - ADAM_B1 ** ADAM_STEP)
        v_hat = v2 / (1.0 - ADAM_B2 ** ADAM_STEP)
        g_ref[0] = gv
        d_ref[0] = -ADAM_LR * (m_hat / (jnp.sqrt(v_hat) + ADAM_EPS) + ADAM_WD * w_ref[0])
        nm_ref[0] = m2
        nv_ref[0] = v2

    lay = pl.BlockSpec((1, br, bc), lambda l, j, k: (l, j, k))
    flat = pl.BlockSpec((br, bc), lambda l, j, k: (j, k))
    shp = jax.ShapeDtypeStruct(w.shape, F32)
    return pl.pallas_call(body, name=name, grid=(2, r // br, cols // bc), in_specs=[lay, flat, flat, lay, lay],
                          out_specs=[lay] * 4, out_shape=(shp,) * 4, compiler_params=_cp(40))(w, g_mine, g_other, m, v)


def _pack_small(ts):
    flat = jnp.concatenate([ts[k].reshape(DEPTH, -1) for k in REPLICATED], axis=1)
    return flat.reshape(DEPTH, flat.shape[1] // LANES, LANES)


def _unpack_small(packed, like):
    flat = packed.reshape(DEPTH, -1)
    out, off = {}, 0
    for k in REPLICATED:
        size = like[k].size // DEPTH
        out[k] = flat[:, off:off + size].reshape(like[k].shape)
        off += size
    return out


def _shard_major(a, axis):
    if axis == 1:
        return a.reshape(DEPTH, N_CHIPS, a.shape[1] // N_CHIPS, a.shape[2])
    return a.reshape(DEPTH, a.shape[1], N_CHIPS, a.shape[2] // N_CHIPS).transpose(0, 2, 1, 3)


def kernel(x, c, ctx, c_ctx, mod_w, mod_b, pre_norm, post_norm, w_in, mla_q_norm, mla_w_uq, mla_kv_norm, mla_w_ukv, pool_w, pool_scale, gla_af_w2, gla_af_b, gla_ab_w2, gla_ab_b, gla_norm, w_branch_mla, w_branch_pool, w_branch_gla, w_out, loss_target, m_c_ctx, m_mod_w, m_mod_b, m_pre_norm, m_post_norm, m_w_in, m_mla_q_norm, m_mla_w_uq, m_mla_kv_norm, m_mla_w_ukv, m_pool_w, m_pool_scale, m_gla_af_w2, m_gla_af_b, m_gla_ab_w2, m_gla_ab_b, m_gla_norm, m_w_branch_mla, m_w_branch_pool, m_w_branch_gla, m_w_out, v_c_ctx, v_mod_w, v_mod_b, v_pre_norm, v_post_norm, v_w_in, v_mla_q_norm, v_mla_w_uq, v_mla_kv_norm, v_mla_w_ukv, v_pool_w, v_pool_scale, v_gla_af_w2, v_gla_af_b, v_gla_ab_w2, v_gla_ab_b, v_gla_norm, v_w_branch_mla, v_w_branch_pool, v_w_branch_gla, v_w_out):
    given = dict(locals())
    wts = {k: given[k] for k in WEIGHTS}
    my_chip = 2 * lax.axis_index("x") + lax.axis_index("y")

    view = lambda k, a: jnp.swapaxes(a, 1, 2) if k == 'w_in' else a
    axes = {k: (3 - axis if k == 'w_in' else axis) for k, axis in SHARDED}

    mine = [view(k, wts[k]).astype(BF16) for k, _ in SHARDED]
    gathered = _all_gather_shards(mine)
    full = dict(wts)
    for (k, _), own, got in zip(SHARDED, mine, gathered):
        full[k] = jnp.concatenate([jnp.where(my_chip == s, own, got[s]) for s in range(N_CHIPS)], axis=axes[k])

    loss_local, grad_x, grads = _local_step(x, c, ctx, loss_target, full)
    loss = lax.psum(loss_local, ("x", "y", "c"))

    small = _pack_small(grads)
    bufs = [_shard_major(grads[k], axes[k]) for k, _ in SHARDED]
    bufs.append(jnp.broadcast_to(small[:, None], (DEPTH, N_CHIPS) + small.shape[1:]))
    got = _to_sibling(bufs, True, "swap_halves")
    chip_sum = [_add_cores(b, g, f"add_cores{i}") for i, (b, g) in enumerate(zip(bufs, got))]
    recv = _scatter_to_chips(chip_sum)
    mine_red = [_sum_chips(cs, rc, f"sum_chips{i}") for i, (cs, rc) in enumerate(zip(chip_sum, recv))]
    other_red = _to_sibling(mine_red, False, "join_halves")

    outs = {}
    for i, (k, _) in enumerate(SHARDED):
        res = _adamw(view(k, wts[k]), mine_red[i], other_red[i], view(k, given['m_' + k]), view(k, given['v_' + k]),
                     f"adamw{i}")
        outs[k] = tuple(view(k, r) for r in res)
    packed = _adamw(_pack_small(wts), mine_red[-1], other_red[-1],
                    _pack_small({k: given['m_' + k] for k in REPLICATED}),
                    _pack_small({k: given['v_' + k] for k in REPLICATED}), "adamw_small")
    unpacked = [_unpack_small(p, wts) for p in packed]
    for k in REPLICATED:
        outs[k] = tuple(u[k] for u in unpacked)
    return (loss, grad_x, *[outs[k][q] for q in range(4) for k in WEIGHTS])
```

```python
import functools

import numpy as np
import jax
import jax.numpy as jnp
from jax import lax
from jax.experimental import pallas as pl
from jax.experimental.pallas import tpu as pltpu

F32 = jnp.float32
BF16 = jnp.bfloat16
HIGHEST = lax.Precision.HIGHEST
MESH = pl.DeviceIdType.MESH

D = 1024
DEPTH = 2
EPS = 1e-6
GRID_W = 64
MLA_H, MLA_NOPE, MLA_ROPE, MLA_V = 8, 64, 32, 64
MLA_QK = MLA_NOPE + MLA_ROPE
ROPE_BASE = 10000.0
POOL_WINDOWS = (2, 4, 8, 16)
GLA_H, GLA_DK, GLA_DV, GLA_RANK, GLA_TAU = 4, 64, 128, 16, 16.0
GLA_C = 128
EXP_CLAMP = 80.0
ADAM_LR, ADAM_B1, ADAM_B2, ADAM_EPS, ADAM_WD, ADAM_STEP = 0.001, 0.9, 0.999, 1e-08, 0.01, 10

TM = 256
LANES = 128
N_CHIPS = 4

IN_NAMES = ('mla_q', 'mla_kv', 'mla_kr', 'mla_gate', 'pool_x', 'pool_gate',
            'gla_q', 'gla_k', 'gla_v', 'gla_af', 'gla_ab', 'gla_gate', 'merge')
IN_SIZES = (256, 128, 32, 512, 512, 512, 256, 256, 512, 16, 16, 512, 3 * D)
NAT_OFF = dict(zip(IN_NAMES, [int(o) for o in np.cumsum((0,) + IN_SIZES[:-1])]))
NAT_SIZE = dict(zip(IN_NAMES, IN_SIZES))
PAD_ORDER = ('merge', 'mla_gate', 'mla_q', 'mla_kv', 'mla_kr', 'pool_x', 'pool_gate',
             'gla_v', 'gla_gate', 'gla_q', 'gla_k', 'gla_af', 'gla_ab')
SLAB = {n: max(NAT_SIZE[n], LANES) for n in IN_NAMES}
PAD_OFF = dict(zip(PAD_ORDER, [int(o) for o in np.cumsum([0] + [SLAB[n] for n in PAD_ORDER[:-1]])]))
D_PAD = sum(SLAB.values())
IN_SLAB = {n: 0 for n in IN_NAMES}
IN_SLAB['mla_kr'] = MLA_NOPE
MLA_HP = 128
DZ_OFF = dict(merge=0, mla_gate=3072, gla_gate=3584, mla_q=4096, mla_kv=4352, mla_kr=4480, pool=4608,
              gla_v=5632, gla_q=6144, gla_k=6400, gla_af=6656, gla_ab=6784)
DZ_PARTS = ([(n, 0, NAT_SIZE[n]) for n in ('merge', 'mla_gate', 'gla_gate', 'mla_q', 'mla_kv', 'mla_kr')]
            + [(n, g * LANES, LANES) for g in range(4) for n in ('pool_x', 'pool_gate')]
            + [(n, 0, NAT_SIZE[n]) for n in ('gla_v', 'gla_q', 'gla_k', 'gla_af', 'gla_ab')])


def _blk(name):
    return PAD_OFF[name] // SLAB[name]


SHARDED = (('mod_w', 2), ('w_in', 2), ('mla_w_uq', 2), ('mla_w_ukv', 2), ('gla_af_w2', 2), ('gla_ab_w2', 2),
           ('w_branch_mla', 2), ('w_branch_pool', 2), ('w_branch_gla', 2), ('w_out', 1))
REPLICATED = ('c_ctx', 'mod_b', 'pre_norm', 'post_norm', 'mla_q_norm', 'mla_kv_norm', 'pool_w', 'pool_scale',
              'gla_af_b', 'gla_ab_b', 'gla_norm')
WEIGHTS = ('c_ctx', 'mod_w', 'mod_b', 'pre_norm', 'post_norm', 'w_in', 'mla_q_norm', 'mla_w_uq', 'mla_kv_norm',
           'mla_w_ukv', 'pool_w', 'pool_scale', 'gla_af_w2', 'gla_af_b', 'gla_ab_w2', 'gla_ab_b', 'gla_norm',
           'w_branch_mla', 'w_branch_pool', 'w_branch_gla', 'w_out')


def _cp(vmem_mb=None, sem=None):
    kw = {}
    if vmem_mb is not None:
        kw['vmem_limit_bytes'] = vmem_mb * 1024 * 1024
    if sem is not None:
        kw['dimension_semantics'] = sem
    return pltpu.CompilerParams(**kw)


DZ_ANY = pl.BlockSpec(memory_space=pl.ANY)


def _bdot(a, b):
    return jnp.dot(a.astype(BF16), b.astype(BF16), preferred_element_type=F32)


def _bdot_nt(a, b):
    return lax.dot_general(a.astype(BF16), b.astype(BF16), (((1,), (1,)), ((), ())), preferred_element_type=F32)


def _bdot_tn(a, b):
    return lax.dot_general(a.astype(BF16), b.astype(BF16), (((0,), (0,)), ((), ())), preferred_element_type=F32)


def _xdot(a, b):
    return jnp.dot(a, b, precision=HIGHEST, preferred_element_type=F32)


def _xdot_tn(a, b):
    return lax.dot_general(a, b, (((0,), (0,)), ((), ())), precision=HIGHEST, preferred_element_type=F32)


NN = (((1,), (0,)), ((), ()))
NT = (((1,), (1,)), ((), ()))
TN = (((0,), (0,)), ((), ()))


def _split(a):
    hi = a.astype(BF16)
    return hi, (a - hi.astype(F32)).astype(BF16)


def _dot2(a, b, dims):
    f = lambda u, v: lax.dot_general(u, v, dims, preferred_element_type=F32)
    if isinstance(a, tuple):
        return f(a[0], b) + f(a[1], b)
    return f(a, b[0]) + f(a, b[1])


def _sigmoid(x):
    return jax.nn.sigmoid(x)


def _silu(x):
    return x * _sigmoid(x)


def _dsilu(x):
    s = _sigmoid(x)
    return s * (1.0 + x * (1.0 - s))


def _rstd(x):
    return lax.rsqrt(jnp.mean(x * x, axis=-1, keepdims=True) + EPS)


def _norm_bwd(xhat, r, dy):
    return r * (dy - xhat * jnp.mean(xhat * dy, axis=-1, keepdims=True))


def _colsum(a):
    return jnp.sum(a, axis=0, keepdims=True)


def _modrow(i, tpe):
    return jnp.where(i % tpe == 0, 4, i // tpe)


def _rot(x):
    n = x.shape[-1]
    lane = lax.broadcasted_iota(jnp.int32, x.shape, x.ndim - 1)
    return jnp.where(lane % 16 < 8, -pltpu.roll(x, n - 8, x.ndim - 1), pltpu.roll(x, 8, x.ndim - 1))


def _mod_fwd(cv, mod_w, mod_b):
    def body(cv_ref, w_ref, b_ref, o_ref):
        s = _silu(cv_ref[...])
        for l in range(DEPTH):
            o_ref[l] = _bdot(s, w_ref[l]) + b_ref[l]

    return pl.pallas_call(body, name="mod_fwd", out_shape=jax.ShapeDtypeStruct((DEPTH, 8, 3 * D), F32),
                          compiler_params=_cp(40))(cv, mod_w, mod_b)


def _mod_bwd(cv, sel, st_a, st_b, w_t, dcv_in, name):
    def body(cv_ref, sel_ref, sa_ref, sb_ref, wt_ref, dcin_ref, dw_ref, db_ref, dcv_ref, dcc_ref, dpre_ref, dpost_ref):
        cvv = cv_ref[...]
        s = _silu(cvv)
        sel_v = sel_ref[...]
        dmod = jnp.concatenate([_xdot(sel_v, sa_ref[0]), _xdot(sel_v, sa_ref[1]), _xdot(sel_v, sb_ref[0])], axis=1)
        dw_ref[...] = _bdot_tn(s, dmod)
        db_ref[...] = _colsum(dmod)
        dcv = dcin_ref[...] + _bdot(dmod, wt_ref[...])
        dcv_ref[...] = dcv
        dcc_ref[...] = dcv * _dsilu(cvv)
        dpre_ref[...] = _colsum(sa_ref[2])
        dpost_ref[...] = _colsum(sb_ref[1])

    shapes = (jax.ShapeDtypeStruct((D, 3 * D), F32), jax.ShapeDtypeStruct((1, 3 * D), F32),
              jax.ShapeDtypeStruct((8, D), F32), jax.ShapeDtypeStruct((8, D), F32),
              jax.ShapeDtypeStruct((1, D), F32), jax.ShapeDtypeStruct((1, D), F32))
    return pl.pallas_call(body, name=name, out_shape=shapes, compiler_params=_cp(48))(cv, sel, st_a, st_b, w_t, dcv_in)


def _pre_fwd(x, modl, pre_g, w_t, tpe, name):
    n = x.shape[0]
    nt = n // TM
    ncb = 3
    tn = D_PAD // ncb
    tm = 2 * TM if n % (2 * TM) == 0 else TM

    def norm_body(x_ref, m_ref, g_ref, h_ref):
        xv = x_ref[...]
        m = m_ref[0]
        h_ref[...] = (xv * _rstd(xv) * g_ref[...] * (1.0 + m[:, D:2 * D]) + m[:, 0:D]).astype(BF16)

    h = pl.pallas_call(
        norm_body, name=name + "_norm", grid=(nt,),
        in_specs=[pl.BlockSpec((TM, D), lambda i: (i, 0)),
                  pl.BlockSpec((1, 1, 3 * D), lambda i: (_modrow(i, tpe), 0, 0)),
                  pl.BlockSpec((1, D), lambda i: (0, 0))],
        out_specs=pl.BlockSpec((TM, D), lambda i: (i, 0)),
        out_shape=jax.ShapeDtypeStruct((n, D), BF16),
        compiler_params=_cp(32, ("arbitrary",)),
    )(x, modl.reshape(8, 1, 3 * D), pre_g)

    def mm_body(h_ref, wt_ref, z_ref):
        z_ref[...] = lax.dot_general(h_ref[...], wt_ref[...], NT, preferred_element_type=F32)

    z = pl.pallas_call(
        mm_body, name=name, grid=(ncb, n // tm),
        in_specs=[pl.BlockSpec((tm, D), lambda j, i: (i, 0)), pl.BlockSpec((tn, D), lambda j, i: (j, 0))],
        out_specs=pl.BlockSpec((tm, tn), lambda j, i: (i, j)),
        out_shape=jax.ShapeDtypeStruct((n, D_PAD), F32),
        compiler_params=_cp(48, ("arbitrary", "arbitrary")),
    )(h, w_t)
    return z, h


def _pre_bwd(dz, w_t, x, dxres, modl, pre_g, tpe, name, latent_only=False):
    n = x.shape[0]
    nt = n // TM
    if latent_only:
        dx_rows = n - (n // (tpe * TM)) * TM
        dx_map = lambda i: ((i // tpe) * (tpe - 1) + jnp.maximum(i % tpe - 1, 0), 0)
    else:
        dx_rows, dx_map = n, (lambda i: (i, 0))

    def body(dz_ref, wt_ref, x_ref, dr_ref, m_ref, g_ref, dx_ref, st_ref):
        dh = jnp.dot(dz_ref[...], wt_ref[...], preferred_element_type=F32)
        xv = x_ref[...]
        r = _rstd(xv)
        xn = xv * r
        m = m_ref[0]
        sc1 = 1.0 + m[:, D:2 * D]
        g = g_ref[...]
        st_ref[0, 0:1, :] = _colsum(dh)
        st_ref[0, 1:2, :] = _colsum(dh * xn * g)
        st_ref[0, 2:3, :] = _colsum(dh * xn * sc1)
        st_ref[0, 3:8, :] = jnp.zeros((5, D), F32)
        dx_ref[...] = dr_ref[...] + _norm_bwd(xn, r, dh * g * sc1)

    return pl.pallas_call(
        body, name=name, grid=(nt,),
        in_specs=[pl.BlockSpec((TM, D_PAD), lambda i: (i, 0)),
                  pl.BlockSpec((D_PAD, D), lambda i: (0, 0)),
                  pl.BlockSpec((TM, D), lambda i: (i, 0)),
                  pl.BlockSpec((TM, D), lambda i: (i, 0)),
                  pl.BlockSpec((1, 1, 3 * D), lambda i: (_modrow(i, tpe), 0, 0)),
                  pl.BlockSpec((1, D), lambda i: (0, 0))],
        out_specs=[pl.BlockSpec((TM, D), dx_map),
                   pl.BlockSpec((1, 8, D), lambda i: (i, 0, 0))],
        out_shape=(jax.ShapeDtypeStruct((dx_rows, D), F32), jax.ShapeDtypeStruct((nt, 8, D), F32)),
        compiler_params=_cp(56, ("arbitrary",)),
    )(dz, w_t, x, dxres, modl.reshape(8, 1, 3 * D), pre_g)


def _matmul_tn(a, b, tm, tk, name):
    n, k1 = a.shape
    k2 = b.shape[1]

    def body(a_ref, b_ref, o_ref):
        @pl.when(pl.program_id(1) == 0)
        def _():
            o_ref[...] = jnp.zeros(o_ref.shape, F32)

        o_ref[...] += lax.dot_general(a_ref[...], b_ref[...], (((0,), (0,)), ((), ())), preferred_element_type=F32)

    return pl.pallas_call(
        body, name=name, grid=(k1 // tm, n // tk),
        in_specs=[pl.BlockSpec((tk, tm), lambda i, k: (k, i)), pl.BlockSpec((tk, k2), lambda i, k: (k, 0))],
        out_specs=pl.BlockSpec((tm, k2), lambda i, k: (i, 0)),
        out_shape=jax.ShapeDtypeStruct((k1, k2), F32),
        compiler_params=_cp(48, ("arbitrary", "arbitrary")),
    )(a, b)


def _mla_pre(z, q_norm, kv_norm, w_uq, w_ukv, cq, sq, ck, sk, tpe, name):
    n = z.shape[0]
    nt = n // TM
    scale = MLA_QK ** -0.5

    def body(zq_ref, zkv_ref, zkr_ref, gq_ref, gkv_ref, wq_ref, wkv_ref, cq_ref, sq_ref, ck_ref, sk_ref,
             q_ref, kv_ref, kr_ref):
        zq = zq_ref[...]
        qn = zq * _rstd(zq) * gq_ref[...]
        qraw = _bdot(qn, wq_ref[...])
        q_ref[...] = ((qraw * cq_ref[...] + _rot(qraw) * sq_ref[...]) * scale).astype(BF16)
        zkv = zkv_ref[...]
        kvn = zkv * _rstd(zkv) * gkv_ref[...]
        kv_ref[...] = _bdot(kvn, wkv_ref[...]).astype(BF16)
        zkr = zkr_ref[...]
        kr_ref[...] = (zkr * ck_ref[...] + _rot(zkr) * sk_ref[...]).astype(BF16)

    hq, hkv = MLA_H * MLA_HP, MLA_H * (MLA_NOPE + MLA_V)
    const = lambda i: (0, 0)
    tab = lambda i: (i % tpe, 0)
    return pl.pallas_call(
        body, name=name, grid=(nt,),
        in_specs=[pl.BlockSpec((TM, 256), lambda i: (i, _blk('mla_q'))),
                  pl.BlockSpec((TM, 128), lambda i: (i, _blk('mla_kv'))),
                  pl.BlockSpec((TM, 128), lambda i: (i, _blk('mla_kr'))),
                  pl.BlockSpec((1, 256), const), pl.BlockSpec((1, 128), const),
                  pl.BlockSpec((256, hq), const), pl.BlockSpec((128, hkv), const),
                  pl.BlockSpec((TM, hq), tab), pl.BlockSpec((TM, hq), tab),
                  pl.BlockSpec((TM, 128), tab), pl.BlockSpec((TM, 128), tab)],
        out_specs=[pl.BlockSpec((TM, hq), lambda i: (i, 0)), pl.BlockSpec((TM, hkv), lambda i: (i, 0)),
                   pl.BlockSpec((TM, 128), lambda i: (i, 0))],
        out_shape=(jax.ShapeDtypeStruct((n, hq), BF16), jax.ShapeDtypeStruct((n, hkv), BF16),
                   jax.ShapeDtypeStruct((n, 128), BF16)),
        compiler_params=_cp(32, ("arbitrary",)),
    )(z, z, z, q_norm, kv_norm, w_uq, w_ukv, cq, sq, ck, sk)


def _mla_pre_bwd(z, dq, dkv, dkr, q_norm, kv_norm, w_uq_t, w_ukv_t, cq, sq, ck, sk, dz, tpe, name):
    n = z.shape[0]
    nt = n // TM
    scale = MLA_QK ** -0.5
    hq, hkv = MLA_H * MLA_HP, MLA_H * (MLA_NOPE + MLA_V)

    def body(zq_ref, zkv_ref, dq_ref, dkv_ref, dkr_ref, gq_ref, gkv_ref, wqt_ref, wkvt_ref, cq_ref, sq_ref,
             ck_ref, sk_ref, dz_in, dz_ref, dwq_ref, dwkv_ref, dgq_ref, dgkv_ref):
        @pl.when(pl.program_id(0) == 0)
        def _():
            dwq_ref[...] = jnp.zeros(dwq_ref.shape, F32)
            dwkv_ref[...] = jnp.zeros(dwkv_ref.shape, F32)
            dgq_ref[...] = jnp.zeros(dgq_ref.shape, F32)
            dgkv_ref[...] = jnp.zeros(dgkv_ref.shape, F32)

        zq = zq_ref[...]
        rq = _rstd(zq)
        qhat = zq * rq
        gq = gq_ref[...]
        dqs = dq_ref[...] * scale
        dqraw = dqs * cq_ref[...] - _rot(dqs * sq_ref[...])
        dwq_ref[...] += _bdot_tn(qhat * gq, dqraw)
        dqn = _bdot(dqraw, wqt_ref[...])
        dgq_ref[...] += _colsum(dqn * qhat)
        dz_ref[:, 0:256] = _norm_bwd(qhat, rq, dqn * gq).astype(BF16)

        zkv = zkv_ref[...]
        rkv = _rstd(zkv)
        khat = zkv * rkv
        gkv = gkv_ref[...]
        dkvv = dkv_ref[...]
        dwkv_ref[...] += _bdot_tn(khat * gkv, dkvv)
        dkvn = _bdot(dkvv, wkvt_ref[...])
        dgkv_ref[...] += _colsum(dkvn * khat)
        dz_ref[:, 256:384] = _norm_bwd(khat, rkv, dkvn * gkv).astype(BF16)

        dkr = dkr_ref[...]
        dz_ref[:, 384:512] = (dkr * ck_ref[...] - _rot(dkr * sk_ref[...])).astype(BF16)

    const = lambda i: (0, 0)
    tab = lambda i: (i % tpe, 0)
    row = lambda i: (i, 0)
    return pl.pallas_call(
        body, name=name, grid=(nt,),
        in_specs=[pl.BlockSpec((TM, 256), lambda i: (i, _blk('mla_q'))),
                  pl.BlockSpec((TM, 128), lambda i: (i, _blk('mla_kv'))),
                  pl.BlockSpec((TM, hq), row), pl.BlockSpec((TM, hkv), row), pl.BlockSpec((TM, 128), row),
                  pl.BlockSpec((1, 256), const), pl.BlockSpec((1, 128), const),
                  pl.BlockSpec((hq, 256), const), pl.BlockSpec((hkv, 128), const),
                  pl.BlockSpec((TM, hq), tab), pl.BlockSpec((TM, hq), tab),
                  pl.BlockSpec((TM, 128), tab), pl.BlockSpec((TM, 128), tab), DZ_ANY],
        out_specs=[pl.BlockSpec((TM, 512), lambda i: (i, DZ_OFF['mla_q'] // 512)),
                   pl.BlockSpec((256, hq), const), pl.BlockSpec((128, hkv), const),
                   pl.BlockSpec((1, 256), const), pl.BlockSpec((1, 128), const)],
        out_shape=(jax.ShapeDtypeStruct(dz.shape, dz.dtype), jax.ShapeDtypeStruct((256, hq), F32),
                   jax.ShapeDtypeStruct((128, hkv), F32), jax.ShapeDtypeStruct((1, 256), F32),
                   jax.ShapeDtypeStruct((1, 128), F32)),
        input_output_aliases={13: 0},
        compiler_params=_cp(32, ("arbitrary",)),
    )(z, z, dq, dkv, dkr, q_norm, kv_norm, w_uq_t, w_ukv_t, cq, sq, ck, sk, dz)


def _attn_head(q_ref, kv_ref, kr_ref, hh, nk):
    kvh = kv_ref[0:nk, hh * MLA_HP:(hh + 1) * MLA_HP]
    lane = lax.broadcasted_iota(jnp.int32, kvh.shape, 1)
    kh = jnp.where(lane < MLA_NOPE, kvh, kr_ref[0:nk, :])
    qh = q_ref[:, hh * MLA_HP:(hh + 1) * MLA_HP]
    return kvh, kh, qh, lax.dot_general(qh, kh, (((1,), (1,)), ((), ())), preferred_element_type=F32)


def _by_segment(j, lc, t, fn):
    pl.when(j == 0)(functools.partial(fn, lc))
    pl.when(j != 0)(functools.partial(fn, t))


def _attn_specs(nb, tpe, t):
    tile = lambda b, p, j: (b * tpe + j, p)
    return [pl.BlockSpec((TM, 2 * MLA_HP), tile),
            pl.BlockSpec((t, 2 * MLA_HP), lambda b, p, j: (b, p)),
            pl.BlockSpec((t, MLA_HP), lambda b, p, j: (b, 0))]


def _attn_fwd(q, kv, kr, nb, lc, name):
    n = q.shape[0]
    t = n // nb
    tpe = t // TM

    def body(q_ref, kv_ref, kr_ref, o_ref, lse_ref):
        def run(nk):
            lane = lax.broadcasted_iota(jnp.int32, (TM, MLA_HP), 1)
            res, lses = [], []
            for hh in range(2):
                kvh, _, _, s = _attn_head(q_ref, kv_ref, kr_ref, hh, nk)
                m = jnp.max(s, axis=-1, keepdims=True)
                p = jnp.exp(s - m)
                l = jnp.sum(p, axis=-1, keepdims=True)
                res.append(jnp.dot(p.astype(BF16), kvh, preferred_element_type=F32) / l)
                lses.append(m + jnp.log(l))
            o_ref[...] = jnp.where(lane < MLA_V, pltpu.roll(res[0], MLA_V, 1), res[1])
            lane2 = lax.broadcasted_iota(jnp.int32, (TM, 2), 1)
            lse_ref[0] = jnp.where(lane2 == 0, lses[0], lses[1])

        _by_segment(pl.program_id(2), lc, t, run)

    return pl.pallas_call(
        body, name=name, grid=(nb, MLA_H // 2, tpe),
        in_specs=_attn_specs(nb, tpe, t),
        out_specs=[pl.BlockSpec((TM, 2 * MLA_V), lambda b, p, j: (b * tpe + j, p)),
                   pl.BlockSpec((1, TM, 2), lambda b, p, j: (p, b * tpe + j, 0))],
        out_shape=(jax.ShapeDtypeStruct((n, MLA_H * MLA_V), F32), jax.ShapeDtypeStruct((MLA_H // 2, n, 2), F32)),
        compiler_params=_cp(48, ("arbitrary", "arbitrary", "arbitrary")),
    )(q, kv, kr)


def _attn_bwd(q, kv, kr, o, lse, do, nb, lc, name):
    n = q.shape[0]
    t = n // nb
    tpe = t // TM

    def body(q_ref, kv_ref, kr_ref, o_ref, lse_ref, do_ref, dq_ref, dkv_ref, dkr_ref):
        p_id, j = pl.program_id(1), pl.program_id(2)

        @pl.when(j == 0)
        def _():
            dkv_ref[...] = jnp.zeros(dkv_ref.shape, F32)

        @pl.when((j == 0) & (p_id == 0))
        def _():
            dkr_ref[...] = jnp.zeros(dkr_ref.shape, F32)

        def run(nk):
            lane = lax.broadcasted_iota(jnp.int32, (TM, MLA_HP), 1)
            lane_t = lax.broadcasted_iota(jnp.int32, (nk, MLA_HP), 1)
            lane2 = lax.broadcasted_iota(jnp.int32, (TM, 2), 1)
            lse = lse_ref[0]
            dov, ov = do_ref[...], o_ref[...]
            dkr = jnp.zeros((nk, MLA_HP), F32)
            for hh in range(2):
                kvh, kh, qh, s = _attn_head(q_ref, kv_ref, kr_ref, hh, nk)
                p = jnp.exp(s - jnp.sum(jnp.where(lane2 == hh, lse, 0.0), axis=1, keepdims=True))
                do_pos = jnp.where(lane >= MLA_NOPE, pltpu.roll(dov, MLA_V, 1) if hh == 0 else dov, 0.0)
                o_pos = jnp.where(lane >= MLA_NOPE, pltpu.roll(ov, MLA_V, 1) if hh == 0 else ov, 0.0)
                delta = jnp.sum(do_pos * o_pos, axis=-1, keepdims=True)
                dob = do_pos.astype(BF16)
                dp = lax.dot_general(dob, kvh, (((1,), (1,)), ((), ())), preferred_element_type=F32)
                ds = (p * (dp - delta)).astype(BF16)
                dq_ref[:, hh * MLA_HP:(hh + 1) * MLA_HP] = jnp.dot(ds, kh, preferred_element_type=F32)
                dkf = lax.dot_general(ds, qh, (((0,), (0,)), ((), ())), preferred_element_type=F32)
                dvp = lax.dot_general(p.astype(BF16), dob, (((0,), (0,)), ((), ())), preferred_element_type=F32)
                dkv_ref[0:nk, hh * MLA_HP:(hh + 1) * MLA_HP] += jnp.where(lane_t < MLA_NOPE, dkf, dvp)
                dkr = dkr + jnp.where(lane_t >= MLA_NOPE, dkf, 0.0)
            dkr_ref[0:nk, :] += dkr

        _by_segment(j, lc, t, run)

    tile = lambda b, p, j: (b * tpe + j, p)
    return pl.pallas_call(
        body, name=name, grid=(nb, MLA_H // 2, tpe),
        in_specs=_attn_specs(nb, tpe, t) + [pl.BlockSpec((TM, 2 * MLA_V), tile),
                                            pl.BlockSpec((1, TM, 2), lambda b, p, j: (p, b * tpe + j, 0)),
                                            pl.BlockSpec((TM, 2 * MLA_V), tile)],
        out_specs=[pl.BlockSpec((TM, 2 * MLA_HP), tile),
                   pl.BlockSpec((t, 2 * MLA_HP), lambda b, p, j: (b, p)),
                   pl.BlockSpec((t, MLA_HP), lambda b, p, j: (b, 0))],
        out_shape=(jax.ShapeDtypeStruct((n, MLA_H * MLA_HP), F32), jax.ShapeDtypeStruct((n, MLA_H * MLA_HP), F32),
                   jax.ShapeDtypeStruct((n, MLA_HP), F32)),
        compiler_params=_cp(56, ("arbitrary", "arbitrary", "arbitrary")),
    )(q, kv, kr, o, lse, do)


def _pool_window(ug, pos, seglen, w, transpose):
    t = ug.shape[0]
    cnt = (jnp.minimum(pos + w // 2, seglen) - jnp.maximum(pos - w // 2, 0)).astype(F32)
    if transpose:
        ug = ug / cnt
    acc = jnp.zeros_like(ug)
    for j in range(-(w // 2), w // 2):
        jj = -j if transpose else j
        src = pos + jj
        valid = (src >= 0) & (src < seglen)
        acc = acc + jnp.where(valid, pltpu.roll(ug, (-jj) % t, 0), 0.0)
    return acc if transpose else acc / cnt


def _by_group(g, fn):
    for k, w in enumerate(POOL_WINDOWS):
        pl.when(g == k)(functools.partial(fn, w))


def _pool_specs(t):
    px, pg = PAD_OFF['pool_x'] // LANES, PAD_OFF['pool_gate'] // LANES
    return [pl.BlockSpec((t, LANES), lambda g, b: (b, px + g)),
            pl.BlockSpec((t, LANES), lambda g, b: (b, pg + g)),
            pl.BlockSpec((1, LANES, LANES), lambda g, b: (g, 0, 0)),
            pl.BlockSpec((1, LANES), lambda g, b: (0, g)),
            pl.BlockSpec((t, 1), lambda g, b: (0, 0)), pl.BlockSpec((t, 1), lambda g, b: (0, 0))]


def _pool_fwd(z, pool_w, pool_scale, pos, seglen, nb, name):
    n = z.shape[0]
    t = n // nb

    def body(u_ref, zg_ref, pw_ref, ps_ref, pos_ref, sl_ref, y_ref):
        def run(w):
            u = u_ref[...]
            pooled = _pool_window(u, pos_ref[...], sl_ref[...], w, False) - u
            y_ref[...] = (_bdot(pooled, pw_ref[0]) * ps_ref[...] * _silu(zg_ref[...])).astype(BF16)

        _by_group(pl.program_id(0), run)

    return pl.pallas_call(
        body, name=name, grid=(4, nb), in_specs=_pool_specs(t),
        out_specs=pl.BlockSpec((t, LANES), lambda g, b: (b, g)),
        out_shape=jax.ShapeDtypeStruct((n, 512), BF16),
        compiler_params=_cp(48, ("arbitrary", "arbitrary")),
    )(z, z, pool_w, pool_scale, pos, seglen)


def _pool_bwd(z, dy, pool_w, pool_w_t, pool_scale, pos, seglen, dz, nb, name):
    n = z.shape[0]
    t = n // nb

    def body(u_ref, zg_ref, pw_ref, ps_ref, pos_ref, sl_ref, dy_ref, pwt_ref, dz_in, dz_ref, dpw_ref, dps_ref):
        @pl.when(pl.program_id(1) == 0)
        def _():
            dpw_ref[...] = jnp.zeros(dpw_ref.shape, F32)
            dps_ref[...] = jnp.zeros(dps_ref.shape, F32)

        def run(w):
            u = u_ref[...]
            pos_v, sl_v = pos_ref[...], sl_ref[...]
            pooled = _pool_window(u, pos_v, sl_v, w, False) - u
            mixed = _bdot(pooled, pw_ref[0])
            zg = zg_ref[...]
            sg = _silu(zg)
            ps = ps_ref[...]
            dyv = dy_ref[...]
            dps_ref[...] += _colsum(dyv * mixed * sg)
            dz_ref[:, LANES:2 * LANES] = (dyv * mixed * ps * _dsilu(zg)).astype(BF16)
            dmixed = dyv * ps * sg
            dpw_ref[0] += _bdot_tn(pooled, dmixed)
            dpooled = _bdot(dmixed, pwt_ref[0])
            dz_ref[:, 0:LANES] = (_pool_window(dpooled, pos_v, sl_v, w, True) - dpooled).astype(BF16)

        _by_group(pl.program_id(0), run)

    blk = pl.BlockSpec((t, LANES), lambda g, b: (b, g))
    return pl.pallas_call(
        body, name=name, grid=(4, nb),
        in_specs=_pool_specs(t) + [blk, pl.BlockSpec((1, LANES, LANES), lambda g, b: (g, 0, 0)), DZ_ANY],
        out_specs=[pl.BlockSpec((t, 2 * LANES), lambda g, b: (b, DZ_OFF['pool'] // (2 * LANES) + g)),
                   pl.BlockSpec((1, LANES, LANES), lambda g, b: (g, 0, 0)),
                   pl.BlockSpec((1, LANES), lambda g, b: (0, g))],
        out_shape=(jax.ShapeDtypeStruct(dz.shape, dz.dtype),
                   jax.ShapeDtypeStruct((4, 128, 128), F32), jax.ShapeDtypeStruct((1, 512), F32)),
        input_output_aliases={8: 0},
        compiler_params=_cp(48, ("arbitrary", "arbitrary")),
    )(z, z, pool_w, pool_scale, pos, seglen, dy, pool_w_t, dz)


def _gla_chunk(q_ref, k_ref, a_ref, w2_ref, b2_ref, reverse):
    c = GLA_C
    x = _bdot(a_ref[...], w2_ref[0]) + b2_ref[0]
    la = (jnp.minimum(x, 0.0) - jnp.log(1.0 + jnp.exp(-jnp.abs(x)))) * (1.0 / GLA_TAU)
    row = lax.broadcasted_iota(jnp.int32, (c, c), 0)
    col = lax.broadcasted_iota(jnp.int32, (c, c), 1)
    tri = (col >= row) if reverse else (col <= row)
    tri_t = (col <= row) if reverse else (col >= row)
    b = _xdot(tri.astype(F32), la)
    tok = lax.broadcasted_iota(jnp.int32, la.shape, 0)
    bref = _colsum(jnp.where((tok >= c // 2) if reverse else (tok < c // 2), la, 0.0))
    blast = _colsum(la)
    eq = jnp.exp(jnp.minimum(b - bref, EXP_CLAMP))
    ek = jnp.exp(jnp.minimum(bref - b, EXP_CLAMP))
    qs = q_ref[...] * (GLA_DK ** -0.5)
    kk = k_ref[...]
    eb = jnp.exp(b)
    etail = jnp.exp(blast - b)
    return dict(x=x, la=la, tri=tri, tri_t=tri_t, eq=eq, ek=ek, qs=qs, kk=kk, qd=qs * eq, kd=kk * ek, qe=qs * eb,
                kl=kk * etail, eb=eb, etail=etail)


def _pair(a, p):
    return a[:, p * LANES:(p + 1) * LANES]


def _head_masks():
    lane = lax.broadcasted_iota(jnp.int32, (GLA_C, LANES), 1)
    return (lane < GLA_DK, lane >= GLA_DK)


def _state_decay(la, p):
    return jnp.exp(_xdot_tn(_pair(la, p), jnp.ones((GLA_C, GLA_DV), F32)))


def _gla_chunk_maps(nb, nc, ncc, order):
    def rmap(j):
        return jnp.where(j < ncc, ncc - 1 - j, nc - 1 - (j - ncc))

    if order == 'scan':
        return (lambda b, j: b * nc + j), (lambda b, j: b * nc + rmap(j))
    return (lambda b, j: b * nc + nc - 1 - j), (lambda b, j: b * nc + rmap(nc - 1 - j))


def _gla_in_specs(maps):
    specs = []
    for d, cm in enumerate(maps):
        gate = 'gla_af' if d == 0 else 'gla_ab'
        specs += [pl.BlockSpec((GLA_C, 256), lambda b, j, cm=cm: (cm(b, j), _blk('gla_q'))),
                  pl.BlockSpec((GLA_C, 256), lambda b, j, cm=cm: (cm(b, j), _blk('gla_k'))),
                  pl.BlockSpec((GLA_C, 512), lambda b, j, cm=cm: (cm(b, j), _blk('gla_v'))),
                  pl.BlockSpec((GLA_C, LANES), lambda b, j, cm=cm, gate=gate: (cm(b, j), _blk(gate))),
                  pl.BlockSpec((1, LANES, 256), lambda b, j, d=d: (d, 0, 0)),
                  pl.BlockSpec((1, 1, 256), lambda b, j, d=d: (d, 0, 0))]
    return specs


def _gla_fwd(z, w2, b2, nb, lc, name):
    n = z.shape[0]
    nc = n // nb // GLA_C
    maps = _gla_chunk_maps(nb, nc, lc // GLA_C, 'scan')

    def body(*refs):
        ins, (of_ref, ob_ref, sf_ref, sb_ref, s_sc) = refs[:12], refs[12:]

        @pl.when(pl.program_id(1) == 0)
        def _():
            s_sc[...] = jnp.zeros(s_sc.shape, F32)

        masks = _head_masks()
        for d, (o_ref, st_ref) in enumerate(((of_ref, sf_ref), (ob_ref, sb_ref))):
            q_ref, k_ref, v_ref, a_ref, w2_ref, b2_ref = ins[6 * d:6 * d + 6]
            ch = _gla_chunk(q_ref, k_ref, a_ref, w2_ref, b2_ref, d == 1)
            for p in range(2):
                s_prev = s_sc[d, p]
                st_ref[0, p] = s_prev
                s_new = _state_decay(ch['la'], p) * s_prev
                kd_p = _pair(ch['kd'], p)
                for hh in range(2):
                    h = 2 * p + hh
                    vv = v_ref[:, h * GLA_DV:(h + 1) * GLA_DV]
                    att = jnp.where(ch['tri'], _bdot_nt(jnp.where(masks[hh], _pair(ch['qd'], p), 0.0), kd_p), 0.0)
                    o_ref[:, h * GLA_DV:(h + 1) * GLA_DV] = (
                        _bdot(att, vv) + _bdot(jnp.where(masks[hh], _pair(ch['qe'], p), 0.0), s_prev))
                    s_new = s_new + _dot2(_split(jnp.where(masks[hh], _pair(ch['kl'], p), 0.0)), vv.astype(BF16), TN)
                s_sc[d, p] = s_new

    o_shape = jax.ShapeDtypeStruct((n, 512), F32)
    st_shape = jax.ShapeDtypeStruct((n // GLA_C, 2, LANES, GLA_DV), F32)
    return pl.pallas_call(
        body, name=name, grid=(nb, nc),
        in_specs=_gla_in_specs(maps),
        out_specs=[pl.BlockSpec((GLA_C, 512), lambda b, j: (maps[0](b, j), 0)),
                   pl.BlockSpec((GLA_C, 512), lambda b, j: (maps[1](b, j), 0)),
                   pl.BlockSpec((1, 2, LANES, GLA_DV), lambda b, j: (maps[0](b, j), 0, 0, 0)),
                   pl.BlockSpec((1, 2, LANES, GLA_DV), lambda b, j: (maps[1](b, j), 0, 0, 0))],
        out_shape=(o_shape, o_shape, st_shape, st_shape),
        scratch_shapes=[pltpu.VMEM((2, 2, LANES, GLA_DV), F32)],
        compiler_params=_cp(32, ("arbitrary", "arbitrary")),
    )(z, z, z, z, w2, b2, z, z, z, z, w2, b2)


def _gla_bwd(z, w2, w2_t, b2, st_f, st_b, dog, nb, lc, name):
    n = z.shape[0]
    nc = n // nb // GLA_C
    maps = _gla_chunk_maps(nb, nc, lc // GLA_C, 'back')

    def body(*refs):
        ins, extra, outs, (ds_sc, sfx_sc) = refs[:12], refs[12:18], refs[18:30], refs[30:]

        @pl.when(pl.program_id(1) == 0)
        def _():
            ds_sc[...] = jnp.zeros(ds_sc.shape, F32)
            sfx_sc[...] = jnp.zeros(sfx_sc.shape, F32)

        @pl.when((pl.program_id(0) == 0) & (pl.program_id(1) == 0))
        def _():
            for r in outs[8:12]:
                r[...] = jnp.zeros(r.shape, F32)

        masks = _head_masks()
        for d in range(2):
            q_ref, k_ref, v_ref, a_ref, w2_ref, b2_ref = ins[6 * d:6 * d + 6]
            w2t_ref, st_ref, do_ref = extra[3 * d:3 * d + 3]
            dq_ref, dk_ref, dv_ref, da_ref = outs[4 * d:4 * d + 4]
            dw2_ref, db2_ref = outs[8 + 2 * d], outs[9 + 2 * d]
            ch = _gla_chunk(q_ref, k_ref, a_ref, w2_ref, b2_ref, d == 1)
            dqs, dks, dbs = [], [], []
            for p in range(2):
                s_prev = st_ref[0, p]
                ds_new = ds_sc[d, p]
                qd_p, kd_p, qe_p, kl_p = (_pair(ch[nme], p) for nme in ('qd', 'kd', 'qe', 'kl'))
                ds_prev = _state_decay(ch['la'], p) * ds_new
                qd_b, kd_b = qd_p.astype(BF16), kd_p.astype(BF16)
                sp_s, dsn_s = _split(s_prev), _split(ds_new)
                dq_h, dk_h, db_h = [], [], []
                for hh in range(2):
                    h = 2 * p + hh
                    vv = v_ref[:, h * GLA_DV:(h + 1) * GLA_DV]
                    dov = do_ref[:, h * GLA_DV:(h + 1) * GLA_DV]
                    att = jnp.where(ch['tri'], _bdot_nt(jnp.where(masks[hh], qd_p, 0.0), kd_p), 0.0)
                    dv_ref[:, h * GLA_DV:(h + 1) * GLA_DV] = (
                        _bdot_tn(att, dov) + _bdot(jnp.where(masks[hh], kl_p, 0.0), ds_new))
                    vv_b, dov_b = vv.astype(BF16), dov.astype(BF16)
                    datt_b = jnp.where(ch['tri'], lax.dot_general(dov_b, vv_b, NT, preferred_element_type=F32),
                                       0.0).astype(BF16)
                    dq_in = lax.dot_general(datt_b, kd_b, NN, preferred_element_type=F32)
                    dk_in = lax.dot_general(datt_b, qd_b, TN, preferred_element_type=F32)
                    dq_st = _dot2(dov_b, sp_s, NT) * _pair(ch['eb'], p)
                    dk_st = _dot2(vv_b, dsn_s, NT) * _pair(ch['etail'], p)
                    dq_h.append(dq_in * _pair(ch['eq'], p) + dq_st)
                    dk_h.append(dk_in * _pair(ch['ek'], p) + dk_st)
                    db_h.append((qd_b.astype(F32) * dq_in - kd_b.astype(F32) * dk_in)
                                + (_pair(ch['qs'], p) * dq_st - _pair(ch['kk'], p) * dk_st))
                    ds_prev = ds_prev + _dot2(_split(jnp.where(masks[hh], qe_p, 0.0)), dov_b, TN)
                ds_sc[d, p] = ds_prev
                dqs.append(jnp.where(masks[0], dq_h[0], dq_h[1]))
                dks.append(jnp.where(masks[0], dk_h[0], dk_h[1]))
                dbs.append(jnp.where(masks[0], db_h[0], db_h[1]))
            dq_ref[...] = jnp.concatenate(dqs, axis=1) * (GLA_DK ** -0.5)
            dk_ref[...] = jnp.concatenate(dks, axis=1)
            db = jnp.concatenate(dbs, axis=1)
            dla = _xdot(ch['tri_t'].astype(F32), db) + sfx_sc[d]
            sfx_sc[d] = sfx_sc[d] + _colsum(db)
            dx = dla * (1.0 / GLA_TAU) * _sigmoid(-ch['x'])
            da_ref[...] = _bdot(dx, w2t_ref[0])
            dw2_ref[...] += _bdot_tn(a_ref[...], dx)
            db2_ref[...] += _colsum(dx)

    extra_specs, out_specs = [], []
    for d, cm in enumerate(maps):
        extra_specs += [pl.BlockSpec((1, 256, LANES), lambda b, j, d=d: (d, 0, 0)),
                        pl.BlockSpec((1, 2, LANES, GLA_DV), lambda b, j, cm=cm: (cm(b, j), 0, 0, 0)),
                        pl.BlockSpec((GLA_C, 512), lambda b, j, cm=cm: (cm(b, j), 0))]
        out_specs += [pl.BlockSpec((GLA_C, 256), lambda b, j, cm=cm: (cm(b, j), 0)),
                      pl.BlockSpec((GLA_C, 256), lambda b, j, cm=cm: (cm(b, j), 0)),
                      pl.BlockSpec((GLA_C, 512), lambda b, j, cm=cm: (cm(b, j), 0)),
                      pl.BlockSpec((GLA_C, LANES), lambda b, j, cm=cm: (cm(b, j), 0))]
    const2 = lambda b, j: (0, 0)
    out_specs += [pl.BlockSpec((LANES, 256), const2), pl.BlockSpec((1, 256), const2)] * 2
    per_dir = (jax.ShapeDtypeStruct((n, 256), F32), jax.ShapeDtypeStruct((n, 256), F32),
               jax.ShapeDtypeStruct((n, 512), F32), jax.ShapeDtypeStruct((n, LANES), F32))
    wshape = (jax.ShapeDtypeStruct((LANES, 256), F32), jax.ShapeDtypeStruct((1, 256), F32))
    return pl.pallas_call(
        body, name=name, grid=(nb, nc),
        in_specs=_gla_in_specs(maps) + extra_specs,
        out_specs=out_specs,
        out_shape=per_dir + per_dir + wshape + wshape,
        scratch_shapes=[pltpu.VMEM((2, 2, LANES, GLA_DV), F32), pltpu.VMEM((2, 1, 256), F32)],
        compiler_params=_cp(32, ("arbitrary", "arbitrary")),
    )(z, z, z, z, w2, b2, z, z, z, z, w2, b2, w2_t, st_f, dog, w2_t, st_b, dog)


def _gla_into_dz(dz, dq_f, dq_r, dk_f, dk_r, dv_f, dv_r, da_f, da_r, name):
    n = dq_f.shape[0]
    row = lambda i: (i, 0)
    w256, w512, w128 = (pl.BlockSpec((TM, w), row) for w in (256, 512, 128))
    shp = jax.ShapeDtypeStruct(dz.shape, dz.dtype)

    def v_body(dvf, dvr, dz_in, o_ref):
        o_ref[...] = (dvf[...] + dvr[...]).astype(BF16)

    dz = pl.pallas_call(
        v_body, name=name + "_v", grid=(n // TM,), in_specs=[w512, w512, DZ_ANY],
        out_specs=pl.BlockSpec((TM, 512), lambda i: (i, DZ_OFF['gla_v'] // 512)), out_shape=shp,
        input_output_aliases={2: 0}, compiler_params=_cp(32, ("arbitrary",)))(dv_f, dv_r, dz)

    def qk_body(dqf, dqr, dkf, dkr, daf, dar, dz_in, o_ref):
        o_ref[:, 0:256] = (dqf[...] + dqr[...]).astype(BF16)
        o_ref[:, 256:512] = (dkf[...] + dkr[...]).astype(BF16)
        o_ref[:, 512:640] = daf[...].astype(BF16)
        o_ref[:, 640:768] = dar[...].astype(BF16)

    return pl.pallas_call(
        qk_body, name=name + "_qk", grid=(n // TM,), in_specs=[w256] * 4 + [w128] * 2 + [DZ_ANY],
        out_specs=pl.BlockSpec((TM, 768), lambda i: (i, DZ_OFF['gla_q'] // 768)), out_shape=shp,
        input_output_aliases={6: 0}, compiler_params=_cp(32, ("arbitrary",)))(dq_f, dq_r, dk_f, dk_r, da_f, da_r, dz)


def _gla_out_norm(og):
    hats, rs = [], []
    for h in range(GLA_H):
        seg = og[:, h * GLA_DV:(h + 1) * GLA_DV]
        r = _rstd(seg)
        hats.append(seg * r)
        rs.append(r)
    return hats, rs


def _merge_branches(zm_ref, zgm_ref, zgg_ref, om_ref, yp_ref, ogf_ref, ogb_ref, gn_ref, wbm_ref, wbp_ref, wbg_ref):
    zgm, zgg = zgm_ref[...], zgg_ref[...]
    om = om_ref[...]
    y_mla = om * _silu(zgm)
    hats, rs = _gla_out_norm(ogf_ref[...] + ogb_ref[...])
    gn = gn_ref[...]
    sgg = _silu(zgg)
    y_gla = jnp.concatenate([hats[h] * gn for h in range(GLA_H)], axis=1) * sgg
    ys = (y_mla, yp_ref[...], y_gla)
    ps = (_bdot(y_mla, wbm_ref[...]), jnp.dot(yp_ref[...], wbp_ref[...], preferred_element_type=F32),
          _bdot(y_gla, wbg_ref[...]))
    zm = zm_ref[...]
    gs = tuple(_sigmoid(zm[:, a * D:(a + 1) * D]) for a in range(3))
    merged = gs[0] * ps[0] + gs[1] * ps[1] + gs[2] * ps[2]
    return dict(zgm=zgm, zgg=zgg, om=om, hats=hats, rs=rs, gn=gn, sgg=sgg, ys=ys, ps=ps, gs=gs, merged=merged)


def _merge_in_specs(tpe):
    row = lambda i: (i, 0)
    const = lambda i: (0, 0)
    return [pl.BlockSpec((TM, 3 * D), lambda i: (i, _blk('merge'))),
            pl.BlockSpec((TM, 512), lambda i: (i, _blk('mla_gate'))),
            pl.BlockSpec((TM, 512), lambda i: (i, _blk('gla_gate'))),
            pl.BlockSpec((TM, 512), row), pl.BlockSpec((TM, 512), row), pl.BlockSpec((TM, 512), row),
            pl.BlockSpec((TM, 512), row), pl.BlockSpec((1, 128), const),
            pl.BlockSpec((512, D), const), pl.BlockSpec((512, D), const), pl.BlockSpec((512, D), const),
            pl.BlockSpec((1, 1, 3 * D), lambda i: (_modrow(i, tpe), 0, 0)), pl.BlockSpec((1, D), const)]


def _merge_fwd(x, z, o_mla, y_pool, ogf, ogb, gla_n, wbm, wbp, wbg, wout, modl, post_g, tpe, name):
    n = x.shape[0]

    def body(zm_ref, zgm_ref, zgg_ref, om_ref, yp_ref, ogf_ref, ogb_ref, gn_ref, wbm_ref, wbp_ref, wbg_ref,
             m_ref, pg_ref, x_ref, wo_ref, xn_ref, out_ref):
        mb = _merge_branches(zm_ref, zgm_ref, zgg_ref, om_ref, yp_ref, ogf_ref, ogb_ref, gn_ref,
                             wbm_ref, wbp_ref, wbg_ref)
        out = _bdot(mb['merged'], wo_ref[...])
        gate = m_ref[0][:, 2 * D:3 * D]
        xn_ref[...] = x_ref[...] + gate * (out * _rstd(out) * pg_ref[...])
        out_ref[...] = out

    row = lambda i: (i, 0)
    return pl.pallas_call(
        body, name=name, grid=(n // TM,),
        in_specs=_merge_in_specs(tpe) + [pl.BlockSpec((TM, D), row), pl.BlockSpec((D, D), lambda i: (0, 0))],
        out_specs=[pl.BlockSpec((TM, D), row), pl.BlockSpec((TM, D), row)],
        out_shape=(jax.ShapeDtypeStruct((n, D), F32), jax.ShapeDtypeStruct((n, D), F32)),
        compiler_params=_cp(48, ("arbitrary",)),
    )(z, z, z, o_mla, y_pool, ogf, ogb, gla_n, wbm, wbp, wbg, modl.reshape(8, 1, 3 * D), post_g, x, wout)


def _merge_bwd(dxn, out, z, o_mla, y_pool, ogf, ogb, gla_n, wbm, wbp, wbg, wbm_t, wbp_t, wbg_t, wout_t,
               modl, post_g, tpe, name):
    n = out.shape[0]
    nt = n // TM

    def body(zm_ref, zgm_ref, zgg_ref, om_ref, yp_ref, ogf_ref, ogb_ref, gn_ref, wbm_ref, wbp_ref, wbg_ref,
             m_ref, pg_ref, dxn_ref, out_ref, wbmt_ref, wbpt_ref, wbgt_ref, wot_ref,
             dz_ref, dom_ref, dyp_ref, dog_ref, st_ref,
             dwbm_ref, dwbp_ref, dwbg_ref, dwo_ref, dgn_ref):
        @pl.when(pl.program_id(0) == 0)
        def _():
            for r in (dwbm_ref, dwbp_ref, dwbg_ref, dwo_ref, dgn_ref):
                r[...] = jnp.zeros(r.shape, F32)

        mb = _merge_branches(zm_ref, zgm_ref, zgg_ref, om_ref, yp_ref, ogf_ref, ogb_ref, gn_ref,
                             wbm_ref, wbp_ref, wbg_ref)
        out = out_ref[...]
        r2 = _rstd(out)
        on = out * r2
        pg = pg_ref[...]
        gate = m_ref[0][:, 2 * D:3 * D]
        dxn_v = dxn_ref[...]
        st_ref[0, 0:1, :] = _colsum(dxn_v * on * pg)
        st_ref[0, 1:2, :] = _colsum(dxn_v * gate * on)
        st_ref[0, 2:8, :] = jnp.zeros((6, D), F32)
        dout = _norm_bwd(on, r2, dxn_v * gate * pg)
        dwo_ref[...] += _bdot_tn(mb['merged'], dout)
        dmerged = _bdot(dout, wot_ref[...])
        dys = []
        for a, (dw_ref, wt_ref) in enumerate(((dwbm_ref, wbmt_ref), (dwbp_ref, wbpt_ref), (dwbg_ref, wbgt_ref))):
            g = mb['gs'][a]
            dz_ref[:, a * D:(a + 1) * D] = (dmerged * mb['ps'][a] * g * (1.0 - g)).astype(BF16)
            dp = dmerged * g
            dw_ref[...] += _bdot_tn(mb['ys'][a], dp)
            dys.append(_bdot(dp, wt_ref[...]))
        dom_ref[...] = dys[0] * _silu(mb['zgm'])
        dz_ref[:, 3 * D:3 * D + 512] = (dys[0] * mb['om'] * _dsilu(mb['zgm'])).astype(BF16)
        dyp_ref[...] = dys[1]
        gn = mb['gn']
        dgn = jnp.zeros((1, GLA_DV), F32)
        dzgg, dog = [], []
        for h in range(GLA_H):
            sl = slice(h * GLA_DV, (h + 1) * GLA_DV)
            dyg = dys[2][:, sl]
            hat = mb['hats'][h]
            dzgg.append(dyg * hat * gn * _dsilu(mb['zgg'][:, sl]))
            dn = dyg * mb['sgg'][:, sl]
            dgn = dgn + _colsum(dn * hat)
            dog.append(_norm_bwd(hat, mb['rs'][h], dn * gn))
        dgn_ref[...] += dgn
        dz_ref[:, 3 * D + 512:4 * D] = jnp.concatenate(dzgg, axis=1).astype(BF16)
        dog_ref[...] = jnp.concatenate(dog, axis=1)

    row = lambda i: (i, 0)
    const = lambda i: (0, 0)
    wspec = pl.BlockSpec((512, D), const)
    wtspec = pl.BlockSpec((D, 512), const)
    return pl.pallas_call(
        body, name=name, grid=(nt,),
        in_specs=_merge_in_specs(tpe) + [pl.BlockSpec((TM, D), row), pl.BlockSpec((TM, D), row),
                                         wtspec, wtspec, wtspec,
                                         pl.BlockSpec((D, D), const)],
        out_specs=[pl.BlockSpec((TM, 4 * D), row), pl.BlockSpec((TM, 512), row),
                   pl.BlockSpec((TM, 512), row), pl.BlockSpec((TM, 512), row),
                   pl.BlockSpec((1, 8, D), lambda i: (i, 0, 0)),
                   wspec, wspec, wspec, pl.BlockSpec((D, D), const), pl.BlockSpec((1, 128), const)],
        out_shape=(jax.ShapeDtypeStruct((n, D_PAD), BF16), jax.ShapeDtypeStruct((n, 512), F32),
                   jax.ShapeDtypeStruct((n, 512), F32), jax.ShapeDtypeStruct((n, 512), F32),
                   jax.ShapeDtypeStruct((nt, 8, D), F32),
                   jax.ShapeDtypeStruct((512, D), F32), jax.ShapeDtypeStruct((512, D), F32),
                   jax.ShapeDtypeStruct((512, D), F32), jax.ShapeDtypeStruct((D, D), F32),
                   jax.ShapeDtypeStruct((1, 128), F32)),
        compiler_params=_cp(56, ("arbitrary",)),
    )(z, z, z, o_mla, y_pool, ogf, ogb, gla_n, wbm, wbp, wbg, modl.reshape(8, 1, 3 * D), post_g,
      dxn, out, wbm_t, wbp_t, wbg_t, wout_t)


def _loss_grad(xf, tgt, nb, tpe):
    n = xf.shape[0]

    def body(x_ref, t_ref, dx_ref, l_ref):
        j = pl.program_id(1)
        d = x_ref[...] - t_ref[...]
        latent = j > 0
        dx_ref[...] = jnp.where(latent, d * (1.0 / D), 0.0)
        l_ref[...] = jnp.full(l_ref.shape, jnp.where(latent, 0.5 / D * jnp.sum(d * d), 0.0), F32)

    return pl.pallas_call(
        body, name="loss_grad", grid=(nb, tpe),
        in_specs=[pl.BlockSpec((TM, D), lambda b, j: (b * tpe + j, 0)),
                  pl.BlockSpec((TM, D), lambda b, j: (b * (tpe - 1) + jnp.maximum(j - 1, 0), 0))],
        out_specs=[pl.BlockSpec((TM, D), lambda b, j: (b * tpe + j, 0)),
                   pl.BlockSpec((1, 8, 128), lambda b, j: (b * tpe + j, 0, 0))],
        out_shape=(jax.ShapeDtypeStruct((n, D), F32), jax.ShapeDtypeStruct((n // TM, 8, 128), F32)),
        compiler_params=_cp(32, ("arbitrary", "arbitrary")),
    )(xf, tgt)


def _to_padded(w_nat):
    parts = []
    for nme in PAD_ORDER:
        p = w_nat[NAT_OFF[nme]:NAT_OFF[nme] + NAT_SIZE[nme]]
        if SLAB[nme] > NAT_SIZE[nme]:
            p = jnp.pad(p, [(IN_SLAB[nme], SLAB[nme] - NAT_SIZE[nme] - IN_SLAB[nme]), (0, 0)])
        parts.append(p)
    return jnp.concatenate(parts, axis=0)


def _to_dz(w_nat):
    parts = []
    for nme, off, size in DZ_PARTS:
        p = w_nat[NAT_OFF[nme] + off:NAT_OFF[nme] + off + size]
        if size < LANES:
            p = jnp.pad(p, [(IN_SLAB[nme], LANES - size - IN_SLAB[nme]), (0, 0)])
        parts.append(p)
    return jnp.concatenate(parts, axis=0)


def _from_dz(dw):
    found, pos = {}, 0
    for nme, off, size in DZ_PARTS:
        start = pos + (IN_SLAB[nme] if size < LANES else 0)
        found.setdefault(nme, []).append(dw[start:start + size])
        pos += max(size, LANES)
    return jnp.concatenate([p for nme in IN_NAMES for p in found[nme]], axis=0)


def _rope_tables(lc, l):
    half = MLA_ROPE // 2
    inv = ROPE_BASE ** (-jnp.arange(0, half, 2, dtype=F32) / half)
    tok = jnp.arange(l)
    ang_r = (tok // GRID_W).astype(F32)[:, None] * inv
    ang_c = (tok % GRID_W).astype(F32)[:, None] * inv
    ang = jnp.concatenate([ang_r, ang_r, ang_c, ang_c], axis=-1)
    cos = jnp.concatenate([jnp.ones((lc, MLA_ROPE), F32), jnp.cos(ang)], axis=0)
    sin = jnp.concatenate([jnp.zeros((lc, MLA_ROPE), F32), jnp.sin(ang)], axis=0)
    t = lc + l
    tail = MLA_HP - MLA_QK
    ck = jnp.concatenate([jnp.ones((t, MLA_NOPE), F32), cos, jnp.ones((t, tail), F32)], axis=1)
    sk = jnp.concatenate([jnp.zeros((t, MLA_NOPE), F32), sin, jnp.zeros((t, tail), F32)], axis=1)
    return jnp.tile(ck, (1, MLA_H)), jnp.tile(sk, (1, MLA_H)), ck, sk


def _pad_heads(w):
    lead = w.shape[:-1]
    w = w.reshape(lead + (MLA_H, MLA_QK))
    return jnp.pad(w, [(0, 0)] * len(lead) + [(0, 0), (0, MLA_HP - MLA_QK)]).reshape(lead + (MLA_H * MLA_HP,))


def _unpad_heads(w):
    lead = w.shape[:-1]
    return w.reshape(lead + (MLA_H, MLA_HP))[..., :MLA_QK].reshape(lead + (MLA_H * MLA_QK,))


def _local_step(x, c, ctx, tgt, wf):
    nb, l, _ = x.shape
    lc = ctx.shape[1]
    assert lc == TM and l % TM == 0
    t = lc + l
    tpe = t // TM
    n = nb * t
    nt = n // TM
    bf = lambda a: a.astype(BF16)

    xs = jnp.concatenate([ctx, x], axis=1).reshape(n, D)
    assert nb <= 4
    cv = jnp.concatenate([c, jnp.zeros((4 - nb, D), F32), wf['c_ctx'][None, :], jnp.zeros((3, D), F32)], axis=0)
    mod_w_b = bf(wf['mod_w'])
    mod_all = _mod_fwd(cv, mod_w_b, wf['mod_b'].reshape(DEPTH, 1, 3 * D))
    cq, sq, ck, sk = _rope_tables(lc, l)
    pos = jnp.concatenate([jnp.arange(lc), jnp.arange(l)]).astype(jnp.int32)[:, None]
    seglen = jnp.concatenate([jnp.full((lc,), lc), jnp.full((l,), l)]).astype(jnp.int32)[:, None]
    tiles = np.arange(nt)
    ntp = -(-nt // LANES) * LANES
    sel = np.zeros((8, ntp), np.float32)
    sel[np.where(tiles % tpe == 0, 4, tiles // tpe), tiles] = 1.0
    sel = jnp.asarray(sel)

    def tile_sums(st):
        return jnp.pad(st.transpose(1, 0, 2), ((0, 0), (0, ntp - nt), (0, 0)))

    lw = []
    for ly in range(DEPTH):
        w_in_t = _to_padded(bf(wf['w_in'][ly]))
        w_in_dz = _to_dz(bf(wf['w_in'][ly]))
        w_uq_p = _pad_heads(bf(wf['mla_w_uq'][ly]))
        w2 = jnp.pad(jnp.stack([bf(wf['gla_af_w2'][ly]), bf(wf['gla_ab_w2'][ly])]),
                     ((0, 0), (0, LANES - GLA_RANK), (0, 0)))
        lw.append(dict(
            w_in_t=w_in_t, w_in_dz=w_in_dz,
            w_uq=w_uq_p, w_uq_t=w_uq_p.T,
            w_ukv=bf(wf['mla_w_ukv'][ly]), w_ukv_t=bf(wf['mla_w_ukv'][ly]).T,
            pool_w=bf(wf['pool_w'][ly]), pool_w_t=bf(wf['pool_w'][ly]).transpose(0, 2, 1),
            w2=w2, w2_t=w2.transpose(0, 2, 1),
            b2=jnp.stack([wf['gla_af_b'][ly], wf['gla_ab_b'][ly]]).reshape(2, 1, GLA_H * GLA_DK),
            wbm=bf(wf['w_branch_mla'][ly]), wbp=bf(wf['w_branch_pool'][ly]), wbg=bf(wf['w_branch_gla'][ly]),
            wout=bf(wf['w_out'][ly]),
            wbm_t=bf(wf['w_branch_mla'][ly]).T, wbp_t=bf(wf['w_branch_pool'][ly]).T,
            wbg_t=bf(wf['w_branch_gla'][ly]).T, wout_t=bf(wf['w_out'][ly]).T,
            pre_g=wf['pre_norm'][ly][None, :], post_g=wf['post_norm'][ly][None, :],
            q_norm=wf['mla_q_norm'][ly][None, :], kv_norm=wf['mla_kv_norm'][ly][None, :],
            pool_scale=wf['pool_scale'][ly][None, :], gla_norm=wf['gla_norm'][ly][None, :]))

    saved = []
    xcur = xs
    for ly in range(DEPTH):
        w = lw[ly]
        z, h = _pre_fwd(xcur, mod_all[ly], w['pre_g'], w['w_in_t'], tpe, f"pre_fwd{ly}")
        qb, kvb, krb = _mla_pre(z, w['q_norm'], w['kv_norm'], w['w_uq'], w['w_ukv'], cq, sq, ck, sk, tpe, f"mla_pre{ly}")
        o_mla, lse = _attn_fwd(qb, kvb, krb, nb, lc, f"attn_fwd{ly}")
        y_pool = _pool_fwd(z, w['pool_w'], w['pool_scale'], pos, seglen, nb, f"pool_fwd{ly}")
        ogf, ogb, st_f, st_r = _gla_fwd(z, w['w2'], w['b2'], nb, lc, f"gla_fwd{ly}")
        xnew, out = _merge_fwd(xcur, z, o_mla, y_pool, ogf, ogb, w['gla_norm'], w['wbm'], w['wbp'], w['wbg'],
                               w['wout'], mod_all[ly], w['post_g'], tpe, f"merge_fwd{ly}")
        saved.append(dict(x=xcur, z=z, h=h, qb=qb, kvb=kvb, krb=krb, lse=lse, o_mla=o_mla, y_pool=y_pool,
                          st_f=st_f, st_r=st_r, ogf=ogf, ogb=ogb, out=out))
        xcur = xnew

    dxcur, lparts = _loss_grad(xcur, tgt.reshape(nb * l, D), nb, tpe)
    loss = jnp.sum(lparts[:, 0, 0])

    g = {k: [None] * DEPTH for k in WEIGHTS if k != 'c_ctx'}
    dcv = jnp.zeros((8, D), F32)
    dcc = None
    for ly in reversed(range(DEPTH)):
        w, s = lw[ly], saved[ly]
        (dz, dom, dyp, dog, st_b, g['w_branch_mla'][ly], g['w_branch_pool'][ly], g['w_branch_gla'][ly],
         g['w_out'][ly], dgn) = _merge_bwd(
            dxcur, s['out'], s['z'], s['o_mla'], s['y_pool'], s['ogf'], s['ogb'], w['gla_norm'], w['wbm'], w['wbp'],
            w['wbg'], w['wbm_t'], w['wbp_t'], w['wbg_t'], w['wout_t'], mod_all[ly], w['post_g'], tpe,
            f"merge_bwd{ly}")
        g['gla_norm'][ly] = dgn[0]
        dq, dkv, dkr = _attn_bwd(s['qb'], s['kvb'], s['krb'], s['o_mla'], s['lse'], dom, nb, lc, f"attn_bwd{ly}")
        dz, dwq, g['mla_w_ukv'][ly], dgq, dgkv = _mla_pre_bwd(
            s['z'], dq, dkv, dkr, w['q_norm'], w['kv_norm'], w['w_uq_t'], w['w_ukv_t'], cq, sq, ck, sk, dz, tpe,
            f"mla_pre_bwd{ly}")
        g['mla_w_uq'][ly] = _unpad_heads(dwq)
        g['mla_q_norm'][ly], g['mla_kv_norm'][ly] = dgq[0], dgkv[0]
        dz, g['pool_w'][ly], dps = _pool_bwd(s['z'], dyp, w['pool_w'], w['pool_w_t'], w['pool_scale'],
                                             pos, seglen, dz, nb, f"pool_bwd{ly}")
        g['pool_scale'][ly] = dps[0]
        (dq_f, dk_f, dv_f, da_f, dq_r, dk_r, dv_r, da_r, dw2_f, db2_f, dw2_r, db2_r) = _gla_bwd(
            s['z'], w['w2'], w['w2_t'], w['b2'], s['st_f'], s['st_r'], dog, nb, lc, f"gla_bwd{ly}")
        dz = _gla_into_dz(dz, dq_f, dq_r, dk_f, dk_r, dv_f, dv_r, da_f, da_r, f"gla_dz{ly}")
        g['gla_af_w2'][ly], g['gla_ab_w2'][ly] = dw2_f[:GLA_RANK], dw2_r[:GLA_RANK]
        g['gla_af_b'][ly], g['gla_ab_b'][ly] = db2_f[0], db2_r[0]
        dxcur, st_a = _pre_bwd(dz, w['w_in_dz'], s['x'], dxcur, mod_all[ly], w['pre_g'], tpe, f"pre_bwd{ly}",
                               latent_only=(ly == 0))
        tk = next(k for k in (3072, 1024, 512, TM) if n % k == 0)
        g['w_in'][ly] = _from_dz(_matmul_tn(dz, s['h'], 1152, tk, f"w_in_grad{ly}"))
        dmw, dmb, dcv, dcc, dpre, dpost = _mod_bwd(cv, sel, tile_sums(st_a), tile_sums(st_b),
                                                   mod_w_b[ly].T, dcv, f"mod_bwd{ly}")
        g['mod_w'][ly], g['mod_b'][ly] = dmw, dmb[0]
        g['pre_norm'][ly], g['post_norm'][ly] = dpre[0], dpost[0]

    grads = {k: jnp.stack(v) for k, v in g.items()}
    grads['c_ctx'] = dcc[4]
    return loss, dxcur.reshape(nb, l, D), grads


def _place():
    x, y, c = lax.axis_index("x"), lax.axis_index("y"), lax.axis_index("c")
    chips = [(1 - x, y), (x, 1 - y), (1 - x, 1 - y)]
    return x, y, c, chips


def _hbm_call(body, name, out_shape, n_in, sems):
    any_spec = pl.BlockSpec(memory_space=pl.ANY)
    return pl.pallas_call(body, name=name, out_shape=out_shape, in_specs=[any_spec] * n_in,
                          out_specs=jax.tree.map(lambda _: any_spec, out_shape), scratch_shapes=sems)


def _all_gather_shards(ws):
    n = len(ws)

    def body(*refs):
        ins, outs, (send_sems, recv_sems) = refs[:n], refs[n:2 * n], refs[2 * n:]
        x, y, c, chips = _place()

        def copy(k, q, chip, half, to, src=None):
            dst = outs[k].at[2 * chip[0] + chip[1], half]
            return pltpu.make_async_remote_copy(src_ref=dst if src is None else src, dst_ref=dst,
                                                send_sem=send_sems.at[k, q], recv_sem=recv_sems.at[k, q],
                                                device_id=to, device_id_type=MESH)

        first = [copy(k, j, (x, y), c, (*chip, c), src=ins[k].at[c]) for k in range(n) for j, chip in enumerate(chips)]
        for cp in first:
            cp.start()
        passed = []
        for k in range(n):
            for j, chip in enumerate(chips):
                copy(k, j, chip, c, (x, y, c)).wait_recv()
                passed.append(copy(k, 3 + j, chip, c, (x, y, 1 - c)))
                passed[-1].start()
        for k in range(n):
            for j, chip in enumerate(chips):
                copy(k, 3 + j, chip, 1 - c, (x, y, 1 - c)).wait_recv()
        for cp in first + passed:
            cp.wait_send()

    shapes = tuple(jax.ShapeDtypeStruct((N_CHIPS,) + w.shape, w.dtype) for w in ws)
    return _hbm_call(body, "all_gather_shards", shapes, n,
                     [pltpu.SemaphoreType.DMA((n, 6)), pltpu.SemaphoreType.DMA((n, 6))])(*ws)


def _to_sibling(arrs, other_layer, name):
    n = len(arrs)

    def body(*refs):
        ins, outs, (send_sems, recv_sems) = refs[:n], refs[n:2 * n], refs[2 * n:]
        x, y, c, _ = _place()
        cps = [pltpu.make_async_remote_copy(src_ref=ins[k].at[1 - c] if other_layer else ins[k], dst_ref=outs[k],
                                            send_sem=send_sems.at[k], recv_sem=recv_sems.at[k],
                                            device_id=(x, y, 1 - c), device_id_type=MESH) for k in range(n)]
        for cp in cps:
            cp.start()
        for cp in cps:
            cp.wait()

    shapes = tuple(jax.ShapeDtypeStruct(a.shape[1:] if other_layer else a.shape, a.dtype) for a in arrs)
    return _hbm_call(body, name, shapes, n, [pltpu.SemaphoreType.DMA((n,)), pltpu.SemaphoreType.DMA((n,))])(*arrs)


def _scatter_to_chips(hs):
    n = len(hs)

    def body(*refs):
        ins, outs, (send_sems, recv_sems) = refs[:n], refs[n:2 * n], refs[2 * n:]
        x, y, c, chips = _place()
        me = 2 * x + y
        sends = []
        for k in range(n):
            for j, chip in enumerate(chips):
                cp = pltpu.make_async_remote_copy(src_ref=ins[k].at[2 * chip[0] + chip[1]], dst_ref=outs[k].at[me],
                                                  send_sem=send_sems.at[k, j], recv_sem=recv_sems.at[k, j],
                                                  device_id=(*chip, c), device_id_type=MESH)
                cp.start()
                sends.append(cp)
        for k in range(n):
            for j, chip in enumerate(chips):
                dst = outs[k].at[2 * chip[0] + chip[1]]
                pltpu.make_async_remote_copy(src_ref=dst, dst_ref=dst, send_sem=send_sems.at[k, j],
                                             recv_sem=recv_sems.at[k, j], device_id=(*chip, c),
                                             device_id_type=MESH).wait_recv()
        for cp in sends:
            cp.wait_send()

    shapes = tuple(jax.ShapeDtypeStruct(h.shape, h.dtype) for h in hs)
    return _hbm_call(body, "scatter_to_chips", shapes, n,
                     [pltpu.SemaphoreType.DMA((n, 3)), pltpu.SemaphoreType.DMA((n, 3))])(*hs)


BLOCK_BYTES = 10 * 1024 * 1024


def _blocks(r, cols, pos_bytes):
    rows = sorted({d for d in range(8, r + 1, 8) if r % d == 0} | {r})
    wide = sorted({d for d in range(LANES, cols + 1, LANES) if cols % d == 0} | {cols})
    fits = [(br * bc, bc, br) for br in rows for bc in wide if br * bc * pos_bytes <= BLOCK_BYTES]
    if not fits:
        return rows[0], wide[0]
    _, bc, br = max(fits)
    return br, bc


def _add_cores(b, got, name):
    _, ns, r, cols = b.shape
    br, bc = _blocks(r, cols, 2 * 4 + 4 + 2)

    def body(b_ref, g_ref, o_ref):
        mine = jnp.where(lax.axis_index("c") == 0, b_ref[0, 0], b_ref[1, 0])
        o_ref[0] = (mine + g_ref[0]).astype(BF16)

    spec = pl.BlockSpec((1, br, bc), lambda i, j, k: (i, j, k))
    return pl.pallas_call(body, name=name, grid=(ns, r // br, cols // bc),
                          in_specs=[pl.BlockSpec((2, 1, br, bc), lambda i, j, k: (0, i, j, k)), spec], out_specs=spec,
                          out_shape=jax.ShapeDtypeStruct((ns, r, cols), BF16), compiler_params=_cp(40))(b, got)


def _sum_chips(own, got, name):
    _, r, cols = own.shape
    br, bc = _blocks(r, cols, 2 * N_CHIPS * 2 + 4)

    def body(own_ref, got_ref, o_ref):
        me = 2 * lax.axis_index("x") + lax.axis_index("y")
        part = [jnp.where(me == j, own_ref[j], got_ref[j]).astype(F32) for j in range(N_CHIPS)]
        o_ref[...] = ((part[0] + part[1]) + part[2]) + part[3]

    spec = pl.BlockSpec((N_CHIPS, br, bc), lambda j, k: (0, j, k))
    return pl.pallas_call(body, name=name, grid=(r // br, cols // bc), in_specs=[spec, spec],
                          out_specs=pl.BlockSpec((br, bc), lambda j, k: (j, k)),
                          out_shape=jax.ShapeDtypeStruct((r, cols), F32), compiler_params=_cp(40))(own, got)


def _adamw(w, g_mine, g_other, m, v, name):
    _, r, cols = w.shape
    br, bc = _blocks(r, cols, 9 * 4)

    def body(w_ref, gm_ref, go_ref, m_ref, v_ref, g_ref, d_ref, nm_ref, nv_ref):
        gv = jnp.where(pl.program_id(0) == lax.axis_index("c"), gm_ref[...], go_ref[...])
        m2 = ADAM_B1 * m_ref[0] + (1.0 - ADAM_B1) * gv
        v2 = ADAM_B2 * v_ref[0] + (1.0 - ADAM_B2) * jnp.square(gv)
        m_hat = m2 / (1.0 - ADAM_B1 ** ADAM_STEP)
        v_hat = v2 / (1.0 - ADAM_B2 ** ADAM_STEP)
        g_ref[0] = gv
        d_ref[0] = -ADAM_LR * (m_hat / (jnp.sqrt(v_hat) + ADAM_EPS) + ADAM_WD * w_ref[0])
        nm_ref[0] = m2
        nv_ref[0] = v2

    lay = pl.BlockSpec((1, br, bc), lambda l, j, k: (l, j, k))
    flat = pl.BlockSpec((br, bc), lambda l, j, k: (j, k))
    shp = jax.ShapeDtypeStruct(w.shape, F32)
    return pl.pallas_call(body, name=name, grid=(2, r // br, cols // bc), in_specs=[lay, flat, flat, lay, lay],
                          out_specs=[lay] * 4, out_shape=(shp,) * 4, compiler_params=_cp(40))(w, g_mine, g_other, m, v)


def _pack_small(ts):
    flat = jnp.concatenate([ts[k].reshape(DEPTH, -1) for k in REPLICATED], axis=1)
    return flat.reshape(DEPTH, flat.shape[1] // LANES, LANES)


def _unpack_small(packed, like):
    flat = packed.reshape(DEPTH, -1)
    out, off = {}, 0
    for k in REPLICATED:
        size = like[k].size // DEPTH
        out[k] = flat[:, off:off + size].reshape(like[k].shape)
        off += size
    return out


def _shard_major(a, axis):
    if axis == 1:
        return a.reshape(DEPTH, N_CHIPS, a.shape[1] // N_CHIPS, a.shape[2])
    return a.reshape(DEPTH, a.shape[1], N_CHIPS, a.shape[2] // N_CHIPS).transpose(0, 2, 1, 3)


def kernel(x, c, ctx, c_ctx, mod_w, mod_b, pre_norm, post_norm, w_in, mla_q_norm, mla_w_uq, mla_kv_norm, mla_w_ukv, pool_w, pool_scale, gla_af_w2, gla_af_b, gla_ab_w2, gla_ab_b, gla_norm, w_branch_mla, w_branch_pool, w_branch_gla, w_out, loss_target, m_c_ctx, m_mod_w, m_mod_b, m_pre_norm, m_post_norm, m_w_in, m_mla_q_norm, m_mla_w_uq, m_mla_kv_norm, m_mla_w_ukv, m_pool_w, m_pool_scale, m_gla_af_w2, m_gla_af_b, m_gla_ab_w2, m_gla_ab_b, m_gla_norm, m_w_branch_mla, m_w_branch_pool, m_w_branch_gla, m_w_out, v_c_ctx, v_mod_w, v_mod_b, v_pre_norm, v_post_norm, v_w_in, v_mla_q_norm, v_mla_w_uq, v_mla_kv_norm, v_mla_w_ukv, v_pool_w, v_pool_scale, v_gla_af_w2, v_gla_af_b, v_gla_ab_w2, v_gla_ab_b, v_gla_norm, v_w_branch_mla, v_w_branch_pool, v_w_branch_gla, v_w_out):
    given = dict(locals())
    wts = {k: given[k] for k in WEIGHTS}
    my_chip = 2 * lax.axis_index("x") + lax.axis_index("y")

    view = lambda k, a: jnp.swapaxes(a, 1, 2) if k == 'w_in' else a
    axes = {k: (3 - axis if k == 'w_in' else axis) for k, axis in SHARDED}

    mine = [view(k, wts[k]).astype(BF16) for k, _ in SHARDED]
    gathered = _all_gather_shards(mine)
    full = dict(wts)
    for (k, _), own, got in zip(SHARDED, mine, gathered):
        full[k] = jnp.concatenate([jnp.where(my_chip == s, own, got[s]) for s in range(N_CHIPS)], axis=axes[k])

    loss_local, grad_x, grads = _local_step(x, c, ctx, loss_target, full)
    loss = lax.psum(loss_local, ("x", "y", "c"))

    small = _pack_small(grads)
    bufs = [_shard_major(grads[k], axes[k]) for k, _ in SHARDED]
    bufs.append(jnp.broadcast_to(small[:, None], (DEPTH, N_CHIPS) + small.shape[1:]))
    got = _to_sibling(bufs, True, "swap_halves")
    chip_sum = [_add_cores(b, g, f"add_cores{i}") for i, (b, g) in enumerate(zip(bufs, got))]
    recv = _scatter_to_chips(chip_sum)
    mine_red = [_sum_chips(cs, rc, f"sum_chips{i}") for i, (cs, rc) in enumerate(zip(chip_sum, recv))]
    other_red = _to_sibling(mine_red, False, "join_halves")

    outs = {}
    for i, (k, _) in enumerate(SHARDED):
        res = _adamw(view(k, wts[k]), mine_red[i], other_red[i], view(k, given['m_' + k]), view(k, given['v_' + k]),
                     f"adamw{i}")
        outs[k] = tuple(view(k, r) for r in res)
    packed = _adamw(_pack_small(wts), mine_red[-1], other_red[-1],
                    _pack_small({k: given['m_' + k] for k in REPLICATED}),
                    _pack_small({k: given['v_' + k] for k in REPLICATED}), "adamw_small")
    unpacked = [_unpack_small(p, wts) for p in packed]
    for k in REPLICATED:
        outs[k] = tuple(u[k] for u in unpacked)
    return (loss, grad_x, *[outs[k][q] for q in range(4) for k in WEIGHTS])
```

```python
import functools

import numpy as np
import jax
import jax.numpy as jnp
from jax import lax
from jax.experimental import pallas as pl
from jax.experimental.pallas import tpu as pltpu

F32 = jnp.float32
BF16 = jnp.bfloat16
HIGHEST = lax.Precision.HIGHEST
MESH = pl.DeviceIdType.MESH

D = 1024
DEPTH = 2
EPS = 1e-6
GRID_W = 64
MLA_H, MLA_NOPE, MLA_ROPE, MLA_V = 8, 64, 32, 64
MLA_QK = MLA_NOPE + MLA_ROPE
ROPE_BASE = 10000.0
POOL_WINDOWS = (2, 4, 8, 16)
GLA_H, GLA_DK, GLA_DV, GLA_RANK, GLA_TAU = 4, 64, 128, 16, 16.0
GLA_C = 128
EXP_CLAMP = 80.0
ADAM_LR, ADAM_B1, ADAM_B2, ADAM_EPS, ADAM_WD, ADAM_STEP = 0.001, 0.9, 0.999, 1e-08, 0.01, 10

TM = 256
LANES = 128
N_CHIPS = 4

IN_NAMES = ('mla_q', 'mla_kv', 'mla_kr', 'mla_gate', 'pool_x', 'pool_gate',
            'gla_q', 'gla_k', 'gla_v', 'gla_af', 'gla_ab', 'gla_gate', 'merge')
IN_SIZES = (256, 128, 32, 512, 512, 512, 256, 256, 512, 16, 16, 512, 3 * D)
NAT_OFF = dict(zip(IN_NAMES, [int(o) for o in np.cumsum((0,) + IN_SIZES[:-1])]))
NAT_SIZE = dict(zip(IN_NAMES, IN_SIZES))
PAD_ORDER = ('merge', 'mla_gate', 'mla_q', 'mla_kv', 'mla_kr', 'pool_x', 'pool_gate',
             'gla_v', 'gla_gate', 'gla_q', 'gla_k', 'gla_af', 'gla_ab')
SLAB = {n: max(NAT_SIZE[n], LANES) for n in IN_NAMES}
PAD_OFF = dict(zip(PAD_ORDER, [int(o) for o in np.cumsum([0] + [SLAB[n] for n in PAD_ORDER[:-1]])]))
D_PAD = sum(SLAB.values())
IN_SLAB = {n: 0 for n in IN_NAMES}
IN_SLAB['mla_kr'] = MLA_NOPE
MLA_HP = 128
DZ_OFF = dict(merge=0, mla_gate=3072, gla_gate=3584, mla_q=4096, mla_kv=4352, mla_kr=4480, pool=4608,
              gla_v=5632, gla_q=6144, gla_k=6400, gla_af=6656, gla_ab=6784)
DZ_PARTS = ([(n, 0, NAT_SIZE[n]) for n in ('merge', 'mla_gate', 'gla_gate', 'mla_q', 'mla_kv', 'mla_kr')]
            + [(n, g * LANES, LANES) for g in range(4) for n in ('pool_x', 'pool_gate')]
            + [(n, 0, NAT_SIZE[n]) for n in ('gla_v', 'gla_q', 'gla_k', 'gla_af', 'gla_ab')])


def _blk(name):
    return PAD_OFF[name] // SLAB[name]


SHARDED = (('mod_w', 2), ('w_in', 2), ('mla_w_uq', 2), ('mla_w_ukv', 2), ('gla_af_w2', 2), ('gla_ab_w2', 2),
           ('w_branch_mla', 2), ('w_branch_pool', 2), ('w_branch_gla', 2), ('w_out', 1))
REPLICATED = ('c_ctx', 'mod_b', 'pre_norm', 'post_norm', 'mla_q_norm', 'mla_kv_norm', 'pool_w', 'pool_scale',
              'gla_af_b', 'gla_ab_b', 'gla_norm')
WEIGHTS = ('c_ctx', 'mod_w', 'mod_b', 'pre_norm', 'post_norm', 'w_in', 'mla_q_norm', 'mla_w_uq', 'mla_kv_norm',
           'mla_w_ukv', 'pool_w', 'pool_scale', 'gla_af_w2', 'gla_af_b', 'gla_ab_w2', 'gla_ab_b', 'gla_norm',
           'w_branch_mla', 'w_branch_pool', 'w_branch_gla', 'w_out')


def _cp(vmem_mb=None, sem=None):
    kw = {}
    if vmem_mb is not None:
        kw['vmem_limit_bytes'] = vmem_mb * 1024 * 1024
    if sem is not None:
        kw['dimension_semantics'] = sem
    return pltpu.CompilerParams(**kw)


DZ_ANY = pl.BlockSpec(memory_space=pl.ANY)


def _bdot(a, b):
    return jnp.dot(a.astype(BF16), b.astype(BF16), preferred_element_type=F32)


def _bdot_nt(a, b):
    return lax.dot_general(a.astype(BF16), b.astype(BF16), (((1,), (1,)), ((), ())), preferred_element_type=F32)


def _bdot_tn(a, b):
    return lax.dot_general(a.astype(BF16), b.astype(BF16), (((0,), (0,)), ((), ())), preferred_element_type=F32)


def _xdot(a, b):
    return jnp.dot(a, b, precision=HIGHEST, preferred_element_type=F32)


def _xdot_tn(a, b):
    return lax.dot_general(a, b, (((0,), (0,)), ((), ())), precision=HIGHEST, preferred_element_type=F32)


NN = (((1,), (0,)), ((), ()))
NT = (((1,), (1,)), ((), ()))
TN = (((0,), (0,)), ((), ()))


def _split(a):
    hi = a.astype(BF16)
    return hi, (a - hi.astype(F32)).astype(BF16)


def _dot2(a, b, dims):
    f = lambda u, v: lax.dot_general(u, v, dims, preferred_element_type=F32)
    if isinstance(a, tuple):
        return f(a[0], b) + f(a[1], b)
    return f(a, b[0]) + f(a, b[1])


def _sigmoid(x):
    return jax.nn.sigmoid(x)


def _silu(x):
    return x * _sigmoid(x)


def _dsilu(x):
    s = _sigmoid(x)
    return s * (1.0 + x * (1.0 - s))


def _rstd(x):
    return lax.rsqrt(jnp.mean(x * x, axis=-1, keepdims=True) + EPS)


def _norm_bwd(xhat, r, dy):
    return r * (dy - xhat * jnp.mean(xhat * dy, axis=-1, keepdims=True))


def _colsum(a):
    return jnp.sum(a, axis=0, keepdims=True)


def _modrow(i, tpe):
    return jnp.where(i % tpe == 0, 4, i // tpe)


def _rot(x):
    n = x.shape[-1]
    lane = lax.broadcasted_iota(jnp.int32, x.shape, x.ndim - 1)
    return jnp.where(lane % 16 < 8, -pltpu.roll(x, n - 8, x.ndim - 1), pltpu.roll(x, 8, x.ndim - 1))


def _mod_fwd(cv, mod_w, mod_b):
    def body(cv_ref, w_ref, b_ref, o_ref):
        s = _silu(cv_ref[...])
        for l in range(DEPTH):
            o_ref[l] = _bdot(s, w_ref[l]) + b_ref[l]

    return pl.pallas_call(body, name="mod_fwd", out_shape=jax.ShapeDtypeStruct((DEPTH, 8, 3 * D), F32),
                          compiler_params=_cp(40))(cv, mod_w, mod_b)


def _mod_bwd(cv, sel, st_a, st_b, w_t, dcv_in, name):
    def body(cv_ref, sel_ref, sa_ref, sb_ref, wt_ref, dcin_ref, dw_ref, db_ref, dcv_ref, dcc_ref, dpre_ref, dpost_ref):
        cvv = cv_ref[...]
        s = _silu(cvv)
        sel_v = sel_ref[...]
        dmod = jnp.concatenate([_xdot(sel_v, sa_ref[0]), _xdot(sel_v, sa_ref[1]), _xdot(sel_v, sb_ref[0])], axis=1)
        dw_ref[...] = _bdot_tn(s, dmod)
        db_ref[...] = _colsum(dmod)
        dcv = dcin_ref[...] + _bdot(dmod, wt_ref[...])
        dcv_ref[...] = dcv
        dcc_ref[...] = dcv * _dsilu(cvv)
        dpre_ref[...] = _colsum(sa_ref[2])
        dpost_ref[...] = _colsum(sb_ref[1])

    shapes = (jax.ShapeDtypeStruct((D, 3 * D), F32), jax.ShapeDtypeStruct((1, 3 * D), F32),
              jax.ShapeDtypeStruct((8, D), F32), jax.ShapeDtypeStruct((8, D), F32),
              jax.ShapeDtypeStruct((1, D), F32), jax.ShapeDtypeStruct((1, D), F32))
    return pl.pallas_call(body, name=name, out_shape=shapes, compiler_params=_cp(48))(cv, sel, st_a, st_b, w_t, dcv_in)


def _pre_fwd(x, modl, pre_g, w_t, tpe, name):
    n = x.shape[0]
    nt = n // TM
    ncb = 3
    tn = D_PAD // ncb
    tm = 2 * TM if n % (2 * TM) == 0 else TM

    def norm_body(x_ref, m_ref, g_ref, h_ref):
        xv = x_ref[...]
        m = m_ref[0]
        h_ref[...] = (xv * _rstd(xv) * g_ref[...] * (1.0 + m[:, D:2 * D]) + m[:, 0:D]).astype(BF16)

    h = pl.pallas_call(
        norm_body, name=name + "_norm", grid=(nt,),
        in_specs=[pl.BlockSpec((TM, D), lambda i: (i, 0)),
                  pl.BlockSpec((1, 1, 3 * D), lambda i: (_modrow(i, tpe), 0, 0)),
                  pl.BlockSpec((1, D), lambda i: (0, 0))],
        out_specs=pl.BlockSpec((TM, D), lambda i: (i, 0)),
        out_shape=jax.ShapeDtypeStruct((n, D), BF16),
        compiler_params=_cp(32, ("arbitrary",)),
    )(x, modl.reshape(8, 1, 3 * D), pre_g)

    def mm_body(h_ref, wt_ref, z_ref):
        z_ref[...] = lax.dot_general(h_ref[...], wt_ref[...], NT, preferred_element_type=F32)

    z = pl.pallas_call(
        mm_body, name=name, grid=(ncb, n // tm),
        in_specs=[pl.BlockSpec((tm, D), lambda j, i: (i, 0)), pl.BlockSpec((tn, D), lambda j, i: (j, 0))],
        out_specs=pl.BlockSpec((tm, tn), lambda j, i: (i, j)),
        out_shape=jax.ShapeDtypeStruct((n, D_PAD), F32),
        compiler_params=_cp(48, ("arbitrary", "arbitrary")),
    )(h, w_t)
    return z, h


def _pre_bwd(dz, w_t, x, dxres, modl, pre_g, tpe, name, latent_only=False):
    n = x.shape[0]
    nt = n // TM
    if latent_only:
        dx_rows = n - (n // (tpe * TM)) * TM
        dx_map = lambda i: ((i // tpe) * (tpe - 1) + jnp.maximum(i % tpe - 1, 0), 0)
    else:
        dx_rows, dx_map = n, (lambda i: (i, 0))

    def body(dz_ref, wt_ref, x_ref, dr_ref, m_ref, g_ref, dx_ref, st_ref):
        dh = jnp.dot(dz_ref[...], wt_ref[...], preferred_element_type=F32)
        xv = x_ref[...]
        r = _rstd(xv)
        xn = xv * r
        m = m_ref[0]
        sc1 = 1.0 + m[:, D:2 * D]
        g = g_ref[...]
        st_ref[0, 0:1, :] = _colsum(dh)
        st_ref[0, 1:2, :] = _colsum(dh * xn * g)
        st_ref[0, 2:3, :] = _colsum(dh * xn * sc1)
        st_ref[0, 3:8, :] = jnp.zeros((5, D), F32)
        dx_ref[...] = dr_ref[...] + _norm_bwd(xn, r, dh * g * sc1)

    return pl.pallas_call(
        body, name=name, grid=(nt,),
        in_specs=[pl.BlockSpec((TM, D_PAD), lambda i: (i, 0)),
                  pl.BlockSpec((D_PAD, D), lambda i: (0, 0)),
                  pl.BlockSpec((TM, D), lambda i: (i, 0)),
                  pl.BlockSpec((TM, D), lambda i: (i, 0)),
                  pl.BlockSpec((1, 1, 3 * D), lambda i: (_modrow(i, tpe), 0, 0)),
                  pl.BlockSpec((1, D), lambda i: (0, 0))],
        out_specs=[pl.BlockSpec((TM, D), dx_map),
                   pl.BlockSpec((1, 8, D), lambda i: (i, 0, 0))],
        out_shape=(jax.ShapeDtypeStruct((dx_rows, D), F32), jax.ShapeDtypeStruct((nt, 8, D), F32)),
        compiler_params=_cp(56, ("arbitrary",)),
    )(dz, w_t, x, dxres, modl.reshape(8, 1, 3 * D), pre_g)


def _matmul_tn(a, b, tm, tk, name):
    n, k1 = a.shape
    k2 = b.shape[1]

    def body(a_ref, b_ref, o_ref):
        @pl.when(pl.program_id(1) == 0)
        def _():
            o_ref[...] = jnp.zeros(o_ref.shape, F32)

        o_ref[...] += lax.dot_general(a_ref[...], b_ref[...], (((0,), (0,)), ((), ())), preferred_element_type=F32)

    return pl.pallas_call(
        body, name=name, grid=(k1 // tm, n // tk),
        in_specs=[pl.BlockSpec((tk, tm), lambda i, k: (k, i)), pl.BlockSpec((tk, k2), lambda i, k: (k, 0))],
        out_specs=pl.BlockSpec((tm, k2), lambda i, k: (i, 0)),
        out_shape=jax.ShapeDtypeStruct((k1, k2), F32),
        compiler_params=_cp(48, ("arbitrary", "arbitrary")),
    )(a, b)


def _mla_pre(z, q_norm, kv_norm, w_uq, w_ukv, cq, sq, ck, sk, tpe, name):
    n = z.shape[0]
    nt = n // TM
    scale = MLA_QK ** -0.5

    def body(zq_ref, zkv_ref, zkr_ref, gq_ref, gkv_ref, wq_ref, wkv_ref, cq_ref, sq_ref, ck_ref, sk_ref,
             q_ref, kv_ref, kr_ref):
        zq = zq_ref[...]
        qn = zq * _rstd(zq) * gq_ref[...]
        qraw = _bdot(qn, wq_ref[...])
        q_ref[...] = ((qraw * cq_ref[...] + _rot(qraw) * sq_ref[...]) * scale).astype(BF16)
        zkv = zkv_ref[...]
        kvn = zkv * _rstd(zkv) * gkv_ref[...]
        kv_ref[...] = _bdot(kvn, wkv_ref[...]).astype(BF16)
        zkr = zkr_ref[...]
        kr_ref[...] = (zkr * ck_ref[...] + _rot(zkr) * sk_ref[...]).astype(BF16)

    hq, hkv = MLA_H * MLA_HP, MLA_H * (MLA_NOPE + MLA_V)
    const = lambda i: (0, 0)
    tab = lambda i: (i % tpe, 0)
    return pl.pallas_call(
        body, name=name, grid=(nt,),
        in_specs=[pl.BlockSpec((TM, 256), lambda i: (i, _blk('mla_q'))),
                  pl.BlockSpec((TM, 128), lambda i: (i, _blk('mla_kv'))),
                  pl.BlockSpec((TM, 128), lambda i: (i, _blk('mla_kr'))),
                  pl.BlockSpec((1, 256), const), pl.BlockSpec((1, 128), const),
                  pl.BlockSpec((256, hq), const), pl.BlockSpec((128, hkv), const),
                  pl.BlockSpec((TM, hq), tab), pl.BlockSpec((TM, hq), tab),
                  pl.BlockSpec((TM, 128), tab), pl.BlockSpec((TM, 128), tab)],
        out_specs=[pl.BlockSpec((TM, hq), lambda i: (i, 0)), pl.BlockSpec((TM, hkv), lambda i: (i, 0)),
                   pl.BlockSpec((TM, 128), lambda i: (i, 0))],
        out_shape=(jax.ShapeDtypeStruct((n, hq), BF16), jax.ShapeDtypeStruct((n, hkv), BF16),
                   jax.ShapeDtypeStruct((n, 128), BF16)),
        compiler_params=_cp(32, ("arbitrary",)),
    )(z, z, z, q_norm, kv_norm, w_uq, w_ukv, cq, sq, ck, sk)


def _mla_pre_bwd(z, dq, dkv, dkr, q_norm, kv_norm, w_uq_t, w_ukv_t, cq, sq, ck, sk, dz, tpe, name):
    n = z.shape[0]
    nt = n // TM
    scale = MLA_QK ** -0.5
    hq, hkv = MLA_H * MLA_HP, MLA_H * (MLA_NOPE + MLA_V)

    def body(zq_ref, zkv_ref, dq_ref, dkv_ref, dkr_ref, gq_ref, gkv_ref, wqt_ref, wkvt_ref, cq_ref, sq_ref,
             ck_ref, sk_ref, dz_in, dz_ref, dwq_ref, dwkv_ref, dgq_ref, dgkv_ref):
        @pl.when(pl.program_id(0) == 0)
        def _():
            dwq_ref[...] = jnp.zeros(dwq_ref.shape, F32)
            dwkv_ref[...] = jnp.zeros(dwkv_ref.shape, F32)
            dgq_ref[...] = jnp.zeros(dgq_ref.shape, F32)
            dgkv_ref[...] = jnp.zeros(dgkv_ref.shape, F32)

        zq = zq_ref[...]
        rq = _rstd(zq)
        qhat = zq * rq
        gq = gq_ref[...]
        dqs = dq_ref[...] * scale
        dqraw = dqs * cq_ref[...] - _rot(dqs * sq_ref[...])
        dwq_ref[...] += _bdot_tn(qhat * gq, dqraw)
        dqn = _bdot(dqraw, wqt_ref[...])
        dgq_ref[...] += _colsum(dqn * qhat)
        dz_ref[:, 0:256] = _norm_bwd(qhat, rq, dqn * gq).astype(BF16)

        zkv = zkv_ref[...]
        rkv = _rstd(zkv)
        khat = zkv * rkv
        gkv = gkv_ref[...]
        dkvv = dkv_ref[...]
        dwkv_ref[...] += _bdot_tn(khat * gkv, dkvv)
        dkvn = _bdot(dkvv, wkvt_ref[...])
        dgkv_ref[...] += _colsum(dkvn * khat)
        dz_ref[:, 256:384] = _norm_bwd(khat, rkv, dkvn * gkv).astype(BF16)

        dkr = dkr_ref[...]
        dz_ref[:, 384:512] = (dkr * ck_ref[...] - _rot(dkr * sk_ref[...])).astype(BF16)

    const = lambda i: (0, 0)
    tab = lambda i: (i % tpe, 0)
    row = lambda i: (i, 0)
    return pl.pallas_call(
        body, name=name, grid=(nt,),
        in_specs=[pl.BlockSpec((TM, 256), lambda i: (i, _blk('mla_q'))),
                  pl.BlockSpec((TM, 128), lambda i: (i, _blk('mla_kv'))),
                  pl.BlockSpec((TM, hq), row), pl.BlockSpec((TM, hkv), row), pl.BlockSpec((TM, 128), row),
                  pl.BlockSpec((1, 256), const), pl.BlockSpec((1, 128), const),
                  pl.BlockSpec((hq, 256), const), pl.BlockSpec((hkv, 128), const),
                  pl.BlockSpec((TM, hq), tab), pl.BlockSpec((TM, hq), tab),
                  pl.BlockSpec((TM, 128), tab), pl.BlockSpec((TM, 128), tab), DZ_ANY],
        out_specs=[pl.BlockSpec((TM, 512), lambda i: (i, DZ_OFF['mla_q'] // 512)),
                   pl.BlockSpec((256, hq), const), pl.BlockSpec((128, hkv), const),
                   pl.BlockSpec((1, 256), const), pl.BlockSpec((1, 128), const)],
        out_shape=(jax.ShapeDtypeStruct(dz.shape, dz.dtype), jax.ShapeDtypeStruct((256, hq), F32),
                   jax.ShapeDtypeStruct((128, hkv), F32), jax.ShapeDtypeStruct((1, 256), F32),
                   jax.ShapeDtypeStruct((1, 128), F32)),
        input_output_aliases={13: 0},
        compiler_params=_cp(32, ("arbitrary",)),
    )(z, z, dq, dkv, dkr, q_norm, kv_norm, w_uq_t, w_ukv_t, cq, sq, ck, sk, dz)


def _attn_head(q_ref, kv_ref, kr_ref, hh, nk):
    kvh = kv_ref[0:nk, hh * MLA_HP:(hh + 1) * MLA_HP]
    lane = lax.broadcasted_iota(jnp.int32, kvh.shape, 1)
    kh = jnp.where(lane < MLA_NOPE, kvh, kr_ref[0:nk, :])
    qh = q_ref[:, hh * MLA_HP:(hh + 1) * MLA_HP]
    return kvh, kh, qh, lax.dot_general(qh, kh, (((1,), (1,)), ((), ())), preferred_element_type=F32)


def _by_segment(j, lc, t, fn):
    pl.when(j == 0)(functools.partial(fn, lc))
    pl.when(j != 0)(functools.partial(fn, t))


def _attn_specs(nb, tpe, t):
    tile = lambda b, p, j: (b * tpe + j, p)
    return [pl.BlockSpec((TM, 2 * MLA_HP), tile),
            pl.BlockSpec((t, 2 * MLA_HP), lambda b, p, j: (b, p)),
            pl.BlockSpec((t, MLA_HP), lambda b, p, j: (b, 0))]


def _attn_fwd(q, kv, kr, nb, lc, name):
    n = q.shape[0]
    t = n // nb
    tpe = t // TM

    def body(q_ref, kv_ref, kr_ref, o_ref, lse_ref):
        def run(nk):
            lane = lax.broadcasted_iota(jnp.int32, (TM, MLA_HP), 1)
            res, lses = [], []
            for hh in range(2):
                kvh, _, _, s = _attn_head(q_ref, kv_ref, kr_ref, hh, nk)
                m = jnp.max(s, axis=-1, keepdims=True)
                p = jnp.exp(s - m)
                l = jnp.sum(p, axis=-1, keepdims=True)
                res.append(jnp.dot(p.astype(BF16), kvh, preferred_element_type=F32) / l)
                lses.append(m + jnp.log(l))
            o_ref[...] = jnp.where(lane < MLA_V, pltpu.roll(res[0], MLA_V, 1), res[1])
            lane2 = lax.broadcasted_iota(jnp.int32, (TM, 2), 1)
            lse_ref[0] = jnp.where(lane2 == 0, lses[0], lses[1])

        _by_segment(pl.program_id(2), lc, t, run)

    return pl.pallas_call(
        body, name=name, grid=(nb, MLA_H // 2, tpe),
        in_specs=_attn_specs(nb, tpe, t),
        out_specs=[pl.BlockSpec((TM, 2 * MLA_V), lambda b, p, j: (b * tpe + j, p)),
                   pl.BlockSpec((1, TM, 2), lambda b, p, j: (p, b * tpe + j, 0))],
        out_shape=(jax.ShapeDtypeStruct((n, MLA_H * MLA_V), F32), jax.ShapeDtypeStruct((MLA_H // 2, n, 2), F32)),
        compiler_params=_cp(48, ("arbitrary", "arbitrary", "arbitrary")),
    )(q, kv, kr)


def _attn_bwd(q, kv, kr, o, lse, do, nb, lc, name):
    n = q.shape[0]
    t = n // nb
    tpe = t // TM

    def body(q_ref, kv_ref, kr_ref, o_ref, lse_ref, do_ref, dq_ref, dkv_ref, dkr_ref):
        p_id, j = pl.program_id(1), pl.program_id(2)

        @pl.when(j == 0)
        def _():
            dkv_ref[...] = jnp.zeros(dkv_ref.shape, F32)

        @pl.when((j == 0) & (p_id == 0))
        def _():
            dkr_ref[...] = jnp.zeros(dkr_ref.shape, F32)

        def run(nk):
            lane = lax.broadcasted_iota(jnp.int32, (TM, MLA_HP), 1)
            lane_t = lax.broadcasted_iota(jnp.int32, (nk, MLA_HP), 1)
            lane2 = lax.broadcasted_iota(jnp.int32, (TM, 2), 1)
            lse = lse_ref[0]
            dov, ov = do_ref[...], o_ref[...]
            dkr = jnp.zeros((nk, MLA_HP), F32)
            for hh in range(2):
                kvh, kh, qh, s = _attn_head(q_ref, kv_ref, kr_ref, hh, nk)
                p = jnp.exp(s - jnp.sum(jnp.where(lane2 == hh, lse, 0.0), axis=1, keepdims=True))
                do_pos = jnp.where(lane >= MLA_NOPE, pltpu.roll(dov, MLA_V, 1) if hh == 0 else dov, 0.0)
                o_pos = jnp.where(lane >= MLA_NOPE, pltpu.roll(ov, MLA_V, 1) if hh == 0 else ov, 0.0)
                delta = jnp.sum(do_pos * o_pos, axis=-1, keepdims=True)
                dob = do_pos.astype(BF16)
                dp = lax.dot_general(dob, kvh, (((1,), (1,)), ((), ())), preferred_element_type=F32)
                ds = (p * (dp - delta)).astype(BF16)
                dq_ref[:, hh * MLA_HP:(hh + 1) * MLA_HP] = jnp.dot(ds, kh, preferred_element_type=F32)
                dkf = lax.dot_general(ds, qh, (((0,), (0,)), ((), ())), preferred_element_type=F32)
                dvp = lax.dot_general(p.astype(BF16), dob, (((0,), (0,)), ((), ())), preferred_element_type=F32)
                dkv_ref[0:nk, hh * MLA_HP:(hh + 1) * MLA_HP] += jnp.where(lane_t < MLA_NOPE, dkf, dvp)
                dkr = dkr + jnp.where(lane_t >= MLA_NOPE, dkf, 0.0)
            dkr_ref[0:nk, :] += dkr

        _by_segment(j, lc, t, run)

    tile = lambda b, p, j: (b * tpe + j, p)
    return pl.pallas_call(
        body, name=name, grid=(nb, MLA_H // 2, tpe),
        in_specs=_attn_specs(nb, tpe, t) + [pl.BlockSpec((TM, 2 * MLA_V), tile),
                                            pl.BlockSpec((1, TM, 2), lambda b, p, j: (p, b * tpe + j, 0)),
                                            pl.BlockSpec((TM, 2 * MLA_V), tile)],
        out_specs=[pl.BlockSpec((TM, 2 * MLA_HP), tile),
                   pl.BlockSpec((t, 2 * MLA_HP), lambda b, p, j: (b, p)),
                   pl.BlockSpec((t, MLA_HP), lambda b, p, j: (b, 0))],
        out_shape=(jax.ShapeDtypeStruct((n, MLA_H * MLA_HP), F32), jax.ShapeDtypeStruct((n, MLA_H * MLA_HP), F32),
                   jax.ShapeDtypeStruct((n, MLA_HP), F32)),
        compiler_params=_cp(56, ("arbitrary", "arbitrary", "arbitrary")),
    )(q, kv, kr, o, lse, do)


POOL_PAD = 8


def _pool_window(ug, pos, seglen, lc, w, transpose):
    cnt = (jnp.minimum(pos + w // 2, seglen) - jnp.maximum(pos - w // 2, 0)).astype(F32)
    if transpose:
        ug = ug / cnt
    zeros = jnp.zeros((POOL_PAD, ug.shape[1]), F32)
    up = jnp.concatenate([zeros, ug[:lc], zeros, zeros, ug[lc:], zeros], axis=0)
    tp = up.shape[0]
    acc = up
    for j in range(-(w // 2), w // 2):
        jj = -j if transpose else j
        if jj != 0:
            acc = acc + pltpu.roll(up, (-jj) % tp, 0)
    acc = jnp.concatenate([acc[POOL_PAD:POOL_PAD + lc], acc[3 * POOL_PAD + lc:tp - POOL_PAD]], axis=0)
    return acc if transpose else acc / cnt


def _by_group(g, fn):
    for k, w in enumerate(POOL_WINDOWS):
        pl.when(g == k)(functools.partial(fn, w))


def _pool_specs(t):
    px, pg = PAD_OFF['pool_x'] // LANES, PAD_OFF['pool_gate'] // LANES
    return [pl.BlockSpec((t, LANES), lambda g, b: (b, px + g)),
            pl.BlockSpec((t, LANES), lambda g, b: (b, pg + g)),
            pl.BlockSpec((1, LANES, LANES), lambda g, b: (g, 0, 0)),
            pl.BlockSpec((1, LANES), lambda g, b: (0, g)),
            pl.BlockSpec((t, 1), lambda g, b: (0, 0)), pl.BlockSpec((t, 1), lambda g, b: (0, 0))]


def _pool_fwd(z, pool_w, pool_scale, pos, seglen, nb, lc, name):
    n = z.shape[0]
    t = n // nb

    def body(u_ref, zg_ref, pw_ref, ps_ref, pos_ref, sl_ref, y_ref):
        def run(w):
            u = u_ref[...]
            pooled = _pool_window(u, pos_ref[...], sl_ref[...], lc, w, False) - u
            y_ref[...] = (_bdot(pooled, pw_ref[0]) * ps_ref[...] * _silu(zg_ref[...])).astype(BF16)

        _by_group(pl.program_id(0), run)

    return pl.pallas_call(
        body, name=name, grid=(4, nb), in_specs=_pool_specs(t),
        out_specs=pl.BlockSpec((t, LANES), lambda g, b: (b, g)),
        out_shape=jax.ShapeDtypeStruct((n, 512), BF16),
        compiler_params=_cp(48, ("arbitrary", "arbitrary")),
    )(z, z, pool_w, pool_scale, pos, seglen)


def _pool_bwd(z, dy, pool_w, pool_w_t, pool_scale, pos, seglen, dz, nb, lc, name):
    n = z.shape[0]
    t = n // nb

    def body(u_ref, zg_ref, pw_ref, ps_ref, pos_ref, sl_ref, dy_ref, pwt_ref, dz_in, dz_ref, dpw_ref, dps_ref):
        @pl.when(pl.program_id(1) == 0)
        def _():
            dpw_ref[...] = jnp.zeros(dpw_ref.shape, F32)
            dps_ref[...] = jnp.zeros(dps_ref.shape, F32)

        def run(w):
            u = u_ref[...]
            pos_v, sl_v = pos_ref[...], sl_ref[...]
            pooled = _pool_window(u, pos_v, sl_v, lc, w, False) - u
            mixed = _bdot(pooled, pw_ref[0])
            zg = zg_ref[...]
            sg = _silu(zg)
            ps = ps_ref[...]
            dyv = dy_ref[...]
            dps_ref[...] += _colsum(dyv * mixed * sg)
            dz_ref[:, LANES:2 * LANES] = (dyv * mixed * ps * _dsilu(zg)).astype(BF16)
            dmixed = dyv * ps * sg
            dpw_ref[0] += _bdot_tn(pooled, dmixed)
            dpooled = _bdot(dmixed, pwt_ref[0])
            dz_ref[:, 0:LANES] = (_pool_window(dpooled, pos_v, sl_v, lc, w, True) - dpooled).astype(BF16)

        _by_group(pl.program_id(0), run)

    blk = pl.BlockSpec((t, LANES), lambda g, b: (b, g))
    return pl.pallas_call(
        body, name=name, grid=(4, nb),
        in_specs=_pool_specs(t) + [blk, pl.BlockSpec((1, LANES, LANES), lambda g, b: (g, 0, 0)), DZ_ANY],
        out_specs=[pl.BlockSpec((t, 2 * LANES), lambda g, b: (b, DZ_OFF['pool'] // (2 * LANES) + g)),
                   pl.BlockSpec((1, LANES, LANES), lambda g, b: (g, 0, 0)),
                   pl.BlockSpec((1, LANES), lambda g, b: (0, g))],
        out_shape=(jax.ShapeDtypeStruct(dz.shape, dz.dtype),
                   jax.ShapeDtypeStruct((4, 128, 128), F32), jax.ShapeDtypeStruct((1, 512), F32)),
        input_output_aliases={8: 0},
        compiler_params=_cp(48, ("arbitrary", "arbitrary")),
    )(z, z, pool_w, pool_scale, pos, seglen, dy, pool_w_t, dz)


def _gla_chunk(q_ref, k_ref, a_ref, w2_ref, b2_ref, reverse):
    c = GLA_C
    x = _bdot(a_ref[...], w2_ref[0]) + b2_ref[0]
    la = (jnp.minimum(x, 0.0) - jnp.log(1.0 + jnp.exp(-jnp.abs(x)))) * (1.0 / GLA_TAU)
    row = lax.broadcasted_iota(jnp.int32, (c, c), 0)
    col = lax.broadcasted_iota(jnp.int32, (c, c), 1)
    tri = (col >= row) if reverse else (col <= row)
    tri_t = (col <= row) if reverse else (col >= row)
    b = _xdot(tri.astype(F32), la)
    tok = lax.broadcasted_iota(jnp.int32, la.shape, 0)
    bref = _colsum(jnp.where((tok >= c // 2) if reverse else (tok < c // 2), la, 0.0))
    blast = _colsum(la)
    eq = jnp.exp(jnp.minimum(b - bref, EXP_CLAMP))
    ek = jnp.exp(jnp.minimum(bref - b, EXP_CLAMP))
    qs = q_ref[...] * (GLA_DK ** -0.5)
    kk = k_ref[...]
    eb = jnp.exp(b)
    etail = jnp.exp(blast - b)
    return dict(x=x, la=la, tri=tri, tri_t=tri_t, eq=eq, ek=ek, qs=qs, kk=kk, qd=qs * eq, kd=kk * ek, qe=qs * eb,
                kl=kk * etail, eb=eb, etail=etail)


def _pair(a, p):
    return a[:, p * LANES:(p + 1) * LANES]


def _head_masks():
    lane = lax.broadcasted_iota(jnp.int32, (GLA_C, LANES), 1)
    return (lane < GLA_DK, lane >= GLA_DK)


def _state_decay(la, p):
    return jnp.exp(_xdot_tn(_pair(la, p), jnp.ones((GLA_C, GLA_DV), F32)))


def _gla_chunk_maps(nb, nc, ncc, order):
    def rmap(j):
        return jnp.where(j < ncc, ncc - 1 - j, nc - 1 - (j - ncc))

    if order == 'scan':
        return (lambda b, j: b * nc + j), (lambda b, j: b * nc + rmap(j))
    return (lambda b, j: b * nc + nc - 1 - j), (lambda b, j: b * nc + rmap(nc - 1 - j))


def _gla_in_specs(maps):
    specs = []
    for d, cm in enumerate(maps):
        gate = 'gla_af' if d == 0 else 'gla_ab'
        specs += [pl.BlockSpec((GLA_C, 256), lambda b, j, cm=cm: (cm(b, j), _blk('gla_q'))),
                  pl.BlockSpec((GLA_C, 256), lambda b, j, cm=cm: (cm(b, j), _blk('gla_k'))),
                  pl.BlockSpec((GLA_C, 512), lambda b, j, cm=cm: (cm(b, j), _blk('gla_v'))),
                  pl.BlockSpec((GLA_C, LANES), lambda b, j, cm=cm, gate=gate: (cm(b, j), _blk(gate))),
                  pl.BlockSpec((1, LANES, 256), lambda b, j, d=d: (d, 0, 0)),
                  pl.BlockSpec((1, 1, 256), lambda b, j, d=d: (d, 0, 0))]
    return specs


def _gla_fwd(z, w2, b2, nb, lc, name):
    n = z.shape[0]
    nc = n // nb // GLA_C
    maps = _gla_chunk_maps(nb, nc, lc // GLA_C, 'scan')

    def body(*refs):
        ins, (of_ref, ob_ref, sf_ref, sb_ref, s_sc) = refs[:12], refs[12:]

        @pl.when(pl.program_id(1) == 0)
        def _():
            s_sc[...] = jnp.zeros(s_sc.shape, F32)

        masks = _head_masks()
        for d, (o_ref, st_ref) in enumerate(((of_ref, sf_ref), (ob_ref, sb_ref))):
            q_ref, k_ref, v_ref, a_ref, w2_ref, b2_ref = ins[6 * d:6 * d + 6]
            ch = _gla_chunk(q_ref, k_ref, a_ref, w2_ref, b2_ref, d == 1)
            for p in range(2):
                s_prev = s_sc[d, p]
                st_ref[0, p] = s_prev
                s_new = _state_decay(ch['la'], p) * s_prev
                kd_p = _pair(ch['kd'], p)
                for hh in range(2):
                    h = 2 * p + hh
                    vv = v_ref[:, h * GLA_DV:(h + 1) * GLA_DV]
                    att = jnp.where(ch['tri'], _bdot_nt(jnp.where(masks[hh], _pair(ch['qd'], p), 0.0), kd_p), 0.0)
                    o_ref[:, h * GLA_DV:(h + 1) * GLA_DV] = (
                        _bdot(att, vv) + _bdot(jnp.where(masks[hh], _pair(ch['qe'], p), 0.0), s_prev))
                    s_new = s_new + _dot2(_split(jnp.where(masks[hh], _pair(ch['kl'], p), 0.0)), vv.astype(BF16), TN)
                s_sc[d, p] = s_new

    o_shape = jax.ShapeDtypeStruct((n, 512), F32)
    st_shape = jax.ShapeDtypeStruct((n // GLA_C, 2, LANES, GLA_DV), F32)
    return pl.pallas_call(
        body, name=name, grid=(nb, nc),
        in_specs=_gla_in_specs(maps),
        out_specs=[pl.BlockSpec((GLA_C, 512), lambda b, j: (maps[0](b, j), 0)),
                   pl.BlockSpec((GLA_C, 512), lambda b, j: (maps[1](b, j), 0)),
                   pl.BlockSpec((1, 2, LANES, GLA_DV), lambda b, j: (maps[0](b, j), 0, 0, 0)),
                   pl.BlockSpec((1, 2, LANES, GLA_DV), lambda b, j: (maps[1](b, j), 0, 0, 0))],
        out_shape=(o_shape, o_shape, st_shape, st_shape),
        scratch_shapes=[pltpu.VMEM((2, 2, LANES, GLA_DV), F32)],
        compiler_params=_cp(32, ("arbitrary", "arbitrary")),
    )(z, z, z, z, w2, b2, z, z, z, z, w2, b2)


def _gla_bwd(z, w2, w2_t, b2, st_f, st_b, dog, nb, lc, name):
    n = z.shape[0]
    nc = n // nb // GLA_C
    maps = _gla_chunk_maps(nb, nc, lc // GLA_C, 'back')

    def body(*refs):
        ins, extra, outs, (ds_sc, sfx_sc) = refs[:12], refs[12:18], refs[18:30], refs[30:]

        @pl.when(pl.program_id(1) == 0)
        def _():
            ds_sc[...] = jnp.zeros(ds_sc.shape, F32)
            sfx_sc[...] = jnp.zeros(sfx_sc.shape, F32)

        @pl.when((pl.program_id(0) == 0) & (pl.program_id(1) == 0))
        def _():
            for r in outs[8:12]:
                r[...] = jnp.zeros(r.shape, F32)

        masks = _head_masks()
        for d in range(2):
            q_ref, k_ref, v_ref, a_ref, w2_ref, b2_ref = ins[6 * d:6 * d + 6]
            w2t_ref, st_ref, do_ref = extra[3 * d:3 * d + 3]
            dq_ref, dk_ref, dv_ref, da_ref = outs[4 * d:4 * d + 4]
            dw2_ref, db2_ref = outs[8 + 2 * d], outs[9 + 2 * d]
            ch = _gla_chunk(q_ref, k_ref, a_ref, w2_ref, b2_ref, d == 1)
            dqs, dks, dbs = [], [], []
            for p in range(2):
                s_prev = st_ref[0, p]
                ds_new = ds_sc[d, p]
                qd_p, kd_p, qe_p, kl_p = (_pair(ch[nme], p) for nme in ('qd', 'kd', 'qe', 'kl'))
                ds_prev = _state_decay(ch['la'], p) * ds_new
                qd_b, kd_b = qd_p.astype(BF16), kd_p.astype(BF16)
                sp_s, dsn_s = _split(s_prev), _split(ds_new)
                dq_h, dk_h, db_h = [], [], []
                for hh in range(2):
                    h = 2 * p + hh
                    vv = v_ref[:, h * GLA_DV:(h + 1) * GLA_DV]
                    dov = do_ref[:, h * GLA_DV:(h + 1) * GLA_DV]
                    att = jnp.where(ch['tri'], _bdot_nt(jnp.where(masks[hh], qd_p, 0.0), kd_p), 0.0)
                    dv_ref[:, h * GLA_DV:(h + 1) * GLA_DV] = (
                        _bdot_tn(att, dov) + _bdot(jnp.where(masks[hh], kl_p, 0.0), ds_new))
                    vv_b, dov_b = vv.astype(BF16), dov.astype(BF16)
                    datt_b = jnp.where(ch['tri'], lax.dot_general(dov_b, vv_b, NT, preferred_element_type=F32),
                                       0.0).astype(BF16)
                    dq_in = lax.dot_general(datt_b, kd_b, NN, preferred_element_type=F32)
                    dk_in = lax.dot_general(datt_b, qd_b, TN, preferred_element_type=F32)
                    dq_st = _dot2(dov_b, sp_s, NT) * _pair(ch['eb'], p)
                    dk_st = _dot2(vv_b, dsn_s, NT) * _pair(ch['etail'], p)
                    dq_h.append(dq_in * _pair(ch['eq'], p) + dq_st)
                    dk_h.append(dk_in * _pair(ch['ek'], p) + dk_st)
                    db_h.append((qd_b.astype(F32) * dq_in - kd_b.astype(F32) * dk_in)
                                + (_pair(ch['qs'], p) * dq_st - _pair(ch['kk'], p) * dk_st))
                    ds_prev = ds_prev + _dot2(_split(jnp.where(masks[hh], qe_p, 0.0)), dov_b, TN)
                ds_sc[d, p] = ds_prev
                dqs.append(jnp.where(masks[0], dq_h[0], dq_h[1]))
                dks.append(jnp.where(masks[0], dk_h[0], dk_h[1]))
                dbs.append(jnp.where(masks[0], db_h[0], db_h[1]))
            dq_ref[...] = jnp.concatenate(dqs, axis=1) * (GLA_DK ** -0.5)
            dk_ref[...] = jnp.concatenate(dks, axis=1)
            db = jnp.concatenate(dbs, axis=1)
            dla = _xdot(ch['tri_t'].astype(F32), db) + sfx_sc[d]
            sfx_sc[d] = sfx_sc[d] + _colsum(db)
            dx = dla * (1.0 / GLA_TAU) * _sigmoid(-ch['x'])
            da_ref[...] = _bdot(dx, w2t_ref[0])
            dw2_ref[...] += _bdot_tn(a_ref[...], dx)
            db2_ref[...] += _colsum(dx)

    extra_specs, out_specs = [], []
    for d, cm in enumerate(maps):
        extra_specs += [pl.BlockSpec((1, 256, LANES), lambda b, j, d=d: (d, 0, 0)),
                        pl.BlockSpec((1, 2, LANES, GLA_DV), lambda b, j, cm=cm: (cm(b, j), 0, 0, 0)),
                        pl.BlockSpec((GLA_C, 512), lambda b, j, cm=cm: (cm(b, j), 0))]
        out_specs += [pl.BlockSpec((GLA_C, 256), lambda b, j, cm=cm: (cm(b, j), 0)),
                      pl.BlockSpec((GLA_C, 256), lambda b, j, cm=cm: (cm(b, j), 0)),
                      pl.BlockSpec((GLA_C, 512), lambda b, j, cm=cm: (cm(b, j), 0)),
                      pl.BlockSpec((GLA_C, LANES), lambda b, j, cm=cm: (cm(b, j), 0))]
    const2 = lambda b, j: (0, 0)
    out_specs += [pl.BlockSpec((LANES, 256), const2), pl.BlockSpec((1, 256), const2)] * 2
    per_dir = (jax.ShapeDtypeStruct((n, 256), F32), jax.ShapeDtypeStruct((n, 256), F32),
               jax.ShapeDtypeStruct((n, 512), F32), jax.ShapeDtypeStruct((n, LANES), F32))
    wshape = (jax.ShapeDtypeStruct((LANES, 256), F32), jax.ShapeDtypeStruct((1, 256), F32))
    return pl.pallas_call(
        body, name=name, grid=(nb, nc),
        in_specs=_gla_in_specs(maps) + extra_specs,
        out_specs=out_specs,
        out_shape=per_dir + per_dir + wshape + wshape,
        scratch_shapes=[pltpu.VMEM((2, 2, LANES, GLA_DV), F32), pltpu.VMEM((2, 1, 256), F32)],
        compiler_params=_cp(32, ("arbitrary", "arbitrary")),
    )(z, z, z, z, w2, b2, z, z, z, z, w2, b2, w2_t, st_f, dog, w2_t, st_b, dog)


def _gla_into_dz(dz, dq_f, dq_r, dk_f, dk_r, dv_f, dv_r, da_f, da_r, name):
    n = dq_f.shape[0]
    row = lambda i: (i, 0)
    w256, w512, w128 = (pl.BlockSpec((TM, w), row) for w in (256, 512, 128))
    shp = jax.ShapeDtypeStruct(dz.shape, dz.dtype)

    def v_body(dvf, dvr, dz_in, o_ref):
        o_ref[...] = (dvf[...] + dvr[...]).astype(BF16)

    dz = pl.pallas_call(
        v_body, name=name + "_v", grid=(n // TM,), in_specs=[w512, w512, DZ_ANY],
        out_specs=pl.BlockSpec((TM, 512), lambda i: (i, DZ_OFF['gla_v'] // 512)), out_shape=shp,
        input_output_aliases={2: 0}, compiler_params=_cp(32, ("arbitrary",)))(dv_f, dv_r, dz)

    def qk_body(dqf, dqr, dkf, dkr, daf, dar, dz_in, o_ref):
        o_ref[:, 0:256] = (dqf[...] + dqr[...]).astype(BF16)
        o_ref[:, 256:512] = (dkf[...] + dkr[...]).astype(BF16)
        o_ref[:, 512:640] = daf[...].astype(BF16)
        o_ref[:, 640:768] = dar[...].astype(BF16)

    return pl.pallas_call(
        qk_body, name=name + "_qk", grid=(n // TM,), in_specs=[w256] * 4 + [w128] * 2 + [DZ_ANY],
        out_specs=pl.BlockSpec((TM, 768), lambda i: (i, DZ_OFF['gla_q'] // 768)), out_shape=shp,
        input_output_aliases={6: 0}, compiler_params=_cp(32, ("arbitrary",)))(dq_f, dq_r, dk_f, dk_r, da_f, da_r, dz)


def _gla_out_norm(og):
    hats, rs = [], []
    for h in range(GLA_H):
        seg = og[:, h * GLA_DV:(h + 1) * GLA_DV]
        r = _rstd(seg)
        hats.append(seg * r)
        rs.append(r)
    return hats, rs


def _merge_branches(zm_ref, zgm_ref, zgg_ref, om_ref, yp_ref, ogf_ref, ogb_ref, gn_ref, wbm_ref, wbp_ref, wbg_ref):
    zgm, zgg = zgm_ref[...], zgg_ref[...]
    om = om_ref[...]
    y_mla = om * _silu(zgm)
    hats, rs = _gla_out_norm(ogf_ref[...] + ogb_ref[...])
    gn = gn_ref[...]
    sgg = _silu(zgg)
    y_gla = jnp.concatenate([hats[h] * gn for h in range(GLA_H)], axis=1) * sgg
    ys = (y_mla, yp_ref[...], y_gla)
    ps = (_bdot(y_mla, wbm_ref[...]), jnp.dot(yp_ref[...], wbp_ref[...], preferred_element_type=F32),
          _bdot(y_gla, wbg_ref[...]))
    zm = zm_ref[...]
    gs = tuple(_sigmoid(zm[:, a * D:(a + 1) * D]) for a in range(3))
    merged = gs[0] * ps[0] + gs[1] * ps[1] + gs[2] * ps[2]
    return dict(zgm=zgm, zgg=zgg, om=om, hats=hats, rs=rs, gn=gn, sgg=sgg, ys=ys, ps=ps, gs=gs, merged=merged)


def _merge_in_specs(tpe):
    row = lambda i: (i, 0)
    const = lambda i: (0, 0)
    return [pl.BlockSpec((TM, 3 * D), lambda i: (i, _blk('merge'))),
            pl.BlockSpec((TM, 512), lambda i: (i, _blk('mla_gate'))),
            pl.BlockSpec((TM, 512), lambda i: (i, _blk('gla_gate'))),
            pl.BlockSpec((TM, 512), row), pl.BlockSpec((TM, 512), row), pl.BlockSpec((TM, 512), row),
            pl.BlockSpec((TM, 512), row), pl.BlockSpec((1, 128), const),
            pl.BlockSpec((512, D), const), pl.BlockSpec((512, D), const), pl.BlockSpec((512, D), const),
            pl.BlockSpec((1, 1, 3 * D), lambda i: (_modrow(i, tpe), 0, 0)), pl.BlockSpec((1, D), const)]


def _merge_fwd(x, z, o_mla, y_pool, ogf, ogb, gla_n, wbm, wbp, wbg, wout, modl, post_g, tpe, name):
    n = x.shape[0]

    def body(zm_ref, zgm_ref, zgg_ref, om_ref, yp_ref, ogf_ref, ogb_ref, gn_ref, wbm_ref, wbp_ref, wbg_ref,
             m_ref, pg_ref, x_ref, wo_ref, xn_ref, out_ref):
        mb = _merge_branches(zm_ref, zgm_ref, zgg_ref, om_ref, yp_ref, ogf_ref, ogb_ref, gn_ref,
                             wbm_ref, wbp_ref, wbg_ref)
        out = _bdot(mb['merged'], wo_ref[...])
        gate = m_ref[0][:, 2 * D:3 * D]
        xn_ref[...] = x_ref[...] + gate * (out * _rstd(out) * pg_ref[...])
        out_ref[...] = out

    row = lambda i: (i, 0)
    return pl.pallas_call(
        body, name=name, grid=(n // TM,),
        in_specs=_merge_in_specs(tpe) + [pl.BlockSpec((TM, D), row), pl.BlockSpec((D, D), lambda i: (0, 0))],
        out_specs=[pl.BlockSpec((TM, D), row), pl.BlockSpec((TM, D), row)],
        out_shape=(jax.ShapeDtypeStruct((n, D), F32), jax.ShapeDtypeStruct((n, D), F32)),
        compiler_params=_cp(48, ("arbitrary",)),
    )(z, z, z, o_mla, y_pool, ogf, ogb, gla_n, wbm, wbp, wbg, modl.reshape(8, 1, 3 * D), post_g, x, wout)


def _merge_bwd(dxn, out, z, o_mla, y_pool, ogf, ogb, gla_n, wbm, wbp, wbg, wbm_t, wbp_t, wbg_t, wout_t,
               modl, post_g, tpe, name):
    n = out.shape[0]
    nt = n // TM

    def body(zm_ref, zgm_ref, zgg_ref, om_ref, yp_ref, ogf_ref, ogb_ref, gn_ref, wbm_ref, wbp_ref, wbg_ref,
             m_ref, pg_ref, dxn_ref, out_ref, wbmt_ref, wbpt_ref, wbgt_ref, wot_ref,
             dz_ref, dom_ref, dyp_ref, dog_ref, st_ref,
             dwbm_ref, dwbp_ref, dwbg_ref, dwo_ref, dgn_ref):
        @pl.when(pl.program_id(0) == 0)
        def _():
            for r in (dwbm_ref, dwbp_ref, dwbg_ref, dwo_ref, dgn_ref):
                r[...] = jnp.zeros(r.shape, F32)

        mb = _merge_branches(zm_ref, zgm_ref, zgg_ref, om_ref, yp_ref, ogf_ref, ogb_ref, gn_ref,
                             wbm_ref, wbp_ref, wbg_ref)
        out = out_ref[...]
        r2 = _rstd(out)
        on = out * r2
        pg = pg_ref[...]
        gate = m_ref[0][:, 2 * D:3 * D]
        dxn_v = dxn_ref[...]
        st_ref[0, 0:1, :] = _colsum(dxn_v * on * pg)
        st_ref[0, 1:2, :] = _colsum(dxn_v * gate * on)
        st_ref[0, 2:8, :] = jnp.zeros((6, D), F32)
        dout = _norm_bwd(on, r2, dxn_v * gate * pg)
        dwo_ref[...] += _bdot_tn(mb['merged'], dout)
        dmerged = _bdot(dout, wot_ref[...])
        dys = []
        for a, (dw_ref, wt_ref) in enumerate(((dwbm_ref, wbmt_ref), (dwbp_ref, wbpt_ref), (dwbg_ref, wbgt_ref))):
            g = mb['gs'][a]
            dz_ref[:, a * D:(a + 1) * D] = (dmerged * mb['ps'][a] * g * (1.0 - g)).astype(BF16)
            dp = dmerged * g
            dw_ref[...] += _bdot_tn(mb['ys'][a], dp)
            dys.append(_bdot(dp, wt_ref[...]))
        dom_ref[...] = dys[0] * _silu(mb['zgm'])
        dz_ref[:, 3 * D:3 * D + 512] = (dys[0] * mb['om'] * _dsilu(mb['zgm'])).astype(BF16)
        dyp_ref[...] = dys[1]
        gn = mb['gn']
        dgn = jnp.zeros((1, GLA_DV), F32)
        dzgg, dog = [], []
        for h in range(GLA_H):
            sl = slice(h * GLA_DV, (h + 1) * GLA_DV)
            dyg = dys[2][:, sl]
            hat = mb['hats'][h]
            dzgg.append(dyg * hat * gn * _dsilu(mb['zgg'][:, sl]))
            dn = dyg * mb['sgg'][:, sl]
            dgn = dgn + _colsum(dn * hat)
            dog.append(_norm_bwd(hat, mb['rs'][h], dn * gn))
        dgn_ref[...] += dgn
        dz_ref[:, 3 * D + 512:4 * D] = jnp.concatenate(dzgg, axis=1).astype(BF16)
        dog_ref[...] = jnp.concatenate(dog, axis=1)

    row = lambda i: (i, 0)
    const = lambda i: (0, 0)
    wspec = pl.BlockSpec((512, D), const)
    wtspec = pl.BlockSpec((D, 512), const)
    return pl.pallas_call(
        body, name=name, grid=(nt,),
        in_specs=_merge_in_specs(tpe) + [pl.BlockSpec((TM, D), row), pl.BlockSpec((TM, D), row),
                                         wtspec, wtspec, wtspec,
                                         pl.BlockSpec((D, D), const)],
        out_specs=[pl.BlockSpec((TM, 4 * D), row), pl.BlockSpec((TM, 512), row),
                   pl.BlockSpec((TM, 512), row), pl.BlockSpec((TM, 512), row),
                   pl.BlockSpec((1, 8, D), lambda i: (i, 0, 0)),
                   wspec, wspec, wspec, pl.BlockSpec((D, D), const), pl.BlockSpec((1, 128), const)],
        out_shape=(jax.ShapeDtypeStruct((n, D_PAD), BF16), jax.ShapeDtypeStruct((n, 512), F32),
                   jax.ShapeDtypeStruct((n, 512), F32), jax.ShapeDtypeStruct((n, 512), F32),
                   jax.ShapeDtypeStruct((nt, 8, D), F32),
                   jax.ShapeDtypeStruct((512, D), F32), jax.ShapeDtypeStruct((512, D), F32),
                   jax.ShapeDtypeStruct((512, D), F32), jax.ShapeDtypeStruct((D, D), F32),
                   jax.ShapeDtypeStruct((1, 128), F32)),
        compiler_params=_cp(56, ("arbitrary",)),
    )(z, z, z, o_mla, y_pool, ogf, ogb, gla_n, wbm, wbp, wbg, modl.reshape(8, 1, 3 * D), post_g,
      dxn, out, wbm_t, wbp_t, wbg_t, wout_t)


def _loss_grad(xf, tgt, nb, tpe):
    n = xf.shape[0]

    def body(x_ref, t_ref, dx_ref, l_ref):
        j = pl.program_id(1)
        d = x_ref[...] - t_ref[...]
        latent = j > 0
        dx_ref[...] = jnp.where(latent, d * (1.0 / D), 0.0)
        l_ref[...] = jnp.full(l_ref.shape, jnp.where(latent, 0.5 / D * jnp.sum(d * d), 0.0), F32)

    return pl.pallas_call(
        body, name="loss_grad", grid=(nb, tpe),
        in_specs=[pl.BlockSpec((TM, D), lambda b, j: (b * tpe + j, 0)),
                  pl.BlockSpec((TM, D), lambda b, j: (b * (tpe - 1) + jnp.maximum(j - 1, 0), 0))],
        out_specs=[pl.BlockSpec((TM, D), lambda b, j: (b * tpe + j, 0)),
                   pl.BlockSpec((1, 8, 128), lambda b, j: (b * tpe + j, 0, 0))],
        out_shape=(jax.ShapeDtypeStruct((n, D), F32), jax.ShapeDtypeStruct((n // TM, 8, 128), F32)),
        compiler_params=_cp(32, ("arbitrary", "arbitrary")),
    )(xf, tgt)


def _to_padded(w_nat):
    parts = []
    for nme in PAD_ORDER:
        p = w_nat[NAT_OFF[nme]:NAT_OFF[nme] + NAT_SIZE[nme]]
        if SLAB[nme] > NAT_SIZE[nme]:
            p = jnp.pad(p, [(IN_SLAB[nme], SLAB[nme] - NAT_SIZE[nme] - IN_SLAB[nme]), (0, 0)])
        parts.append(p)
    return jnp.concatenate(parts, axis=0)


def _to_dz(w_nat):
    parts = []
    for nme, off, size in DZ_PARTS:
        p = w_nat[NAT_OFF[nme] + off:NAT_OFF[nme] + off + size]
        if size < LANES:
            p = jnp.pad(p, [(IN_SLAB[nme], LANES - size - IN_SLAB[nme]), (0, 0)])
        parts.append(p)
    return jnp.concatenate(parts, axis=0)


def _from_dz(dw):
    found, pos = {}, 0
    for nme, off, size in DZ_PARTS:
        start = pos + (IN_SLAB[nme] if size < LANES else 0)
        found.setdefault(nme, []).append(dw[start:start + size])
        pos += max(size, LANES)
    return jnp.concatenate([p for nme in IN_NAMES for p in found[nme]], axis=0)


def _rope_tables(lc, l):
    half = MLA_ROPE // 2
    inv = ROPE_BASE ** (-jnp.arange(0, half, 2, dtype=F32) / half)
    tok = jnp.arange(l)
    ang_r = (tok // GRID_W).astype(F32)[:, None] * inv
    ang_c = (tok % GRID_W).astype(F32)[:, None] * inv
    ang = jnp.concatenate([ang_r, ang_r, ang_c, ang_c], axis=-1)
    cos = jnp.concatenate([jnp.ones((lc, MLA_ROPE), F32), jnp.cos(ang)], axis=0)
    sin = jnp.concatenate([jnp.zeros((lc, MLA_ROPE), F32), jnp.sin(ang)], axis=0)
    t = lc + l
    tail = MLA_HP - MLA_QK
    ck = jnp.concatenate([jnp.ones((t, MLA_NOPE), F32), cos, jnp.ones((t, tail), F32)], axis=1)
    sk = jnp.concatenate([jnp.zeros((t, MLA_NOPE), F32), sin, jnp.zeros((t, tail), F32)], axis=1)
    return jnp.tile(ck, (1, MLA_H)), jnp.tile(sk, (1, MLA_H)), ck, sk


def _pad_heads(w):
    lead = w.shape[:-1]
    w = w.reshape(lead + (MLA_H, MLA_QK))
    return jnp.pad(w, [(0, 0)] * len(lead) + [(0, 0), (0, MLA_HP - MLA_QK)]).reshape(lead + (MLA_H * MLA_HP,))


def _unpad_heads(w):
    lead = w.shape[:-1]
    return w.reshape(lead + (MLA_H, MLA_HP))[..., :MLA_QK].reshape(lead + (MLA_H * MLA_QK,))


def _local_step(x, c, ctx, tgt, wf):
    nb, l, _ = x.shape
    lc = ctx.shape[1]
    assert lc == TM and l % TM == 0
    t = lc + l
    tpe = t // TM
    n = nb * t
    nt = n // TM
    bf = lambda a: a.astype(BF16)

    xs = jnp.concatenate([ctx, x], axis=1).reshape(n, D)
    assert nb <= 4
    cv = jnp.concatenate([c, jnp.zeros((4 - nb, D), F32), wf['c_ctx'][None, :], jnp.zeros((3, D), F32)], axis=0)
    mod_w_b = bf(wf['mod_w'])
    mod_all = _mod_fwd(cv, mod_w_b, wf['mod_b'].reshape(DEPTH, 1, 3 * D))
    cq, sq, ck, sk = _rope_tables(lc, l)
    pos = jnp.concatenate([jnp.arange(lc), jnp.arange(l)]).astype(jnp.int32)[:, None]
    seglen = jnp.concatenate([jnp.full((lc,), lc), jnp.full((l,), l)]).astype(jnp.int32)[:, None]
    tiles = np.arange(nt)
    ntp = -(-nt // LANES) * LANES
    sel = np.zeros((8, ntp), np.float32)
    sel[np.where(tiles % tpe == 0, 4, tiles // tpe), tiles] = 1.0
    sel = jnp.asarray(sel)

    def tile_sums(st):
        return jnp.pad(st.transpose(1, 0, 2), ((0, 0), (0, ntp - nt), (0, 0)))

    lw = []
    for ly in range(DEPTH):
        w_in_t = _to_padded(bf(wf['w_in'][ly]))
        w_in_dz = _to_dz(bf(wf['w_in'][ly]))
        w_uq_p = _pad_heads(bf(wf['mla_w_uq'][ly]))
        w2 = jnp.pad(jnp.stack([bf(wf['gla_af_w2'][ly]), bf(wf['gla_ab_w2'][ly])]),
                     ((0, 0), (0, LANES - GLA_RANK), (0, 0)))
        lw.append(dict(
            w_in_t=w_in_t, w_in_dz=w_in_dz,
            w_uq=w_uq_p, w_uq_t=w_uq_p.T,
            w_ukv=bf(wf['mla_w_ukv'][ly]), w_ukv_t=bf(wf['mla_w_ukv'][ly]).T,
            pool_w=bf(wf['pool_w'][ly]), pool_w_t=bf(wf['pool_w'][ly]).transpose(0, 2, 1),
            w2=w2, w2_t=w2.transpose(0, 2, 1),
            b2=jnp.stack([wf['gla_af_b'][ly], wf['gla_ab_b'][ly]]).reshape(2, 1, GLA_H * GLA_DK),
            wbm=bf(wf['w_branch_mla'][ly]), wbp=bf(wf['w_branch_pool'][ly]), wbg=bf(wf['w_branch_gla'][ly]),
            wout=bf(wf['w_out'][ly]),
            wbm_t=bf(wf['w_branch_mla'][ly]).T, wbp_t=bf(wf['w_branch_pool'][ly]).T,
            wbg_t=bf(wf['w_branch_gla'][ly]).T, wout_t=bf(wf['w_out'][ly]).T,
            pre_g=wf['pre_norm'][ly][None, :], post_g=wf['post_norm'][ly][None, :],
            q_norm=wf['mla_q_norm'][ly][None, :], kv_norm=wf['mla_kv_norm'][ly][None, :],
            pool_scale=wf['pool_scale'][ly][None, :], gla_norm=wf['gla_norm'][ly][None, :]))

    saved = []
    xcur = xs
    for ly in range(DEPTH):
        w = lw[ly]
        z, h = _pre_fwd(xcur, mod_all[ly], w['pre_g'], w['w_in_t'], tpe, f"pre_fwd{ly}")
        qb, kvb, krb = _mla_pre(z, w['q_norm'], w['kv_norm'], w['w_uq'], w['w_ukv'], cq, sq, ck, sk, tpe, f"mla_pre{ly}")
        o_mla, lse = _attn_fwd(qb, kvb, krb, nb, lc, f"attn_fwd{ly}")
        y_pool = _pool_fwd(z, w['pool_w'], w['pool_scale'], pos, seglen, nb, lc, f"pool_fwd{ly}")
        ogf, ogb, st_f, st_r = _gla_fwd(z, w['w2'], w['b2'], nb, lc, f"gla_fwd{ly}")
        xnew, out = _merge_fwd(xcur, z, o_mla, y_pool, ogf, ogb, w['gla_norm'], w['wbm'], w['wbp'], w['wbg'],
                               w['wout'], mod_all[ly], w['post_g'], tpe, f"merge_fwd{ly}")
        saved.append(dict(x=xcur, z=z, h=h, qb=qb, kvb=kvb, krb=krb, lse=lse, o_mla=o_mla, y_pool=y_pool,
                          st_f=st_f, st_r=st_r, ogf=ogf, ogb=ogb, out=out))
        xcur = xnew

    dxcur, lparts = _loss_grad(xcur, tgt.reshape(nb * l, D), nb, tpe)
    loss = jnp.sum(lparts[:, 0, 0])

    g = {k: [None] * DEPTH for k in WEIGHTS if k != 'c_ctx'}
    dcv = jnp.zeros((8, D), F32)
    dcc = None
    for ly in reversed(range(DEPTH)):
        w, s = lw[ly], saved[ly]
        (dz, dom, dyp, dog, st_b, g['w_branch_mla'][ly], g['w_branch_pool'][ly], g['w_branch_gla'][ly],
         g['w_out'][ly], dgn) = _merge_bwd(
            dxcur, s['out'], s['z'], s['o_mla'], s['y_pool'], s['ogf'], s['ogb'], w['gla_norm'], w['wbm'], w['wbp'],
            w['wbg'], w['wbm_t'], w['wbp_t'], w['wbg_t'], w['wout_t'], mod_all[ly], w['post_g'], tpe,
            f"merge_bwd{ly}")
        g['gla_norm'][ly] = dgn[0]
        dq, dkv, dkr = _attn_bwd(s['qb'], s['kvb'], s['krb'], s['o_mla'], s['lse'], dom, nb, lc, f"attn_bwd{ly}")
        dz, dwq, g['mla_w_ukv'][ly], dgq, dgkv = _mla_pre_bwd(
            s['z'], dq, dkv, dkr, w['q_norm'], w['kv_norm'], w['w_uq_t'], w['w_ukv_t'], cq, sq, ck, sk, dz, tpe,
            f"mla_pre_bwd{ly}")
        g['mla_w_uq'][ly] = _unpad_heads(dwq)
        g['mla_q_norm'][ly], g['mla_kv_norm'][ly] = dgq[0], dgkv[0]
        dz, g['pool_w'][ly], dps = _pool_bwd(s['z'], dyp, w['pool_w'], w['pool_w_t'], w['pool_scale'],
                                             pos, seglen, dz, nb, lc, f"pool_bwd{ly}")
        g['pool_scale'][ly] = dps[0]
        (dq_f, dk_f, dv_f, da_f, dq_r, dk_r, dv_r, da_r, dw2_f, db2_f, dw2_r, db2_r) = _gla_bwd(
            s['z'], w['w2'], w['w2_t'], w['b2'], s['st_f'], s['st_r'], dog, nb, lc, f"gla_bwd{ly}")
        dz = _gla_into_dz(dz, dq_f, dq_r, dk_f, dk_r, dv_f, dv_r, da_f, da_r, f"gla_dz{ly}")
        g['gla_af_w2'][ly], g['gla_ab_w2'][ly] = dw2_f[:GLA_RANK], dw2_r[:GLA_RANK]
        g['gla_af_b'][ly], g['gla_ab_b'][ly] = db2_f[0], db2_r[0]
        dxcur, st_a = _pre_bwd(dz, w['w_in_dz'], s['x'], dxcur, mod_all[ly], w['pre_g'], tpe, f"pre_bwd{ly}",
                               latent_only=(ly == 0))
        tk = next(k for k in (3072, 1024, 512, TM) if n % k == 0)
        g['w_in'][ly] = _from_dz(_matmul_tn(dz, s['h'], 1152, tk, f"w_in_grad{ly}"))
        dmw, dmb, dcv, dcc, dpre, dpost = _mod_bwd(cv, sel, tile_sums(st_a), tile_sums(st_b),
                                                   mod_w_b[ly].T, dcv, f"mod_bwd{ly}")
        g['mod_w'][ly], g['mod_b'][ly] = dmw, dmb[0]
        g['pre_norm'][ly], g['post_norm'][ly] = dpre[0], dpost[0]

    grads = {k: jnp.stack(v) for k, v in g.items()}
    grads['c_ctx'] = dcc[4]
    return loss, dxcur.reshape(nb, l, D), grads


def _place():
    x, y, c = lax.axis_index("x"), lax.axis_index("y"), lax.axis_index("c")
    chips = [(1 - x, y), (x, 1 - y), (1 - x, 1 - y)]
    return x, y, c, chips


def _hbm_call(body, name, out_shape, n_in, sems):
    any_spec = pl.BlockSpec(memory_space=pl.ANY)
    return pl.pallas_call(body, name=name, out_shape=out_shape, in_specs=[any_spec] * n_in,
                          out_specs=jax.tree.map(lambda _: any_spec, out_shape), scratch_shapes=sems)


def _all_gather_shards(ws):
    n = len(ws)

    def body(*refs):
        ins, outs, (send_sems, recv_sems) = refs[:n], refs[n:2 * n], refs[2 * n:]
        x, y, c, chips = _place()

        def copy(k, q, chip, half, to, src=None):
            dst = outs[k].at[2 * chip[0] + chip[1], half]
            return pltpu.make_async_remote_copy(src_ref=dst if src is None else src, dst_ref=dst,
                                                send_sem=send_sems.at[k, q], recv_sem=recv_sems.at[k, q],
                                                device_id=to, device_id_type=MESH)

        first = [copy(k, j, (x, y), c, (*chip, c), src=ins[k].at[c]) for k in range(n) for j, chip in enumerate(chips)]
        for cp in first:
            cp.start()
        passed = []
        for k in range(n):
            for j, chip in enumerate(chips):
                copy(k, j, chip, c, (x, y, c)).wait_recv()
                passed.append(copy(k, 3 + j, chip, c, (x, y, 1 - c)))
                passed[-1].start()
        for k in range(n):
            for j, chip in enumerate(chips):
                copy(k, 3 + j, chip, 1 - c, (x, y, 1 - c)).wait_recv()
        for cp in first + passed:
            cp.wait_send()

    shapes = tuple(jax.ShapeDtypeStruct((N_CHIPS,) + w.shape, w.dtype) for w in ws)
    return _hbm_call(body, "all_gather_shards", shapes, n,
                     [pltpu.SemaphoreType.DMA((n, 6)), pltpu.SemaphoreType.DMA((n, 6))])(*ws)


def _to_sibling(arrs, other_layer, name):
    n = len(arrs)

    def body(*refs):
        ins, outs, (send_sems, recv_sems) = refs[:n], refs[n:2 * n], refs[2 * n:]
        x, y, c, _ = _place()
        cps = [pltpu.make_async_remote_copy(src_ref=ins[k].at[1 - c] if other_layer else ins[k], dst_ref=outs[k],
                                            send_sem=send_sems.at[k], recv_sem=recv_sems.at[k],
                                            device_id=(x, y, 1 - c), device_id_type=MESH) for k in range(n)]
        for cp in cps:
            cp.start()
        for cp in cps:
            cp.wait()

    shapes = tuple(jax.ShapeDtypeStruct(a.shape[1:] if other_layer else a.shape, a.dtype) for a in arrs)
    return _hbm_call(body, name, shapes, n, [pltpu.SemaphoreType.DMA((n,)), pltpu.SemaphoreType.DMA((n,))])(*arrs)


def _scatter_to_chips(hs):
    n = len(hs)

    def body(*refs):
        ins, outs, (send_sems, recv_sems) = refs[:n], refs[n:2 * n], refs[2 * n:]
        x, y, c, chips = _place()
        me = 2 * x + y
        sends = []
        for k in range(n):
            for j, chip in enumerate(chips):
                cp = pltpu.make_async_remote_copy(src_ref=ins[k].at[2 * chip[0] + chip[1]], dst_ref=outs[k].at[me],
                                                  send_sem=send_sems.at[k, j], recv_sem=recv_sems.at[k, j],
                                                  device_id=(*chip, c), device_id_type=MESH)
                cp.start()
                sends.append(cp)
        for k in range(n):
            for j, chip in enumerate(chips):
                dst = outs[k].at[2 * chip[0] + chip[1]]
                pltpu.make_async_remote_copy(src_ref=dst, dst_ref=dst, send_sem=send_sems.at[k, j],
                                             recv_sem=recv_sems.at[k, j], device_id=(*chip, c),
                                             device_id_type=MESH).wait_recv()
        for cp in sends:
            cp.wait_send()

    shapes = tuple(jax.ShapeDtypeStruct(h.shape, h.dtype) for h in hs)
    return _hbm_call(body, "scatter_to_chips", shapes, n,
                     [pltpu.SemaphoreType.DMA((n, 3)), pltpu.SemaphoreType.DMA((n, 3))])(*hs)


BLOCK_BYTES = 10 * 1024 * 1024


def _blocks(r, cols, pos_bytes):
    rows = sorted({d for d in range(8, r + 1, 8) if r % d == 0} | {r})
    wide = sorted({d for d in range(LANES, cols + 1, LANES) if cols % d == 0} | {cols})
    fits = [(br * bc, bc, br) for br in rows for bc in wide if br * bc * pos_bytes <= BLOCK_BYTES]
    if not fits:
        return rows[0], wide[0]
    _, bc, br = max(fits)
    return br, bc


def _add_cores(b, got, name):
    _, ns, r, cols = b.shape
    br, bc = _blocks(r, cols, 2 * 4 + 4 + 2)

    def body(b_ref, g_ref, o_ref):
        mine = jnp.where(lax.axis_index("c") == 0, b_ref[0, 0], b_ref[1, 0])
        o_ref[0] = (mine + g_ref[0]).astype(BF16)

    spec = pl.BlockSpec((1, br, bc), lambda i, j, k: (i, j, k))
    return pl.pallas_call(body, name=name, grid=(ns, r // br, cols // bc),
                          in_specs=[pl.BlockSpec((2, 1, br, bc), lambda i, j, k: (0, i, j, k)), spec], out_specs=spec,
                          out_shape=jax.ShapeDtypeStruct((ns, r, cols), BF16), compiler_params=_cp(40))(b, got)


def _sum_chips(own, got, name):
    _, r, cols = own.shape
    br, bc = _blocks(r, cols, 2 * N_CHIPS * 2 + 4)

    def body(own_ref, got_ref, o_ref):
        me = 2 * lax.axis_index("x") + lax.axis_index("y")
        part = [jnp.where(me == j, own_ref[j], got_ref[j]).astype(F32) for j in range(N_CHIPS)]
        o_ref[...] = ((part[0] + part[1]) + part[2]) + part[3]

    spec = pl.BlockSpec((N_CHIPS, br, bc), lambda j, k: (0, j, k))
    return pl.pallas_call(body, name=name, grid=(r // br, cols // bc), in_specs=[spec, spec],
                          out_specs=pl.BlockSpec((br, bc), lambda j, k: (j, k)),
                          out_shape=jax.ShapeDtypeStruct((r, cols), F32), compiler_params=_cp(40))(own, got)


def _adamw(w, g_mine, g_other, m, v, name):
    _, r, cols = w.shape
    br, bc = _blocks(r, cols, 9 * 4)

    def body(w_ref, gm_ref, go_ref, m_ref, v_ref, g_ref, d_ref, nm_ref, nv_ref):
        gv = jnp.where(pl.program_id(0) == lax.axis_index("c"), gm_ref[...], go_ref[...])
        m2 = ADAM_B1 * m_ref[0] + (1.0 - ADAM_B1) * gv
        v2 = ADAM_B2 * v_ref[0] + (1.0 - ADAM_B2) * jnp.square(gv)
        m_hat = m2 / (1.0 - ADAM_B1 ** ADAM_STEP)
        v_hat = v2 / (1.0 - ADAM_B2 ** ADAM_STEP)
        g_ref[0] = gv
        d_ref[0] = -ADAM_LR * (m_hat / (jnp.sqrt(v_hat) + ADAM_EPS) + ADAM_WD * w_ref[0])
        nm_ref[0] = m2
        nv_ref[0] = v2

    lay = pl.BlockSpec((1, br, bc), lambda l, j, k: (l, j, k))
    flat = pl.BlockSpec((br, bc), lambda l, j, k: (j, k))
    shp = jax.ShapeDtypeStruct(w.shape, F32)
    return pl.pallas_call(body, name=name, grid=(2, r // br, cols // bc), in_specs=[lay, flat, flat, lay, lay],
                          out_specs=[lay] * 4, out_shape=(shp,) * 4, compiler_params=_cp(40))(w, g_mine, g_other, m, v)


def _pack_small(ts):
    flat = jnp.concatenate([ts[k].reshape(DEPTH, -1) for k in REPLICATED], axis=1)
    return flat.reshape(DEPTH, flat.shape[1] // LANES, LANES)


def _unpack_small(packed, like):
    flat = packed.reshape(DEPTH, -1)
    out, off = {}, 0
    for k in REPLICATED:
        size = like[k].size // DEPTH
        out[k] = flat[:, off:off + size].reshape(like[k].shape)
        off += size
    return out


def _shard_major(a, axis):
    if axis == 1:
        return a.reshape(DEPTH, N_CHIPS, a.shape[1] // N_CHIPS, a.shape[2])
    return a.reshape(DEPTH, a.shape[1], N_CHIPS, a.shape[2] // N_CHIPS).transpose(0, 2, 1, 3)


def kernel(x, c, ctx, c_ctx, mod_w, mod_b, pre_norm, post_norm, w_in, mla_q_norm, mla_w_uq, mla_kv_norm, mla_w_ukv, pool_w, pool_scale, gla_af_w2, gla_af_b, gla_ab_w2, gla_ab_b, gla_norm, w_branch_mla, w_branch_pool, w_branch_gla, w_out, loss_target, m_c_ctx, m_mod_w, m_mod_b, m_pre_norm, m_post_norm, m_w_in, m_mla_q_norm, m_mla_w_uq, m_mla_kv_norm, m_mla_w_ukv, m_pool_w, m_pool_scale, m_gla_af_w2, m_gla_af_b, m_gla_ab_w2, m_gla_ab_b, m_gla_norm, m_w_branch_mla, m_w_branch_pool, m_w_branch_gla, m_w_out, v_c_ctx, v_mod_w, v_mod_b, v_pre_norm, v_post_norm, v_w_in, v_mla_q_norm, v_mla_w_uq, v_mla_kv_norm, v_mla_w_ukv, v_pool_w, v_pool_scale, v_gla_af_w2, v_gla_af_b, v_gla_ab_w2, v_gla_ab_b, v_gla_norm, v_w_branch_mla, v_w_branch_pool, v_w_branch_gla, v_w_out):
    given = dict(locals())
    wts = {k: given[k] for k in WEIGHTS}
    my_chip = 2 * lax.axis_index("x") + lax.axis_index("y")

    view = lambda k, a: jnp.swapaxes(a, 1, 2) if k == 'w_in' else a
    axes = {k: (3 - axis if k == 'w_in' else axis) for k, axis in SHARDED}

    mine = [view(k, wts[k]).astype(BF16) for k, _ in SHARDED]
    gathered = _all_gather_shards(mine)
    full = dict(wts)
    for (k, _), own, got in zip(SHARDED, mine, gathered):
        full[k] = jnp.concatenate([jnp.where(my_chip == s, own, got[s]) for s in range(N_CHIPS)], axis=axes[k])

    loss_local, grad_x, grads = _local_step(x, c, ctx, loss_target, full)
    loss = lax.psum(loss_local, ("x", "y", "c"))

    small = _pack_small(grads)
    bufs = [_shard_major(grads[k], axes[k]) for k, _ in SHARDED]
    bufs.append(jnp.broadcast_to(small[:, None], (DEPTH, N_CHIPS) + small.shape[1:]))
    got = _to_sibling(bufs, True, "swap_halves")
    chip_sum = [_add_cores(b, g, f"add_cores{i}") for i, (b, g) in enumerate(zip(bufs, got))]
    recv = _scatter_to_chips(chip_sum)
    mine_red = [_sum_chips(cs, rc, f"sum_chips{i}") for i, (cs, rc) in enumerate(zip(chip_sum, recv))]
    other_red = _to_sibling(mine_red, False, "join_halves")

    outs = {}
    for i, (k, _) in enumerate(SHARDED):
        res = _adamw(view(k, wts[k]), mine_red[i], other_red[i], view(k, given['m_' + k]), view(k, given['v_' + k]),
                     f"adamw{i}")
        outs[k] = tuple(view(k, r) for r in res)
    packed = _adamw(_pack_small(wts), mine_red[-1], other_red[-1],
                    _pack_small({k: given['m_' + k] for k in REPLICATED}),
                    _pack_small({k: given['v_' + k] for k in REPLICATED}), "adamw_small")
    unpacked = [_unpack_small(p, wts) for p in packed]
    for k in REPLICATED:
        outs[k] = tuple(u[k] for u in unpacked)
    return (loss, grad_x, *[outs[k][q] for q in range(4) for k in WEIGHTS])
```
